```python
import math
import jax
import jax.numpy as jnp
from jax import lax
import numpy as np

D_MODEL = 1024
BATCH = 8
SEQ = 2048
DEPTH = 1

SSM_EXPAND = 2
D_INNER = SSM_EXPAND * D_MODEL
SSM_HEAD_DIM = 64
SSM_HEADS = D_INNER // SSM_HEAD_DIM
SSM_GROUPS = 4
SSM_STATE = 128
SSM_CONV = 4
SSM_CHUNK = 128
D_CONV = D_INNER + 2 * SSM_GROUPS * SSM_STATE

ATT_HEADS = 16
ATT_HEAD_DIM = 64
ATT_KV_GROUPS = 4
ATT_HPG = ATT_HEADS // ATT_KV_GROUPS
ATT_WIDTH = ATT_HEADS * ATT_HEAD_DIM
KV_WIDTH = ATT_KV_GROUPS * ATT_HEAD_DIM
CMP_BLOCK = 32
CMP_STRIDE = 16
CMP_HIDDEN = 256
SLC_BLOCK = 64
SLC_TOPK = 16
WINDOW = 512
ATT_Q_BLOCK = 64
FORCE_BONUS = 1000.0

N_BRANCH = 2
EPS = 1e-6
NEG_INF = -1e30

IN_SIZES = (D_INNER, D_CONV, SSM_HEADS, ATT_WIDTH, 6 * KV_WIDTH, 3 * ATT_HEADS, ATT_WIDTH, N_BRANCH * D_MODEL)
IN_SPLITS = tuple(int(v) for v in np.cumsum(IN_SIZES)[:-1])
IN_TOTAL = int(sum(IN_SIZES))

kernel_name = 'hybrid_ssd_nsa_griffin_block'


def _rmsnorm(x, g):
    xf = x.astype(jnp.float32)
    y = xf * lax.rsqrt(jnp.mean(xf * xf, axis=-1, keepdims=True) + EPS)
    return (y * g.astype(jnp.float32)).astype(x.dtype)


def _alibi_slopes():
    m = 2.0 ** (-8.0 * np.arange(1, ATT_HEADS + 1) / ATT_HEADS)
    return jnp.asarray(m.reshape(ATT_KV_GROUPS, ATT_HPG), dtype=jnp.float32)


def _causal_dwconv(u, w, b):
    k = w.shape[0]
    y = lax.conv_general_dilated(u, w[:, None, :].astype(u.dtype), window_strides=(1,),
                                 padding=[(k - 1, 0)], dimension_numbers=('NWC', 'WIO', 'NWC'),
                                 feature_group_count=u.shape[-1])
    return y + b


def _ssd_chunked(xh, dt, a, bm, cm):
    bsz, s, h, p = xh.shape
    nc = s // SSM_CHUNK
    cl = SSM_CHUNK
    hg = SSM_HEADS // SSM_GROUPS
    xdt = (xh * dt[..., None]).reshape(bsz, nc, cl, SSM_GROUPS, hg, p)
    adt = (dt * a).reshape(bsz, nc, cl, SSM_GROUPS, hg)
    b_ = bm.reshape(bsz, nc, cl, SSM_GROUPS, SSM_STATE)
    c_ = cm.reshape(bsz, nc, cl, SSM_GROUPS, SSM_STATE)
    a_cs = jnp.cumsum(adt, axis=2)
    causal = jnp.tril(jnp.ones((cl, cl), dtype=bool))[None, None, :, :, None, None]
    seg = a_cs[:, :, :, None] - a_cs[:, :, None, :]
    decay = jnp.exp(jnp.where(causal, seg, -jnp.inf))
    cb = jnp.einsum('bclgn,bcsgn->bclsg', c_, b_)
    y_diag = jnp.einsum('bclsg,bclsgh,bcsghp->bclghp', cb, decay, xdt)
    decay_states = jnp.exp(a_cs[:, :, -1:] - a_cs)
    states = jnp.einsum('bclgn,bclgh,bclghp->bcghpn', b_, decay_states, xdt)
    chunk_decay = jnp.exp(a_cs[:, :, -1])

    def step(carry, inp):
        st, dec = inp
        return carry * dec[..., None, None] + st, carry

    h0 = jnp.zeros(states.shape[:1] + states.shape[2:], states.dtype)
    _, prev = lax.scan(step, h0, (jnp.moveaxis(states, 1, 0), jnp.moveaxis(chunk_decay, 1, 0)))
    prev = jnp.moveaxis(prev, 0, 1)
    y_off = jnp.einsum('bclgn,bcghpn,bclgh->bclghp', c_, prev, jnp.exp(a_cs))
    return (y_diag + y_off).reshape(bsz, s, h, p)


def _mamba2_branch(z, xbc, dt_raw, conv_w, conv_b, dt_bias, a_log, d_skip, g_norm):
    bsz, s, _ = z.shape
    xbc = jax.nn.silu(_causal_dwconv(xbc, conv_w, conv_b))
    xs, bm, cm = jnp.split(xbc, [D_INNER, D_INNER + SSM_GROUPS * SSM_STATE], axis=-1)
    xh = xs.reshape(bsz, s, SSM_HEADS, SSM_HEAD_DIM)
    bm = bm.reshape(bsz, s, SSM_GROUPS, SSM_STATE)
    cm = cm.reshape(bsz, s, SSM_GROUPS, SSM_STATE)
    dt = jax.nn.softplus((dt_raw + dt_bias).astype(jnp.float32))
    a = -jnp.exp(a_log.astype(jnp.float32))
    y = _ssd_chunked(xh, dt, a, bm, cm) + xh * d_skip[:, None]
    y = y.reshape(bsz, s, D_INNER) * jax.nn.silu(z)
    gsz = D_INNER // SSM_GROUPS
    y = _rmsnorm(y.reshape(bsz, s, SSM_GROUPS, gsz), g_norm.reshape(SSM_GROUPS, gsz))
    return y.reshape(bsz, s, D_INNER)


def _compress(kv, pos, w1, w2):
    bsz, s, g, d = kv.shape
    n_cmp = (s - CMP_BLOCK) // CMP_STRIDE + 1
    idx = np.arange(n_cmp)[:, None] * CMP_STRIDE + np.arange(CMP_BLOCK)[None, :]
    blk = kv[:, idx] + pos[None, None, :, None, :]
    blk = jnp.transpose(blk, (0, 1, 3, 2, 4)).reshape(bsz, n_cmp, g, CMP_BLOCK * d)
    return jax.nn.silu(blk @ w1) @ w2


def _cmp_to_slc(n_cmp, n_slc):
    cs = np.arange(n_cmp) * CMP_STRIDE
    ss = np.arange(n_slc) * SLC_BLOCK
    lo = np.maximum(cs[:, None], ss[None, :])
    hi = np.minimum(cs[:, None] + CMP_BLOCK, ss[None, :] + SLC_BLOCK)
    return jnp.asarray(np.clip(hi - lo, 0, None) / CMP_BLOCK, dtype=jnp.float32)


def _nsa_branch(q, kv, gate_raw, z_att, cmp_pos_k, cmp_w1_k, cmp_w2_k, cmp_pos_v, cmp_w1_v, cmp_w2_v):
    bsz, s, _ = q.shape
    ng, hg, d = ATT_KV_GROUPS, ATT_HPG, ATT_HEAD_DIM
    scale = ATT_HEAD_DIM ** -0.5
    slopes = _alibi_slopes()
    q = q.reshape(bsz, s, ng, hg, d)
    kc, vc, ks, vs, kw, vw = [t.reshape(bsz, s, ng, d) for t in jnp.split(kv, 6, axis=-1)]
    t_pos = jnp.arange(s)

    kc = _compress(kc, cmp_pos_k, cmp_w1_k, cmp_w2_k)
    vc = _compress(vc, cmp_pos_v, cmp_w1_v, cmp_w2_v)
    n_cmp = kc.shape[1]
    cmp_end = jnp.arange(n_cmp) * CMP_STRIDE + (CMP_BLOCK - 1)
    dist_c = (t_pos[:, None] - cmp_end[None, :]).astype(jnp.float32)
    valid_c = dist_c >= 0
    sc = jnp.einsum('btghd,bjgd->bghtj', q, kc).astype(jnp.float32) * scale - slopes[None, :, :, None, None] * dist_c
    sc = jnp.where(valid_c, sc, NEG_INF)
    p_c = jnp.where(valid_c, jax.nn.softmax(sc, axis=-1), 0.0)
    o_cmp = jnp.einsum('bghtj,bjgd->btghd', p_c, vc)

    n_slc = s // SLC_BLOCK
    n_sel = min(SLC_TOPK, n_slc)
    imp = jnp.einsum('bghtj,jk->bgtk', p_c, _cmp_to_slc(n_cmp, n_slc))
    blk_t = (t_pos // SLC_BLOCK)[:, None]
    kk = jnp.arange(n_slc)[None, :]
    forced = (kk == 0) | (kk == blk_t) | (kk == blk_t - 1)
    score = jnp.where(forced, imp + FORCE_BONUS, jnp.where(kk <= blk_t, imp, -1.0))
    _, sel = lax.top_k(score, n_sel)

    ks_b = ks.reshape(bsz, n_slc, SLC_BLOCK, ng, d).transpose(0, 3, 1, 2, 4)
    vs_b = vs.reshape(bsz, n_slc, SLC_BLOCK, ng, d).transpose(0, 3, 1, 2, 4)
    kw_p = jnp.pad(kw, ((0, 0), (WINDOW, 0), (0, 0), (0, 0)))
    vw_p = jnp.pad(vw, ((0, 0), (WINDOW, 0), (0, 0), (0, 0)))
    b_ix = jnp.arange(bsz)[:, None, None, None]
    g_ix = jnp.arange(ng)[None, :, None, None]
    span = ATT_Q_BLOCK + WINDOW

    def q_block(i):
        t0 = i * ATT_Q_BLOCK
        tq = t0 + jnp.arange(ATT_Q_BLOCK)
        qb = lax.dynamic_slice_in_dim(q, t0, ATT_Q_BLOCK, axis=1)
        ib = lax.dynamic_slice_in_dim(sel, t0, ATT_Q_BLOCK, axis=2)
        kg = ks_b[b_ix, g_ix, ib]
        vg = vs_b[b_ix, g_ix, ib]
        spos = ib[..., None] * SLC_BLOCK + jnp.arange(SLC_BLOCK)
        ds = (tq[None, None, :, None, None] - spos).astype(jnp.float32)[:, :, None]
        ss = jnp.einsum('btghd,bgtnld->bghtnl', qb, kg).astype(jnp.float32) * scale - slopes[None, :, :, None, None, None] * ds
        ss = jnp.where(ds >= 0, ss, NEG_INF)
        ps = jax.nn.softmax(ss.reshape(ss.shape[:4] + (-1,)), axis=-1).reshape(ss.shape)
        o_s = jnp.einsum('bghtnl,bgtnld->btghd', ps, vg)
        kwb = lax.dynamic_slice_in_dim(kw_p, t0, span, axis=1)
        vwb = lax.dynamic_slice_in_dim(vw_p, t0, span, axis=1)
        kpos = t0 - WINDOW + jnp.arange(span)
        dw = (tq[:, None] - kpos[None, :]).astype(jnp.float32)
        ok = (dw >= 0) & (dw < WINDOW) & (kpos[None, :] >= 0)
        sw = jnp.einsum('btghd,bsgd->bghts', qb, kwb).astype(jnp.float32) * scale - slopes[None, :, :, None, None] * dw
        sw = jnp.where(ok, sw, NEG_INF)
        o_w = jnp.einsum('bghts,bsgd->btghd', jax.nn.softmax(sw, axis=-1), vwb)
        return o_s, o_w

    o_slc, o_win = lax.map(q_block, jnp.arange(s // ATT_Q_BLOCK))
    o_slc = jnp.moveaxis(o_slc, 0, 1).reshape(bsz, s, ng, hg, d)
    o_win = jnp.moveaxis(o_win, 0, 1).reshape(bsz, s, ng, hg, d)
    g = jax.nn.sigmoid(gate_raw.astype(jnp.float32)).reshape(bsz, s, ng, hg, 3, 1)
    o = g[..., 0, :] * o_cmp + g[..., 1, :] * o_slc + g[..., 2, :] * o_win
    return o.reshape(bsz, s, ATT_WIDTH) * jax.nn.silu(z_att)


def _layer(x, c, w_ada, b_ada, g_pre, g_post, w_in, conv_w, conv_b, dt_bias, a_log, d_skip,
           g_ssm_norm, w_ssm_out, cmp_pos_k, cmp_w1_k, cmp_w2_k, cmp_pos_v, cmp_w1_v, cmp_w2_v,
           w_nsa_out, w_out):
    bsz, s, dm = x.shape
    shift, scale, gate = jnp.split(c @ w_ada + b_ada, 3, axis=-1)
    h = _rmsnorm(x, g_pre) * (1.0 + scale[:, None, :]) + shift[:, None, :]
    z_ssm, xbc, dt_raw, q, kv, nsa_gate, z_att, merge_gate = jnp.split(h @ w_in, IN_SPLITS, axis=-1)
    y_ssm = _mamba2_branch(z_ssm, xbc, dt_raw, conv_w, conv_b, dt_bias, a_log, d_skip, g_ssm_norm) @ w_ssm_out
    y_nsa = _nsa_branch(q, kv, nsa_gate, z_att, cmp_pos_k, cmp_w1_k, cmp_w2_k,
                        cmp_pos_v, cmp_w1_v, cmp_w2_v) @ w_nsa_out
    mg = jax.nn.sigmoid(merge_gate.reshape(bsz, s, N_BRANCH, dm))
    merged = mg[:, :, 0] * y_ssm + mg[:, :, 1] * y_nsa
    out = merged @ w_out
    return x + gate[:, None, :] * _rmsnorm(out, g_post)


def setup_inputs(seed: int = 0) -> dict:
    key = jax.random.key(seed)
    ks = jax.random.split(key, 24)
    nl = DEPTH

    def nrm(k, shape, sd):
        return jax.random.normal(k, shape, jnp.float32) * sd

    dt0 = jnp.exp(jax.random.uniform(ks[9], (nl, SSM_HEADS), jnp.float32, math.log(1e-3), math.log(1e-1)))
    fan_cmp = CMP_BLOCK * ATT_HEAD_DIM
    return {
        'x': nrm(ks[0], (BATCH, SEQ, D_MODEL), 1.0),
        'c': nrm(ks[1], (BATCH, D_MODEL), 1.0),
        'w_ada': nrm(ks[2], (nl, D_MODEL, 3 * D_MODEL), 0.5 * D_MODEL ** -0.5),
        'b_ada': nrm(ks[3], (nl, 3 * D_MODEL), 0.01),
        'g_pre': 1.0 + nrm(ks[4], (nl, D_MODEL), 0.05),
        'g_post': 1.0 + nrm(ks[5], (nl, D_MODEL), 0.05),
        'w_in': nrm(ks[6], (nl, D_MODEL, IN_TOTAL), D_MODEL ** -0.5),
        'conv_w': nrm(ks[7], (nl, SSM_CONV, D_CONV), SSM_CONV ** -0.5),
        'conv_b': nrm(ks[8], (nl, D_CONV), 0.01),
        'dt_bias': dt0 + jnp.log(-jnp.expm1(-dt0)),
        'a_log': jnp.log(jax.random.uniform(ks[10], (nl, SSM_HEADS), jnp.float32, 1.0, 16.0)),
        'd_skip': 1.0 + nrm(ks[11], (nl, SSM_HEADS), 0.1),
        'g_ssm_norm': 1.0 + nrm(ks[12], (nl, D_INNER), 0.05),
        'w_ssm_out': nrm(ks[13], (nl, D_INNER, D_MODEL), D_INNER ** -0.5),
        'cmp_pos_k': nrm(ks[14], (nl, CMP_BLOCK, ATT_HEAD_DIM), 0.1),
        'cmp_w1_k': nrm(ks[15], (nl, fan_cmp, CMP_HIDDEN), fan_cmp ** -0.5),
        'cmp_w2_k': nrm(ks[16], (nl, CMP_HIDDEN, ATT_HEAD_DIM), CMP_HIDDEN ** -0.5),
        'cmp_pos_v': nrm(ks[17], (nl, CMP_BLOCK, ATT_HEAD_DIM), 0.1),
        'cmp_w1_v': nrm(ks[18], (nl, fan_cmp, CMP_HIDDEN), fan_cmp ** -0.5),
        'cmp_w2_v': nrm(ks[19], (nl, CMP_HIDDEN, ATT_HEAD_DIM), CMP_HIDDEN ** -0.5),
        'w_nsa_out': nrm(ks[20], (nl, ATT_WIDTH, D_MODEL), ATT_WIDTH ** -0.5),
        'w_out': nrm(ks[21], (nl, D_MODEL, D_MODEL), D_MODEL ** -0.5),
    }


def reference(x, c, w_ada, b_ada, g_pre, g_post, w_in, conv_w, conv_b, dt_bias, a_log, d_skip,
              g_ssm_norm, w_ssm_out, cmp_pos_k, cmp_w1_k, cmp_w2_k, cmp_pos_v, cmp_w1_v, cmp_w2_v,
              w_nsa_out, w_out):
    for layer in range(DEPTH):
        x = _layer(x, c, w_ada[layer], b_ada[layer], g_pre[layer], g_post[layer], w_in[layer],
                   conv_w[layer], conv_b[layer], dt_bias[layer], a_log[layer], d_skip[layer],
                   g_ssm_norm[layer], w_ssm_out[layer], cmp_pos_k[layer], cmp_w1_k[layer],
                   cmp_w2_k[layer], cmp_pos_v[layer], cmp_w1_v[layer], cmp_w2_v[layer],
                   w_nsa_out[layer], w_out[layer])
    return x
```

```python
import functools

import numpy as np
import jax
import jax.numpy as jnp
from jax import lax
from jax.experimental import pallas as pl
from jax.experimental.pallas import tpu as pltpu

D_MODEL = 1024
D_INNER = 2048
SSM_HEAD_DIM = 64
SSM_HEADS = 32
SSM_GROUPS = 4
SSM_STATE = 128
SSM_CONV = 4
SSM_CHUNK = 128
D_CONV = D_INNER + 2 * SSM_GROUPS * SSM_STATE

ATT_HEADS = 16
ATT_HEAD_DIM = 64
ATT_KV_GROUPS = 4
ATT_HPG = 4
ATT_WIDTH = 1024
KV_WIDTH = 256
CMP_BLOCK = 32
CMP_STRIDE = 16
CMP_HIDDEN = 256
SLC_BLOCK = 64
SLC_SHIFT = 6
SLC_TOPK = 16
WINDOW = 512
FORCE_BONUS = 1000.0
EPS = 1e-6
NEG = -1e30

F32 = jnp.float32
BF16 = jnp.bfloat16
HIGHEST = lax.Precision.HIGHEST

_SIZES = (D_INNER, D_CONV, SSM_HEADS, ATT_WIDTH, 6 * KV_WIDTH, 3 * ATT_HEADS, ATT_WIDTH, 2 * D_MODEL)
_OFFS = tuple(int(v) for v in np.cumsum((0,) + _SIZES))
IN_Z, IN_XBC, IN_DT, IN_Q, IN_KV, IN_GATE, IN_ZATT, IN_MERGE = _OFFS[:8]

F_XBC, F_ZATT, F_MERGE, F_ZSSM, F_KVC, F_SMALL = 0, 3072, 4096, 6144, 8192, 8704
NF = 8832
SMALL_GATE = 32
B_Q, B_KSEL, B_KWIN = 0, 1024, 1536
NB = 2048

TM_PROJ = 512
TN_F = 2944
TQ = 128
TK = 256
VMEM_LIMIT = 48 * 1024 * 1024


def _cp(sem):
    return pltpu.CompilerParams(dimension_semantics=sem, vmem_limit_bytes=VMEM_LIMIT)


def _silu(v):
    return v * jax.nn.sigmoid(v)


def _ada_kernel(c_ref, w_ref, b_ref, o_ref):
    o_ref[...] = jnp.dot(c_ref[...], w_ref[...], preferred_element_type=F32,
                         precision=HIGHEST) + b_ref[...]


def _ada(c, w_ada, b_ada):
    bsz = c.shape[0]
    return pl.pallas_call(
        _ada_kernel,
        grid=(3,),
        in_specs=[pl.BlockSpec((bsz, D_MODEL), lambda j: (0, 0)),
                  pl.BlockSpec((D_MODEL, D_MODEL), lambda j: (0, j)),
                  pl.BlockSpec((1, D_MODEL), lambda j: (0, j))],
        out_specs=pl.BlockSpec((bsz, D_MODEL), lambda j: (0, j)),
        out_shape=jax.ShapeDtypeStruct((bsz, 3 * D_MODEL), F32),
        compiler_params=_cp(("arbitrary",)),
        name="ada",
    )(c, w_ada, b_ada.reshape(1, 3 * D_MODEL))


def _proj_kernel(x_ref, shift_ref, scale_ref, g_ref, w_ref, o_ref, h_scr):
    @pl.when(pl.program_id(2) == 0)
    def _():
        xf = x_ref[0]
        y = xf * lax.rsqrt(jnp.mean(xf * xf, axis=-1, keepdims=True) + EPS) * g_ref[...]
        h_scr[...] = (y * (1.0 + scale_ref[0]) + shift_ref[0]).astype(BF16)

    o_ref[0] = jnp.dot(h_scr[...], w_ref[...], preferred_element_type=F32).astype(o_ref.dtype)


def _proj(x, shift, scale, g_pre, w, tn, out_dtype, name):
    bsz, s, _ = x.shape
    n = w.shape[1]
    return pl.pallas_call(
        _proj_kernel,
        grid=(bsz, s // TM_PROJ, n // tn),
        in_specs=[pl.BlockSpec((1, TM_PROJ, D_MODEL), lambda b, i, j: (b, i, 0)),
                  pl.BlockSpec((1, 1, D_MODEL), lambda b, i, j: (b, 0, 0)),
                  pl.BlockSpec((1, 1, D_MODEL), lambda b, i, j: (b, 0, 0)),
                  pl.BlockSpec((1, D_MODEL), lambda b, i, j: (0, 0)),
                  pl.BlockSpec((D_MODEL, tn), lambda b, i, j: (0, j))],
        out_specs=pl.BlockSpec((1, TM_PROJ, tn), lambda b, i, j: (b, i, j)),
        out_shape=jax.ShapeDtypeStruct((bsz, s, n), out_dtype),
        scratch_shapes=[pltpu.VMEM((TM_PROJ, D_MODEL), BF16)],
        compiler_params=_cp(("arbitrary", "arbitrary", "arbitrary")),
        name=name,
    )(x, shift, scale, g_pre.reshape(1, D_MODEL), w)


def _ssm_kernel(xbc_ref, z_ref, small_ref, convw_ref, convb_ref, dtb_ref, alog_ref,
                dskip_ref, gn_ref, o_ref, xpad_scr, state_scr, y_scr):
    cl = SSM_CHUNK

    @pl.when(pl.program_id(1) == 0)
    def _():
        xpad_scr[0:8, :] = jnp.zeros((8, D_CONV), F32)
        state_scr[...] = jnp.zeros(state_scr.shape, F32)

    cur = xbc_ref[0]
    xpad_scr[8:8 + cl, :] = cur
    acc = jnp.broadcast_to(convb_ref[...], (cl, D_CONV))
    for k in range(SSM_CONV):
        lo = 8 - (SSM_CONV - 1) + k
        acc = acc + convw_ref[k:k + 1, :] * xpad_scr[lo:lo + cl, :]
    xpad_scr[0:8, :] = cur[cl - 8:cl, :]
    u = _silu(acc)
    xs = u[:, :D_INNER]
    bm = u[:, D_INNER:D_INNER + SSM_GROUPS * SSM_STATE]
    cm = u[:, D_INNER + SSM_GROUPS * SSM_STATE:]

    pre = small_ref[0] + dtb_ref[...]
    dt = jnp.maximum(pre, 0.0) + jnp.log1p(jnp.exp(-jnp.abs(pre)))
    a = -jnp.exp(alog_ref[...])
    adt = dt * a
    row = lax.broadcasted_iota(jnp.int32, (cl, cl), 0)
    col = lax.broadcasted_iota(jnp.int32, (cl, cl), 1)
    causal = row >= col
    tri = causal.astype(F32)
    a_cs = jnp.dot(tri, adt, preferred_element_type=F32, precision=HIGHEST)
    a_cs_t = a_cs.T
    dt_t = dt.T
    a_last = a_cs[cl - 1:cl, :]
    ea = jnp.exp(a_cs)
    dsc = jnp.exp(a_last - a_cs) * dt
    cdec = jnp.exp(a_last)
    lane = lax.broadcasted_iota(jnp.int32, (cl, 2 * SSM_HEAD_DIM), 1)
    first_half = lane < SSM_HEAD_DIM

    hg = SSM_HEADS // SSM_GROUPS
    for g in range(SSM_GROUPS):
        bg = bm[:, g * SSM_STATE:(g + 1) * SSM_STATE]
        cg = cm[:, g * SSM_STATE:(g + 1) * SSM_STATE]
        cb = lax.dot_general(cg.astype(BF16), bg.astype(BF16), (((1,), (1,)), ((), ())),
                             preferred_element_type=F32)
        for pp in range(hg // 2):
            pair = g * (hg // 2) + pp
            xs_pair = xs[:, pair * 128:(pair + 1) * 128]
            st = state_scr[pair]
            rhs = jnp.concatenate([xs_pair, st], axis=0).astype(BF16)
            lhs_list, bs_list = [], []
            for e in range(2):
                h = 2 * pair + e
                seg = a_cs[:, h:h + 1] - a_cs_t[h:h + 1, :]
                lmat = jnp.exp(jnp.where(causal, seg, NEG))
                gmat = cb * lmat * dt_t[h:h + 1, :]
                lhs_list.append(jnp.concatenate([gmat, cg * ea[:, h:h + 1]], axis=1))
                bs_list.append(bg * dsc[:, h:h + 1])
            lhs = jnp.concatenate(lhs_list, axis=0).astype(BF16)
            yy = jnp.dot(lhs, rhs, preferred_element_type=F32)
            y_scr[:, pair * 128:(pair + 1) * 128] = jnp.where(first_half, yy[:cl], yy[cl:])
            bs = jnp.concatenate(bs_list, axis=1).astype(BF16)
            new = lax.dot_general(bs, xs_pair.astype(BF16), (((0,), (0,)), ((), ())),
                                  preferred_element_type=F32)
            dec = jnp.where(first_half[0:1], cdec[:, 2 * pair:2 * pair + 1],
                            cdec[:, 2 * pair + 1:2 * pair + 2])
            state_scr[pair] = st * dec + jnp.where(first_half, new[:SSM_STATE], new[SSM_STATE:])

    y = y_scr[...] + xs * dskip_ref[...]
    y = y * _silu(z_ref[0])
    gsz = D_INNER // SSM_GROUPS
    for g in range(SSM_GROUPS):
        yg = y[:, g * gsz:(g + 1) * gsz]
        yn = yg * lax.rsqrt(jnp.mean(yg * yg, axis=-1, keepdims=True) + EPS)
        o_ref[0, :, g * gsz:(g + 1) * gsz] = (yn * gn_ref[:, g * gsz:(g + 1) * gsz]).astype(o_ref.dtype)


def _ssm(pf, conv_w, conv_b, dt_bias, a_log, d_skip, g_norm):
    bsz, s, _ = pf.shape
    cl = SSM_CHUNK
    pad = 128 - SSM_HEADS
    dtb = jnp.pad(dt_bias, (0, pad)).reshape(1, 128)
    alog = jnp.pad(a_log, (0, pad)).reshape(1, 128)
    dskip = jnp.repeat(d_skip, SSM_HEAD_DIM).reshape(1, D_INNER)
    const = lambda shape: pl.BlockSpec(shape, lambda b, c: (0, 0))
    return pl.pallas_call(
        _ssm_kernel,
        grid=(bsz, s // cl),
        in_specs=[pl.BlockSpec((1, cl, D_CONV), lambda b, c: (b, c, F_XBC // D_CONV)),
                  pl.BlockSpec((1, cl, D_INNER), lambda b, c: (b, c, F_ZSSM // D_INNER)),
                  pl.BlockSpec((1, cl, 128), lambda b, c: (b, c, F_SMALL // 128)),
                  const((SSM_CONV, D_CONV)), const((1, D_CONV)), const((1, 128)), const((1, 128)),
                  const((1, D_INNER)), const((1, D_INNER))],
        out_specs=pl.BlockSpec((1, cl, D_INNER), lambda b, c: (b, c, 0)),
        out_shape=jax.ShapeDtypeStruct((bsz, s, D_INNER), BF16),
        scratch_shapes=[pltpu.VMEM((8 + cl, D_CONV), F32),
                        pltpu.VMEM((SSM_HEADS // 2, SSM_STATE, 2 * SSM_HEAD_DIM), F32),
                        pltpu.VMEM((cl, D_INNER), F32)],
        compiler_params=_cp(("arbitrary", "arbitrary")),
        name="ssm",
    )(pf, pf, pf, conv_w, conv_b.reshape(1, D_CONV), dtb, alog, dskip, g_norm.reshape(1, D_INNER))


def _cmp_kernel(r_ref, pos_ref, w1_ref, w2_ref, o_ref):
    half = CMP_STRIDE * ATT_HEAD_DIM
    r = r_ref[0, 0].astype(BF16)
    w1 = w1_ref[0]
    first = jnp.dot(r, w1[:half], preferred_element_type=F32)
    second = jnp.dot(r, w1[half:], preferred_element_type=F32)
    posb = jnp.broadcast_to(pos_ref[0], (8, 2 * half)).astype(BF16)
    cpos = jnp.dot(posb, w1, preferred_element_type=F32)[0:1]
    nrow = r.shape[0]
    hid = first + pltpu.roll(second, nrow - 1, 0) + cpos
    out = jnp.dot(_silu(hid).astype(BF16), w2_ref[0], preferred_element_type=F32)
    o_ref[0, 0] = out


def _compress(kvc_r, pos, w1, w2):
    bsz, _, nrow, _ = kvc_r.shape
    g = ATT_KV_GROUPS
    return pl.pallas_call(
        _cmp_kernel,
        grid=(bsz, 2 * g),
        in_specs=[pl.BlockSpec((1, 1, nrow, 1024), lambda b, j: (b, j, 0, 0)),
                  pl.BlockSpec((1, 1, 2048), lambda b, j: (j // g, 0, 0)),
                  pl.BlockSpec((1, 2048, CMP_HIDDEN), lambda b, j: (j // g, 0, 0)),
                  pl.BlockSpec((1, CMP_HIDDEN, ATT_HEAD_DIM), lambda b, j: (j // g, 0, 0))],
        out_specs=pl.BlockSpec((1, 1, nrow, ATT_HEAD_DIM), lambda b, j: (b, j, 0, 0)),
        out_shape=jax.ShapeDtypeStruct((bsz, 2 * g, nrow, ATT_HEAD_DIM), F32),
        compiler_params=_cp(("arbitrary", "arbitrary")),
        name="cmp",
    )(kvc_r, pos, w1, w2)


def _stack_heads(q):
    return jnp.concatenate([q[:, h * ATT_HEAD_DIM:(h + 1) * ATT_HEAD_DIM] for h in range(ATT_HPG)], axis=0)


def _slope_col(slopes_ref, g, t):
    return jnp.concatenate([jnp.full((t, 1), slopes_ref[g * ATT_HPG + h], F32) for h in range(ATT_HPG)], axis=0)


def _cmpattn_kernel(slopes_ref, q_ref, kc_ref, vc_ref, m_ref, ocmp_ref, sel_ref):
    g = pl.program_id(1)
    i = pl.program_id(2)
    t0 = i * TQ
    ncmp_pad = kc_ref.shape[2]
    n_slc = sel_ref.shape[3]
    r = ATT_HPG * TQ
    qs = _stack_heads(q_ref[0]).astype(F32)
    kc = kc_ref[0, 0]
    vc = vc_ref[0, 0]
    s = lax.dot_general(qs, kc, (((1,), (1,)), ((), ())), preferred_element_type=F32,
                        precision=HIGHEST)
    jcol = lax.broadcasted_iota(jnp.int32, (1, ncmp_pad), 1)
    cmp_end = jcol * CMP_STRIDE + (CMP_BLOCK - 1)
    slope = _slope_col(slopes_ref, g, TQ)
    s = s + slope * (cmp_end - t0).astype(F32)
    trow = t0 + (lax.broadcasted_iota(jnp.int32, (r, ncmp_pad), 0) & (TQ - 1))
    valid = (cmp_end <= trow) & (jcol < ncmp_pad - 1)
    s = jnp.where(valid, s, NEG)
    mx = jnp.max(s, axis=-1, keepdims=True)
    p = jnp.where(valid, jnp.exp(s - mx), 0.0)
    l = jnp.sum(p, axis=-1, keepdims=True)
    p = p * jnp.where(l > 0.0, 1.0 / jnp.where(l > 0.0, l, 1.0), 0.0)
    o = jnp.dot(p, vc, preferred_element_type=F32, precision=HIGHEST)
    ocmp_ref[0, 0] = o.reshape(ATT_HPG, TQ, ATT_HEAD_DIM)
    psum = jnp.sum(p.reshape(ATT_HPG, TQ, ncmp_pad), axis=0)
    imp = jnp.dot(psum, m_ref[...], preferred_element_type=F32, precision=HIGHEST)

    kk = lax.broadcasted_iota(jnp.int32, (TQ, n_slc), 1)
    blk_t = (t0 + lax.broadcasted_iota(jnp.int32, (TQ, n_slc), 0)) >> SLC_SHIFT
    forced = (kk == 0) | (kk == blk_t) | (kk == blk_t - 1)
    score = jnp.where(forced, imp + FORCE_BONUS, jnp.where(kk <= blk_t, imp, -1.0))
    rank = jnp.zeros((TQ, n_slc), F32)
    for j in range(n_slc):
        cj = score[:, j:j + 1]
        beats = (cj > score) | ((cj == score) & (kk > j))
        rank = rank + jnp.where(beats, 1.0, 0.0)
    sel_ref[0, 0] = jnp.where(rank < float(SLC_TOPK), 1.0, 0.0)


def _cmp_to_slc_matrix(n_cmp_pad, n_cmp, n_slc):
    cs = np.arange(n_cmp_pad) * CMP_STRIDE
    ss = np.arange(n_slc) * SLC_BLOCK
    lo = np.maximum(cs[:, None], ss[None, :])
    hi = np.minimum(cs[:, None] + CMP_BLOCK, ss[None, :] + SLC_BLOCK)
    m = np.clip(hi - lo, 0, None) / CMP_BLOCK
    m[n_cmp:] = 0.0
    return jnp.asarray(m, dtype=F32)


def _alibi_slopes():
    return jnp.asarray(2.0 ** (-8.0 * np.arange(1, ATT_HEADS + 1) / ATT_HEADS), dtype=F32)


def _cmpattn(pb, cmp_kv, s):
    bsz = pb.shape[0]
    g = ATT_KV_GROUPS
    nrow = cmp_kv.shape[2]
    n_cmp = (s - CMP_BLOCK) // CMP_STRIDE + 1
    n_slc = s // SLC_BLOCK
    mmat = _cmp_to_slc_matrix(nrow, n_cmp, n_slc)
    return pl.pallas_call(
        _cmpattn_kernel,
        grid=(bsz, g, s // TQ),
        in_specs=[pl.BlockSpec(memory_space=pltpu.SMEM),
                  pl.BlockSpec((1, TQ, KV_WIDTH), lambda b, gg, i: (b, i, gg)),
                  pl.BlockSpec((1, 1, nrow, ATT_HEAD_DIM), lambda b, gg, i: (b, gg, 0, 0)),
                  pl.BlockSpec((1, 1, nrow, ATT_HEAD_DIM), lambda b, gg, i: (b, g + gg, 0, 0)),
                  pl.BlockSpec((nrow, n_slc), lambda b, gg, i: (0, 0))],
        out_specs=[pl.BlockSpec((1, 1, ATT_HPG, TQ, ATT_HEAD_DIM), lambda b, gg, i: (b, gg, 0, i, 0)),
                   pl.BlockSpec((1, 1, TQ, n_slc), lambda b, gg, i: (b, gg, i, 0))],
        out_shape=[jax.ShapeDtypeStruct((bsz, g, ATT_HPG, s, ATT_HEAD_DIM), F32),
                   jax.ShapeDtypeStruct((bsz, g, s, n_slc), F32)],
        compiler_params=_cp(("arbitrary", "arbitrary", "arbitrary")),
        name="cmpattn",
    )(_alibi_slopes(), pb, cmp_kv, cmp_kv, mmat)


def _flash(qs, slope, kv_ref, bias_ref, bias_off, j_lo, j_hi, t0):
    r = qs.shape[0]
    kiota = lax.broadcasted_iota(jnp.int32, (1, TK), 1)

    def body(j, carry):
        m, l, acc = carry
        start = pl.multiple_of(j * TK, TK)
        kv = kv_ref[0, pl.ds(start, TK), :]
        k = kv[:, :ATT_HEAD_DIM]
        v = kv[:, ATT_HEAD_DIM:]
        s = lax.dot_general(qs, k, (((1,), (1,)), ((), ())), preferred_element_type=F32)
        s = s + slope * (kiota + (start - t0)).astype(F32)
        s = (s.reshape(ATT_HPG, TQ, TK) + bias_ref[j - bias_off][None]).reshape(r, TK)
        m_new = jnp.maximum(m, jnp.max(s, axis=-1, keepdims=True))
        alpha = jnp.exp(m - m_new)
        p = jnp.exp(s - m_new)
        l = alpha * l + jnp.sum(p, axis=-1, keepdims=True)
        acc = alpha * acc + jnp.dot(p.astype(BF16), v, preferred_element_type=F32)
        return m_new, l, acc

    init = (jnp.full((r, 1), NEG, F32), jnp.zeros((r, 1), F32), jnp.zeros((r, ATT_HEAD_DIM), F32))
    _, l, acc = lax.fori_loop(j_lo, j_hi, body, init)
    return acc / l


def _attn_kernel(slopes_ref, q_ref, ksel_ref, kwin_ref, sel_ref, ocmp_ref, small_ref, zatt_ref,
                 o_ref, sbias_scr, wbias_scr):
    g = pl.program_id(1)
    i = pl.program_id(2)
    t0 = i * TQ
    s_len = ksel_ref.shape[1]
    n_slc = sel_ref.shape[3]
    n_kv = s_len // TK
    n_w = wbias_scr.shape[0]
    qs = _stack_heads(q_ref[0])
    slope = _slope_col(slopes_ref, g, TQ)

    trow = t0 + lax.broadcasted_iota(jnp.int32, (TQ, TK), 0)
    cpos = lax.broadcasted_iota(jnp.int32, (TQ, TK), 1)
    blk_iota = lax.broadcasted_iota(jnp.int32, (n_slc, TK), 0)
    cblk = lax.broadcasted_iota(jnp.int32, (n_slc, TK), 1)
    sel = sel_ref[0, 0].astype(BF16)

    for j in range(n_kv):
        expand = jnp.where(blk_iota == ((cblk + j * TK) >> SLC_SHIFT), 1.0, 0.0).astype(BF16)
        hit = jnp.dot(sel, expand, preferred_element_type=F32)
        ok = (hit > 0.5) & (cpos + j * TK <= trow)
        sbias_scr[j] = jnp.where(ok, 0.0, NEG)
    hi = lax.div(t0 + (TQ - 1), TK) + 1
    o_slc = _flash(qs, slope, ksel_ref, sbias_scr, 0, 0, hi, t0)

    w_lo = lax.div(jnp.maximum(t0 - (WINDOW - 1), 0), TK)
    for jj in range(n_w):
        d = trow - (cpos + (w_lo + jj) * TK)
        wbias_scr[jj] = jnp.where((d >= 0) & (d < WINDOW), 0.0, NEG)
    o_win = _flash(qs, slope, kwin_ref, wbias_scr, w_lo, w_lo, hi, t0)

    rr = lax.broadcasted_iota(jnp.int32, (128, 128), 0)
    cc = lax.broadcasted_iota(jnp.int32, (128, 128), 1)
    pick = jnp.where((rr == SMALL_GATE + g * (3 * ATT_HPG) + cc) & (cc < 3 * ATT_HPG), 1.0, 0.0)
    gts = jax.nn.sigmoid(jnp.dot(small_ref[0], pick, preferred_element_type=F32, precision=HIGHEST))

    def gate_col(jg):
        return jnp.concatenate([jnp.broadcast_to(gts[:, 3 * h + jg:3 * h + jg + 1], (TQ, 1))
                                for h in range(ATT_HPG)], axis=0)

    o_cmp = ocmp_ref[0, 0].reshape(ATT_HPG * TQ, ATT_HEAD_DIM)
    o = gate_col(0) * o_cmp + gate_col(1) * o_slc + gate_col(2) * o_win
    o = jnp.concatenate([o[h * TQ:(h + 1) * TQ] for h in range(ATT_HPG)], axis=1)
    o_ref[0] = (o * _silu(zatt_ref[0])).astype(o_ref.dtype)


def _attn(pb, pf, sel, ocmp):
    bsz, s, _ = pb.shape
    g = ATT_KV_GROUPS
    n_slc = sel.shape[3]
    n_w = (TQ + WINDOW - 2) // TK + 2
    return pl.pallas_call(
        _attn_kernel,
        grid=(bsz, g, s // TQ),
        in_specs=[pl.BlockSpec(memory_space=pltpu.SMEM),
                  pl.BlockSpec((1, TQ, KV_WIDTH), lambda b, gg, i: (b, i, B_Q // KV_WIDTH + gg)),
                  pl.BlockSpec((1, s, 128), lambda b, gg, i: (b, 0, B_KSEL // 128 + gg)),
                  pl.BlockSpec((1, s, 128), lambda b, gg, i: (b, 0, B_KWIN // 128 + gg)),
                  pl.BlockSpec((1, 1, TQ, n_slc), lambda b, gg, i: (b, gg, i, 0)),
                  pl.BlockSpec((1, 1, ATT_HPG, TQ, ATT_HEAD_DIM), lambda b, gg, i: (b, gg, 0, i, 0)),
                  pl.BlockSpec((1, TQ, 128), lambda b, gg, i: (b, i, F_SMALL // 128)),
                  pl.BlockSpec((1, TQ, KV_WIDTH), lambda b, gg, i: (b, i, F_ZATT // KV_WIDTH + gg))],
        out_specs=pl.BlockSpec((1, TQ, KV_WIDTH), lambda b, gg, i: (b, i, gg)),
        out_shape=jax.ShapeDtypeStruct((bsz, s, ATT_WIDTH), BF16),
        scratch_shapes=[pltpu.VMEM((s // TK, TQ, TK), F32), pltpu.VMEM((n_w, TQ, TK), F32)],
        compiler_params=_cp(("arbitrary", "arbitrary", "arbitrary")),
        name="attn",
    )(_alibi_slopes(), pb, pb, pb, sel, ocmp, pf, pf)


def _merge_kernel(yssm_ref, onsa_ref, mg_ref, x_ref, gate_ref, gpost_ref, wssm_ref, wnsa_ref, wout_ref, o_ref):
    y_ssm = jnp.dot(yssm_ref[0], wssm_ref[...], preferred_element_type=F32)
    y_nsa = jnp.dot(onsa_ref[0], wnsa_ref[...], preferred_element_type=F32)
    mg = jax.nn.sigmoid(mg_ref[0])
    merged = mg[:, :D_MODEL] * y_ssm + mg[:, D_MODEL:] * y_nsa
    out = jnp.dot(merged.astype(BF16), wout_ref[...], preferred_element_type=F32)
    yn = out * lax.rsqrt(jnp.mean(out * out, axis=-1, keepdims=True) + EPS) * gpost_ref[...]
    o_ref[0] = x_ref[0] + gate_ref[0] * yn


def _merge(yssm, onsa, pf, x, gate, g_post, w_ssm_out, w_nsa_out, w_out):
    bsz, s, _ = x.shape
    tm = 512
    const = lambda shape: pl.BlockSpec(shape, lambda b, i: (0, 0))
    return pl.pallas_call(
        _merge_kernel,
        grid=(bsz, s // tm),
        in_specs=[pl.BlockSpec((1, tm, D_INNER), lambda b, i: (b, i, 0)),
                  pl.BlockSpec((1, tm, ATT_WIDTH), lambda b, i: (b, i, 0)),
                  pl.BlockSpec((1, tm, 2 * D_MODEL), lambda b, i: (b, i, F_MERGE // (2 * D_MODEL))),
                  pl.BlockSpec((1, tm, D_MODEL), lambda b, i: (b, i, 0)),
                  pl.BlockSpec((1, 1, D_MODEL), lambda b, i: (b, 0, 0)),
                  const((1, D_MODEL)),
                  const((D_INNER, D_MODEL)), const((ATT_WIDTH, D_MODEL)), const((D_MODEL, D_MODEL))],
        out_specs=pl.BlockSpec((1, tm, D_MODEL), lambda b, i: (b, i, 0)),
        out_shape=jax.ShapeDtypeStruct((bsz, s, D_MODEL), F32),
        compiler_params=_cp(("arbitrary", "arbitrary")),
        name="merge",
    )(yssm, onsa, pf, x, gate, g_post.reshape(1, D_MODEL), w_ssm_out, w_nsa_out, w_out)


def _split_w_in(w_in):
    def cols(off, n):
        return w_in[:, off:off + n]

    dm = w_in.shape[0]
    small = jnp.concatenate([cols(IN_DT, SSM_HEADS), cols(IN_GATE, 3 * ATT_HEADS),
                             jnp.zeros((dm, 128 - SSM_HEADS - 3 * ATT_HEADS), w_in.dtype)], axis=1)
    wf = jnp.concatenate([cols(IN_XBC, D_CONV), cols(IN_ZATT, ATT_WIDTH), cols(IN_MERGE, 2 * D_MODEL),
                          cols(IN_Z, D_INNER), cols(IN_KV, 2 * KV_WIDTH), small], axis=1)
    hd = ATT_HEAD_DIM

    def interleave(k_off, v_off):
        return [cols(o + gg * hd, hd) for gg in range(ATT_KV_GROUPS) for o in (k_off, v_off)]

    q_scaled = cols(IN_Q, ATT_WIDTH) * (ATT_HEAD_DIM ** -0.5)
    wb = jnp.concatenate([q_scaled] + interleave(IN_KV + 2 * KV_WIDTH, IN_KV + 3 * KV_WIDTH)
                         + interleave(IN_KV + 4 * KV_WIDTH, IN_KV + 5 * KV_WIDTH), axis=1)
    return wf.astype(BF16), wb.astype(BF16)


def _layer(x, c, w_ada, b_ada, g_pre, g_post, w_in, conv_w, conv_b, dt_bias, a_log, d_skip,
           g_ssm_norm, w_ssm_out, cmp_pos_k, cmp_w1_k, cmp_w2_k, cmp_pos_v, cmp_w1_v, cmp_w2_v,
           w_nsa_out, w_out):
    bsz, s, dm = x.shape
    mod = _ada(c, w_ada, b_ada)
    shift = mod[:, None, :dm]
    scale = mod[:, None, dm:2 * dm]
    gate = mod[:, None, 2 * dm:]
    wf, wb = _split_w_in(w_in)
    pf = _proj(x, shift, scale, g_pre, wf, TN_F, F32, "proj_f32")
    pb = _proj(x, shift, scale, g_pre, wb, NB, BF16, "proj_bf16")

    yssm = _ssm(pf, conv_w, conv_b, dt_bias, a_log, d_skip, g_ssm_norm)

    kvc = pf[:, :, F_KVC:F_KVC + 2 * KV_WIDTH]
    kvc = kvc.reshape(bsz, s // CMP_STRIDE, CMP_STRIDE, 2 * ATT_KV_GROUPS, ATT_HEAD_DIM)
    kvc = kvc.transpose(0, 3, 1, 2, 4).reshape(bsz, 2 * ATT_KV_GROUPS, s // CMP_STRIDE,
                                               CMP_STRIDE * ATT_HEAD_DIM)
    pos = jnp.stack([cmp_pos_k.reshape(1, -1), cmp_pos_v.reshape(1, -1)])
    w1 = jnp.stack([cmp_w1_k, cmp_w1_v]).astype(BF16)
    w2 = jnp.stack([cmp_w2_k, cmp_w2_v]).astype(BF16)
    cmp_kv = _compress(kvc, pos, w1, w2)

    ocmp, sel = _cmpattn(pb, cmp_kv, s)
    onsa = _attn(pb, pf, sel, ocmp)
    return _merge(yssm, onsa, pf, x, gate, g_post, w_ssm_out.astype(BF16), w_nsa_out.astype(BF16),
                  w_out.astype(BF16))


@jax.jit
def kernel(x, c, w_ada, b_ada, g_pre, g_post, w_in, conv_w, conv_b, dt_bias, a_log, d_skip, g_ssm_norm,
           w_ssm_out, cmp_pos_k, cmp_w1_k, cmp_w2_k, cmp_pos_v, cmp_w1_v, cmp_w2_v, w_nsa_out, w_out):
    for layer in range(w_in.shape[0]):
        x = _layer(x, c, w_ada[layer], b_ada[layer], g_pre[layer], g_post[layer], w_in[layer],
                   conv_w[layer], conv_b[layer], dt_bias[layer], a_log[layer], d_skip[layer],
                   g_ssm_norm[layer], w_ssm_out[layer], cmp_pos_k[layer], cmp_w1_k[layer],
                   cmp_w2_k[layer], cmp_pos_v[layer], cmp_w1_v[layer], cmp_w2_v[layer],
                   w_nsa_out[layer], w_out[layer])
    return x
```

```python
import numpy as np
import jax
import jax.numpy as jnp
from jax import lax
from jax.experimental import pallas as pl
from jax.experimental.pallas import tpu as pltpu

D_MODEL = 1024
D_INNER = 2048
SSM_HEAD_DIM = 64
SSM_HEADS = 32
SSM_GROUPS = 4
SSM_STATE = 128
SSM_CONV = 4
SSM_CHUNK = 128
D_CONV = D_INNER + 2 * SSM_GROUPS * SSM_STATE

ATT_HEADS = 16
ATT_HEAD_DIM = 64
ATT_KV_GROUPS = 4
ATT_HPG = 4
ATT_WIDTH = 1024
KV_WIDTH = 256
CMP_BLOCK = 32
CMP_STRIDE = 16
CMP_HIDDEN = 256
SLC_BLOCK = 64
SLC_SHIFT = 6
SLC_TOPK = 16
WINDOW = 512
FORCE_BONUS = 1000.0
EPS = 1e-6
NEG = -1e30
BIG = 2.0 ** 100

F32 = jnp.float32
BF16 = jnp.bfloat16
HIGHEST = lax.Precision.HIGHEST
NT = (((1,), (1,)), ((), ()))

_SIZES = (D_INNER, D_CONV, SSM_HEADS, ATT_WIDTH, 6 * KV_WIDTH, 3 * ATT_HEADS, ATT_WIDTH, 2 * D_MODEL)
_OFFS = tuple(int(v) for v in np.cumsum((0,) + _SIZES))
IN_Z, IN_XBC, IN_DT, IN_Q, IN_KV, IN_GATE, IN_ZATT, IN_MERGE = _OFFS[:8]

F_XBC, F_ZATT, F_MERGE, F_ZSSM, F_KVC, F_SMALL = 0, 3072, 4096, 6144, 8192, 8704
NF = 8832
SMALL_GATE = 32
B_Q, B_KSEL, B_KWIN = 0, 1024, 1536
NB = 2048

TM_PROJ = 512
TN_F = 2944
TQ = 256
TK = 256
VMEM_LIMIT = 48 * 1024 * 1024

L_POS = 64
L_TILE = 67
L_MASK = 72
SLOPE_STRIDE = 8


def _cp(sem):
    return pltpu.CompilerParams(dimension_semantics=sem, vmem_limit_bytes=VMEM_LIMIT)


def _silu(v):
    return v * jax.nn.sigmoid(v)


def _ada_kernel(c_ref, w_ref, b_ref, o_ref):
    o_ref[...] = jnp.dot(c_ref[...], w_ref[...], preferred_element_type=F32,
                         precision=HIGHEST) + b_ref[...]


def _ada(c, w_ada, b_ada):
    bsz = c.shape[0]
    return pl.pallas_call(
        _ada_kernel,
        grid=(3,),
        in_specs=[pl.BlockSpec((bsz, D_MODEL), lambda j: (0, 0)),
                  pl.BlockSpec((D_MODEL, D_MODEL), lambda j: (0, j)),
                  pl.BlockSpec((1, D_MODEL), lambda j: (0, j))],
        out_specs=pl.BlockSpec((bsz, D_MODEL), lambda j: (0, j)),
        out_shape=jax.ShapeDtypeStruct((bsz, 3 * D_MODEL), F32),
        compiler_params=_cp(("arbitrary",)),
        name="ada",
    )(c, w_ada, b_ada.reshape(1, 3 * D_MODEL))


def _proj_kernel(x_ref, shift_ref, scale_ref, g_ref, w_ref, o_ref, h_scr):
    @pl.when(pl.program_id(2) == 0)
    def _():
        xf = x_ref[0]
        y = xf * lax.rsqrt(jnp.mean(xf * xf, axis=-1, keepdims=True) + EPS) * g_ref[...]
        h_scr[...] = (y * (1.0 + scale_ref[0]) + shift_ref[0]).astype(BF16)

    o_ref[0] = jnp.dot(h_scr[...], w_ref[...], preferred_element_type=F32).astype(o_ref.dtype)


def _proj(x, shift, scale, g_pre, w, tn, out_dtype, name):
    bsz, s, _ = x.shape
    n = w.shape[1]
    return pl.pallas_call(
        _proj_kernel,
        grid=(bsz, s // TM_PROJ, n // tn),
        in_specs=[pl.BlockSpec((1, TM_PROJ, D_MODEL), lambda b, i, j: (b, i, 0)),
                  pl.BlockSpec((1, 1, D_MODEL), lambda b, i, j: (b, 0, 0)),
                  pl.BlockSpec((1, 1, D_MODEL), lambda b, i, j: (b, 0, 0)),
                  pl.BlockSpec((1, D_MODEL), lambda b, i, j: (0, 0)),
                  pl.BlockSpec((D_MODEL, tn), lambda b, i, j: (0, j))],
        out_specs=pl.BlockSpec((1, TM_PROJ, tn), lambda b, i, j: (b, i, j)),
        out_shape=jax.ShapeDtypeStruct((bsz, s, n), out_dtype),
        scratch_shapes=[pltpu.VMEM((TM_PROJ, D_MODEL), BF16)],
        compiler_params=_cp(("arbitrary", "arbitrary", "arbitrary")),
        name=name,
    )(x, shift, scale, g_pre.reshape(1, D_MODEL), w)


def _ssm_kernel(xbc_ref, z_ref, small_ref, convw_ref, convb_ref, dtb_ref, alog_ref,
                dskip_ref, gn_ref, o_ref, xpad_scr, state_scr, y_scr):
    cl = SSM_CHUNK

    @pl.when(pl.program_id(1) == 0)
    def _():
        xpad_scr[0:8, :] = jnp.zeros((8, D_CONV), F32)
        state_scr[...] = jnp.zeros(state_scr.shape, F32)

    cur = xbc_ref[0]
    xpad_scr[8:8 + cl, :] = cur
    acc = jnp.broadcast_to(convb_ref[...], (cl, D_CONV))
    for k in range(SSM_CONV):
        lo = 8 - (SSM_CONV - 1) + k
        acc = acc + convw_ref[k:k + 1, :] * xpad_scr[lo:lo + cl, :]
    xpad_scr[0:8, :] = cur[cl - 8:cl, :]
    u = _silu(acc)
    xs = u[:, :D_INNER]
    bm = u[:, D_INNER:D_INNER + SSM_GROUPS * SSM_STATE]
    cm = u[:, D_INNER + SSM_GROUPS * SSM_STATE:]

    pre = small_ref[0] + dtb_ref[...]
    dt = jnp.maximum(pre, 0.0) + jnp.log1p(jnp.exp(-jnp.abs(pre)))
    a = -jnp.exp(alog_ref[...])
    adt = dt * a
    row = lax.broadcasted_iota(jnp.int32, (cl, cl), 0)
    col = lax.broadcasted_iota(jnp.int32, (cl, cl), 1)
    causal = row >= col
    tri = causal.astype(F32)
    a_cs = jnp.dot(tri, adt, preferred_element_type=F32, precision=HIGHEST)
    a_cs_t = a_cs.T
    dt_t = dt.T
    a_last = a_cs[cl - 1:cl, :]
    ea = jnp.exp(a_cs)
    dsc = jnp.exp(a_last - a_cs) * dt
    cdec = jnp.exp(a_last)
    lane = lax.broadcasted_iota(jnp.int32, (cl, 2 * SSM_HEAD_DIM), 1)
    first_half = lane < SSM_HEAD_DIM

    hg = SSM_HEADS // SSM_GROUPS
    for g in range(SSM_GROUPS):
        bg = bm[:, g * SSM_STATE:(g + 1) * SSM_STATE]
        cg = cm[:, g * SSM_STATE:(g + 1) * SSM_STATE]
        cb = lax.dot_general(cg.astype(BF16), bg.astype(BF16), NT, preferred_element_type=F32)
        for pp in range(hg // 2):
            pair = g * (hg // 2) + pp
            xs_pair = xs[:, pair * 128:(pair + 1) * 128]
            st = state_scr[pair]
            rhs = jnp.concatenate([xs_pair, st], axis=0).astype(BF16)
            lhs_list, bs_list = [], []
            for e in range(2):
                h = 2 * pair + e
                seg = a_cs[:, h:h + 1] - a_cs_t[h:h + 1, :]
                lmat = jnp.exp(jnp.where(causal, seg, NEG))
                gmat = cb * lmat * dt_t[h:h + 1, :]
                lhs_list.append(jnp.concatenate([gmat, cg * ea[:, h:h + 1]], axis=1))
                bs_list.append(bg * dsc[:, h:h + 1])
            lhs = jnp.concatenate(lhs_list, axis=0).astype(BF16)
            yy = jnp.dot(lhs, rhs, preferred_element_type=F32)
            y_scr[:, pair * 128:(pair + 1) * 128] = jnp.where(first_half, yy[:cl], yy[cl:])
            bs = jnp.concatenate(bs_list, axis=1).astype(BF16)
            new = lax.dot_general(bs, xs_pair.astype(BF16), (((0,), (0,)), ((), ())),
                                  preferred_element_type=F32)
            dec = jnp.where(first_half[0:1], cdec[:, 2 * pair:2 * pair + 1],
                            cdec[:, 2 * pair + 1:2 * pair + 2])
            state_scr[pair] = st * dec + jnp.where(first_half, new[:SSM_STATE], new[SSM_STATE:])

    y = y_scr[...] + xs * dskip_ref[...]
    y = y * _silu(z_ref[0])
    gsz = D_INNER // SSM_GROUPS
    for g in range(SSM_GROUPS):
        yg = y[:, g * gsz:(g + 1) * gsz]
        yn = yg * lax.rsqrt(jnp.mean(yg * yg, axis=-1, keepdims=True) + EPS)
        o_ref[0, :, g * gsz:(g + 1) * gsz] = (yn * gn_ref[:, g * gsz:(g + 1) * gsz]).astype(o_ref.dtype)


def _ssm(pf, conv_w, conv_b, dt_bias, a_log, d_skip, g_norm):
    bsz, s, _ = pf.shape
    cl = SSM_CHUNK
    pad = 128 - SSM_HEADS
    dtb = jnp.pad(dt_bias, (0, pad)).reshape(1, 128)
    alog = jnp.pad(a_log, (0, pad)).reshape(1, 128)
    dskip = jnp.repeat(d_skip, SSM_HEAD_DIM).reshape(1, D_INNER)
    const = lambda shape: pl.BlockSpec(shape, lambda b, c: (0, 0))
    return pl.pallas_call(
        _ssm_kernel,
        grid=(bsz, s // cl),
        in_specs=[pl.BlockSpec((1, cl, D_CONV), lambda b, c: (b, c, F_XBC // D_CONV)),
                  pl.BlockSpec((1, cl, D_INNER), lambda b, c: (b, c, F_ZSSM // D_INNER)),
                  pl.BlockSpec((1, cl, 128), lambda b, c: (b, c, F_SMALL // 128)),
                  const((SSM_CONV, D_CONV)), const((1, D_CONV)), const((1, 128)), const((1, 128)),
                  const((1, D_INNER)), const((1, D_INNER))],
        out_specs=pl.BlockSpec((1, cl, D_INNER), lambda b, c: (b, c, 0)),
        out_shape=jax.ShapeDtypeStruct((bsz, s, D_INNER), BF16),
        scratch_shapes=[pltpu.VMEM((8 + cl, D_CONV), F32),
                        pltpu.VMEM((SSM_HEADS // 2, SSM_STATE, 2 * SSM_HEAD_DIM), F32),
                        pltpu.VMEM((cl, D_INNER), F32)],
        compiler_params=_cp(("arbitrary", "arbitrary")),
        name="ssm",
    )(pf, pf, pf, conv_w, conv_b.reshape(1, D_CONV), dtb, alog, dskip, g_norm.reshape(1, D_INNER))


def _cmp_kernel(r_ref, pos_ref, w1_ref, w2_ref, o_ref):
    half = CMP_STRIDE * ATT_HEAD_DIM
    r = r_ref[0, 0].astype(BF16)
    w1 = w1_ref[0]
    first = jnp.dot(r, w1[:half], preferred_element_type=F32)
    second = jnp.dot(r, w1[half:], preferred_element_type=F32)
    posb = jnp.broadcast_to(pos_ref[0], (8, 2 * half)).astype(BF16)
    cpos = jnp.dot(posb, w1, preferred_element_type=F32)[0:1]
    nrow = r.shape[0]
    hid = first + pltpu.roll(second, nrow - 1, 0) + cpos
    out = jnp.dot(_silu(hid).astype(BF16), w2_ref[0], preferred_element_type=F32)
    o_ref[0, 0] = out


def _compress(kvc_r, pos, w1, w2):
    bsz, _, nrow, _ = kvc_r.shape
    g = ATT_KV_GROUPS
    return pl.pallas_call(
        _cmp_kernel,
        grid=(bsz, 2 * g),
        in_specs=[pl.BlockSpec((1, 1, nrow, 1024), lambda b, j: (b, j, 0, 0)),
                  pl.BlockSpec((1, 1, 2048), lambda b, j: (j // g, 0, 0)),
                  pl.BlockSpec((1, 2048, CMP_HIDDEN), lambda b, j: (j // g, 0, 0)),
                  pl.BlockSpec((1, CMP_HIDDEN, ATT_HEAD_DIM), lambda b, j: (j // g, 0, 0))],
        out_specs=pl.BlockSpec((1, 1, nrow, ATT_HEAD_DIM), lambda b, j: (b, j, 0, 0)),
        out_shape=jax.ShapeDtypeStruct((bsz, 2 * g, nrow, ATT_HEAD_DIM), F32),
        compiler_params=_cp(("arbitrary", "arbitrary")),
        name="cmp",
    )(kvc_r, pos, w1, w2)


def _bf16_round_np(x):
    u = np.asarray(x, np.float32).view(np.uint32)
    u = (u + (((u >> 16) & 1) + 0x7FFF)) & np.uint32(0xFFFF0000)
    return u.view(np.float32)


def _slope_table():
    slope = (2.0 ** (-8.0 * np.arange(1, ATT_HEADS + 1) / ATT_HEADS)).astype(np.float32)
    p0 = _bf16_round_np(slope)
    p1 = _bf16_round_np(slope - p0)
    p2 = _bf16_round_np(slope - p0 - p1)
    tab = np.zeros((ATT_HEADS, SLOPE_STRIDE), np.float32)
    tab[:, 0], tab[:, 1], tab[:, 2] = p0, p1, p2
    tab[:, 3:6] = tab[:, 0:3] * TK
    tab[:, 6] = slope
    return jnp.asarray(tab.reshape(-1))


def _key_aux_table(s):
    pos = np.arange(s)
    tab = np.zeros((s, 128), np.float32)
    tab[:, L_POS:L_POS + 3] = (pos % TK)[:, None]
    tab[:, L_TILE:L_TILE + 3] = (pos // TK)[:, None]
    tab[pos, L_MASK + pos // SLC_BLOCK] = 1.0
    return jnp.asarray(tab, dtype=BF16)


def _bias_table():
    r = np.arange(TQ)[:, None]
    c = np.arange(TK)[None, :]
    tabs = []
    for jj in range(3):
        d = (2 - jj) * TK + r - c
        tabs.append(np.where((d >= 0) & (d < WINDOW), 0.0, NEG))
    tabs.append(np.where(r - c >= 0, 0.0, NEG))
    return jnp.asarray(np.stack(tabs), dtype=F32)


def _gate_expand_table():
    tab = np.zeros((ATT_KV_GROUPS, 128, 3 * KV_WIDTH), np.float32)
    for g in range(ATT_KV_GROUPS):
        for h in range(ATT_HPG):
            for j in range(3):
                row = SMALL_GATE + g * 3 * ATT_HPG + 3 * h + j
                tab[g, row, j * KV_WIDTH + h * ATT_HEAD_DIM:j * KV_WIDTH + (h + 1) * ATT_HEAD_DIM] = 1.0
    return jnp.asarray(tab)


def _cmp_to_slc_matrix_t(n_cmp_pad, n_cmp, n_slc):
    cs = np.arange(n_cmp_pad) * CMP_STRIDE
    ss = np.arange(n_slc) * SLC_BLOCK
    lo = np.maximum(cs[:, None], ss[None, :])
    hi = np.minimum(cs[:, None] + CMP_BLOCK, ss[None, :] + SLC_BLOCK)
    m = np.clip(hi - lo, 0, None) / CMP_BLOCK
    m[n_cmp:] = 0.0
    return jnp.asarray(m.T, dtype=F32)


def _cmpattn_kernel(slopes_ref, q_ref, ck_ref, cv_ref, mt_ref, ocmp_ref, qmask_ref,
                    khi_scr, klo_scr, vbd_scr):
    g = pl.program_id(1)
    i = pl.program_id(2)
    t0 = i * TQ
    npad = ck_ref.shape[2]
    n_slc = mt_ref.shape[0]

    @pl.when(i == 0)
    def _():
        kc = ck_ref[0, 0]
        vc = cv_ref[0, 0]
        k_hi = kc.astype(BF16).astype(F32)
        k_lo = kc - k_hi
        zero = jnp.zeros_like(kc)
        for h in range(ATT_HPG):
            def bd(a):
                return jnp.concatenate([a if hh == h else zero for hh in range(ATT_HPG)], axis=1).astype(BF16)
            khi_scr[h * npad:(h + 1) * npad, :] = bd(k_hi)
            klo_scr[h * npad:(h + 1) * npad, :] = bd(k_lo)
            vbd_scr[h * npad:(h + 1) * npad, :] = bd(vc)

    q = q_ref[0]
    s = (lax.dot_general(q, khi_scr[...], NT, preferred_element_type=F32)
         + lax.dot_general(q, klo_scr[...], NT, preferred_element_type=F32))
    jcol = lax.broadcasted_iota(jnp.int32, (1, npad), 1)
    cmp_end = jcol * CMP_STRIDE + (CMP_BLOCK - 1)
    trow = t0 + lax.broadcasted_iota(jnp.int32, (TQ, npad), 0)
    valid = (cmp_end <= trow) & (jcol < npad - 1)
    rel = (cmp_end - t0).astype(F32)
    psum = jnp.zeros((TQ, npad), F32)
    ps = []
    for h in range(ATT_HPG):
        slope = slopes_ref[(g * ATT_HPG + h) * SLOPE_STRIDE + 6]
        sh = jnp.where(valid, s[:, h * npad:(h + 1) * npad] + slope * rel, NEG)
        mx = jnp.max(sh, axis=-1, keepdims=True)
        p = jnp.where(valid, jnp.exp(sh - mx), 0.0)
        l = jnp.sum(p, axis=-1, keepdims=True)
        p = p * jnp.where(l > 0.0, 1.0 / jnp.where(l > 0.0, l, 1.0), 0.0)
        ps.append(p)
        psum = psum + p
    ocmp_ref[0] = jnp.dot(jnp.concatenate(ps, axis=1).astype(BF16), vbd_scr[...],
                          preferred_element_type=F32)
    imp_t = lax.dot_general(mt_ref[...], psum, NT, preferred_element_type=F32,
                            precision=HIGHEST)

    nrb = n_slc // 8
    blk_t = (t0 + lax.broadcasted_iota(jnp.int32, (8, TQ), 1)) >> SLC_SHIFT
    sub = lax.broadcasted_iota(jnp.int32, (8, TQ), 0)
    score = []
    for rb in range(nrb):
        kk = sub + rb * 8
        imp = imp_t[rb * 8:(rb + 1) * 8]
        forced = (kk == 0) | (kk == blk_t) | (kk == blk_t - 1)
        score.append(jnp.where(forced, imp + FORCE_BONUS, jnp.where(kk <= blk_t, imp, -1.0)))
    rank = [jnp.zeros((8, TQ), F32) for _ in range(nrb)]
    for j in range(n_slc):
        cj = jnp.broadcast_to(score[j // 8][j % 8:j % 8 + 1, :], (8, TQ))
        for rb in range(nrb):
            ge = jnp.where(cj >= score[rb], 1.0, 0.0)
            gt = jnp.where(cj > score[rb], 1.0, 0.0)
            if rb * 8 > j:
                beats = ge
            elif rb * 8 + 7 <= j:
                beats = gt
            else:
                beats = jnp.where(sub > j - rb * 8, ge, gt)
            rank[rb] = rank[rb] + beats
    rows = [jnp.zeros((L_MASK, TQ), F32)]
    rows += [jnp.where(rank[rb] < float(SLC_TOPK), 0.0, -BIG) for rb in range(nrb)]
    rows += [jnp.zeros((128 - L_MASK - n_slc, TQ), F32)]
    qmask_ref[0, 0] = jnp.concatenate(rows, axis=0).T.astype(BF16)


def _cmpattn(pb, cmp_kv, s):
    bsz = pb.shape[0]
    g = ATT_KV_GROUPS
    nrow = cmp_kv.shape[2]
    n_cmp = (s - CMP_BLOCK) // CMP_STRIDE + 1
    n_slc = s // SLC_BLOCK
    mt = _cmp_to_slc_matrix_t(nrow, n_cmp, n_slc)
    return pl.pallas_call(
        _cmpattn_kernel,
        grid=(bsz, g, s // TQ),
        in_specs=[pl.BlockSpec(memory_space=pltpu.SMEM),
                  pl.BlockSpec((1, TQ, KV_WIDTH), lambda b, gg, i: (b, i, B_Q // KV_WIDTH + gg)),
                  pl.BlockSpec((1, 1, nrow, ATT_HEAD_DIM), lambda b, gg, i: (b, gg, 0, 0)),
                  pl.BlockSpec((1, 1, nrow, ATT_HEAD_DIM), lambda b, gg, i: (b, g + gg, 0, 0)),
                  pl.BlockSpec((n_slc, nrow), lambda b, gg, i: (0, 0))],
        out_specs=[pl.BlockSpec((1, TQ, KV_WIDTH), lambda b, gg, i: (b, i, gg)),
                   pl.BlockSpec((1, 1, TQ, 128), lambda b, gg, i: (b, gg, i, 0))],
        out_shape=[jax.ShapeDtypeStruct((bsz, s, ATT_WIDTH), F32),
                   jax.ShapeDtypeStruct((bsz, g, s, 128), BF16)],
        scratch_shapes=[pltpu.VMEM((ATT_HPG * nrow, KV_WIDTH), BF16),
                        pltpu.VMEM((ATT_HPG * nrow, KV_WIDTH), BF16),
                        pltpu.VMEM((ATT_HPG * nrow, KV_WIDTH), BF16)],
        compiler_params=_cp(("arbitrary", "arbitrary", "arbitrary")),
        name="cmpattn",
    )(_slope_table(), pb, cmp_kv, cmp_kv, mt)


def _flash_step(qa_ref, kv_ref, aux_ref, m_scr, acc_scr, j, bias):
    r = qa_ref.shape[0]
    start = pl.multiple_of(j * TK, TK)
    kv = kv_ref[0, pl.ds(start, TK), :]
    left = jnp.where(lax.broadcasted_iota(jnp.int32, (TK, 128), 1) < ATT_HEAD_DIM, 1.0, 0.0).astype(BF16)
    k_aug = kv * left + aux_ref[pl.ds(start, TK), :]
    v_aug = kv * (1.0 - left) + left
    s = lax.dot_general(qa_ref[...], k_aug, NT, preferred_element_type=F32)
    if bias is not None:
        s = (s.reshape(ATT_HPG, TQ, TK) + bias[None]).reshape(r, TK)
    m_prev = m_scr[...]
    m_new = jnp.maximum(m_prev, jnp.max(s, axis=-1, keepdims=True))
    alpha = jnp.exp(m_prev - m_new)
    p = jnp.exp(s - jnp.concatenate([m_new] * (TK // 128), axis=1))
    acc_scr[...] = alpha * acc_scr[...] + jnp.dot(p.astype(BF16), v_aug, preferred_element_type=F32)
    m_scr[...] = m_new


def _flash_out(acc_scr):
    left = lax.broadcasted_iota(jnp.int32, (TQ, 128), 1) < ATT_HEAD_DIM
    slabs = []
    for pair in range(ATT_HPG // 2):
        a0 = acc_scr[(2 * pair) * TQ:(2 * pair + 1) * TQ, :]
        a1 = acc_scr[(2 * pair + 1) * TQ:(2 * pair + 2) * TQ, :]
        r0 = pltpu.roll(a0, ATT_HEAD_DIM, 1)
        r1 = pltpu.roll(a1, ATT_HEAD_DIM, 1)
        slabs.append(jnp.where(left, r0, a1) / jnp.where(left, a0, r1))
    return jnp.concatenate(slabs, axis=1)


def _attn_kernel(slopes_ref, q_ref, ksel_ref, kwin_ref, aux_ref, qmask_ref, ocmp_ref, small_ref,
                 zatt_ref, pick_ref, bias_ref, o_ref, qas_scr, qaw_scr, m_scr, acc_scr):
    g = pl.program_id(1)
    i = pl.program_id(2)

    qf = q_ref[0].astype(F32)
    qm = qmask_ref[0, 0].astype(F32)
    lane_row = lax.broadcasted_iota(jnp.int32, (1, 128), 1)
    left = lax.broadcasted_iota(jnp.int32, (TQ, 128), 1) < ATT_HEAD_DIM
    for h in range(ATT_HPG):
        slab = qf[:, 128 * (h // 2):128 * (h // 2) + 128]
        if h % 2:
            slab = pltpu.roll(slab, ATT_HEAD_DIM, 1)
        ext = jnp.zeros((1, 128), F32)
        for c in range(6):
            ext = jnp.where(lane_row == L_POS + c, slopes_ref[(g * ATT_HPG + h) * SLOPE_STRIDE + c], ext)
        base = jnp.where(left, slab, ext)
        qaw_scr[h * TQ:(h + 1) * TQ, :] = base.astype(BF16)
        qas_scr[h * TQ:(h + 1) * TQ, :] = (base + qm).astype(BF16)

    def reset():
        m_scr[...] = jnp.full(m_scr.shape, NEG, F32)
        acc_scr[...] = jnp.zeros(acc_scr.shape, F32)

    reset()

    def sel_body(j, carry):
        _flash_step(qas_scr, ksel_ref, aux_ref, m_scr, acc_scr, j, None)
        return carry

    lax.fori_loop(0, i, sel_body, 0)
    _flash_step(qas_scr, ksel_ref, aux_ref, m_scr, acc_scr, i, bias_ref[3])
    o_slc = _flash_out(acc_scr)

    reset()
    for jj in range(2):
        @pl.when(i - 2 + jj >= 0)
        def _():
            _flash_step(qaw_scr, kwin_ref, aux_ref, m_scr, acc_scr, i - 2 + jj, bias_ref[jj])
    _flash_step(qaw_scr, kwin_ref, aux_ref, m_scr, acc_scr, i, bias_ref[2])
    o_win = _flash_out(acc_scr)

    gts = jax.nn.sigmoid(jnp.dot(small_ref[0], pick_ref[0], preferred_element_type=F32,
                                 precision=HIGHEST))
    o = (gts[:, :KV_WIDTH] * ocmp_ref[0] + gts[:, KV_WIDTH:2 * KV_WIDTH] * o_slc
         + gts[:, 2 * KV_WIDTH:] * o_win)
    o_ref[0] = (o * _silu(zatt_ref[0])).astype(o_ref.dtype)


def _attn(pb, pf, qmask, ocmp):
    bsz, s, _ = pb.shape
    g = ATT_KV_GROUPS
    r = ATT_HPG * TQ
    return pl.pallas_call(
        _attn_kernel,
        grid=(bsz, g, s // TQ),
        in_specs=[pl.BlockSpec(memory_space=pltpu.SMEM),
                  pl.BlockSpec((1, TQ, KV_WIDTH), lambda b, gg, i: (b, i, B_Q // KV_WIDTH + gg)),
                  pl.BlockSpec((1, s, 128), lambda b, gg, i: (b, 0, B_KSEL // 128 + gg)),
                  pl.BlockSpec((1, s, 128), lambda b, gg, i: (b, 0, B_KWIN // 128 + gg)),
                  pl.BlockSpec((s, 128), lambda b, gg, i: (0, 0)),
                  pl.BlockSpec((1, 1, TQ, 128), lambda b, gg, i: (b, gg, i, 0)),
                  pl.BlockSpec((1, TQ, KV_WIDTH), lambda b, gg, i: (b, i, gg)),
                  pl.BlockSpec((1, TQ, 128), lambda b, gg, i: (b, i, F_SMALL // 128)),
                  pl.BlockSpec((1, TQ, KV_WIDTH), lambda b, gg, i: (b, i, F_ZATT // KV_WIDTH + gg)),
                  pl.BlockSpec((1, 128, 3 * KV_WIDTH), lambda b, gg, i: (gg, 0, 0)),
                  pl.BlockSpec((4, TQ, TK), lambda b, gg, i: (0, 0, 0))],
        out_specs=pl.BlockSpec((1, TQ, KV_WIDTH), lambda b, gg, i: (b, i, gg)),
        out_shape=jax.ShapeDtypeStruct((bsz, s, ATT_WIDTH), BF16),
        scratch_shapes=[pltpu.VMEM((r, 128), BF16), pltpu.VMEM((r, 128), BF16),
                        pltpu.VMEM((r, 128), F32), pltpu.VMEM((r, 128), F32)],
        compiler_params=_cp(("arbitrary", "arbitrary", "arbitrary")),
        name="attn",
    )(_slope_table(), pb, pb, pb, _key_aux_table(s), qmask, ocmp, pf, pf, _gate_expand_table(),
      _bias_table())


def _merge_kernel(yssm_ref, onsa_ref, mg_ref, x_ref, gate_ref, gpost_ref, wssm_ref, wnsa_ref, wout_ref, o_ref):
    y_ssm = jnp.dot(yssm_ref[0], wssm_ref[...], preferred_element_type=F32)
    y_nsa = jnp.dot(onsa_ref[0], wnsa_ref[...], preferred_element_type=F32)
    mg = jax.nn.sigmoid(mg_ref[0])
    merged = mg[:, :D_MODEL] * y_ssm + mg[:, D_MODEL:] * y_nsa
    out = jnp.dot(merged.astype(BF16), wout_ref[...], preferred_element_type=F32)
    yn = out * lax.rsqrt(jnp.mean(out * out, axis=-1, keepdims=True) + EPS) * gpost_ref[...]
    o_ref[0] = x_ref[0] + gate_ref[0] * yn


def _merge(yssm, onsa, pf, x, gate, g_post, w_ssm_out, w_nsa_out, w_out):
    bsz, s, _ = x.shape
    tm = 512
    const = lambda shape: pl.BlockSpec(shape, lambda b, i: (0, 0))
    return pl.pallas_call(
        _merge_kernel,
        grid=(bsz, s // tm),
        in_specs=[pl.BlockSpec((1, tm, D_INNER), lambda b, i: (b, i, 0)),
                  pl.BlockSpec((1, tm, ATT_WIDTH), lambda b, i: (b, i, 0)),
                  pl.BlockSpec((1, tm, 2 * D_MODEL), lambda b, i: (b, i, F_MERGE // (2 * D_MODEL))),
                  pl.BlockSpec((1, tm, D_MODEL), lambda b, i: (b, i, 0)),
                  pl.BlockSpec((1, 1, D_MODEL), lambda b, i: (b, 0, 0)),
                  const((1, D_MODEL)),
                  const((D_INNER, D_MODEL)), const((ATT_WIDTH, D_MODEL)), const((D_MODEL, D_MODEL))],
        out_specs=pl.BlockSpec((1, tm, D_MODEL), lambda b, i: (b, i, 0)),
        out_shape=jax.ShapeDtypeStruct((bsz, s, D_MODEL), F32),
        compiler_params=_cp(("arbitrary", "arbitrary")),
        name="merge",
    )(yssm, onsa, pf, x, gate, g_post.reshape(1, D_MODEL), w_ssm_out, w_nsa_out, w_out)


def _split_w_in(w_in):
    def cols(off, n):
        return w_in[:, off:off + n]

    dm = w_in.shape[0]
    small = jnp.concatenate([cols(IN_DT, SSM_HEADS), cols(IN_GATE, 3 * ATT_HEADS),
                             jnp.zeros((dm, 128 - SSM_HEADS - 3 * ATT_HEADS), w_in.dtype)], axis=1)
    wf = jnp.concatenate([cols(IN_XBC, D_CONV), cols(IN_ZATT, ATT_WIDTH), cols(IN_MERGE, 2 * D_MODEL),
                          cols(IN_Z, D_INNER), cols(IN_KV, 2 * KV_WIDTH), small], axis=1)
    hd = ATT_HEAD_DIM

    def interleave(k_off, v_off):
        return [cols(o + gg * hd, hd) for gg in range(ATT_KV_GROUPS) for o in (k_off, v_off)]

    q_scaled = cols(IN_Q, ATT_WIDTH) * (ATT_HEAD_DIM ** -0.5)
    wb = jnp.concatenate([q_scaled] + interleave(IN_KV + 2 * KV_WIDTH, IN_KV + 3 * KV_WIDTH)
                         + interleave(IN_KV + 4 * KV_WIDTH, IN_KV + 5 * KV_WIDTH), axis=1)
    return wf.astype(BF16), wb.astype(BF16)


def _layer(x, c, w_ada, b_ada, g_pre, g_post, w_in, conv_w, conv_b, dt_bias, a_log, d_skip,
           g_ssm_norm, w_ssm_out, cmp_pos_k, cmp_w1_k, cmp_w2_k, cmp_pos_v, cmp_w1_v, cmp_w2_v,
           w_nsa_out, w_out):
    bsz, s, dm = x.shape
    mod = _ada(c, w_ada, b_ada)
    shift = mod[:, None, :dm]
    scale = mod[:, None, dm:2 * dm]
    gate = mod[:, None, 2 * dm:]
    wf, wb = _split_w_in(w_in)
    pf = _proj(x, shift, scale, g_pre, wf, TN_F, F32, "proj_f32")
    pb = _proj(x, shift, scale, g_pre, wb, NB, BF16, "proj_bf16")

    yssm = _ssm(pf, conv_w, conv_b, dt_bias, a_log, d_skip, g_ssm_norm)

    kvc = pf[:, :, F_KVC:F_KVC + 2 * KV_WIDTH]
    kvc = kvc.reshape(bsz, s // CMP_STRIDE, CMP_STRIDE, 2 * ATT_KV_GROUPS, ATT_HEAD_DIM)
    kvc = kvc.transpose(0, 3, 1, 2, 4).reshape(bsz, 2 * ATT_KV_GROUPS, s // CMP_STRIDE,
                                               CMP_STRIDE * ATT_HEAD_DIM)
    pos = jnp.stack([cmp_pos_k.reshape(1, -1), cmp_pos_v.reshape(1, -1)])
    w1 = jnp.stack([cmp_w1_k, cmp_w1_v]).astype(BF16)
    w2 = jnp.stack([cmp_w2_k, cmp_w2_v]).astype(BF16)
    cmp_kv = _compress(kvc, pos, w1, w2)

    ocmp, qmask = _cmpattn(pb, cmp_kv, s)
    onsa = _attn(pb, pf, qmask, ocmp)
    return _merge(yssm, onsa, pf, x, gate, g_post, w_ssm_out.astype(BF16), w_nsa_out.astype(BF16),
                  w_out.astype(BF16))


@jax.jit
def kernel(x, c, w_ada, b_ada, g_pre, g_post, w_in, conv_w, conv_b, dt_bias, a_log, d_skip, g_ssm_norm,
           w_ssm_out, cmp_pos_k, cmp_w1_k, cmp_w2_k, cmp_pos_v, cmp_w1_v, cmp_w2_v, w_nsa_out, w_out):
    for layer in range(w_in.shape[0]):
        x = _layer(x, c, w_ada[layer], b_ada[layer], g_pre[layer], g_post[layer], w_in[layer],
                   conv_w[layer], conv_b[layer], dt_bias[layer], a_log[layer], d_skip[layer],
                   g_ssm_norm[layer], w_ssm_out[layer], cmp_pos_k[layer], cmp_w1_k[layer],
                   cmp_w2_k[layer], cmp_pos_v[layer], cmp_w1_v[layer], cmp_w2_v[layer],
                   w_nsa_out[layer], w_out[layer])
    return x
```

```python
import numpy as np
import jax
import jax.numpy as jnp
from jax import lax
from jax.experimental import pallas as pl
from jax.experimental.pallas import tpu as pltpu

D_MODEL = 1024
D_INNER = 2048
SSM_HEAD_DIM = 64
SSM_HEADS = 32
SSM_GROUPS = 4
SSM_STATE = 128
SSM_CONV = 4
SSM_CHUNK = 128
D_CONV = D_INNER + 2 * SSM_GROUPS * SSM_STATE

ATT_HEADS = 16
ATT_HEAD_DIM = 64
ATT_KV_GROUPS = 4
ATT_HPG = 4
ATT_WIDTH = 1024
KV_WIDTH = 256
CMP_BLOCK = 32
CMP_STRIDE = 16
CMP_HIDDEN = 256
SLC_BLOCK = 64
SLC_SHIFT = 6
SLC_TOPK = 16
WINDOW = 512
FORCE_BONUS = 1000.0
EPS = 1e-6
NEG = -1e30
BIG = 2.0 ** 100

F32 = jnp.float32
BF16 = jnp.bfloat16
HIGHEST = lax.Precision.HIGHEST
NT = (((1,), (1,)), ((), ()))

_SIZES = (D_INNER, D_CONV, SSM_HEADS, ATT_WIDTH, 6 * KV_WIDTH, 3 * ATT_HEADS, ATT_WIDTH, 2 * D_MODEL)
_OFFS = tuple(int(v) for v in np.cumsum((0,) + _SIZES))
IN_Z, IN_XBC, IN_DT, IN_Q, IN_KV, IN_GATE, IN_ZATT, IN_MERGE = _OFFS[:8]

F_XBC, F_ZATT, F_MERGE, F_ZSSM, F_SMALL = 0, 3072, 4096, 6144, 8192
NF = 8320
SMALL_GATE = 32
B_Q, B_KSEL, B_KWIN, B_KVC = 0, 1024, 1536, 2048
NB = 2560

TM_PROJ = 512
TN_F = 1664
TQ = 256
TK = 256
TQ_CMP = 512
VMEM_LIMIT = 48 * 1024 * 1024

L_POS = 64
L_TILE = 67
L_MASK = 72
SLOPE_STRIDE = 8


def _cp(sem):
    return pltpu.CompilerParams(dimension_semantics=sem, vmem_limit_bytes=VMEM_LIMIT)


def _sigmoid(v):
    return 0.5 * jnp.tanh(0.5 * v) + 0.5


def _silu(v):
    return v * _sigmoid(v)


def _ada_kernel(c_ref, w_ref, b_ref, o_ref):
    o_ref[...] = jnp.dot(c_ref[...], w_ref[...], preferred_element_type=F32,
                         precision=HIGHEST) + b_ref[...]


def _ada(c, w_ada, b_ada):
    bsz = c.shape[0]
    return pl.pallas_call(
        _ada_kernel,
        grid=(3,),
        in_specs=[pl.BlockSpec((bsz, D_MODEL), lambda j: (0, 0)),
                  pl.BlockSpec((D_MODEL, D_MODEL), lambda j: (0, j)),
                  pl.BlockSpec((1, D_MODEL), lambda j: (0, j))],
        out_specs=pl.BlockSpec((bsz, D_MODEL), lambda j: (0, j)),
        out_shape=jax.ShapeDtypeStruct((bsz, 3 * D_MODEL), F32),
        compiler_params=_cp(("arbitrary",)),
        name="ada",
    )(c, w_ada, b_ada.reshape(1, 3 * D_MODEL))


def _proj_kernel(x_ref, shift_ref, scale_ref, g_ref, w_ref, o_ref, h_scr):
    @pl.when(pl.program_id(2) == 0)
    def _():
        xf = x_ref[0]
        y = xf * lax.rsqrt(jnp.mean(xf * xf, axis=-1, keepdims=True) + EPS) * g_ref[...]
        h_scr[...] = (y * (1.0 + scale_ref[0]) + shift_ref[0]).astype(BF16)

    o_ref[0] = jnp.dot(h_scr[...], w_ref[...], preferred_element_type=F32).astype(o_ref.dtype)


def _proj(x, shift, scale, g_pre, w, tn, out_dtype, name):
    bsz, s, _ = x.shape
    n = w.shape[1]
    return pl.pallas_call(
        _proj_kernel,
        grid=(bsz, s // TM_PROJ, n // tn),
        in_specs=[pl.BlockSpec((1, TM_PROJ, D_MODEL), lambda b, i, j: (b, i, 0)),
                  pl.BlockSpec((1, 1, D_MODEL), lambda b, i, j: (b, 0, 0)),
                  pl.BlockSpec((1, 1, D_MODEL), lambda b, i, j: (b, 0, 0)),
                  pl.BlockSpec((1, D_MODEL), lambda b, i, j: (0, 0)),
                  pl.BlockSpec((D_MODEL, tn), lambda b, i, j: (0, j))],
        out_specs=pl.BlockSpec((1, TM_PROJ, tn), lambda b, i, j: (b, i, j)),
        out_shape=jax.ShapeDtypeStruct((bsz, s, n), out_dtype),
        scratch_shapes=[pltpu.VMEM((TM_PROJ, D_MODEL), BF16)],
        compiler_params=_cp(("arbitrary", "arbitrary", "arbitrary")),
        name=name,
    )(x, shift, scale, g_pre.reshape(1, D_MODEL), w)


def _ssm_kernel(xbc_ref, z_ref, small_ref, convw_ref, convb_ref, dtb_ref, alog_ref,
                dskip_ref, gn_ref, o_ref, xpad_scr, state_scr, y_scr):
    cl = SSM_CHUNK

    @pl.when(pl.program_id(1) == 0)
    def _():
        xpad_scr[0:8, :] = jnp.zeros((8, D_CONV), F32)
        state_scr[...] = jnp.zeros(state_scr.shape, F32)

    cur = xbc_ref[0]
    xpad_scr[8:8 + cl, :] = cur
    acc = jnp.broadcast_to(convb_ref[...], (cl, D_CONV))
    for k in range(SSM_CONV):
        lo = 8 - (SSM_CONV - 1) + k
        acc = acc + convw_ref[k:k + 1, :] * xpad_scr[lo:lo + cl, :]
    xpad_scr[0:8, :] = cur[cl - 8:cl, :]
    u = _silu(acc)
    xs = u[:, :D_INNER]
    bm = u[:, D_INNER:D_INNER + SSM_GROUPS * SSM_STATE]
    cm = u[:, D_INNER + SSM_GROUPS * SSM_STATE:]

    pre = small_ref[0] + dtb_ref[...]
    dt = jnp.maximum(pre, 0.0) + jnp.log1p(jnp.exp(-jnp.abs(pre)))
    a = -jnp.exp(alog_ref[...])
    adt = dt * a
    row = lax.broadcasted_iota(jnp.int32, (cl, cl), 0)
    col = lax.broadcasted_iota(jnp.int32, (cl, cl), 1)
    causal = row >= col
    tri = causal.astype(F32)
    a_cs = jnp.dot(tri, adt, preferred_element_type=F32, precision=HIGHEST)
    a_cs_t = a_cs.T
    dt_t = dt.T
    a_last = a_cs[cl - 1:cl, :]
    ea = jnp.exp(a_cs)
    dsc = jnp.exp(a_last - a_cs) * dt
    cdec = jnp.exp(a_last)
    lane = lax.broadcasted_iota(jnp.int32, (cl, 2 * SSM_HEAD_DIM), 1)
    first_half = lane < SSM_HEAD_DIM

    hg = SSM_HEADS // SSM_GROUPS
    for g in range(SSM_GROUPS):
        bg = bm[:, g * SSM_STATE:(g + 1) * SSM_STATE]
        cg = cm[:, g * SSM_STATE:(g + 1) * SSM_STATE]
        cb = lax.dot_general(cg.astype(BF16), bg.astype(BF16), NT, preferred_element_type=F32)
        for pp in range(hg // 2):
            pair = g * (hg // 2) + pp
            xs_pair = xs[:, pair * 128:(pair + 1) * 128]
            st = state_scr[pair]
            rhs = jnp.concatenate([xs_pair, st], axis=0).astype(BF16)
            lhs_list, bs_list = [], []
            for e in range(2):
                h = 2 * pair + e
                seg = a_cs[:, h:h + 1] - a_cs_t[h:h + 1, :]
                lmat = jnp.exp(jnp.where(causal, seg, NEG))
                gmat = cb * lmat * dt_t[h:h + 1, :]
                lhs_list.append(jnp.concatenate([gmat, cg * ea[:, h:h + 1]], axis=1))
                bs_list.append(bg * dsc[:, h:h + 1])
            lhs = jnp.concatenate(lhs_list, axis=0).astype(BF16)
            yy = jnp.dot(lhs, rhs, preferred_element_type=F32)
            y_scr[:, pair * 128:(pair + 1) * 128] = jnp.where(first_half, yy[:cl], yy[cl:])
            bs = jnp.concatenate(bs_list, axis=1).astype(BF16)
            new = lax.dot_general(bs, xs_pair.astype(BF16), (((0,), (0,)), ((), ())),
                                  preferred_element_type=F32)
            dec = jnp.where(first_half[0:1], cdec[:, 2 * pair:2 * pair + 1],
                            cdec[:, 2 * pair + 1:2 * pair + 2])
            state_scr[pair] = st * dec + jnp.where(first_half, new[:SSM_STATE], new[SSM_STATE:])

    y = y_scr[...] + xs * dskip_ref[...]
    y = y * _silu(z_ref[0])
    gsz = D_INNER // SSM_GROUPS
    for g in range(SSM_GROUPS):
        yg = y[:, g * gsz:(g + 1) * gsz]
        yn = yg * lax.rsqrt(jnp.mean(yg * yg, axis=-1, keepdims=True) + EPS)
        o_ref[0, :, g * gsz:(g + 1) * gsz] = (yn * gn_ref[:, g * gsz:(g + 1) * gsz]).astype(o_ref.dtype)


def _ssm(pf, conv_w, conv_b, dt_bias, a_log, d_skip, g_norm):
    bsz, s, _ = pf.shape
    cl = SSM_CHUNK
    pad = 128 - SSM_HEADS
    dtb = jnp.pad(dt_bias, (0, pad)).reshape(1, 128)
    alog = jnp.pad(a_log, (0, pad)).reshape(1, 128)
    dskip = jnp.repeat(d_skip, SSM_HEAD_DIM).reshape(1, D_INNER)
    const = lambda shape: pl.BlockSpec(shape, lambda b, c: (0, 0))
    return pl.pallas_call(
        _ssm_kernel,
        grid=(bsz, s // cl),
        in_specs=[pl.BlockSpec((1, cl, D_CONV), lambda b, c: (b, c, F_XBC // D_CONV)),
                  pl.BlockSpec((1, cl, D_INNER), lambda b, c: (b, c, F_ZSSM // D_INNER)),
                  pl.BlockSpec((1, cl, 128), lambda b, c: (b, c, F_SMALL // 128)),
                  const((SSM_CONV, D_CONV)), const((1, D_CONV)), const((1, 128)), const((1, 128)),
                  const((1, D_INNER)), const((1, D_INNER))],
        out_specs=pl.BlockSpec((1, cl, D_INNER), lambda b, c: (b, c, 0)),
        out_shape=jax.ShapeDtypeStruct((bsz, s, D_INNER), BF16),
        scratch_shapes=[pltpu.VMEM((8 + cl, D_CONV), F32),
                        pltpu.VMEM((SSM_HEADS // 2, SSM_STATE, 2 * SSM_HEAD_DIM), F32),
                        pltpu.VMEM((cl, D_INNER), F32)],
        compiler_params=_cp(("arbitrary", "arbitrary")),
        name="ssm",
    )(pf, pf, pf, conv_w, conv_b.reshape(1, D_CONV), dtb, alog, dskip, g_norm.reshape(1, D_INNER))


def _cmp_kernel(r_ref, pos_ref, w1_ref, w2_ref, o_ref):
    half = CMP_STRIDE * ATT_HEAD_DIM
    r = r_ref[0, 0]
    w1 = w1_ref[0]
    first = jnp.dot(r, w1[:half], preferred_element_type=F32)
    second = jnp.dot(r, w1[half:], preferred_element_type=F32)
    posb = jnp.broadcast_to(pos_ref[0], (8, 2 * half)).astype(BF16)
    cpos = jnp.dot(posb, w1, preferred_element_type=F32)[0:1]
    nrow = r.shape[0]
    hid = first + pltpu.roll(second, nrow - 1, 0) + cpos
    out = jnp.dot(_silu(hid).astype(BF16), w2_ref[0], preferred_element_type=F32)
    o_ref[0, 0] = out


def _compress(kvc_r, pos, w1, w2):
    bsz, _, nrow, _ = kvc_r.shape
    g = ATT_KV_GROUPS
    return pl.pallas_call(
        _cmp_kernel,
        grid=(bsz, 2 * g),
        in_specs=[pl.BlockSpec((1, 1, nrow, 1024), lambda b, j: (b, j, 0, 0)),
                  pl.BlockSpec((1, 1, 2048), lambda b, j: (j // g, 0, 0)),
                  pl.BlockSpec((1, 2048, CMP_HIDDEN), lambda b, j: (j // g, 0, 0)),
                  pl.BlockSpec((1, CMP_HIDDEN, ATT_HEAD_DIM), lambda b, j: (j // g, 0, 0))],
        out_specs=pl.BlockSpec((1, 1, nrow, ATT_HEAD_DIM), lambda b, j: (b, j, 0, 0)),
        out_shape=jax.ShapeDtypeStruct((bsz, 2 * g, nrow, ATT_HEAD_DIM), F32),
        compiler_params=_cp(("arbitrary", "arbitrary")),
        name="cmp",
    )(kvc_r, pos, w1, w2)


def _bf16_round_np(x):
    u = np.asarray(x, np.float32).view(np.uint32)
    u = (u + (((u >> 16) & 1) + 0x7FFF)) & np.uint32(0xFFFF0000)
    return u.view(np.float32)


def _slope_table():
    slope = (2.0 ** (-8.0 * np.arange(1, ATT_HEADS + 1) / ATT_HEADS)).astype(np.float32)
    p0 = _bf16_round_np(slope)
    p1 = _bf16_round_np(slope - p0)
    p2 = _bf16_round_np(slope - p0 - p1)
    tab = np.zeros((ATT_HEADS, SLOPE_STRIDE), np.float32)
    tab[:, 0], tab[:, 1], tab[:, 2] = p0, p1, p2
    tab[:, 3:6] = tab[:, 0:3] * TK
    tab[:, 6] = slope
    return jnp.asarray(tab.reshape(-1))


def _key_aux_table(s):
    pos = np.arange(s)
    tab = np.zeros((s, 128), np.float32)
    tab[:, L_POS:L_POS + 3] = (pos % TK)[:, None]
    tab[:, L_TILE:L_TILE + 3] = (pos // TK)[:, None]
    tab[pos, L_MASK + pos // SLC_BLOCK] = 1.0
    return jnp.asarray(tab, dtype=BF16)


def _bias_table():
    r = np.arange(TQ)[:, None]
    c = np.arange(TK)[None, :]
    tabs = []
    for jj in range(3):
        d = (2 - jj) * TK + r - c
        tabs.append(np.where((d >= 0) & (d < WINDOW), 0.0, NEG))
    tabs.append(np.where(r - c >= 0, 0.0, NEG))
    return jnp.asarray(np.stack(tabs), dtype=F32)


def _gate_expand_table():
    tab = np.zeros((ATT_KV_GROUPS, 128, 3 * KV_WIDTH), np.float32)
    for g in range(ATT_KV_GROUPS):
        for h in range(ATT_HPG):
            for j in range(3):
                row = SMALL_GATE + g * 3 * ATT_HPG + 3 * h + j
                tab[g, row, j * KV_WIDTH + h * ATT_HEAD_DIM:j * KV_WIDTH + (h + 1) * ATT_HEAD_DIM] = 1.0
    return jnp.asarray(tab)


def _cmp_to_slc_matrix_t(n_cmp_pad, n_cmp, n_slc):
    cs = np.arange(n_cmp_pad) * CMP_STRIDE
    ss = np.arange(n_slc) * SLC_BLOCK
    lo = np.maximum(cs[:, None], ss[None, :])
    hi = np.minimum(cs[:, None] + CMP_BLOCK, ss[None, :] + SLC_BLOCK)
    m = np.clip(hi - lo, 0, None) / CMP_BLOCK
    m[n_cmp:] = 0.0
    return jnp.asarray(m.T, dtype=F32)


def _cmpattn_kernel(slopes_ref, q_ref, ck_ref, cv_ref, mt_ref, ocmp_ref, qmask_ref,
                    khi_scr, klo_scr, vbd_scr):
    g = pl.program_id(1)
    i = pl.program_id(2)
    tq = q_ref.shape[1]
    t0 = i * tq
    npad = ck_ref.shape[2]
    n_slc = mt_ref.shape[0]

    @pl.when(i == 0)
    def _():
        kc = ck_ref[0, 0]
        vc = cv_ref[0, 0]
        k_hi = kc.astype(BF16).astype(F32)
        k_lo = kc - k_hi
        zero = jnp.zeros_like(kc)
        for h in range(ATT_HPG):
            def bd(a):
                return jnp.concatenate([a if hh == h else zero for hh in range(ATT_HPG)], axis=1).astype(BF16)
            khi_scr[h * npad:(h + 1) * npad, :] = bd(k_hi)
            klo_scr[h * npad:(h + 1) * npad, :] = bd(k_lo)
            vbd_scr[h * npad:(h + 1) * npad, :] = bd(vc)

    q = q_ref[0]
    s = (lax.dot_general(q, khi_scr[...], NT, preferred_element_type=F32)
         + lax.dot_general(q, klo_scr[...], NT, preferred_element_type=F32))
    jcol = lax.broadcasted_iota(jnp.int32, (1, npad), 1)
    cmp_end = jcol * CMP_STRIDE + (CMP_BLOCK - 1)
    trow = t0 + lax.broadcasted_iota(jnp.int32, (tq, npad), 0)
    valid = (cmp_end <= trow) & (jcol < npad - 1)
    rel = (cmp_end - t0).astype(F32)
    psum = jnp.zeros((tq, npad), F32)
    ps = []
    for h in range(ATT_HPG):
        slope = slopes_ref[(g * ATT_HPG + h) * SLOPE_STRIDE + 6]
        sh = jnp.where(valid, s[:, h * npad:(h + 1) * npad] + slope * rel, NEG)
        mx = jnp.max(sh, axis=-1, keepdims=True)
        p = jnp.where(valid, jnp.exp(sh - mx), 0.0)
        l = jnp.sum(p, axis=-1, keepdims=True)
        p = p * jnp.where(l > 0.0, 1.0 / jnp.where(l > 0.0, l, 1.0), 0.0)
        ps.append(p)
        psum = psum + p
    ocmp_ref[0] = jnp.dot(jnp.concatenate(ps, axis=1).astype(BF16), vbd_scr[...],
                          preferred_element_type=F32)
    imp_t = lax.dot_general(mt_ref[...], psum, NT, preferred_element_type=F32,
                            precision=HIGHEST)

    nrb = n_slc // 8
    blk_t = (t0 + lax.broadcasted_iota(jnp.int32, (8, tq), 1)) >> SLC_SHIFT
    sub = lax.broadcasted_iota(jnp.int32, (8, tq), 0)
    score = []
    for rb in range(nrb):
        kk = sub + rb * 8
        imp = imp_t[rb * 8:(rb + 1) * 8]
        forced = (kk == 0) | (kk == blk_t) | (kk == blk_t - 1)
        score.append(jnp.where(forced, imp + FORCE_BONUS, jnp.where(kk <= blk_t, imp, -1.0)))
    rank = [jnp.zeros((8, tq), F32) for _ in range(nrb)]
    for j in range(n_slc):
        cj = jnp.broadcast_to(score[j // 8][j % 8:j % 8 + 1, :], (8, tq))
        for rb in range(nrb):
            ge = jnp.where(cj >= score[rb], 1.0, 0.0)
            gt = jnp.where(cj > score[rb], 1.0, 0.0)
            if rb * 8 > j:
                beats = ge
            elif rb * 8 + 7 <= j:
                beats = gt
            else:
                beats = jnp.where(sub > j - rb * 8, ge, gt)
            rank[rb] = rank[rb] + beats
    rows = [jnp.zeros((L_MASK, tq), F32)]
    rows += [jnp.where(rank[rb] < float(SLC_TOPK), 0.0, -BIG) for rb in range(nrb)]
    rows += [jnp.zeros((128 - L_MASK - n_slc, tq), F32)]
    qmask_ref[0, 0] = jnp.concatenate(rows, axis=0).T.astype(BF16)


def _cmpattn(pb, cmp_kv, s):
    bsz = pb.shape[0]
    g = ATT_KV_GROUPS
    nrow = cmp_kv.shape[2]
    n_cmp = (s - CMP_BLOCK) // CMP_STRIDE + 1
    n_slc = s // SLC_BLOCK
    mt = _cmp_to_slc_matrix_t(nrow, n_cmp, n_slc)
    return pl.pallas_call(
        _cmpattn_kernel,
        grid=(bsz, g, s // TQ_CMP),
        in_specs=[pl.BlockSpec(memory_space=pltpu.SMEM),
                  pl.BlockSpec((1, TQ_CMP, KV_WIDTH), lambda b, gg, i: (b, i, B_Q // KV_WIDTH + gg)),
                  pl.BlockSpec((1, 1, nrow, ATT_HEAD_DIM), lambda b, gg, i: (b, gg, 0, 0)),
                  pl.BlockSpec((1, 1, nrow, ATT_HEAD_DIM), lambda b, gg, i: (b, g + gg, 0, 0)),
                  pl.BlockSpec((n_slc, nrow), lambda b, gg, i: (0, 0))],
        out_specs=[pl.BlockSpec((1, TQ_CMP, KV_WIDTH), lambda b, gg, i: (b, i, gg)),
                   pl.BlockSpec((1, 1, TQ_CMP, 128), lambda b, gg, i: (b, gg, i, 0))],
        out_shape=[jax.ShapeDtypeStruct((bsz, s, ATT_WIDTH), F32),
                   jax.ShapeDtypeStruct((bsz, g, s, 128), BF16)],
        scratch_shapes=[pltpu.VMEM((ATT_HPG * nrow, KV_WIDTH), BF16),
                        pltpu.VMEM((ATT_HPG * nrow, KV_WIDTH), BF16),
                        pltpu.VMEM((ATT_HPG * nrow, KV_WIDTH), BF16)],
        compiler_params=_cp(("arbitrary", "arbitrary", "arbitrary")),
        name="cmpattn",
    )(_slope_table(), pb, cmp_kv, cmp_kv, mt)


def _flash_step(qa_ref, kv_ref, aux_ref, m_scr, acc_scr, j, ntile, bias):
    r = qa_ref.shape[0]
    nk = ntile * TK
    start = pl.multiple_of(j * TK, TK)
    kv = kv_ref[0, pl.ds(start, nk), :]
    left = jnp.where(lax.broadcasted_iota(jnp.int32, (nk, 128), 1) < ATT_HEAD_DIM, 1.0, 0.0).astype(BF16)
    k_aug = kv * left + aux_ref[pl.ds(start, nk), :]
    v_aug = kv * (1.0 - left) + left
    s = lax.dot_general(qa_ref[...], k_aug, NT, preferred_element_type=F32)
    if bias is not None:
        parts = []
        for t, b in enumerate(bias):
            st = s[:, t * TK:(t + 1) * TK]
            if b is not None:
                st = (st.reshape(ATT_HPG, TQ, TK) + b[None]).reshape(r, TK)
            parts.append(st)
        s = jnp.concatenate(parts, axis=1) if ntile > 1 else parts[0]
    m_prev = m_scr[...]
    m_new = jnp.maximum(m_prev, jnp.max(s, axis=-1, keepdims=True))
    alpha = jnp.exp(m_prev - m_new)
    p = jnp.exp(s - jnp.concatenate([m_new] * (nk // 128), axis=1))
    acc_scr[...] = alpha * acc_scr[...] + jnp.dot(p.astype(BF16), v_aug, preferred_element_type=F32)
    m_scr[...] = m_new


def _flash_out(acc_scr):
    left = lax.broadcasted_iota(jnp.int32, (TQ, 128), 1) < ATT_HEAD_DIM
    slabs = []
    for pair in range(ATT_HPG // 2):
        a0 = acc_scr[(2 * pair) * TQ:(2 * pair + 1) * TQ, :]
        a1 = acc_scr[(2 * pair + 1) * TQ:(2 * pair + 2) * TQ, :]
        r0 = pltpu.roll(a0, ATT_HEAD_DIM, 1)
        r1 = pltpu.roll(a1, ATT_HEAD_DIM, 1)
        slabs.append(jnp.where(left, r0, a1) / jnp.where(left, a0, r1))
    return jnp.concatenate(slabs, axis=1)


def _attn_kernel(slopes_ref, q_ref, ksel_ref, kwin_ref, aux_ref, qmask_ref, ocmp_ref, small_ref,
                 zatt_ref, pick_ref, bias_ref, o_ref, qas_scr, qaw_scr, m_scr, acc_scr):
    g = pl.program_id(1)
    i = pl.program_id(2)

    qf = q_ref[0].astype(F32)
    qm = qmask_ref[0, 0].astype(F32)
    lane_row = lax.broadcasted_iota(jnp.int32, (1, 128), 1)
    left = lax.broadcasted_iota(jnp.int32, (TQ, 128), 1) < ATT_HEAD_DIM
    for h in range(ATT_HPG):
        slab = qf[:, 128 * (h // 2):128 * (h // 2) + 128]
        if h % 2:
            slab = pltpu.roll(slab, ATT_HEAD_DIM, 1)
        ext = jnp.zeros((1, 128), F32)
        for c in range(6):
            ext = jnp.where(lane_row == L_POS + c, slopes_ref[(g * ATT_HPG + h) * SLOPE_STRIDE + c], ext)
        base = jnp.where(left, slab, ext)
        qaw_scr[h * TQ:(h + 1) * TQ, :] = base.astype(BF16)
        qas_scr[h * TQ:(h + 1) * TQ, :] = (base + qm).astype(BF16)

    def reset():
        m_scr[...] = jnp.full(m_scr.shape, NEG, F32)
        acc_scr[...] = jnp.zeros(acc_scr.shape, F32)

    reset()

    def sel_body(jp, carry):
        _flash_step(qas_scr, ksel_ref, aux_ref, m_scr, acc_scr, 2 * jp, 2, None)
        return carry

    lax.fori_loop(0, lax.div(i, 2), sel_body, 0)

    @pl.when(lax.rem(i, 2) == 0)
    def _():
        _flash_step(qas_scr, ksel_ref, aux_ref, m_scr, acc_scr, i, 1, [bias_ref[3]])

    @pl.when(lax.rem(i, 2) == 1)
    def _():
        _flash_step(qas_scr, ksel_ref, aux_ref, m_scr, acc_scr, i - 1, 2, [None, bias_ref[3]])

    o_slc = _flash_out(acc_scr)

    reset()
    for nt in (1, 2, 3):
        cond = (i == nt - 1) if nt < 3 else (i >= 2)

        @pl.when(cond)
        def _():
            _flash_step(qaw_scr, kwin_ref, aux_ref, m_scr, acc_scr, i - (nt - 1), nt,
                        [bias_ref[3 - nt + t] for t in range(nt)])

    o_win = _flash_out(acc_scr)

    gts = _sigmoid(jnp.dot(small_ref[0], pick_ref[0], preferred_element_type=F32,
                                 precision=HIGHEST))
    o = (gts[:, :KV_WIDTH] * ocmp_ref[0] + gts[:, KV_WIDTH:2 * KV_WIDTH] * o_slc
         + gts[:, 2 * KV_WIDTH:] * o_win)
    o_ref[0] = (o * _silu(zatt_ref[0])).astype(o_ref.dtype)


def _attn(pb, pf, qmask, ocmp):
    bsz, s, _ = pb.shape
    g = ATT_KV_GROUPS
    r = ATT_HPG * TQ
    return pl.pallas_call(
        _attn_kernel,
        grid=(bsz, g, s // TQ),
        in_specs=[pl.BlockSpec(memory_space=pltpu.SMEM),
                  pl.BlockSpec((1, TQ, KV_WIDTH), lambda b, gg, i: (b, i, B_Q // KV_WIDTH + gg)),
                  pl.BlockSpec((1, s, 128), lambda b, gg, i: (b, 0, B_KSEL // 128 + gg)),
                  pl.BlockSpec((1, s, 128), lambda b, gg, i: (b, 0, B_KWIN // 128 + gg)),
                  pl.BlockSpec((s, 128), lambda b, gg, i: (0, 0)),
                  pl.BlockSpec((1, 1, TQ, 128), lambda b, gg, i: (b, gg, i, 0)),
                  pl.BlockSpec((1, TQ, KV_WIDTH), lambda b, gg, i: (b, i, gg)),
                  pl.BlockSpec((1, TQ, 128), lambda b, gg, i: (b, i, F_SMALL // 128)),
                  pl.BlockSpec((1, TQ, KV_WIDTH), lambda b, gg, i: (b, i, F_ZATT // KV_WIDTH + gg)),
                  pl.BlockSpec((1, 128, 3 * KV_WIDTH), lambda b, gg, i: (gg, 0, 0)),
                  pl.BlockSpec((4, TQ, TK), lambda b, gg, i: (0, 0, 0))],
        out_specs=pl.BlockSpec((1, TQ, KV_WIDTH), lambda b, gg, i: (b, i, gg)),
        out_shape=jax.ShapeDtypeStruct((bsz, s, ATT_WIDTH), BF16),
        scratch_shapes=[pltpu.VMEM((r, 128), BF16), pltpu.VMEM((r, 128), BF16),
                        pltpu.VMEM((r, 128), F32), pltpu.VMEM((r, 128), F32)],
        compiler_params=_cp(("arbitrary", "arbitrary", "arbitrary")),
        name="attn",
    )(_slope_table(), pb, pb, pb, _key_aux_table(s), qmask, ocmp, pf, pf, _gate_expand_table(),
      _bias_table())


def _merge_kernel(yssm_ref, onsa_ref, mg_ref, x_ref, gate_ref, gpost_ref, wssm_ref, wnsa_ref, wout_ref, o_ref):
    y_ssm = jnp.dot(yssm_ref[0], wssm_ref[...], preferred_element_type=F32)
    y_nsa = jnp.dot(onsa_ref[0], wnsa_ref[...], preferred_element_type=F32)
    mg = _sigmoid(mg_ref[0])
    merged = mg[:, :D_MODEL] * y_ssm + mg[:, D_MODEL:] * y_nsa
    out = jnp.dot(merged.astype(BF16), wout_ref[...], preferred_element_type=F32)
    yn = out * lax.rsqrt(jnp.mean(out * out, axis=-1, keepdims=True) + EPS) * gpost_ref[...]
    o_ref[0] = x_ref[0] + gate_ref[0] * yn


def _merge(yssm, onsa, pf, x, gate, g_post, w_ssm_out, w_nsa_out, w_out):
    bsz, s, _ = x.shape
    tm = 512
    const = lambda shape: pl.BlockSpec(shape, lambda b, i: (0, 0))
    return pl.pallas_call(
        _merge_kernel,
        grid=(bsz, s // tm),
        in_specs=[pl.BlockSpec((1, tm, D_INNER), lambda b, i: (b, i, 0)),
                  pl.BlockSpec((1, tm, ATT_WIDTH), lambda b, i: (b, i, 0)),
                  pl.BlockSpec((1, tm, 2 * D_MODEL), lambda b, i: (b, i, F_MERGE // (2 * D_MODEL))),
                  pl.BlockSpec((1, tm, D_MODEL), lambda b, i: (b, i, 0)),
                  pl.BlockSpec((1, 1, D_MODEL), lambda b, i: (b, 0, 0)),
                  const((1, D_MODEL)),
                  const((D_INNER, D_MODEL)), const((ATT_WIDTH, D_MODEL)), const((D_MODEL, D_MODEL))],
        out_specs=pl.BlockSpec((1, tm, D_MODEL), lambda b, i: (b, i, 0)),
        out_shape=jax.ShapeDtypeStruct((bsz, s, D_MODEL), F32),
        compiler_params=_cp(("arbitrary", "arbitrary")),
        name="merge",
    )(yssm, onsa, pf, x, gate, g_post.reshape(1, D_MODEL), w_ssm_out, w_nsa_out, w_out)


def _split_w_in(w_in):
    def cols(off, n):
        return w_in[:, off:off + n]

    dm = w_in.shape[0]
    small = jnp.concatenate([cols(IN_DT, SSM_HEADS), cols(IN_GATE, 3 * ATT_HEADS),
                             jnp.zeros((dm, 128 - SSM_HEADS - 3 * ATT_HEADS), w_in.dtype)], axis=1)
    wf = jnp.concatenate([cols(IN_XBC, D_CONV), cols(IN_ZATT, ATT_WIDTH), cols(IN_MERGE, 2 * D_MODEL),
                          cols(IN_Z, D_INNER), small], axis=1)
    hd = ATT_HEAD_DIM

    def interleave(k_off, v_off):
        return [cols(o + gg * hd, hd) for gg in range(ATT_KV_GROUPS) for o in (k_off, v_off)]

    q_scaled = cols(IN_Q, ATT_WIDTH) * (ATT_HEAD_DIM ** -0.5)
    wb = jnp.concatenate([q_scaled] + interleave(IN_KV + 2 * KV_WIDTH, IN_KV + 3 * KV_WIDTH)
                         + interleave(IN_KV + 4 * KV_WIDTH, IN_KV + 5 * KV_WIDTH)
                         + [cols(IN_KV, 2 * KV_WIDTH)], axis=1)
    return wf.astype(BF16), wb.astype(BF16)


def _layer(x, c, w_ada, b_ada, g_pre, g_post, w_in, conv_w, conv_b, dt_bias, a_log, d_skip,
           g_ssm_norm, w_ssm_out, cmp_pos_k, cmp_w1_k, cmp_w2_k, cmp_pos_v, cmp_w1_v, cmp_w2_v,
           w_nsa_out, w_out):
    bsz, s, dm = x.shape
    mod = _ada(c, w_ada, b_ada)
    shift = mod[:, None, :dm]
    scale = mod[:, None, dm:2 * dm]
    gate = mod[:, None, 2 * dm:]
    wf, wb = _split_w_in(w_in)
    pf = _proj(x, shift, scale, g_pre, wf, TN_F, F32, "proj_f32")
    pb = _proj(x, shift, scale, g_pre, wb, NB, BF16, "proj_bf16")

    yssm = _ssm(pf, conv_w, conv_b, dt_bias, a_log, d_skip, g_ssm_norm)

    kvc = pb[:, :, B_KVC:B_KVC + 2 * KV_WIDTH]
    kvc = kvc.reshape(bsz, s // CMP_STRIDE, CMP_STRIDE, 2 * ATT_KV_GROUPS, ATT_HEAD_DIM)
    kvc = kvc.transpose(0, 3, 1, 2, 4).reshape(bsz, 2 * ATT_KV_GROUPS, s // CMP_STRIDE,
                                               CMP_STRIDE * ATT_HEAD_DIM)
    pos = jnp.stack([cmp_pos_k.reshape(1, -1), cmp_pos_v.reshape(1, -1)])
    w1 = jnp.stack([cmp_w1_k, cmp_w1_v]).astype(BF16)
    w2 = jnp.stack([cmp_w2_k, cmp_w2_v]).astype(BF16)
    cmp_kv = _compress(kvc, pos, w1, w2)

    ocmp, qmask = _cmpattn(pb, cmp_kv, s)
    onsa = _attn(pb, pf, qmask, ocmp)
    return _merge(yssm, onsa, pf, x, gate, g_post, w_ssm_out.astype(BF16), w_nsa_out.astype(BF16),
                  w_out.astype(BF16))


@jax.jit
def kernel(x, c, w_ada, b_ada, g_pre, g_post, w_in, conv_w, conv_b, dt_bias, a_log, d_skip, g_ssm_norm,
           w_ssm_out, cmp_pos_k, cmp_w1_k, cmp_w2_k, cmp_pos_v, cmp_w1_v, cmp_w2_v, w_nsa_out, w_out):
    for layer in range(w_in.shape[0]):
        x = _layer(x, c, w_ada[layer], b_ada[layer], g_pre[layer], g_post[layer], w_in[layer],
                   conv_w[layer], conv_b[layer], dt_bias[layer], a_log[layer], d_skip[layer],
                   g_ssm_norm[layer], w_ssm_out[layer], cmp_pos_k[layer], cmp_w1_k[layer],
                   cmp_w2_k[layer], cmp_pos_v[layer], cmp_w1_v[layer], cmp_w2_v[layer],
                   w_nsa_out[layer], w_out[layer])
    return x
```

```python
import numpy as np
import jax
import jax.numpy as jnp
from jax import lax
from jax.experimental import pallas as pl
from jax.experimental.pallas import tpu as pltpu

D_MODEL = 1024
D_INNER = 2048
SSM_HEAD_DIM = 64
SSM_HEADS = 32
SSM_GROUPS = 4
SSM_STATE = 128
SSM_CONV = 4
SSM_CHUNK = 128
D_CONV = D_INNER + 2 * SSM_GROUPS * SSM_STATE
CONV_TAIL = 16

ATT_HEADS = 16
ATT_HEAD_DIM = 64
ATT_KV_GROUPS = 4
ATT_HPG = 4
ATT_WIDTH = 1024
KV_WIDTH = 256
CMP_BLOCK = 32
CMP_STRIDE = 16
CMP_HIDDEN = 256
SLC_BLOCK = 64
SLC_SHIFT = 6
SLC_TOPK = 16
WINDOW = 512
FORCE_BONUS = 1000.0
EPS = 1e-6
NEG = -1e30
BIG = 2.0 ** 100

F32 = jnp.float32
BF16 = jnp.bfloat16
HIGHEST = lax.Precision.HIGHEST
NT = (((1,), (1,)), ((), ()))

_SIZES = (D_INNER, D_CONV, SSM_HEADS, ATT_WIDTH, 6 * KV_WIDTH, 3 * ATT_HEADS, ATT_WIDTH, 2 * D_MODEL)
_OFFS = tuple(int(v) for v in np.cumsum((0,) + _SIZES))
IN_Z, IN_XBC, IN_DT, IN_Q, IN_KV, IN_GATE, IN_ZATT, IN_MERGE = _OFFS[:8]

B_XBC, B_ZATT, B_MERGE, B_ZSSM, B_Q, B_KSEL, B_KWIN, B_KVC = 0, 3072, 4096, 6144, 8192, 9216, 9728, 10240
NB = 10752
SMALL_GATE = 32
LOG2E = 1.4426950408889634

TM_PROJ = 1024
TN_PROJ = 2688
TQ = 256
TK = 256
TQ_CMP = 512
VMEM_LIMIT = 48 * 1024 * 1024

L_POS = 64
L_TILE = 67
L_MASK = 72
SLOPE_STRIDE = 8


def _cp(sem):
    return pltpu.CompilerParams(dimension_semantics=sem, vmem_limit_bytes=VMEM_LIMIT)


def _sigmoid(v):
    return 0.5 * jnp.tanh(0.5 * v) + 0.5


def _silu(v):
    return v * _sigmoid(v)


def _ada_kernel(c_ref, w_ref, b_ref, o_ref):
    o_ref[...] = jnp.dot(c_ref[...], w_ref[...], preferred_element_type=F32,
                         precision=HIGHEST) + b_ref[...]


def _ada(c, w_ada, b_ada):
    bsz = c.shape[0]
    return pl.pallas_call(
        _ada_kernel,
        grid=(3,),
        in_specs=[pl.BlockSpec((bsz, D_MODEL), lambda j: (0, 0)),
                  pl.BlockSpec((D_MODEL, D_MODEL), lambda j: (0, j)),
                  pl.BlockSpec((1, D_MODEL), lambda j: (0, j))],
        out_specs=pl.BlockSpec((bsz, D_MODEL), lambda j: (0, j)),
        out_shape=jax.ShapeDtypeStruct((bsz, 3 * D_MODEL), F32),
        compiler_params=_cp(("arbitrary",)),
        name="ada",
    )(c, w_ada, b_ada.reshape(1, 3 * D_MODEL))


def _proj_kernel(x_ref, shift_ref, scale_ref, g_ref, w_ref, ws_ref, o_ref, osm_ref, h_scr):
    @pl.when(pl.program_id(2) == 0)
    def _():
        xf = x_ref[0]
        y = xf * lax.rsqrt(jnp.mean(xf * xf, axis=-1, keepdims=True) + EPS) * g_ref[...]
        h_scr[...] = (y * (1.0 + scale_ref[0]) + shift_ref[0]).astype(BF16)
        osm_ref[0] = jnp.dot(h_scr[...], ws_ref[...], preferred_element_type=F32)

    o_ref[0] = jnp.dot(h_scr[...], w_ref[...], preferred_element_type=F32).astype(o_ref.dtype)


def _proj(x, shift, scale, g_pre, w, w_small):
    bsz, s, _ = x.shape
    n = w.shape[1]
    return pl.pallas_call(
        _proj_kernel,
        grid=(bsz, s // TM_PROJ, n // TN_PROJ),
        in_specs=[pl.BlockSpec((1, TM_PROJ, D_MODEL), lambda b, i, j: (b, i, 0)),
                  pl.BlockSpec((1, 1, D_MODEL), lambda b, i, j: (b, 0, 0)),
                  pl.BlockSpec((1, 1, D_MODEL), lambda b, i, j: (b, 0, 0)),
                  pl.BlockSpec((1, D_MODEL), lambda b, i, j: (0, 0)),
                  pl.BlockSpec((D_MODEL, TN_PROJ), lambda b, i, j: (0, j)),
                  pl.BlockSpec((D_MODEL, 128), lambda b, i, j: (0, 0))],
        out_specs=[pl.BlockSpec((1, TM_PROJ, TN_PROJ), lambda b, i, j: (b, i, j)),
                   pl.BlockSpec((1, TM_PROJ, 128), lambda b, i, j: (b, i, 0))],
        out_shape=[jax.ShapeDtypeStruct((bsz, s, n), BF16),
                   jax.ShapeDtypeStruct((bsz, s, 128), F32)],
        scratch_shapes=[pltpu.VMEM((TM_PROJ, D_MODEL), BF16)],
        compiler_params=_cp(("arbitrary", "arbitrary", "arbitrary")),
        name="proj",
    )(x, shift, scale, g_pre.reshape(1, D_MODEL), w, w_small)


def _ssm_kernel(xbc_ref, z_ref, small_ref, shift_ref, convw_ref, convb_ref, dtb_ref, alog_ref,
                dskip_ref, gn_ref, o_ref, tail_scr, state_scr, y_scr):
    cl = SSM_CHUNK

    @pl.when(pl.program_id(1) == 0)
    def _():
        tail_scr[...] = jnp.zeros(tail_scr.shape, BF16)
        state_scr[...] = jnp.zeros(state_scr.shape, F32)

    cur = xbc_ref[0]
    xpad = jnp.concatenate([tail_scr[...], cur], axis=0)
    delayed = jnp.dot(shift_ref[...], xpad, preferred_element_type=F32)
    tail_scr[...] = cur[cl - CONV_TAIL:cl, :]
    acc = convb_ref[...] + convw_ref[SSM_CONV - 1:SSM_CONV, :] * cur.astype(F32)
    for k in range(SSM_CONV - 1):
        acc = acc + convw_ref[k:k + 1, :] * delayed[k * cl:(k + 1) * cl]
    u = _silu(acc)
    xs = u[:, :D_INNER]
    bm = u[:, D_INNER:D_INNER + SSM_GROUPS * SSM_STATE]
    cm = u[:, D_INNER + SSM_GROUPS * SSM_STATE:]

    pre = small_ref[0] + dtb_ref[...]
    dt = jnp.maximum(pre, 0.0) + jnp.log1p(jnp.exp(-jnp.abs(pre)))
    a = -jnp.exp(alog_ref[...])
    adt = dt * a
    row = lax.broadcasted_iota(jnp.int32, (cl, cl), 0)
    col = lax.broadcasted_iota(jnp.int32, (cl, cl), 1)
    causal = row >= col
    tri = causal.astype(F32)
    a_cs = jnp.dot(tri, adt, preferred_element_type=F32, precision=HIGHEST)
    a_cs_t = a_cs.T
    dt_t = dt.T
    a_last = a_cs[cl - 1:cl, :]
    ea = jnp.exp(a_cs)
    dsc = jnp.exp(a_last - a_cs) * dt
    cdec = jnp.exp(a_last)
    lane = lax.broadcasted_iota(jnp.int32, (cl, 2 * SSM_HEAD_DIM), 1)
    first_half = lane < SSM_HEAD_DIM

    hg = SSM_HEADS // SSM_GROUPS
    for g in range(SSM_GROUPS):
        bg = bm[:, g * SSM_STATE:(g + 1) * SSM_STATE]
        cg = cm[:, g * SSM_STATE:(g + 1) * SSM_STATE]
        cb = lax.dot_general(cg.astype(BF16), bg.astype(BF16), NT, preferred_element_type=F32)
        for pp in range(hg // 2):
            pair = g * (hg // 2) + pp
            xs_pair = xs[:, pair * 128:(pair + 1) * 128]
            st = state_scr[pair]
            rhs = jnp.concatenate([xs_pair, st], axis=0).astype(BF16)
            lhs_list, bs_list = [], []
            for e in range(2):
                h = 2 * pair + e
                seg = a_cs[:, h:h + 1] - a_cs_t[h:h + 1, :]
                lmat = jnp.exp(jnp.where(causal, seg, NEG))
                gmat = cb * lmat * dt_t[h:h + 1, :]
                lhs_list.append(jnp.concatenate([gmat, cg * ea[:, h:h + 1]], axis=1))
                bs_list.append(bg * dsc[:, h:h + 1])
            lhs = jnp.concatenate(lhs_list, axis=0).astype(BF16)
            yy = jnp.dot(lhs, rhs, preferred_element_type=F32)
            y_scr[:, pair * 128:(pair + 1) * 128] = jnp.where(first_half, yy[:cl], yy[cl:])
            bs = jnp.concatenate(bs_list, axis=1).astype(BF16)
            new = lax.dot_general(bs, xs_pair.astype(BF16), (((0,), (0,)), ((), ())),
                                  preferred_element_type=F32)
            dec = jnp.where(first_half[0:1], cdec[:, 2 * pair:2 * pair + 1],
                            cdec[:, 2 * pair + 1:2 * pair + 2])
            state_scr[pair] = st * dec + jnp.where(first_half, new[:SSM_STATE], new[SSM_STATE:])

    y = y_scr[...] + xs * dskip_ref[...]
    y = y * _silu(z_ref[0].astype(F32))
    gsz = D_INNER // SSM_GROUPS
    for g in range(SSM_GROUPS):
        yg = y[:, g * gsz:(g + 1) * gsz]
        yn = yg * lax.rsqrt(jnp.mean(yg * yg, axis=-1, keepdims=True) + EPS)
        o_ref[0, :, g * gsz:(g + 1) * gsz] = (yn * gn_ref[:, g * gsz:(g + 1) * gsz]).astype(o_ref.dtype)


def _conv_shift_matrix():
    cl = SSM_CHUNK
    m = np.zeros(((SSM_CONV - 1) * cl, CONV_TAIL + cl), np.float32)
    t = np.arange(cl)
    for k in range(SSM_CONV - 1):
        m[k * cl + t, CONV_TAIL + t - (SSM_CONV - 1 - k)] = 1.0
    return jnp.asarray(m, dtype=BF16)


def _ssm(pb, small, conv_w, conv_b, dt_bias, a_log, d_skip, g_norm):
    bsz, s, _ = pb.shape
    cl = SSM_CHUNK
    pad = 128 - SSM_HEADS
    dtb = jnp.pad(dt_bias, (0, pad)).reshape(1, 128)
    alog = jnp.pad(a_log, (0, pad)).reshape(1, 128)
    dskip = jnp.repeat(d_skip, SSM_HEAD_DIM).reshape(1, D_INNER)
    const = lambda shape: pl.BlockSpec(shape, lambda b, c: (0, 0))
    return pl.pallas_call(
        _ssm_kernel,
        grid=(bsz, s // cl),
        in_specs=[pl.BlockSpec((1, cl, D_CONV), lambda b, c: (b, c, B_XBC // D_CONV)),
                  pl.BlockSpec((1, cl, D_INNER), lambda b, c: (b, c, B_ZSSM // D_INNER)),
                  pl.BlockSpec((1, cl, 128), lambda b, c: (b, c, 0)),
                  const(((SSM_CONV - 1) * cl, CONV_TAIL + cl)),
                  const((SSM_CONV, D_CONV)), const((1, D_CONV)), const((1, 128)), const((1, 128)),
                  const((1, D_INNER)), const((1, D_INNER))],
        out_specs=pl.BlockSpec((1, cl, D_INNER), lambda b, c: (b, c, 0)),
        out_shape=jax.ShapeDtypeStruct((bsz, s, D_INNER), BF16),
        scratch_shapes=[pltpu.VMEM((CONV_TAIL, D_CONV), BF16),
                        pltpu.VMEM((SSM_HEADS // 2, SSM_STATE, 2 * SSM_HEAD_DIM), F32),
                        pltpu.VMEM((cl, D_INNER), F32)],
        compiler_params=_cp(("arbitrary", "arbitrary")),
        name="ssm",
    )(pb, pb, small, _conv_shift_matrix(), conv_w, conv_b.reshape(1, D_CONV), dtb, alog, dskip,
      g_norm.reshape(1, D_INNER))


def _cmp_kernel(r_ref, pos_ref, w1_ref, w2_ref, o_ref):
    half = CMP_STRIDE * ATT_HEAD_DIM
    r = r_ref[0, 0]
    w1 = w1_ref[0]
    first = jnp.dot(r, w1[:half], preferred_element_type=F32)
    second = jnp.dot(r, w1[half:], preferred_element_type=F32)
    posb = jnp.broadcast_to(pos_ref[0], (8, 2 * half)).astype(BF16)
    cpos = jnp.dot(posb, w1, preferred_element_type=F32)[0:1]
    nrow = r.shape[0]
    hid = first + pltpu.roll(second, nrow - 1, 0) + cpos
    out = jnp.dot(_silu(hid).astype(BF16), w2_ref[0], preferred_element_type=F32)
    o_ref[0, 0] = out


def _compress(kvc_r, pos, w1, w2):
    bsz, _, nrow, _ = kvc_r.shape
    g = ATT_KV_GROUPS
    return pl.pallas_call(
        _cmp_kernel,
        grid=(bsz, 2 * g),
        in_specs=[pl.BlockSpec((1, 1, nrow, 1024), lambda b, j: (b, j, 0, 0)),
                  pl.BlockSpec((1, 1, 2048), lambda b, j: (j // g, 0, 0)),
                  pl.BlockSpec((1, 2048, CMP_HIDDEN), lambda b, j: (j // g, 0, 0)),
                  pl.BlockSpec((1, CMP_HIDDEN, ATT_HEAD_DIM), lambda b, j: (j // g, 0, 0))],
        out_specs=pl.BlockSpec((1, 1, nrow, ATT_HEAD_DIM), lambda b, j: (b, j, 0, 0)),
        out_shape=jax.ShapeDtypeStruct((bsz, 2 * g, nrow, ATT_HEAD_DIM), F32),
        compiler_params=_cp(("arbitrary", "arbitrary")),
        name="cmp",
    )(kvc_r, pos, w1, w2)


def _bf16_round_np(x):
    u = np.asarray(x, np.float32).view(np.uint32)
    u = (u + (((u >> 16) & 1) + 0x7FFF)) & np.uint32(0xFFFF0000)
    return u.view(np.float32)


def _slope_table():
    slope = (2.0 ** (-8.0 * np.arange(1, ATT_HEADS + 1) / ATT_HEADS)).astype(np.float32)
    slope = (slope.astype(np.float64) * LOG2E).astype(np.float32)
    p0 = _bf16_round_np(slope)
    p1 = _bf16_round_np(slope - p0)
    p2 = _bf16_round_np(slope - p0 - p1)
    tab = np.zeros((ATT_HEADS, SLOPE_STRIDE), np.float32)
    tab[:, 0], tab[:, 1], tab[:, 2] = p0, p1, p2
    tab[:, 3:6] = tab[:, 0:3] * TK
    tab[:, 6] = slope
    return jnp.asarray(tab.reshape(-1))


def _key_aux_table(s):
    pos = np.arange(s)
    tab = np.zeros((s, 128), np.float32)
    tab[:, L_POS:L_POS + 3] = (pos % TK)[:, None]
    tab[:, L_TILE:L_TILE + 3] = (pos // TK)[:, None]
    tab[pos, L_MASK + pos // SLC_BLOCK] = 1.0
    return jnp.asarray(tab, dtype=BF16)


def _bias_table():
    r = np.arange(TQ)[:, None]
    c = np.arange(TK)[None, :]
    tabs = []
    for jj in range(3):
        d = (2 - jj) * TK + r - c
        tabs.append(np.where((d >= 0) & (d < WINDOW), 0.0, NEG))
    tabs.append(np.where(r - c >= 0, 0.0, NEG))
    return jnp.asarray(np.stack(tabs), dtype=F32)


def _gate_expand_table():
    tab = np.zeros((ATT_KV_GROUPS, 128, 3 * KV_WIDTH), np.float32)
    for g in range(ATT_KV_GROUPS):
        for h in range(ATT_HPG):
            for j in range(3):
                row = SMALL_GATE + g * 3 * ATT_HPG + 3 * h + j
                tab[g, row, j * KV_WIDTH + h * ATT_HEAD_DIM:j * KV_WIDTH + (h + 1) * ATT_HEAD_DIM] = 1.0
    return jnp.asarray(tab)


def _cmp_to_slc_matrix_t(n_cmp_pad, n_cmp, n_slc):
    cs = np.arange(n_cmp_pad) * CMP_STRIDE
    ss = np.arange(n_slc) * SLC_BLOCK
    lo = np.maximum(cs[:, None], ss[None, :])
    hi = np.minimum(cs[:, None] + CMP_BLOCK, ss[None, :] + SLC_BLOCK)
    m = np.clip(hi - lo, 0, None) / CMP_BLOCK
    m[n_cmp:] = 0.0
    return jnp.asarray(m.T, dtype=F32)


def _cmpattn_kernel(slopes_ref, q_ref, ck_ref, cv_ref, mt_ref, ocmp_ref, qmask_ref,
                    khi_scr, klo_scr, vbd_scr):
    g = pl.program_id(1)
    i = pl.program_id(2)
    tq = q_ref.shape[1]
    t0 = i * tq
    npad = ck_ref.shape[2]
    n_slc = mt_ref.shape[0]

    @pl.when(i == 0)
    def _():
        kc = ck_ref[0, 0]
        vc = cv_ref[0, 0]
        k_hi = kc.astype(BF16).astype(F32)
        k_lo = kc - k_hi
        zero = jnp.zeros_like(kc)
        for h in range(ATT_HPG):
            def bd(a):
                return jnp.concatenate([a if hh == h else zero for hh in range(ATT_HPG)], axis=1).astype(BF16)
            khi_scr[h * npad:(h + 1) * npad, :] = bd(k_hi)
            klo_scr[h * npad:(h + 1) * npad, :] = bd(k_lo)
            vbd_scr[h * npad:(h + 1) * npad, :] = bd(vc)

    q = q_ref[0]
    s = (lax.dot_general(q, khi_scr[...], NT, preferred_element_type=F32)
         + lax.dot_general(q, klo_scr[...], NT, preferred_element_type=F32))
    jcol = lax.broadcasted_iota(jnp.int32, (1, npad), 1)
    cmp_end = jcol * CMP_STRIDE + (CMP_BLOCK - 1)
    trow = t0 + lax.broadcasted_iota(jnp.int32, (tq, npad), 0)
    valid = (cmp_end <= trow) & (jcol < npad - 1)
    rel = (cmp_end - t0).astype(F32)
    psum = jnp.zeros((tq, npad), F32)
    ps = []
    for h in range(ATT_HPG):
        slope = slopes_ref[(g * ATT_HPG + h) * SLOPE_STRIDE + 6]
        sh = jnp.where(valid, s[:, h * npad:(h + 1) * npad] + slope * rel, NEG)
        mx = jnp.max(sh, axis=-1, keepdims=True)
        p = jnp.where(valid, jnp.exp2(sh - mx), 0.0)
        l = jnp.sum(p, axis=-1, keepdims=True)
        p = p * jnp.where(l > 0.0, 1.0 / jnp.where(l > 0.0, l, 1.0), 0.0)
        ps.append(p)
        psum = psum + p
    ocmp_ref[0] = jnp.dot(jnp.concatenate(ps, axis=1).astype(BF16), vbd_scr[...],
                          preferred_element_type=F32)
    imp_t = lax.dot_general(mt_ref[...], psum, NT, preferred_element_type=F32,
                            precision=HIGHEST)

    nrb = n_slc // 8
    blk_t = (t0 + lax.broadcasted_iota(jnp.int32, (8, tq), 1)) >> SLC_SHIFT
    sub = lax.broadcasted_iota(jnp.int32, (8, tq), 0)
    score = []
    for rb in range(nrb):
        kk = sub + rb * 8
        imp = imp_t[rb * 8:(rb + 1) * 8]
        forced = (kk == 0) | (kk == blk_t) | (kk == blk_t - 1)
        score.append(jnp.where(forced, imp + FORCE_BONUS, jnp.where(kk <= blk_t, imp, -1.0)))
    rank = [jnp.zeros((8, tq), F32) for _ in range(nrb)]
    for j in range(n_slc):
        cj = jnp.broadcast_to(score[j // 8][j % 8:j % 8 + 1, :], (8, tq))
        for rb in range(nrb):
            ge = jnp.where(cj >= score[rb], 1.0, 0.0)
            gt = jnp.where(cj > score[rb], 1.0, 0.0)
            if rb * 8 > j:
                beats = ge
            elif rb * 8 + 7 <= j:
                beats = gt
            else:
                beats = jnp.where(sub > j - rb * 8, ge, gt)
            rank[rb] = rank[rb] + beats
    rows = [jnp.zeros((L_MASK, tq), F32)]
    rows += [jnp.where(rank[rb] < float(SLC_TOPK), 0.0, -BIG) for rb in range(nrb)]
    rows += [jnp.zeros((128 - L_MASK - n_slc, tq), F32)]
    qmask_ref[0, 0] = jnp.concatenate(rows, axis=0).T.astype(BF16)


def _cmpattn(pb, cmp_kv, s):
    bsz = pb.shape[0]
    g = ATT_KV_GROUPS
    nrow = cmp_kv.shape[2]
    n_cmp = (s - CMP_BLOCK) // CMP_STRIDE + 1
    n_slc = s // SLC_BLOCK
    mt = _cmp_to_slc_matrix_t(nrow, n_cmp, n_slc)
    return pl.pallas_call(
        _cmpattn_kernel,
        grid=(bsz, g, s // TQ_CMP),
        in_specs=[pl.BlockSpec(memory_space=pltpu.SMEM),
                  pl.BlockSpec((1, TQ_CMP, KV_WIDTH), lambda b, gg, i: (b, i, B_Q // KV_WIDTH + gg)),
                  pl.BlockSpec((1, 1, nrow, ATT_HEAD_DIM), lambda b, gg, i: (b, gg, 0, 0)),
                  pl.BlockSpec((1, 1, nrow, ATT_HEAD_DIM), lambda b, gg, i: (b, g + gg, 0, 0)),
                  pl.BlockSpec((n_slc, nrow), lambda b, gg, i: (0, 0))],
        out_specs=[pl.BlockSpec((1, TQ_CMP, KV_WIDTH), lambda b, gg, i: (b, i, gg)),
                   pl.BlockSpec((1, 1, TQ_CMP, 128), lambda b, gg, i: (b, gg, i, 0))],
        out_shape=[jax.ShapeDtypeStruct((bsz, s, ATT_WIDTH), F32),
                   jax.ShapeDtypeStruct((bsz, g, s, 128), BF16)],
        scratch_shapes=[pltpu.VMEM((ATT_HPG * nrow, KV_WIDTH), BF16),
                        pltpu.VMEM((ATT_HPG * nrow, KV_WIDTH), BF16),
                        pltpu.VMEM((ATT_HPG * nrow, KV_WIDTH), BF16)],
        compiler_params=_cp(("arbitrary", "arbitrary", "arbitrary")),
        name="cmpattn",
    )(_slope_table(), pb, cmp_kv, cmp_kv, mt)


def _flash_step(qa_ref, kv_ref, aux_ref, m_scr, acc_scr, j, ntile, bias):
    r = qa_ref.shape[0]
    nk = ntile * TK
    start = pl.multiple_of(j * TK, TK)
    kv = kv_ref[0, pl.ds(start, nk), :]
    left = jnp.where(lax.broadcasted_iota(jnp.int32, (nk, 128), 1) < ATT_HEAD_DIM, 1.0, 0.0).astype(BF16)
    k_aug = kv * left + aux_ref[pl.ds(start, nk), :]
    v_aug = kv * (1.0 - left) + left
    s = lax.dot_general(qa_ref[...], k_aug, NT, preferred_element_type=F32)
    if bias is not None:
        parts = []
        for t, b in enumerate(bias):
            st = s[:, t * TK:(t + 1) * TK]
            if b is not None:
                st = (st.reshape(ATT_HPG, TQ, TK) + b[None]).reshape(r, TK)
            parts.append(st)
        s = jnp.concatenate(parts, axis=1) if ntile > 1 else parts[0]
    m_prev = m_scr[...]
    m_new = jnp.maximum(m_prev, jnp.max(s, axis=-1, keepdims=True))
    alpha = jnp.exp2(m_prev - m_new)
    p = jnp.exp2(s - jnp.concatenate([m_new] * (nk // 128), axis=1))
    acc_scr[...] = alpha * acc_scr[...] + jnp.dot(p.astype(BF16), v_aug, preferred_element_type=F32)
    m_scr[...] = m_new


def _flash_out(acc_scr):
    left = lax.broadcasted_iota(jnp.int32, (TQ, 128), 1) < ATT_HEAD_DIM
    slabs = []
    for pair in range(ATT_HPG // 2):
        a0 = acc_scr[(2 * pair) * TQ:(2 * pair + 1) * TQ, :]
        a1 = acc_scr[(2 * pair + 1) * TQ:(2 * pair + 2) * TQ, :]
        r0 = pltpu.roll(a0, ATT_HEAD_DIM, 1)
        r1 = pltpu.roll(a1, ATT_HEAD_DIM, 1)
        slabs.append(jnp.where(left, r0, a1) / jnp.where(left, a0, r1))
    return jnp.concatenate(slabs, axis=1)


def _attn_kernel(slopes_ref, q_ref, ksel_ref, kwin_ref, aux_ref, qmask_ref, ocmp_ref, small_ref,
                 zatt_ref, pick_ref, bias_ref, o_ref, qas_scr, qaw_scr, m_scr, acc_scr):
    g = pl.program_id(1)
    i = pl.program_id(2)

    qf = q_ref[0].astype(F32)
    qm = qmask_ref[0, 0].astype(F32)
    lane_row = lax.broadcasted_iota(jnp.int32, (1, 128), 1)
    left = lax.broadcasted_iota(jnp.int32, (TQ, 128), 1) < ATT_HEAD_DIM
    for h in range(ATT_HPG):
        slab = qf[:, 128 * (h // 2):128 * (h // 2) + 128]
        if h % 2:
            slab = pltpu.roll(slab, ATT_HEAD_DIM, 1)
        ext = jnp.zeros((1, 128), F32)
        for c in range(6):
            ext = jnp.where(lane_row == L_POS + c, slopes_ref[(g * ATT_HPG + h) * SLOPE_STRIDE + c], ext)
        base = jnp.where(left, slab, ext)
        qaw_scr[h * TQ:(h + 1) * TQ, :] = base.astype(BF16)
        qas_scr[h * TQ:(h + 1) * TQ, :] = (base + qm).astype(BF16)

    def reset():
        m_scr[...] = jnp.full(m_scr.shape, NEG, F32)
        acc_scr[...] = jnp.zeros(acc_scr.shape, F32)

    reset()

    def sel_body(jp, carry):
        _flash_step(qas_scr, ksel_ref, aux_ref, m_scr, acc_scr, 2 * jp, 2, None)
        return carry

    lax.fori_loop(0, lax.div(i, 2), sel_body, 0)

    @pl.when(lax.rem(i, 2) == 0)
    def _():
        _flash_step(qas_scr, ksel_ref, aux_ref, m_scr, acc_scr, i, 1, [bias_ref[3]])

    @pl.when(lax.rem(i, 2) == 1)
    def _():
        _flash_step(qas_scr, ksel_ref, aux_ref, m_scr, acc_scr, i - 1, 2, [None, bias_ref[3]])

    o_slc = _flash_out(acc_scr)

    reset()
    for nt in (1, 2, 3):
        cond = (i == nt - 1) if nt < 3 else (i >= 2)

        @pl.when(cond)
        def _():
            _flash_step(qaw_scr, kwin_ref, aux_ref, m_scr, acc_scr, i - (nt - 1), nt,
                        [bias_ref[3 - nt + t] for t in range(nt)])

    o_win = _flash_out(acc_scr)

    gts = _sigmoid(jnp.dot(small_ref[0], pick_ref[0], preferred_element_type=F32,
                                 precision=HIGHEST))
    o = (gts[:, :KV_WIDTH] * ocmp_ref[0] + gts[:, KV_WIDTH:2 * KV_WIDTH] * o_slc
         + gts[:, 2 * KV_WIDTH:] * o_win)
    o_ref[0] = (o * _silu(zatt_ref[0].astype(F32))).astype(o_ref.dtype)


def _attn(pb, small, qmask, ocmp):
    bsz, s, _ = pb.shape
    g = ATT_KV_GROUPS
    r = ATT_HPG * TQ
    return pl.pallas_call(
        _attn_kernel,
        grid=(bsz, g, s // TQ),
        in_specs=[pl.BlockSpec(memory_space=pltpu.SMEM),
                  pl.BlockSpec((1, TQ, KV_WIDTH), lambda b, gg, i: (b, i, B_Q // KV_WIDTH + gg)),
                  pl.BlockSpec((1, s, 128), lambda b, gg, i: (b, 0, B_KSEL // 128 + gg)),
                  pl.BlockSpec((1, s, 128), lambda b, gg, i: (b, 0, B_KWIN // 128 + gg)),
                  pl.BlockSpec((s, 128), lambda b, gg, i: (0, 0)),
                  pl.BlockSpec((1, 1, TQ, 128), lambda b, gg, i: (b, gg, i, 0)),
                  pl.BlockSpec((1, TQ, KV_WIDTH), lambda b, gg, i: (b, i, gg)),
                  pl.BlockSpec((1, TQ, 128), lambda b, gg, i: (b, i, 0)),
                  pl.BlockSpec((1, TQ, KV_WIDTH), lambda b, gg, i: (b, i, B_ZATT // KV_WIDTH + gg)),
                  pl.BlockSpec((1, 128, 3 * KV_WIDTH), lambda b, gg, i: (gg, 0, 0)),
                  pl.BlockSpec((4, TQ, TK), lambda b, gg, i: (0, 0, 0))],
        out_specs=pl.BlockSpec((1, TQ, KV_WIDTH), lambda b, gg, i: (b, i, gg)),
        out_shape=jax.ShapeDtypeStruct((bsz, s, ATT_WIDTH), BF16),
        scratch_shapes=[pltpu.VMEM((r, 128), BF16), pltpu.VMEM((r, 128), BF16),
                        pltpu.VMEM((r, 128), F32), pltpu.VMEM((r, 128), F32)],
        compiler_params=_cp(("arbitrary", "arbitrary", "arbitrary")),
        name="attn",
    )(_slope_table(), pb, pb, pb, _key_aux_table(s), qmask, ocmp, small, pb, _gate_expand_table(),
      _bias_table())


def _merge_kernel(yssm_ref, onsa_ref, mg_ref, x_ref, gate_ref, gpost_ref, wssm_ref, wnsa_ref, wout_ref, o_ref):
    y_ssm = jnp.dot(yssm_ref[0], wssm_ref[...], preferred_element_type=F32)
    y_nsa = jnp.dot(onsa_ref[0], wnsa_ref[...], preferred_element_type=F32)
    mg = _sigmoid(mg_ref[0].astype(F32))
    merged = mg[:, :D_MODEL] * y_ssm + mg[:, D_MODEL:] * y_nsa
    out = jnp.dot(merged.astype(BF16), wout_ref[...], preferred_element_type=F32)
    yn = out * lax.rsqrt(jnp.mean(out * out, axis=-1, keepdims=True) + EPS) * gpost_ref[...]
    o_ref[0] = x_ref[0] + gate_ref[0] * yn


def _merge(yssm, onsa, pb, x, gate, g_post, w_ssm_out, w_nsa_out, w_out):
    bsz, s, _ = x.shape
    tm = 512
    const = lambda shape: pl.BlockSpec(shape, lambda b, i: (0, 0))
    return pl.pallas_call(
        _merge_kernel,
        grid=(bsz, s // tm),
        in_specs=[pl.BlockSpec((1, tm, D_INNER), lambda b, i: (b, i, 0)),
                  pl.BlockSpec((1, tm, ATT_WIDTH), lambda b, i: (b, i, 0)),
                  pl.BlockSpec((1, tm, 2 * D_MODEL), lambda b, i: (b, i, B_MERGE // (2 * D_MODEL))),
                  pl.BlockSpec((1, tm, D_MODEL), lambda b, i: (b, i, 0)),
                  pl.BlockSpec((1, 1, D_MODEL), lambda b, i: (b, 0, 0)),
                  const((1, D_MODEL)),
                  const((D_INNER, D_MODEL)), const((ATT_WIDTH, D_MODEL)), const((D_MODEL, D_MODEL))],
        out_specs=pl.BlockSpec((1, tm, D_MODEL), lambda b, i: (b, i, 0)),
        out_shape=jax.ShapeDtypeStruct((bsz, s, D_MODEL), F32),
        compiler_params=_cp(("arbitrary", "arbitrary")),
        name="merge",
    )(yssm, onsa, pb, x, gate, g_post.reshape(1, D_MODEL), w_ssm_out, w_nsa_out, w_out)


def _split_w_in(w_in):
    def cols(off, n):
        return w_in[:, off:off + n].astype(BF16)

    dm = w_in.shape[0]
    small = jnp.concatenate([cols(IN_DT, SSM_HEADS), cols(IN_GATE, 3 * ATT_HEADS),
                             jnp.zeros((dm, 128 - SSM_HEADS - 3 * ATT_HEADS), BF16)], axis=1)
    hd = ATT_HEAD_DIM

    def interleave(k_off, v_off):
        return [cols(o + gg * hd, hd) for gg in range(ATT_KV_GROUPS) for o in (k_off, v_off)]

    q_scaled = (w_in[:, IN_Q:IN_Q + ATT_WIDTH] * (ATT_HEAD_DIM ** -0.5 * LOG2E)).astype(BF16)
    wb = jnp.concatenate([cols(IN_XBC, D_CONV), cols(IN_ZATT, ATT_WIDTH), cols(IN_MERGE, 2 * D_MODEL),
                          cols(IN_Z, D_INNER), q_scaled]
                         + interleave(IN_KV + 2 * KV_WIDTH, IN_KV + 3 * KV_WIDTH)
                         + interleave(IN_KV + 4 * KV_WIDTH, IN_KV + 5 * KV_WIDTH)
                         + [cols(IN_KV, 2 * KV_WIDTH)], axis=1)
    return wb, small


def _layer(x, c, w_ada, b_ada, g_pre, g_post, w_in, conv_w, conv_b, dt_bias, a_log, d_skip,
           g_ssm_norm, w_ssm_out, cmp_pos_k, cmp_w1_k, cmp_w2_k, cmp_pos_v, cmp_w1_v, cmp_w2_v,
           w_nsa_out, w_out):
    bsz, s, dm = x.shape
    mod = _ada(c, w_ada, b_ada)
    shift = mod[:, None, :dm]
    scale = mod[:, None, dm:2 * dm]
    gate = mod[:, None, 2 * dm:]
    wb, w_small = _split_w_in(w_in)
    pb, small = _proj(x, shift, scale, g_pre, wb, w_small)

    yssm = _ssm(pb, small, conv_w, conv_b, dt_bias, a_log, d_skip, g_ssm_norm)

    kvc = pb[:, :, B_KVC:B_KVC + 2 * KV_WIDTH]
    kvc = kvc.reshape(bsz, s // CMP_STRIDE, CMP_STRIDE, 2 * ATT_KV_GROUPS, ATT_HEAD_DIM)
    kvc = kvc.transpose(0, 3, 1, 2, 4).reshape(bsz, 2 * ATT_KV_GROUPS, s // CMP_STRIDE,
                                               CMP_STRIDE * ATT_HEAD_DIM)
    pos = jnp.stack([cmp_pos_k.reshape(1, -1), cmp_pos_v.reshape(1, -1)])
    w1 = jnp.stack([cmp_w1_k, cmp_w1_v]).astype(BF16)
    w2 = jnp.stack([cmp_w2_k, cmp_w2_v]).astype(BF16)
    cmp_kv = _compress(kvc, pos, w1, w2)

    ocmp, qmask = _cmpattn(pb, cmp_kv, s)
    onsa = _attn(pb, small, qmask, ocmp)
    return _merge(yssm, onsa, pb, x, gate, g_post, w_ssm_out.astype(BF16), w_nsa_out.astype(BF16),
                  w_out.astype(BF16))


@jax.jit
def kernel(x, c, w_ada, b_ada, g_pre, g_post, w_in, conv_w, conv_b, dt_bias, a_log, d_skip, g_ssm_norm,
           w_ssm_out, cmp_pos_k, cmp_w1_k, cmp_w2_k, cmp_pos_v, cmp_w1_v, cmp_w2_v, w_nsa_out, w_out):
    for layer in range(w_in.shape[0]):
        x = _layer(x, c, w_ada[layer], b_ada[layer], g_pre[layer], g_post[layer], w_in[layer],
                   conv_w[layer], conv_b[layer], dt_bias[layer], a_log[layer], d_skip[layer],
                   g_ssm_norm[layer], w_ssm_out[layer], cmp_pos_k[layer], cmp_w1_k[layer],
                   cmp_w2_k[layer], cmp_pos_v[layer], cmp_w1_v[layer], cmp_w2_v[layer],
                   w_nsa_out[layer], w_out[layer])
    return x
```

```python
import numpy as np
import jax
import jax.numpy as jnp
from jax import lax
from jax.experimental import pallas as pl
from jax.experimental.pallas import tpu as pltpu

D_MODEL = 1024
D_INNER = 2048
SSM_HEAD_DIM = 64
SSM_HEADS = 32
SSM_GROUPS = 4
SSM_STATE = 128
SSM_CONV = 4
SSM_CHUNK = 128
D_CONV = D_INNER + 2 * SSM_GROUPS * SSM_STATE
CONV_TAIL = 16

ATT_HEADS = 16
ATT_HEAD_DIM = 64
ATT_KV_GROUPS = 4
ATT_HPG = 4
ATT_WIDTH = 1024
KV_WIDTH = 256
CMP_BLOCK = 32
CMP_STRIDE = 16
CMP_HIDDEN = 256
SLC_BLOCK = 64
SLC_SHIFT = 6
SLC_TOPK = 16
WINDOW = 512
FORCE_BONUS = 1000.0
EPS = 1e-6
NEG = -1e30
BIG = 2.0 ** 100

F32 = jnp.float32
BF16 = jnp.bfloat16
HIGHEST = lax.Precision.HIGHEST
NT = (((1,), (1,)), ((), ()))

_SIZES = (D_INNER, D_CONV, SSM_HEADS, ATT_WIDTH, 6 * KV_WIDTH, 3 * ATT_HEADS, ATT_WIDTH, 2 * D_MODEL)
_OFFS = tuple(int(v) for v in np.cumsum((0,) + _SIZES))
IN_Z, IN_XBC, IN_DT, IN_Q, IN_KV, IN_GATE, IN_ZATT, IN_MERGE = _OFFS[:8]

B_XBC, B_ZATT, B_MERGE, B_ZSSM, B_Q, B_KSEL, B_KWIN, B_KVC = 0, 3072, 4096, 6144, 8192, 9216, 9728, 10240
NB = 10752
SMALL_GATE = 32
LOG2E = 1.4426950408889634

TM_PROJ = 1024
TN_PROJ = 2688
TQ = 512
TK = 512
POS_BASE = 256
TQ_CMP = 1024
VMEM_LIMIT = 48 * 1024 * 1024

L_POS = 64
L_TILE = 67
L_MASK = 72
SLOPE_STRIDE = 8


def _cp(sem):
    return pltpu.CompilerParams(dimension_semantics=sem, vmem_limit_bytes=VMEM_LIMIT)


def _sigmoid(v):
    return 0.5 * jnp.tanh(0.5 * v) + 0.5


def _silu(v):
    return v * _sigmoid(v)


def _ada_kernel(c_ref, w_ref, b_ref, o_ref):
    o_ref[...] = jnp.dot(c_ref[...], w_ref[...], preferred_element_type=F32,
                         precision=HIGHEST) + b_ref[...]


def _ada(c, w_ada, b_ada):
    bsz = c.shape[0]
    return pl.pallas_call(
        _ada_kernel,
        grid=(3,),
        in_specs=[pl.BlockSpec((bsz, D_MODEL), lambda j: (0, 0)),
                  pl.BlockSpec((D_MODEL, D_MODEL), lambda j: (0, j)),
                  pl.BlockSpec((1, D_MODEL), lambda j: (0, j))],
        out_specs=pl.BlockSpec((bsz, D_MODEL), lambda j: (0, j)),
        out_shape=jax.ShapeDtypeStruct((bsz, 3 * D_MODEL), F32),
        compiler_params=_cp(("arbitrary",)),
        name="ada",
    )(c, w_ada, b_ada.reshape(1, 3 * D_MODEL))


def _proj_kernel(x_ref, shift_ref, scale_ref, g_ref, w_ref, ws_ref, o_ref, osm_ref, h_scr):
    @pl.when(pl.program_id(2) == 0)
    def _():
        xf = x_ref[0]
        y = xf * lax.rsqrt(jnp.mean(xf * xf, axis=-1, keepdims=True) + EPS) * g_ref[...]
        h_scr[...] = (y * (1.0 + scale_ref[0]) + shift_ref[0]).astype(BF16)
        osm_ref[0] = jnp.dot(h_scr[...], ws_ref[...], preferred_element_type=F32)

    o_ref[0] = jnp.dot(h_scr[...], w_ref[...], preferred_element_type=F32).astype(o_ref.dtype)


def _proj(x, shift, scale, g_pre, w, w_small):
    bsz, s, _ = x.shape
    n = w.shape[1]
    return pl.pallas_call(
        _proj_kernel,
        grid=(bsz, s // TM_PROJ, n // TN_PROJ),
        in_specs=[pl.BlockSpec((1, TM_PROJ, D_MODEL), lambda b, i, j: (b, i, 0)),
                  pl.BlockSpec((1, 1, D_MODEL), lambda b, i, j: (b, 0, 0)),
                  pl.BlockSpec((1, 1, D_MODEL), lambda b, i, j: (b, 0, 0)),
                  pl.BlockSpec((1, D_MODEL), lambda b, i, j: (0, 0)),
                  pl.BlockSpec((D_MODEL, TN_PROJ), lambda b, i, j: (0, j)),
                  pl.BlockSpec((D_MODEL, 128), lambda b, i, j: (0, 0))],
        out_specs=[pl.BlockSpec((1, TM_PROJ, TN_PROJ), lambda b, i, j: (b, i, j)),
                   pl.BlockSpec((1, TM_PROJ, 128), lambda b, i, j: (b, i, 0))],
        out_shape=[jax.ShapeDtypeStruct((bsz, s, n), BF16),
                   jax.ShapeDtypeStruct((bsz, s, 128), F32)],
        scratch_shapes=[pltpu.VMEM((TM_PROJ, D_MODEL), BF16)],
        compiler_params=_cp(("arbitrary", "arbitrary", "arbitrary")),
        name="proj",
    )(x, shift, scale, g_pre.reshape(1, D_MODEL), w, w_small)


def _ssm_kernel(xbc_ref, z_ref, small_ref, shift_ref, spread_ref, convw_ref, convb_ref, dtb_ref,
                alog_ref, dskip_ref, gn_ref, o_ref, tail_scr, state_scr, y_scr):
    cl = SSM_CHUNK

    @pl.when(pl.program_id(1) == 0)
    def _():
        tail_scr[...] = jnp.zeros(tail_scr.shape, BF16)
        state_scr[...] = jnp.zeros(state_scr.shape, F32)

    cur = xbc_ref[0]
    xpad = jnp.concatenate([tail_scr[...], cur], axis=0)
    delayed = jnp.dot(shift_ref[...], xpad, preferred_element_type=F32)
    tail_scr[...] = cur[cl - CONV_TAIL:cl, :]
    acc = convb_ref[...] + convw_ref[SSM_CONV - 1:SSM_CONV, :] * cur.astype(F32)
    for k in range(SSM_CONV - 1):
        acc = acc + convw_ref[k:k + 1, :] * delayed[k * cl:(k + 1) * cl]
    u = _silu(acc)
    xs = u[:, :D_INNER]
    bm = u[:, D_INNER:D_INNER + SSM_GROUPS * SSM_STATE]
    cm = u[:, D_INNER + SSM_GROUPS * SSM_STATE:]

    pre = small_ref[0] + dtb_ref[...]
    dt = jnp.maximum(pre, 0.0) + jnp.log1p(jnp.exp(-jnp.abs(pre)))
    a = -jnp.exp(alog_ref[...])
    adt = dt * a
    row = lax.broadcasted_iota(jnp.int32, (cl, cl), 0)
    col = lax.broadcasted_iota(jnp.int32, (cl, cl), 1)
    causal = row >= col
    tri = causal.astype(F32)
    a_cs = jnp.dot(tri, adt, preferred_element_type=F32, precision=HIGHEST)
    a_cs_t = a_cs.T
    dt_t = dt.T
    a_last = a_cs[cl - 1:cl, :]
    ea = jnp.exp(a_cs)
    dsc = jnp.exp(a_last - a_cs) * dt

    def spread(v):
        hi = v.astype(BF16)
        lo = (v - hi.astype(F32)).astype(BF16)
        return jnp.dot(jnp.concatenate([hi, lo], axis=1), spread_ref[...], preferred_element_type=F32)

    ea_x = spread(ea)
    dsc_x = spread(dsc)
    cdec_x = ea_x[cl - 1:cl, :]
    xsd = xs * dsc_x
    lane = lax.broadcasted_iota(jnp.int32, (cl, 2 * SSM_HEAD_DIM), 1)
    first_half = lane < SSM_HEAD_DIM

    hg = SSM_HEADS // SSM_GROUPS
    gw = hg * SSM_HEAD_DIM
    for g in range(SSM_GROUPS):
        bg = bm[:, g * SSM_STATE:(g + 1) * SSM_STATE].astype(BF16)
        cg = cm[:, g * SSM_STATE:(g + 1) * SSM_STATE].astype(BF16)
        gc = slice(g * gw, (g + 1) * gw)
        cb = lax.dot_general(cg, bg, NT, preferred_element_type=F32)
        st = state_scr[g]
        y_scr[:, gc] = jnp.dot(cg, st.astype(BF16), preferred_element_type=F32) * ea_x[:, gc]
        new = lax.dot_general(bg, xsd[:, gc].astype(BF16), (((0,), (0,)), ((), ())),
                              preferred_element_type=F32)
        state_scr[g] = st * cdec_x[:, gc] + new
        for pp in range(hg // 2):
            pair = g * (hg // 2) + pp
            pc = slice(pair * 128, (pair + 1) * 128)
            gm = []
            for e in range(2):
                h = 2 * pair + e
                seg = a_cs[:, h:h + 1] - a_cs_t[h:h + 1, :]
                lmat = jnp.exp(jnp.where(causal, seg, NEG))
                gm.append(cb * lmat * dt_t[h:h + 1, :])
            lhs = jnp.concatenate(gm, axis=0).astype(BF16)
            yy = jnp.dot(lhs, xs[:, pc].astype(BF16), preferred_element_type=F32)
            y_scr[:, pc] = y_scr[:, pc] + jnp.where(first_half, yy[:cl], yy[cl:])

    y = y_scr[...] + xs * dskip_ref[...]
    y = y * _silu(z_ref[0].astype(F32))
    gsz = D_INNER // SSM_GROUPS
    for g in range(SSM_GROUPS):
        yg = y[:, g * gsz:(g + 1) * gsz]
        yn = yg * lax.rsqrt(jnp.mean(yg * yg, axis=-1, keepdims=True) + EPS)
        o_ref[0, :, g * gsz:(g + 1) * gsz] = (yn * gn_ref[:, g * gsz:(g + 1) * gsz]).astype(o_ref.dtype)


def _conv_shift_matrix():
    cl = SSM_CHUNK
    m = np.zeros(((SSM_CONV - 1) * cl, CONV_TAIL + cl), np.float32)
    t = np.arange(cl)
    for k in range(SSM_CONV - 1):
        m[k * cl + t, CONV_TAIL + t - (SSM_CONV - 1 - k)] = 1.0
    return jnp.asarray(m, dtype=BF16)


def _head_spread_matrix():
    m = np.zeros((2, 128, D_INNER), np.float32)
    for h in range(SSM_HEADS):
        m[:, h, h * SSM_HEAD_DIM:(h + 1) * SSM_HEAD_DIM] = 1.0
    return jnp.asarray(m.reshape(256, D_INNER), dtype=BF16)


def _ssm(pb, small, conv_w, conv_b, dt_bias, a_log, d_skip, g_norm):
    bsz, s, _ = pb.shape
    cl = SSM_CHUNK
    pad = 128 - SSM_HEADS
    dtb = jnp.pad(dt_bias, (0, pad)).reshape(1, 128)
    alog = jnp.pad(a_log, (0, pad)).reshape(1, 128)
    dskip = jnp.repeat(d_skip, SSM_HEAD_DIM).reshape(1, D_INNER)
    const = lambda shape: pl.BlockSpec(shape, lambda b, c: (0, 0))
    return pl.pallas_call(
        _ssm_kernel,
        grid=(bsz, s // cl),
        in_specs=[pl.BlockSpec((1, cl, D_CONV), lambda b, c: (b, c, B_XBC // D_CONV)),
                  pl.BlockSpec((1, cl, D_INNER), lambda b, c: (b, c, B_ZSSM // D_INNER)),
                  pl.BlockSpec((1, cl, 128), lambda b, c: (b, c, 0)),
                  const(((SSM_CONV - 1) * cl, CONV_TAIL + cl)), const((256, D_INNER)),
                  const((SSM_CONV, D_CONV)), const((1, D_CONV)), const((1, 128)), const((1, 128)),
                  const((1, D_INNER)), const((1, D_INNER))],
        out_specs=pl.BlockSpec((1, cl, D_INNER), lambda b, c: (b, c, 0)),
        out_shape=jax.ShapeDtypeStruct((bsz, s, D_INNER), BF16),
        scratch_shapes=[pltpu.VMEM((CONV_TAIL, D_CONV), BF16),
                        pltpu.VMEM((SSM_GROUPS, SSM_STATE, D_INNER // SSM_GROUPS), F32),
                        pltpu.VMEM((cl, D_INNER), F32)],
        compiler_params=_cp(("arbitrary", "arbitrary")),
        name="ssm",
    )(pb, pb, small, _conv_shift_matrix(), _head_spread_matrix(), conv_w, conv_b.reshape(1, D_CONV),
      dtb, alog, dskip,
      g_norm.reshape(1, D_INNER))


def _cmp_kernel(r_ref, pos_ref, w1_ref, w2_ref, o_ref):
    half = CMP_STRIDE * ATT_HEAD_DIM
    r = r_ref[0, 0]
    w1 = w1_ref[0]
    first = jnp.dot(r, w1[:half], preferred_element_type=F32)
    second = jnp.dot(r, w1[half:], preferred_element_type=F32)
    posb = jnp.broadcast_to(pos_ref[0], (8, 2 * half)).astype(BF16)
    cpos = jnp.dot(posb, w1, preferred_element_type=F32)[0:1]
    nrow = r.shape[0]
    hid = first + pltpu.roll(second, nrow - 1, 0) + cpos
    out = jnp.dot(_silu(hid).astype(BF16), w2_ref[0], preferred_element_type=F32)
    o_ref[0, 0] = out


def _compress(kvc_r, pos, w1, w2):
    bsz, _, nrow, _ = kvc_r.shape
    g = ATT_KV_GROUPS
    return pl.pallas_call(
        _cmp_kernel,
        grid=(bsz, 2 * g),
        in_specs=[pl.BlockSpec((1, 1, nrow, 1024), lambda b, j: (b, j, 0, 0)),
                  pl.BlockSpec((1, 1, 2048), lambda b, j: (j // g, 0, 0)),
                  pl.BlockSpec((1, 2048, CMP_HIDDEN), lambda b, j: (j // g, 0, 0)),
                  pl.BlockSpec((1, CMP_HIDDEN, ATT_HEAD_DIM), lambda b, j: (j // g, 0, 0))],
        out_specs=pl.BlockSpec((1, 1, nrow, ATT_HEAD_DIM), lambda b, j: (b, j, 0, 0)),
        out_shape=jax.ShapeDtypeStruct((bsz, 2 * g, nrow, ATT_HEAD_DIM), F32),
        compiler_params=_cp(("arbitrary", "arbitrary")),
        name="cmp",
    )(kvc_r, pos, w1, w2)


def _bf16_round_np(x):
    u = np.asarray(x, np.float32).view(np.uint32)
    u = (u + (((u >> 16) & 1) + 0x7FFF)) & np.uint32(0xFFFF0000)
    return u.view(np.float32)


def _slope_table():
    slope = (2.0 ** (-8.0 * np.arange(1, ATT_HEADS + 1) / ATT_HEADS)).astype(np.float32)
    slope = (slope.astype(np.float64) * LOG2E).astype(np.float32)
    p0 = _bf16_round_np(slope)
    p1 = _bf16_round_np(slope - p0)
    p2 = _bf16_round_np(slope - p0 - p1)
    tab = np.zeros((ATT_HEADS, SLOPE_STRIDE), np.float32)
    tab[:, 0], tab[:, 1], tab[:, 2] = p0, p1, p2
    tab[:, 3:6] = tab[:, 0:3] * POS_BASE
    tab[:, 6] = slope
    return jnp.asarray(tab.reshape(-1))


def _key_aux_table(s):
    pos = np.arange(s)
    tab = np.zeros((s, 128), np.float32)
    tab[:, L_POS:L_POS + 3] = (pos % POS_BASE)[:, None]
    tab[:, L_TILE:L_TILE + 3] = (pos // POS_BASE)[:, None]
    tab[pos, L_MASK + pos // SLC_BLOCK] = 1.0
    return jnp.asarray(tab, dtype=BF16)


def _bias_table():
    r = np.arange(TQ)[:, None]
    c = np.arange(TK)[None, :]
    tabs = []
    for jj in range(2):
        d = (1 - jj) * TK + r - c
        tabs.append(np.where((d >= 0) & (d < WINDOW), 0.0, NEG))
    tabs.append(np.where(r - c >= 0, 0.0, NEG))
    return jnp.asarray(np.stack(tabs), dtype=F32)


def _gate_expand_table():
    tab = np.zeros((ATT_KV_GROUPS, 2, 128, 3 * KV_WIDTH), np.float32)
    for g in range(ATT_KV_GROUPS):
        for h in range(ATT_HPG):
            for j in range(3):
                row = SMALL_GATE + g * 3 * ATT_HPG + 3 * h + j
                tab[g, :, row, j * KV_WIDTH + h * ATT_HEAD_DIM:j * KV_WIDTH + (h + 1) * ATT_HEAD_DIM] = 1.0
    return jnp.asarray(tab.reshape(ATT_KV_GROUPS, 256, 3 * KV_WIDTH), dtype=BF16)


def _cmp_to_slc_matrix_t(n_cmp_pad, n_cmp, n_slc):
    cs = np.arange(n_cmp_pad) * CMP_STRIDE
    ss = np.arange(n_slc) * SLC_BLOCK
    lo = np.maximum(cs[:, None], ss[None, :])
    hi = np.minimum(cs[:, None] + CMP_BLOCK, ss[None, :] + SLC_BLOCK)
    m = np.clip(hi - lo, 0, None) / CMP_BLOCK
    m[n_cmp:] = 0.0
    return jnp.asarray(m.T, dtype=F32)


def _cmpattn_kernel(slopes_ref, q_ref, ck_ref, cv_ref, mt_ref, ocmp_ref, qmask_ref,
                    khi_scr, klo_scr, vbd_scr):
    g = pl.program_id(1)
    i = pl.program_id(2)
    tq = q_ref.shape[1]
    t0 = i * tq
    npad = ck_ref.shape[2]
    n_slc = mt_ref.shape[0]

    @pl.when(i == 0)
    def _():
        kc = ck_ref[0, 0]
        vc = cv_ref[0, 0]
        k_hi = kc.astype(BF16).astype(F32)
        k_lo = kc - k_hi
        zero = jnp.zeros_like(kc)
        for h in range(ATT_HPG):
            def bd(a):
                return jnp.concatenate([a if hh == h else zero for hh in range(ATT_HPG)], axis=1).astype(BF16)
            khi_scr[h * npad:(h + 1) * npad, :] = bd(k_hi)
            klo_scr[h * npad:(h + 1) * npad, :] = bd(k_lo)
            vbd_scr[h * npad:(h + 1) * npad, :] = bd(vc)

    q = q_ref[0]
    s = (lax.dot_general(q, khi_scr[...], NT, preferred_element_type=F32)
         + lax.dot_general(q, klo_scr[...], NT, preferred_element_type=F32))
    jcol = lax.broadcasted_iota(jnp.int32, (1, npad), 1)
    cmp_end = jcol * CMP_STRIDE + (CMP_BLOCK - 1)
    trow = t0 + lax.broadcasted_iota(jnp.int32, (tq, npad), 0)
    valid = (cmp_end <= trow) & (jcol < npad - 1)
    rel = (cmp_end - t0).astype(F32)
    psum = jnp.zeros((tq, npad), F32)
    ps = []
    for h in range(ATT_HPG):
        slope = slopes_ref[(g * ATT_HPG + h) * SLOPE_STRIDE + 6]
        sh = jnp.where(valid, s[:, h * npad:(h + 1) * npad] + slope * rel, NEG)
        mx = jnp.max(sh, axis=-1, keepdims=True)
        p = jnp.where(valid, jnp.exp2(sh - mx), 0.0)
        l = jnp.sum(p, axis=-1, keepdims=True)
        p = p * jnp.where(l > 0.0, 1.0 / jnp.where(l > 0.0, l, 1.0), 0.0)
        ps.append(p)
        psum = psum + p
    ocmp_ref[0] = jnp.dot(jnp.concatenate(ps, axis=1).astype(BF16), vbd_scr[...],
                          preferred_element_type=F32)
    imp_t = lax.dot_general(mt_ref[...], psum, NT, preferred_element_type=F32,
                            precision=HIGHEST)

    nrb = n_slc // 8
    blk_t = (t0 + lax.broadcasted_iota(jnp.int32, (8, tq), 1)) >> SLC_SHIFT
    sub = lax.broadcasted_iota(jnp.int32, (8, tq), 0)
    score = []
    for rb in range(nrb):
        kk = sub + rb * 8
        imp = imp_t[rb * 8:(rb + 1) * 8]
        forced = (kk == 0) | (kk == blk_t) | (kk == blk_t - 1)
        score.append(jnp.where(forced, imp + FORCE_BONUS, jnp.where(kk <= blk_t, imp, -1.0)))
    rank = [jnp.zeros((8, tq), F32) for _ in range(nrb)]
    for j in range(n_slc):
        cj = jnp.broadcast_to(score[j // 8][j % 8:j % 8 + 1, :], (8, tq))
        for rb in range(nrb):
            ge = jnp.where(cj >= score[rb], 1.0, 0.0)
            gt = jnp.where(cj > score[rb], 1.0, 0.0)
            if rb * 8 > j:
                beats = ge
            elif rb * 8 + 7 <= j:
                beats = gt
            else:
                beats = jnp.where(sub > j - rb * 8, ge, gt)
            rank[rb] = rank[rb] + beats
    rows = [jnp.zeros((L_MASK, tq), F32)]
    rows += [jnp.where(rank[rb] < float(SLC_TOPK), 0.0, -BIG) for rb in range(nrb)]
    rows += [jnp.zeros((128 - L_MASK - n_slc, tq), F32)]
    qmask_ref[0, 0] = jnp.concatenate(rows, axis=0).T.astype(BF16)


def _cmpattn(pb, cmp_kv, s):
    bsz = pb.shape[0]
    g = ATT_KV_GROUPS
    nrow = cmp_kv.shape[2]
    n_cmp = (s - CMP_BLOCK) // CMP_STRIDE + 1
    n_slc = s // SLC_BLOCK
    mt = _cmp_to_slc_matrix_t(nrow, n_cmp, n_slc)
    return pl.pallas_call(
        _cmpattn_kernel,
        grid=(bsz, g, s // TQ_CMP),
        in_specs=[pl.BlockSpec(memory_space=pltpu.SMEM),
                  pl.BlockSpec((1, TQ_CMP, KV_WIDTH), lambda b, gg, i: (b, i, B_Q // KV_WIDTH + gg)),
                  pl.BlockSpec((1, 1, nrow, ATT_HEAD_DIM), lambda b, gg, i: (b, gg, 0, 0)),
                  pl.BlockSpec((1, 1, nrow, ATT_HEAD_DIM), lambda b, gg, i: (b, g + gg, 0, 0)),
                  pl.BlockSpec((n_slc, nrow), lambda b, gg, i: (0, 0))],
        out_specs=[pl.BlockSpec((1, TQ_CMP, KV_WIDTH), lambda b, gg, i: (b, i, gg)),
                   pl.BlockSpec((1, 1, TQ_CMP, 128), lambda b, gg, i: (b, gg, i, 0))],
        out_shape=[jax.ShapeDtypeStruct((bsz, s, ATT_WIDTH), F32),
                   jax.ShapeDtypeStruct((bsz, g, s, 128), BF16)],
        scratch_shapes=[pltpu.VMEM((ATT_HPG * nrow, KV_WIDTH), BF16),
                        pltpu.VMEM((ATT_HPG * nrow, KV_WIDTH), BF16),
                        pltpu.VMEM((ATT_HPG * nrow, KV_WIDTH), BF16)],
        compiler_params=_cp(("arbitrary", "arbitrary", "arbitrary")),
        name="cmpattn",
    )(_slope_table(), pb, cmp_kv, cmp_kv, mt)


def _flash_step(qa_ref, kv_ref, aux_ref, m_scr, acc_scr, j, ntile, bias, first=False):
    r = qa_ref.shape[0]
    nk = ntile * TK
    start = pl.multiple_of(j * TK, TK)
    kv = kv_ref[0, pl.ds(start, nk), :]
    left = jnp.where(lax.broadcasted_iota(jnp.int32, (nk, 128), 1) < ATT_HEAD_DIM, 1.0, 0.0).astype(BF16)
    k_aug = kv * left + aux_ref[pl.ds(start, nk), :]
    v_aug = kv * (1.0 - left) + left
    s = lax.dot_general(qa_ref[...], k_aug, NT, preferred_element_type=F32)
    if bias is not None:
        parts = []
        for t, b in enumerate(bias):
            st = s[:, t * TK:(t + 1) * TK]
            if b is not None:
                st = (st.reshape(ATT_HPG, TQ, TK) + b[None]).reshape(r, TK)
            parts.append(st)
        s = jnp.concatenate(parts, axis=1) if ntile > 1 else parts[0]
    m_cur = jnp.max(s, axis=-1, keepdims=True)
    if first:
        m_new = jnp.broadcast_to(m_cur, (r, 128))
    else:
        m_prev = m_scr[...]
        m_new = jnp.maximum(m_prev, m_cur)
    p = jnp.exp2(s - jnp.concatenate([m_new] * (nk // 128), axis=1))
    pv = jnp.dot(p.astype(BF16), v_aug, preferred_element_type=F32)
    acc_scr[...] = pv if first else jnp.exp2(m_prev - m_new) * acc_scr[...] + pv
    m_scr[...] = m_new


def _flash_out(acc_scr):
    left = lax.broadcasted_iota(jnp.int32, (TQ, 128), 1) < ATT_HEAD_DIM
    slabs = []
    for pair in range(ATT_HPG // 2):
        a0 = acc_scr[(2 * pair) * TQ:(2 * pair + 1) * TQ, :]
        a1 = acc_scr[(2 * pair + 1) * TQ:(2 * pair + 2) * TQ, :]
        r0 = pltpu.roll(a0, ATT_HEAD_DIM, 1)
        r1 = pltpu.roll(a1, ATT_HEAD_DIM, 1)
        slabs.append(jnp.where(left, r0, a1) / jnp.where(left, a0, r1))
    return jnp.concatenate(slabs, axis=1)


def _attn_kernel(slopes_ref, q_ref, ksel_ref, kwin_ref, aux_ref, qmask_ref, ocmp_ref, small_ref,
                 zatt_ref, pick_ref, bias_ref, o_ref, qas_scr, qaw_scr, m_scr, acc_scr):
    g = pl.program_id(1)
    i = pl.program_id(2)

    qf = q_ref[0].astype(F32)
    qm = qmask_ref[0, 0].astype(F32)
    lane_row = lax.broadcasted_iota(jnp.int32, (1, 128), 1)
    left = lax.broadcasted_iota(jnp.int32, (TQ, 128), 1) < ATT_HEAD_DIM
    for h in range(ATT_HPG):
        slab = qf[:, 128 * (h // 2):128 * (h // 2) + 128]
        if h % 2:
            slab = pltpu.roll(slab, ATT_HEAD_DIM, 1)
        ext = jnp.zeros((1, 128), F32)
        for c in range(6):
            ext = jnp.where(lane_row == L_POS + c, slopes_ref[(g * ATT_HPG + h) * SLOPE_STRIDE + c], ext)
        base = jnp.where(left, slab, ext)
        qaw_scr[h * TQ:(h + 1) * TQ, :] = base.astype(BF16)
        qas_scr[h * TQ:(h + 1) * TQ, :] = (base + qm).astype(BF16)

    _flash_step(qas_scr, ksel_ref, aux_ref, m_scr, acc_scr, i, 1, [bias_ref[2]], first=True)

    def sel_body(j, carry):
        _flash_step(qas_scr, ksel_ref, aux_ref, m_scr, acc_scr, j, 1, None)
        return carry

    lax.fori_loop(0, i, sel_body, 0)
    o_slc = _flash_out(acc_scr)

    _flash_step(qaw_scr, kwin_ref, aux_ref, m_scr, acc_scr, i, 1, [bias_ref[1]], first=True)

    @pl.when(i >= 1)
    def _():
        _flash_step(qaw_scr, kwin_ref, aux_ref, m_scr, acc_scr, i - 1, 1, [bias_ref[0]])

    o_win = _flash_out(acc_scr)

    gate = _sigmoid(small_ref[0])
    g_hi = gate.astype(BF16)
    g_lo = (gate - g_hi.astype(F32)).astype(BF16)
    gts = jnp.dot(jnp.concatenate([g_hi, g_lo], axis=1), pick_ref[0],
                  preferred_element_type=F32)
    o = (gts[:, :KV_WIDTH] * ocmp_ref[0] + gts[:, KV_WIDTH:2 * KV_WIDTH] * o_slc
         + gts[:, 2 * KV_WIDTH:] * o_win)
    o_ref[0] = (o * _silu(zatt_ref[0].astype(F32))).astype(o_ref.dtype)


def _attn(pb, small, qmask, ocmp):
    bsz, s, _ = pb.shape
    g = ATT_KV_GROUPS
    r = ATT_HPG * TQ
    return pl.pallas_call(
        _attn_kernel,
        grid=(bsz, g, s // TQ),
        in_specs=[pl.BlockSpec(memory_space=pltpu.SMEM),
                  pl.BlockSpec((1, TQ, KV_WIDTH), lambda b, gg, i: (b, i, B_Q // KV_WIDTH + gg)),
                  pl.BlockSpec((1, s, 128), lambda b, gg, i: (b, 0, B_KSEL // 128 + gg)),
                  pl.BlockSpec((1, s, 128), lambda b, gg, i: (b, 0, B_KWIN // 128 + gg)),
                  pl.BlockSpec((s, 128), lambda b, gg, i: (0, 0)),
                  pl.BlockSpec((1, 1, TQ, 128), lambda b, gg, i: (b, gg, i, 0)),
                  pl.BlockSpec((1, TQ, KV_WIDTH), lambda b, gg, i: (b, i, gg)),
                  pl.BlockSpec((1, TQ, 128), lambda b, gg, i: (b, i, 0)),
                  pl.BlockSpec((1, TQ, KV_WIDTH), lambda b, gg, i: (b, i, B_ZATT // KV_WIDTH + gg)),
                  pl.BlockSpec((1, 256, 3 * KV_WIDTH), lambda b, gg, i: (gg, 0, 0)),
                  pl.BlockSpec((3, TQ, TK), lambda b, gg, i: (0, 0, 0))],
        out_specs=pl.BlockSpec((1, TQ, KV_WIDTH), lambda b, gg, i: (b, i, gg)),
        out_shape=jax.ShapeDtypeStruct((bsz, s, ATT_WIDTH), BF16),
        scratch_shapes=[pltpu.VMEM((r, 128), BF16), pltpu.VMEM((r, 128), BF16),
                        pltpu.VMEM((r, 128), F32), pltpu.VMEM((r, 128), F32)],
        compiler_params=_cp(("arbitrary", "arbitrary", "arbitrary")),
        name="attn",
    )(_slope_table(), pb, pb, pb, _key_aux_table(s), qmask, ocmp, small, pb, _gate_expand_table(),
      _bias_table())


def _merge_kernel(yssm_ref, onsa_ref, mg_ref, x_ref, gate_ref, gpost_ref, wssm_ref, wnsa_ref, wout_ref, o_ref):
    y_ssm = jnp.dot(yssm_ref[0], wssm_ref[...], preferred_element_type=F32)
    y_nsa = jnp.dot(onsa_ref[0], wnsa_ref[...], preferred_element_type=F32)
    mg = _sigmoid(mg_ref[0].astype(F32))
    merged = mg[:, :D_MODEL] * y_ssm + mg[:, D_MODEL:] * y_nsa
    out = jnp.dot(merged.astype(BF16), wout_ref[...], preferred_element_type=F32)
    yn = out * lax.rsqrt(jnp.mean(out * out, axis=-1, keepdims=True) + EPS) * gpost_ref[...]
    o_ref[0] = x_ref[0] + gate_ref[0] * yn


def _merge(yssm, onsa, pb, x, gate, g_post, w_ssm_out, w_nsa_out, w_out):
    bsz, s, _ = x.shape
    tm = 512
    const = lambda shape: pl.BlockSpec(shape, lambda b, i: (0, 0))
    return pl.pallas_call(
        _merge_kernel,
        grid=(bsz, s // tm),
        in_specs=[pl.BlockSpec((1, tm, D_INNER), lambda b, i: (b, i, 0)),
                  pl.BlockSpec((1, tm, ATT_WIDTH), lambda b, i: (b, i, 0)),
                  pl.BlockSpec((1, tm, 2 * D_MODEL), lambda b, i: (b, i, B_MERGE // (2 * D_MODEL))),
                  pl.BlockSpec((1, tm, D_MODEL), lambda b, i: (b, i, 0)),
                  pl.BlockSpec((1, 1, D_MODEL), lambda b, i: (b, 0, 0)),
                  const((1, D_MODEL)),
                  const((D_INNER, D_MODEL)), const((ATT_WIDTH, D_MODEL)), const((D_MODEL, D_MODEL))],
        out_specs=pl.BlockSpec((1, tm, D_MODEL), lambda b, i: (b, i, 0)),
        out_shape=jax.ShapeDtypeStruct((bsz, s, D_MODEL), F32),
        compiler_params=_cp(("arbitrary", "arbitrary")),
        name="merge",
    )(yssm, onsa, pb, x, gate, g_post.reshape(1, D_MODEL), w_ssm_out, w_nsa_out, w_out)


def _split_w_in(w_in):
    def cols(off, n):
        return w_in[:, off:off + n].astype(BF16)

    dm = w_in.shape[0]
    small = jnp.concatenate([cols(IN_DT, SSM_HEADS), cols(IN_GATE, 3 * ATT_HEADS),
                             jnp.zeros((dm, 128 - SSM_HEADS - 3 * ATT_HEADS), BF16)], axis=1)
    hd = ATT_HEAD_DIM

    def interleave(k_off, v_off):
        return [cols(o + gg * hd, hd) for gg in range(ATT_KV_GROUPS) for o in (k_off, v_off)]

    q_scaled = (w_in[:, IN_Q:IN_Q + ATT_WIDTH] * (ATT_HEAD_DIM ** -0.5 * LOG2E)).astype(BF16)
    wb = jnp.concatenate([cols(IN_XBC, D_CONV), cols(IN_ZATT, ATT_WIDTH), cols(IN_MERGE, 2 * D_MODEL),
                          cols(IN_Z, D_INNER), q_scaled]
                         + interleave(IN_KV + 2 * KV_WIDTH, IN_KV + 3 * KV_WIDTH)
                         + interleave(IN_KV + 4 * KV_WIDTH, IN_KV + 5 * KV_WIDTH)
                         + [cols(IN_KV, 2 * KV_WIDTH)], axis=1)
    return wb, small


def _layer(x, c, w_ada, b_ada, g_pre, g_post, w_in, conv_w, conv_b, dt_bias, a_log, d_skip,
           g_ssm_norm, w_ssm_out, cmp_pos_k, cmp_w1_k, cmp_w2_k, cmp_pos_v, cmp_w1_v, cmp_w2_v,
           w_nsa_out, w_out):
    bsz, s, dm = x.shape
    mod = _ada(c, w_ada, b_ada)
    shift = mod[:, None, :dm]
    scale = mod[:, None, dm:2 * dm]
    gate = mod[:, None, 2 * dm:]
    wb, w_small = _split_w_in(w_in)
    pb, small = _proj(x, shift, scale, g_pre, wb, w_small)

    yssm = _ssm(pb, small, conv_w, conv_b, dt_bias, a_log, d_skip, g_ssm_norm)

    kvc = pb[:, :, B_KVC:B_KVC + 2 * KV_WIDTH]
    kvc = kvc.reshape(bsz, s // CMP_STRIDE, CMP_STRIDE, 2 * ATT_KV_GROUPS, ATT_HEAD_DIM)
    kvc = kvc.transpose(0, 3, 1, 2, 4).reshape(bsz, 2 * ATT_KV_GROUPS, s // CMP_STRIDE,
                                               CMP_STRIDE * ATT_HEAD_DIM)
    pos = jnp.stack([cmp_pos_k.reshape(1, -1), cmp_pos_v.reshape(1, -1)])
    w1 = jnp.stack([cmp_w1_k, cmp_w1_v]).astype(BF16)
    w2 = jnp.stack([cmp_w2_k, cmp_w2_v]).astype(BF16)
    cmp_kv = _compress(kvc, pos, w1, w2)

    ocmp, qmask = _cmpattn(pb, cmp_kv, s)
    onsa = _attn(pb, small, qmask, ocmp)
    return _merge(yssm, onsa, pb, x, gate, g_post, w_ssm_out.astype(BF16), w_nsa_out.astype(BF16),
                  w_out.astype(BF16))


@jax.jit
def kernel(x, c, w_ada, b_ada, g_pre, g_post, w_in, conv_w, conv_b, dt_bias, a_log, d_skip, g_ssm_norm,
           w_ssm_out, cmp_pos_k, cmp_w1_k, cmp_w2_k, cmp_pos_v, cmp_w1_v, cmp_w2_v, w_nsa_out, w_out):
    for layer in range(w_in.shape[0]):
        x = _layer(x, c, w_ada[layer], b_ada[layer], g_pre[layer], g_post[layer], w_in[layer],
                   conv_w[layer], conv_b[layer], dt_bias[layer], a_log[layer], d_skip[layer],
                   g_ssm_norm[layer], w_ssm_out[layer], cmp_pos_k[layer], cmp_w1_k[layer],
                   cmp_w2_k[layer], cmp_pos_v[layer], cmp_w1_v[layer], cmp_w2_v[layer],
                   w_nsa_out[layer], w_out[layer])
    return x
```

```python
import numpy as np
import jax
import jax.numpy as jnp
from jax import lax
from jax.experimental import pallas as pl
from jax.experimental.pallas import tpu as pltpu

D_MODEL = 1024
D_INNER = 2048
SSM_HEAD_DIM = 64
SSM_HEADS = 32
SSM_GROUPS = 4
SSM_STATE = 128
SSM_CONV = 4
SSM_CHUNK = 128
D_CONV = D_INNER + 2 * SSM_GROUPS * SSM_STATE
CONV_TAIL = 16

ATT_HEADS = 16
ATT_HEAD_DIM = 64
ATT_KV_GROUPS = 4
ATT_HPG = 4
ATT_WIDTH = 1024
KV_WIDTH = 256
CMP_BLOCK = 32
CMP_STRIDE = 16
CMP_HIDDEN = 256
SLC_BLOCK = 64
SLC_SHIFT = 6
SLC_TOPK = 16
WINDOW = 512
FORCE_BONUS = 1000.0
EPS = 1e-6
NEG = -1e30
BIG = 2.0 ** 100

F32 = jnp.float32
BF16 = jnp.bfloat16
HIGHEST = lax.Precision.HIGHEST
NT = (((1,), (1,)), ((), ()))

_SIZES = (D_INNER, D_CONV, SSM_HEADS, ATT_WIDTH, 6 * KV_WIDTH, 3 * ATT_HEADS, ATT_WIDTH, 2 * D_MODEL)
_OFFS = tuple(int(v) for v in np.cumsum((0,) + _SIZES))
IN_Z, IN_XBC, IN_DT, IN_Q, IN_KV, IN_GATE, IN_ZATT, IN_MERGE = _OFFS[:8]

B_XBC, B_ZATT, B_MERGE, B_ZSSM, B_Q, B_KSEL, B_KWIN = 0, 3072, 4096, 6144, 8192, 9216, 9728
NB = 10240
SMALL_GATE = 32
LOG2E = 1.4426950408889634

TM_PROJ = 1024
TN_PROJ = 2560
TQ = 512
TK = 512
QH = 256
KT = 256
POS_BASE = 256
TQ_CMP = 1024
VMEM_LIMIT = 48 * 1024 * 1024

L_POS = 64
L_TILE = 67
L_MASK = 72
SLOPE_STRIDE = 8


def _cp(sem):
    return pltpu.CompilerParams(dimension_semantics=sem, vmem_limit_bytes=VMEM_LIMIT)


def _sigmoid(v):
    return 0.5 * jnp.tanh(0.5 * v) + 0.5


def _silu(v):
    return v * _sigmoid(v)


def _ada_kernel(c_ref, w_ref, b_ref, o_ref):
    o_ref[...] = jnp.dot(c_ref[...], w_ref[...], preferred_element_type=F32,
                         precision=HIGHEST) + b_ref[...]


def _ada(c, w_ada, b_ada):
    bsz = c.shape[0]
    return pl.pallas_call(
        _ada_kernel,
        grid=(3,),
        in_specs=[pl.BlockSpec((bsz, D_MODEL), lambda j: (0, 0)),
                  pl.BlockSpec((D_MODEL, D_MODEL), lambda j: (0, j)),
                  pl.BlockSpec((1, D_MODEL), lambda j: (0, j))],
        out_specs=pl.BlockSpec((bsz, D_MODEL), lambda j: (0, j)),
        out_shape=jax.ShapeDtypeStruct((bsz, 3 * D_MODEL), F32),
        compiler_params=_cp(("arbitrary",)),
        name="ada",
    )(c, w_ada, b_ada.reshape(1, 3 * D_MODEL))


def _proj_kernel(x_ref, shift_ref, scale_ref, g_ref, w_ref, ws_ref, wkv_ref, o_ref, osm_ref, okv_ref, h_scr):
    @pl.when(pl.program_id(2) == 0)
    def _():
        xf = x_ref[0]
        y = xf * lax.rsqrt(jnp.mean(xf * xf, axis=-1, keepdims=True) + EPS) * g_ref[...]
        h_scr[...] = (y * (1.0 + scale_ref[0]) + shift_ref[0]).astype(BF16)
        osm_ref[0] = jnp.dot(h_scr[...], ws_ref[...], preferred_element_type=F32)
        okv_ref[0] = jnp.dot(h_scr[...], wkv_ref[...], preferred_element_type=F32).astype(okv_ref.dtype)

    o_ref[0] = jnp.dot(h_scr[...], w_ref[...], preferred_element_type=F32).astype(o_ref.dtype)


def _proj(x, shift, scale, g_pre, w, w_small, w_kvc):
    bsz, s, _ = x.shape
    n = w.shape[1]
    nkv = w_kvc.shape[1]
    return pl.pallas_call(
        _proj_kernel,
        grid=(bsz, s // TM_PROJ, n // TN_PROJ),
        in_specs=[pl.BlockSpec((1, TM_PROJ, D_MODEL), lambda b, i, j: (b, i, 0)),
                  pl.BlockSpec((1, 1, D_MODEL), lambda b, i, j: (b, 0, 0)),
                  pl.BlockSpec((1, 1, D_MODEL), lambda b, i, j: (b, 0, 0)),
                  pl.BlockSpec((1, D_MODEL), lambda b, i, j: (0, 0)),
                  pl.BlockSpec((D_MODEL, TN_PROJ), lambda b, i, j: (0, j)),
                  pl.BlockSpec((D_MODEL, 128), lambda b, i, j: (0, 0)),
                  pl.BlockSpec((D_MODEL, nkv), lambda b, i, j: (0, 0))],
        out_specs=[pl.BlockSpec((1, TM_PROJ, TN_PROJ), lambda b, i, j: (b, i, j)),
                   pl.BlockSpec((1, TM_PROJ, 128), lambda b, i, j: (b, i, 0)),
                   pl.BlockSpec((1, TM_PROJ, nkv), lambda b, i, j: (b, i, 0))],
        out_shape=[jax.ShapeDtypeStruct((bsz, s, n), BF16),
                   jax.ShapeDtypeStruct((bsz, s, 128), F32),
                   jax.ShapeDtypeStruct((bsz, s, nkv), BF16)],
        scratch_shapes=[pltpu.VMEM((TM_PROJ, D_MODEL), BF16)],
        compiler_params=_cp(("arbitrary", "arbitrary", "arbitrary")),
        name="proj",
    )(x, shift, scale, g_pre.reshape(1, D_MODEL), w, w_small, w_kvc)


def _ssm_kernel(xbc_ref, z_ref, small_ref, shift_ref, spread_ref, convw_ref, convb_ref, dtb_ref,
                alog_ref, dskip_ref, gn_ref, o_ref, tail_scr, state_scr, y_scr):
    cl = SSM_CHUNK

    @pl.when(pl.program_id(1) == 0)
    def _():
        tail_scr[...] = jnp.zeros(tail_scr.shape, BF16)
        state_scr[...] = jnp.zeros(state_scr.shape, F32)

    cur = xbc_ref[0]
    xpad = jnp.concatenate([tail_scr[...], cur], axis=0)
    delayed = jnp.dot(shift_ref[...], xpad, preferred_element_type=F32)
    tail_scr[...] = cur[cl - CONV_TAIL:cl, :]
    acc = convb_ref[...] + convw_ref[SSM_CONV - 1:SSM_CONV, :] * cur.astype(F32)
    for k in range(SSM_CONV - 1):
        acc = acc + convw_ref[k:k + 1, :] * delayed[k * cl:(k + 1) * cl]
    u = _silu(acc)
    xs = u[:, :D_INNER]
    bm = u[:, D_INNER:D_INNER + SSM_GROUPS * SSM_STATE]
    cm = u[:, D_INNER + SSM_GROUPS * SSM_STATE:]

    pre = small_ref[0] + dtb_ref[...]
    dt = jnp.maximum(pre, 0.0) + jnp.log1p(jnp.exp(-jnp.abs(pre)))
    a = -jnp.exp(alog_ref[...])
    adt = dt * a
    row = lax.broadcasted_iota(jnp.int32, (cl, cl), 0)
    col = lax.broadcasted_iota(jnp.int32, (cl, cl), 1)
    causal = row >= col
    tri = causal.astype(F32)
    a_cs = jnp.dot(tri, adt, preferred_element_type=F32, precision=HIGHEST)
    a_cs_t = a_cs.T
    dt_t = dt.T
    a_last = a_cs[cl - 1:cl, :]
    ea = jnp.exp(a_cs)
    dsc = jnp.exp(a_last - a_cs) * dt

    def spread(v):
        hi = v.astype(BF16)
        lo = (v - hi.astype(F32)).astype(BF16)
        return jnp.dot(jnp.concatenate([hi, lo], axis=1), spread_ref[...], preferred_element_type=F32)

    ea_x = spread(ea)
    dsc_x = spread(dsc)
    cdec_x = ea_x[cl - 1:cl, :]
    xsd = xs * dsc_x
    lane = lax.broadcasted_iota(jnp.int32, (cl, 2 * SSM_HEAD_DIM), 1)
    first_half = lane < SSM_HEAD_DIM

    hg = SSM_HEADS // SSM_GROUPS
    gw = hg * SSM_HEAD_DIM
    for g in range(SSM_GROUPS):
        bg = bm[:, g * SSM_STATE:(g + 1) * SSM_STATE].astype(BF16)
        cg = cm[:, g * SSM_STATE:(g + 1) * SSM_STATE].astype(BF16)
        gc = slice(g * gw, (g + 1) * gw)
        cb = lax.dot_general(cg, bg, NT, preferred_element_type=F32)
        st = state_scr[g]
        y_scr[:, gc] = jnp.dot(cg, st.astype(BF16), preferred_element_type=F32) * ea_x[:, gc]
        new = lax.dot_general(bg, xsd[:, gc].astype(BF16), (((0,), (0,)), ((), ())),
                              preferred_element_type=F32)
        state_scr[g] = st * cdec_x[:, gc] + new
        for pp in range(hg // 2):
            pair = g * (hg // 2) + pp
            pc = slice(pair * 128, (pair + 1) * 128)
            gm = []
            for e in range(2):
                h = 2 * pair + e
                seg = a_cs[:, h:h + 1] - a_cs_t[h:h + 1, :]
                lmat = jnp.exp(jnp.where(causal, seg, NEG))
                gm.append(cb * lmat * dt_t[h:h + 1, :])
            lhs = jnp.concatenate(gm, axis=0).astype(BF16)
            yy = jnp.dot(lhs, xs[:, pc].astype(BF16), preferred_element_type=F32)
            y_scr[:, pc] = y_scr[:, pc] + jnp.where(first_half, yy[:cl], yy[cl:])

    y = y_scr[...] + xs * dskip_ref[...]
    y = y * _silu(z_ref[0].astype(F32))
    gsz = D_INNER // SSM_GROUPS
    for g in range(SSM_GROUPS):
        yg = y[:, g * gsz:(g + 1) * gsz]
        yn = yg * lax.rsqrt(jnp.mean(yg * yg, axis=-1, keepdims=True) + EPS)
        o_ref[0, :, g * gsz:(g + 1) * gsz] = (yn * gn_ref[:, g * gsz:(g + 1) * gsz]).astype(o_ref.dtype)


def _conv_shift_matrix():
    cl = SSM_CHUNK
    m = np.zeros(((SSM_CONV - 1) * cl, CONV_TAIL + cl), np.float32)
    t = np.arange(cl)
    for k in range(SSM_CONV - 1):
        m[k * cl + t, CONV_TAIL + t - (SSM_CONV - 1 - k)] = 1.0
    return jnp.asarray(m, dtype=BF16)


def _head_spread_matrix():
    m = np.zeros((2, 128, D_INNER), np.float32)
    for h in range(SSM_HEADS):
        m[:, h, h * SSM_HEAD_DIM:(h + 1) * SSM_HEAD_DIM] = 1.0
    return jnp.asarray(m.reshape(256, D_INNER), dtype=BF16)


def _ssm(pb, small, conv_w, conv_b, dt_bias, a_log, d_skip, g_norm):
    bsz, s, _ = pb.shape
    cl = SSM_CHUNK
    pad = 128 - SSM_HEADS
    dtb = jnp.pad(dt_bias, (0, pad)).reshape(1, 128)
    alog = jnp.pad(a_log, (0, pad)).reshape(1, 128)
    dskip = jnp.repeat(d_skip, SSM_HEAD_DIM).reshape(1, D_INNER)
    const = lambda shape: pl.BlockSpec(shape, lambda b, c: (0, 0))
    return pl.pallas_call(
        _ssm_kernel,
        grid=(bsz, s // cl),
        in_specs=[pl.BlockSpec((1, cl, D_CONV), lambda b, c: (b, c, B_XBC // D_CONV)),
                  pl.BlockSpec((1, cl, D_INNER), lambda b, c: (b, c, B_ZSSM // D_INNER)),
                  pl.BlockSpec((1, cl, 128), lambda b, c: (b, c, 0)),
                  const(((SSM_CONV - 1) * cl, CONV_TAIL + cl)), const((256, D_INNER)),
                  const((SSM_CONV, D_CONV)), const((1, D_CONV)), const((1, 128)), const((1, 128)),
                  const((1, D_INNER)), const((1, D_INNER))],
        out_specs=pl.BlockSpec((1, cl, D_INNER), lambda b, c: (b, c, 0)),
        out_shape=jax.ShapeDtypeStruct((bsz, s, D_INNER), BF16),
        scratch_shapes=[pltpu.VMEM((CONV_TAIL, D_CONV), BF16),
                        pltpu.VMEM((SSM_GROUPS, SSM_STATE, D_INNER // SSM_GROUPS), F32),
                        pltpu.VMEM((cl, D_INNER), F32)],
        compiler_params=_cp(("arbitrary", "arbitrary")),
        name="ssm",
    )(pb, pb, small, _conv_shift_matrix(), _head_spread_matrix(), conv_w, conv_b.reshape(1, D_CONV),
      dtb, alog, dskip,
      g_norm.reshape(1, D_INNER))


def _cmp_kernel(r_ref, pos_ref, w1_ref, w2_ref, o_ref):
    half = CMP_STRIDE * ATT_HEAD_DIM
    hd = ATT_HEAD_DIM
    width = 2 * KV_WIDTH
    blk = r_ref[0].astype(F32)
    nrow = blk.shape[0]
    for kv in range(2):
        w1 = w1_ref[kv]
        posb = jnp.broadcast_to(pos_ref[kv], (8, 2 * half)).astype(BF16)
        cpos = jnp.dot(posb, w1, preferred_element_type=F32)[0:1]
        for gg in range(ATT_KV_GROUPS):
            j = kv * ATT_KV_GROUPS + gg
            r = jnp.concatenate([blk[:, t * width + j * hd:t * width + (j + 1) * hd]
                                 for t in range(CMP_STRIDE)], axis=1).astype(BF16)
            first = jnp.dot(r, w1[:half], preferred_element_type=F32)
            second = jnp.dot(r, w1[half:], preferred_element_type=F32)
            hid = first + pltpu.roll(second, nrow - 1, 0) + cpos
            o_ref[0, j] = jnp.dot(_silu(hid).astype(BF16), w2_ref[kv], preferred_element_type=F32)


def _compress(kvc_rows, pos, w1, w2):
    bsz, nrow, wid = kvc_rows.shape
    ng = 2 * ATT_KV_GROUPS
    return pl.pallas_call(
        _cmp_kernel,
        grid=(bsz,),
        in_specs=[pl.BlockSpec((1, nrow, wid), lambda b: (b, 0, 0)),
                  pl.BlockSpec((2, 1, 2048), lambda b: (0, 0, 0)),
                  pl.BlockSpec((2, 2048, CMP_HIDDEN), lambda b: (0, 0, 0)),
                  pl.BlockSpec((2, CMP_HIDDEN, ATT_HEAD_DIM), lambda b: (0, 0, 0))],
        out_specs=pl.BlockSpec((1, ng, nrow, ATT_HEAD_DIM), lambda b: (b, 0, 0, 0)),
        out_shape=jax.ShapeDtypeStruct((bsz, ng, nrow, ATT_HEAD_DIM), F32),
        compiler_params=_cp(("arbitrary",)),
        name="cmp",
    )(kvc_rows, pos, w1, w2)


def _bf16_round_np(x):
    u = np.asarray(x, np.float32).view(np.uint32)
    u = (u + (((u >> 16) & 1) + 0x7FFF)) & np.uint32(0xFFFF0000)
    return u.view(np.float32)


def _slope_table():
    slope = (2.0 ** (-8.0 * np.arange(1, ATT_HEADS + 1) / ATT_HEADS)).astype(np.float32)
    slope = (slope.astype(np.float64) * LOG2E).astype(np.float32)
    p0 = _bf16_round_np(slope)
    p1 = _bf16_round_np(slope - p0)
    p2 = _bf16_round_np(slope - p0 - p1)
    tab = np.zeros((ATT_HEADS, SLOPE_STRIDE), np.float32)
    tab[:, 0], tab[:, 1], tab[:, 2] = p0, p1, p2
    tab[:, 3:6] = tab[:, 0:3] * POS_BASE
    tab[:, 6] = slope
    return jnp.asarray(tab.reshape(-1))


def _key_aux_table(s):
    pos = np.arange(s)
    tab = np.zeros((s, 128), np.float32)
    tab[:, L_POS:L_POS + 3] = (pos % POS_BASE)[:, None]
    tab[:, L_TILE:L_TILE + 3] = (pos // POS_BASE)[:, None]
    tab[pos, L_MASK + pos // SLC_BLOCK] = 1.0
    return jnp.asarray(tab, dtype=BF16)


def _bias_table():
    r = np.arange(QH)[:, None]
    c = np.arange(KT)[None, :]
    tabs = []
    for jj in range(3):
        d = (2 - jj) * KT + r - c
        tabs.append(np.where((d >= 0) & (d < WINDOW), 0.0, NEG))
    tabs.append(np.where(r - c >= 0, 0.0, NEG))
    return jnp.asarray(np.stack(tabs), dtype=F32)


def _gate_expand_table():
    tab = np.zeros((ATT_KV_GROUPS, 2, 128, 3 * KV_WIDTH), np.float32)
    for g in range(ATT_KV_GROUPS):
        for h in range(ATT_HPG):
            for j in range(3):
                row = SMALL_GATE + g * 3 * ATT_HPG + 3 * h + j
                tab[g, :, row, j * KV_WIDTH + h * ATT_HEAD_DIM:j * KV_WIDTH + (h + 1) * ATT_HEAD_DIM] = 1.0
    return jnp.asarray(tab.reshape(ATT_KV_GROUPS, 256, 3 * KV_WIDTH), dtype=BF16)


def _cmp_to_slc_matrix_t(n_cmp_pad, n_cmp, n_slc):
    cs = np.arange(n_cmp_pad) * CMP_STRIDE
    ss = np.arange(n_slc) * SLC_BLOCK
    lo = np.maximum(cs[:, None], ss[None, :])
    hi = np.minimum(cs[:, None] + CMP_BLOCK, ss[None, :] + SLC_BLOCK)
    m = np.clip(hi - lo, 0, None) / CMP_BLOCK
    m[n_cmp:] = 0.0
    return jnp.asarray(m.T, dtype=F32)


def _cmpattn_kernel(slopes_ref, q_ref, ck_ref, cv_ref, mt_ref, ocmp_ref, qmask_ref,
                    khi_scr, klo_scr, vbd_scr):
    g = pl.program_id(1)
    i = pl.program_id(2)
    tq = q_ref.shape[1]
    t0 = i * tq
    npad = ck_ref.shape[2]
    n_slc = mt_ref.shape[0]

    @pl.when(i == 0)
    def _():
        kc = ck_ref[0, 0]
        vc = cv_ref[0, 0]
        k_hi = kc.astype(BF16).astype(F32)
        k_lo = kc - k_hi
        zero = jnp.zeros_like(kc)
        for h in range(ATT_HPG):
            def bd(a):
                return jnp.concatenate([a if hh == h else zero for hh in range(ATT_HPG)], axis=1).astype(BF16)
            khi_scr[h * npad:(h + 1) * npad, :] = bd(k_hi)
            klo_scr[h * npad:(h + 1) * npad, :] = bd(k_lo)
            vbd_scr[h * npad:(h + 1) * npad, :] = bd(vc)

    q = q_ref[0]
    s = (lax.dot_general(q, khi_scr[...], NT, preferred_element_type=F32)
         + lax.dot_general(q, klo_scr[...], NT, preferred_element_type=F32))
    jcol = lax.broadcasted_iota(jnp.int32, (1, npad), 1)
    cmp_end = jcol * CMP_STRIDE + (CMP_BLOCK - 1)
    trow = t0 + lax.broadcasted_iota(jnp.int32, (tq, npad), 0)
    valid = (cmp_end <= trow) & (jcol < npad - 1)
    rel = (cmp_end - t0).astype(F32)
    psum = jnp.zeros((tq, npad), F32)
    ps = []
    for h in range(ATT_HPG):
        slope = slopes_ref[(g * ATT_HPG + h) * SLOPE_STRIDE + 6]
        sh = jnp.where(valid, s[:, h * npad:(h + 1) * npad] + slope * rel, NEG)
        mx = jnp.max(sh, axis=-1, keepdims=True)
        p = jnp.where(valid, jnp.exp2(sh - mx), 0.0)
        l = jnp.sum(p, axis=-1, keepdims=True)
        p = p * jnp.where(l > 0.0, 1.0 / jnp.where(l > 0.0, l, 1.0), 0.0)
        ps.append(p)
        psum = psum + p
    ocmp_ref[0] = jnp.dot(jnp.concatenate(ps, axis=1).astype(BF16), vbd_scr[...],
                          preferred_element_type=F32)
    imp_t = lax.dot_general(mt_ref[...], psum, NT, preferred_element_type=F32,
                            precision=HIGHEST)

    nrb = n_slc // 8
    blk_t = (t0 + lax.broadcasted_iota(jnp.int32, (8, tq), 1)) >> SLC_SHIFT
    sub = lax.broadcasted_iota(jnp.int32, (8, tq), 0)
    score = []
    for rb in range(nrb):
        kk = sub + rb * 8
        imp = imp_t[rb * 8:(rb + 1) * 8]
        forced = (kk == 0) | (kk == blk_t) | (kk == blk_t - 1)
        score.append(jnp.where(forced, imp + FORCE_BONUS, jnp.where(kk <= blk_t, imp, -1.0)))
    rank = [jnp.zeros((8, tq), F32) for _ in range(nrb)]
    for j in range(n_slc):
        cj = jnp.broadcast_to(score[j // 8][j % 8:j % 8 + 1, :], (8, tq))
        for rb in range(nrb):
            ge = jnp.where(cj >= score[rb], 1.0, 0.0)
            gt = jnp.where(cj > score[rb], 1.0, 0.0)
            if rb * 8 > j:
                beats = ge
            elif rb * 8 + 7 <= j:
                beats = gt
            else:
                beats = jnp.where(sub > j - rb * 8, ge, gt)
            rank[rb] = rank[rb] + beats
    rows = [jnp.zeros((L_MASK, tq), F32)]
    rows += [jnp.where(rank[rb] < float(SLC_TOPK), 0.0, -BIG) for rb in range(nrb)]
    rows += [jnp.zeros((128 - L_MASK - n_slc, tq), F32)]
    qmask_ref[0, 0] = jnp.concatenate(rows, axis=0).T.astype(BF16)


def _cmpattn(pb, cmp_kv, s):
    bsz = pb.shape[0]
    g = ATT_KV_GROUPS
    nrow = cmp_kv.shape[2]
    n_cmp = (s - CMP_BLOCK) // CMP_STRIDE + 1
    n_slc = s // SLC_BLOCK
    mt = _cmp_to_slc_matrix_t(nrow, n_cmp, n_slc)
    return pl.pallas_call(
        _cmpattn_kernel,
        grid=(bsz, g, s // TQ_CMP),
        in_specs=[pl.BlockSpec(memory_space=pltpu.SMEM),
                  pl.BlockSpec((1, TQ_CMP, KV_WIDTH), lambda b, gg, i: (b, i, B_Q // KV_WIDTH + gg)),
                  pl.BlockSpec((1, 1, nrow, ATT_HEAD_DIM), lambda b, gg, i: (b, gg, 0, 0)),
                  pl.BlockSpec((1, 1, nrow, ATT_HEAD_DIM), lambda b, gg, i: (b, g + gg, 0, 0)),
                  pl.BlockSpec((n_slc, nrow), lambda b, gg, i: (0, 0))],
        out_specs=[pl.BlockSpec((1, TQ_CMP, KV_WIDTH), lambda b, gg, i: (b, i, gg)),
                   pl.BlockSpec((1, 1, TQ_CMP, 128), lambda b, gg, i: (b, gg, i, 0))],
        out_shape=[jax.ShapeDtypeStruct((bsz, s, ATT_WIDTH), F32),
                   jax.ShapeDtypeStruct((bsz, g, s, 128), BF16)],
        scratch_shapes=[pltpu.VMEM((ATT_HPG * nrow, KV_WIDTH), BF16),
                        pltpu.VMEM((ATT_HPG * nrow, KV_WIDTH), BF16),
                        pltpu.VMEM((ATT_HPG * nrow, KV_WIDTH), BF16)],
        compiler_params=_cp(("arbitrary", "arbitrary", "arbitrary")),
        name="cmpattn",
    )(_slope_table(), pb, cmp_kv, cmp_kv, mt)


def _flash_step(qa_ref, kv_ref, aux_ref, m_scr, acc_scr, rows, ksub, nsub, bias, first=False):
    r0, nr = rows
    nk = nsub * KT
    start = pl.multiple_of(ksub * KT, KT)
    kv = kv_ref[0, pl.ds(start, nk), :]
    left = jnp.where(lax.broadcasted_iota(jnp.int32, (nk, 128), 1) < ATT_HEAD_DIM, 1.0, 0.0).astype(BF16)
    k_aug = kv * left + aux_ref[pl.ds(start, nk), :]
    v_aug = kv * (1.0 - left) + left
    s = lax.dot_general(qa_ref[r0:r0 + nr, :], k_aug, NT, preferred_element_type=F32)
    if bias is not None:
        parts = []
        for t, b in enumerate(bias):
            st = s[:, t * KT:(t + 1) * KT]
            if b is not None:
                st = (st.reshape(nr // QH, QH, KT) + b[None]).reshape(nr, KT)
            parts.append(st)
        s = jnp.concatenate(parts, axis=1) if nsub > 1 else parts[0]
    m_cur = jnp.max(s, axis=-1, keepdims=True)
    if first:
        m_new = jnp.broadcast_to(m_cur, (nr, 128))
    else:
        m_prev = m_scr[r0:r0 + nr, :]
        m_new = jnp.maximum(m_prev, m_cur)
    p = jnp.exp2(s - jnp.concatenate([m_new] * (nk // 128), axis=1))
    pv_t = lax.dot_general(v_aug, p.astype(BF16), (((0,), (1,)), ((), ())),
                           preferred_element_type=F32)
    if first:
        acc_scr[:, r0:r0 + nr] = pv_t
    else:
        acc_scr[:, r0:r0 + nr] = jnp.exp2(m_prev - m_new).T * acc_scr[:, r0:r0 + nr] + pv_t
    m_scr[r0:r0 + nr, :] = m_new


def _flash_out(acc_scr):
    r = acc_scr.shape[1]
    den = acc_scr[0:8, :]
    num = acc_scr[ATT_HEAD_DIM:2 * ATT_HEAD_DIM, :]
    o_t = (num.reshape(ATT_HEAD_DIM // 8, 8, r) / den[None]).reshape(ATT_HEAD_DIM, r)
    halves = []
    for half in range(TQ // QH):
        cols = [o_t[:, (half * ATT_HPG + h) * QH:(half * ATT_HPG + h + 1) * QH] for h in range(ATT_HPG)]
        halves.append(jnp.concatenate(cols, axis=0).T)
    return jnp.concatenate(halves, axis=0)


def _attn_kernel(slopes_ref, q_ref, ksel_ref, kwin_ref, aux_ref, qmask_ref, ocmp_ref, small_ref,
                 zatt_ref, pick_ref, bias_ref, o_ref, qas_scr, qaw_scr, m_scr, acc_scr):
    g = pl.program_id(1)
    i = pl.program_id(2)

    qf = q_ref[0].astype(F32)
    qm = qmask_ref[0, 0].astype(F32)
    lane_row = lax.broadcasted_iota(jnp.int32, (1, 128), 1)
    left = lax.broadcasted_iota(jnp.int32, (TQ, 128), 1) < ATT_HEAD_DIM
    for h in range(ATT_HPG):
        slab = qf[:, 128 * (h // 2):128 * (h // 2) + 128]
        if h % 2:
            slab = pltpu.roll(slab, ATT_HEAD_DIM, 1)
        ext = jnp.zeros((1, 128), F32)
        for c in range(6):
            ext = jnp.where(lane_row == L_POS + c, slopes_ref[(g * ATT_HPG + h) * SLOPE_STRIDE + c], ext)
        base = jnp.where(left, slab, ext)
        for half in range(TQ // QH):
            rs = (half * ATT_HPG + h) * QH
            tok = slice(half * QH, (half + 1) * QH)
            qaw_scr[rs:rs + QH, :] = base[tok].astype(BF16)
            qas_scr[rs:rs + QH, :] = (base[tok] + qm[tok]).astype(BF16)

    half_rows = ATT_HPG * QH
    rows_a, rows_b, rows_all = (0, half_rows), (half_rows, half_rows), (0, 2 * half_rows)
    k0 = i * (TQ // KT)
    w0, w1, w2, causal = (bias_ref[n] for n in range(4))

    _flash_step(qas_scr, ksel_ref, aux_ref, m_scr, acc_scr, rows_a, k0, 1, [causal], first=True)
    _flash_step(qas_scr, ksel_ref, aux_ref, m_scr, acc_scr, rows_b, k0, 2, [None, causal], first=True)

    def sel_body(j, carry):
        _flash_step(qas_scr, ksel_ref, aux_ref, m_scr, acc_scr, rows_all, j * (TK // KT), TK // KT, None)
        return carry

    lax.fori_loop(0, i, sel_body, 0)
    o_slc = _flash_out(acc_scr)

    @pl.when(i == 0)
    def _():
        _flash_step(qaw_scr, kwin_ref, aux_ref, m_scr, acc_scr, rows_a, 0, 1, [w2], first=True)
        _flash_step(qaw_scr, kwin_ref, aux_ref, m_scr, acc_scr, rows_b, 0, 2, [w1, w2], first=True)

    @pl.when(i >= 1)
    def _():
        _flash_step(qaw_scr, kwin_ref, aux_ref, m_scr, acc_scr, rows_a, k0 - 2, 3, [w0, w1, w2], first=True)
        _flash_step(qaw_scr, kwin_ref, aux_ref, m_scr, acc_scr, rows_b, k0 - 1, 3, [w0, w1, w2], first=True)

    o_win = _flash_out(acc_scr)

    gate = _sigmoid(small_ref[0])
    g_hi = gate.astype(BF16)
    g_lo = (gate - g_hi.astype(F32)).astype(BF16)
    gts = jnp.dot(jnp.concatenate([g_hi, g_lo], axis=1), pick_ref[0],
                  preferred_element_type=F32)
    o = (gts[:, :KV_WIDTH] * ocmp_ref[0] + gts[:, KV_WIDTH:2 * KV_WIDTH] * o_slc
         + gts[:, 2 * KV_WIDTH:] * o_win)
    o_ref[0] = (o * _silu(zatt_ref[0].astype(F32))).astype(o_ref.dtype)


def _attn(pb, small, qmask, ocmp):
    bsz, s, _ = pb.shape
    g = ATT_KV_GROUPS
    r = ATT_HPG * TQ
    return pl.pallas_call(
        _attn_kernel,
        grid=(bsz, g, s // TQ),
        in_specs=[pl.BlockSpec(memory_space=pltpu.SMEM),
                  pl.BlockSpec((1, TQ, KV_WIDTH), lambda b, gg, i: (b, i, B_Q // KV_WIDTH + gg)),
                  pl.BlockSpec((1, s, 128), lambda b, gg, i: (b, 0, B_KSEL // 128 + gg)),
                  pl.BlockSpec((1, s, 128), lambda b, gg, i: (b, 0, B_KWIN // 128 + gg)),
                  pl.BlockSpec((s, 128), lambda b, gg, i: (0, 0)),
                  pl.BlockSpec((1, 1, TQ, 128), lambda b, gg, i: (b, gg, i, 0)),
                  pl.BlockSpec((1, TQ, KV_WIDTH), lambda b, gg, i: (b, i, gg)),
                  pl.BlockSpec((1, TQ, 128), lambda b, gg, i: (b, i, 0)),
                  pl.BlockSpec((1, TQ, KV_WIDTH), lambda b, gg, i: (b, i, B_ZATT // KV_WIDTH + gg)),
                  pl.BlockSpec((1, 256, 3 * KV_WIDTH), lambda b, gg, i: (gg, 0, 0)),
                  pl.BlockSpec((4, QH, KT), lambda b, gg, i: (0, 0, 0))],
        out_specs=pl.BlockSpec((1, TQ, KV_WIDTH), lambda b, gg, i: (b, i, gg)),
        out_shape=jax.ShapeDtypeStruct((bsz, s, ATT_WIDTH), BF16),
        scratch_shapes=[pltpu.VMEM((r, 128), BF16), pltpu.VMEM((r, 128), BF16),
                        pltpu.VMEM((r, 128), F32), pltpu.VMEM((128, r), F32)],
        compiler_params=_cp(("arbitrary", "arbitrary", "arbitrary")),
        name="attn",
    )(_slope_table(), pb, pb, pb, _key_aux_table(s), qmask, ocmp, small, pb, _gate_expand_table(),
      _bias_table())


def _merge_kernel(yssm_ref, onsa_ref, mg_ref, x_ref, gate_ref, gpost_ref, wssm_ref, wnsa_ref, wout_ref, o_ref):
    y_ssm = jnp.dot(yssm_ref[0], wssm_ref[...], preferred_element_type=F32)
    y_nsa = jnp.dot(onsa_ref[0], wnsa_ref[...], preferred_element_type=F32)
    mg = _sigmoid(mg_ref[0].astype(F32))
    merged = mg[:, :D_MODEL] * y_ssm + mg[:, D_MODEL:] * y_nsa
    out = jnp.dot(merged.astype(BF16), wout_ref[...], preferred_element_type=F32)
    yn = out * lax.rsqrt(jnp.mean(out * out, axis=-1, keepdims=True) + EPS) * gpost_ref[...]
    o_ref[0] = x_ref[0] + gate_ref[0] * yn


def _merge(yssm, onsa, pb, x, gate, g_post, w_ssm_out, w_nsa_out, w_out):
    bsz, s, _ = x.shape
    tm = 512
    const = lambda shape: pl.BlockSpec(shape, lambda b, i: (0, 0))
    return pl.pallas_call(
        _merge_kernel,
        grid=(bsz, s // tm),
        in_specs=[pl.BlockSpec((1, tm, D_INNER), lambda b, i: (b, i, 0)),
                  pl.BlockSpec((1, tm, ATT_WIDTH), lambda b, i: (b, i, 0)),
                  pl.BlockSpec((1, tm, 2 * D_MODEL), lambda b, i: (b, i, B_MERGE // (2 * D_MODEL))),
                  pl.BlockSpec((1, tm, D_MODEL), lambda b, i: (b, i, 0)),
                  pl.BlockSpec((1, 1, D_MODEL), lambda b, i: (b, 0, 0)),
                  const((1, D_MODEL)),
                  const((D_INNER, D_MODEL)), const((ATT_WIDTH, D_MODEL)), const((D_MODEL, D_MODEL))],
        out_specs=pl.BlockSpec((1, tm, D_MODEL), lambda b, i: (b, i, 0)),
        out_shape=jax.ShapeDtypeStruct((bsz, s, D_MODEL), F32),
        compiler_params=_cp(("arbitrary", "arbitrary")),
        name="merge",
    )(yssm, onsa, pb, x, gate, g_post.reshape(1, D_MODEL), w_ssm_out, w_nsa_out, w_out)


def _wprep_kernel(w_ref, wb_ref, ws_ref, wkv_ref):
    def cols(off, n):
        lo = off // 128 * 128
        hi = -(-(off + n) // 128) * 128
        return w_ref[:, lo:min(hi, w_ref.shape[1])][:, off - lo:off - lo + n]

    def put(dst, off, n):
        wb_ref[:, dst:dst + n] = cols(off, n).astype(BF16)

    put(B_XBC, IN_XBC, D_CONV)
    put(B_ZATT, IN_ZATT, ATT_WIDTH)
    put(B_MERGE, IN_MERGE, 2 * D_MODEL)
    put(B_ZSSM, IN_Z, D_INNER)
    wb_ref[:, B_Q:B_Q + ATT_WIDTH] = (cols(IN_Q, ATT_WIDTH) * (ATT_HEAD_DIM ** -0.5 * LOG2E)).astype(BF16)
    hd = ATT_HEAD_DIM
    for dst, k_off in ((B_KSEL, IN_KV + 2 * KV_WIDTH), (B_KWIN, IN_KV + 4 * KV_WIDTH)):
        for gg in range(ATT_KV_GROUPS):
            put(dst + 2 * gg * hd, k_off + gg * hd, hd)
            put(dst + (2 * gg + 1) * hd, k_off + KV_WIDTH + gg * hd, hd)
    wkv_ref[...] = cols(IN_KV, 2 * KV_WIDTH).astype(BF16)
    rows = w_ref.shape[0]
    ws_ref[...] = jnp.concatenate(
        [cols(IN_DT, SSM_HEADS), cols(IN_GATE, 3 * ATT_HEADS),
         jnp.zeros((rows, 128 - SSM_HEADS - 3 * ATT_HEADS), F32)], axis=1).astype(BF16)


def _split_w_in(w_in):
    dm, n_in = w_in.shape
    tr = 128
    return pl.pallas_call(
        _wprep_kernel,
        grid=(dm // tr,),
        in_specs=[pl.BlockSpec((tr, n_in), lambda i: (i, 0))],
        out_specs=[pl.BlockSpec((tr, NB), lambda i: (i, 0)), pl.BlockSpec((tr, 128), lambda i: (i, 0)),
                   pl.BlockSpec((tr, 2 * KV_WIDTH), lambda i: (i, 0))],
        out_shape=[jax.ShapeDtypeStruct((dm, NB), BF16), jax.ShapeDtypeStruct((dm, 128), BF16),
                   jax.ShapeDtypeStruct((dm, 2 * KV_WIDTH), BF16)],
        compiler_params=_cp(("arbitrary",)),
        name="wprep",
    )(w_in)


def _layer(x, c, w_ada, b_ada, g_pre, g_post, w_in, conv_w, conv_b, dt_bias, a_log, d_skip,
           g_ssm_norm, w_ssm_out, cmp_pos_k, cmp_w1_k, cmp_w2_k, cmp_pos_v, cmp_w1_v, cmp_w2_v,
           w_nsa_out, w_out):
    bsz, s, dm = x.shape
    mod = _ada(c, w_ada, b_ada)
    shift = mod[:, None, :dm]
    scale = mod[:, None, dm:2 * dm]
    gate = mod[:, None, 2 * dm:]
    wb, w_small, w_kvc = _split_w_in(w_in)
    pb, small, kvc = _proj(x, shift, scale, g_pre, wb, w_small, w_kvc)

    yssm = _ssm(pb, small, conv_w, conv_b, dt_bias, a_log, d_skip, g_ssm_norm)

    kvc = kvc.reshape(bsz, s // CMP_STRIDE, CMP_STRIDE * 2 * KV_WIDTH)
    pos = jnp.stack([cmp_pos_k.reshape(1, -1), cmp_pos_v.reshape(1, -1)])
    w1 = jnp.stack([cmp_w1_k, cmp_w1_v]).astype(BF16)
    w2 = jnp.stack([cmp_w2_k, cmp_w2_v]).astype(BF16)
    cmp_kv = _compress(kvc, pos, w1, w2)

    ocmp, qmask = _cmpattn(pb, cmp_kv, s)
    onsa = _attn(pb, small, qmask, ocmp)
    return _merge(yssm, onsa, pb, x, gate, g_post, w_ssm_out.astype(BF16), w_nsa_out.astype(BF16),
                  w_out.astype(BF16))


@jax.jit
def kernel(x, c, w_ada, b_ada, g_pre, g_post, w_in, conv_w, conv_b, dt_bias, a_log, d_skip, g_ssm_norm,
           w_ssm_out, cmp_pos_k, cmp_w1_k, cmp_w2_k, cmp_pos_v, cmp_w1_v, cmp_w2_v, w_nsa_out, w_out):
    for layer in range(w_in.shape[0]):
        x = _layer(x, c, w_ada[layer], b_ada[layer], g_pre[layer], g_post[layer], w_in[layer],
                   conv_w[layer], conv_b[layer], dt_bias[layer], a_log[layer], d_skip[layer],
                   g_ssm_norm[layer], w_ssm_out[layer], cmp_pos_k[layer], cmp_w1_k[layer],
                   cmp_w2_k[layer], cmp_pos_v[layer], cmp_w1_v[layer], cmp_w2_v[layer],
                   w_nsa_out[layer], w_out[layer])
    return x
```

```python
import numpy as np
import jax
import jax.numpy as jnp
from jax import lax
from jax.experimental import pallas as pl
from jax.experimental.pallas import tpu as pltpu

D_MODEL = 1024
D_INNER = 2048
SSM_HEAD_DIM = 64
SSM_HEADS = 32
SSM_GROUPS = 4
SSM_STATE = 128
SSM_CONV = 4
SSM_CHUNK = 128
D_CONV = D_INNER + 2 * SSM_GROUPS * SSM_STATE
CONV_TAIL = 16

ATT_HEADS = 16
ATT_HEAD_DIM = 64
ATT_KV_GROUPS = 4
ATT_HPG = 4
ATT_WIDTH = 1024
KV_WIDTH = 256
CMP_BLOCK = 32
CMP_STRIDE = 16
CMP_HIDDEN = 256
SLC_BLOCK = 64
SLC_SHIFT = 6
SLC_TOPK = 16
WINDOW = 512
FORCE_BONUS = 1000.0
EPS = 1e-6
NEG = -1e30
BIG = 2.0 ** 100

F32 = jnp.float32
BF16 = jnp.bfloat16
HIGHEST = lax.Precision.HIGHEST
NT = (((1,), (1,)), ((), ()))

_SIZES = (D_INNER, D_CONV, SSM_HEADS, ATT_WIDTH, 6 * KV_WIDTH, 3 * ATT_HEADS, ATT_WIDTH, 2 * D_MODEL)
_OFFS = tuple(int(v) for v in np.cumsum((0,) + _SIZES))
IN_Z, IN_XBC, IN_DT, IN_Q, IN_KV, IN_GATE, IN_ZATT, IN_MERGE = _OFFS[:8]

B_XBC, B_ZATT, B_MERGE, B_ZSSM, B_Q, B_KSEL, B_KWIN = 0, 3072, 4096, 6144, 8192, 9216, 9728
NB = 10240
SMALL_GATE = 32
LOG2E = 1.4426950408889634

TM_PROJ = 1024
TN_PROJ = 2560
TQ = 512
TK = 512
QH = 256
KT = 256
POS_BASE = 256
TQ_CMP = 1024
VMEM_LIMIT = 48 * 1024 * 1024

L_POS = 64
L_TILE = 67
L_MASK = 72
SLOPE_STRIDE = 8


def _cp(sem):
    return pltpu.CompilerParams(dimension_semantics=sem, vmem_limit_bytes=VMEM_LIMIT)


def _sigmoid(v):
    return 0.5 * jnp.tanh(0.5 * v) + 0.5


def _silu(v):
    return v * _sigmoid(v)


def _ada_kernel(c_ref, w_ref, b_ref, o_ref):
    o_ref[...] = jnp.dot(c_ref[...], w_ref[...], preferred_element_type=F32,
                         precision=HIGHEST) + b_ref[...]


def _ada(c, w_ada, b_ada):
    bsz = c.shape[0]
    return pl.pallas_call(
        _ada_kernel,
        grid=(3,),
        in_specs=[pl.BlockSpec((bsz, D_MODEL), lambda j: (0, 0)),
                  pl.BlockSpec((D_MODEL, D_MODEL), lambda j: (0, j)),
                  pl.BlockSpec((1, D_MODEL), lambda j: (0, j))],
        out_specs=pl.BlockSpec((bsz, D_MODEL), lambda j: (0, j)),
        out_shape=jax.ShapeDtypeStruct((bsz, 3 * D_MODEL), F32),
        compiler_params=_cp(("arbitrary",)),
        name="ada",
    )(c, w_ada, b_ada.reshape(1, 3 * D_MODEL))


def _proj_kernel(x_ref, shift_ref, scale_ref, g_ref, w_ref, ws_ref, wkv_ref, o_ref, osm_ref, okv_ref, h_scr):
    @pl.when(pl.program_id(2) == 0)
    def _():
        xf = x_ref[0]
        y = xf * lax.rsqrt(jnp.mean(xf * xf, axis=-1, keepdims=True) + EPS) * g_ref[...]
        h_scr[...] = (y * (1.0 + scale_ref[0]) + shift_ref[0]).astype(BF16)
        osm_ref[0] = lax.dot_general(h_scr[...], ws_ref[...], NT, preferred_element_type=F32)
        okv_ref[0] = lax.dot_general(h_scr[...], wkv_ref[...], NT,
                                     preferred_element_type=F32).astype(okv_ref.dtype)

    o_ref[0] = lax.dot_general(h_scr[...], w_ref[...], NT, preferred_element_type=F32).astype(o_ref.dtype)


def _proj(x, shift, scale, g_pre, w, w_small, w_kvc):
    bsz, s, _ = x.shape
    n = w.shape[0]
    nkv = w_kvc.shape[0]
    return pl.pallas_call(
        _proj_kernel,
        grid=(bsz, s // TM_PROJ, n // TN_PROJ),
        in_specs=[pl.BlockSpec((1, TM_PROJ, D_MODEL), lambda b, i, j: (b, i, 0)),
                  pl.BlockSpec((1, 1, D_MODEL), lambda b, i, j: (b, 0, 0)),
                  pl.BlockSpec((1, 1, D_MODEL), lambda b, i, j: (b, 0, 0)),
                  pl.BlockSpec((1, D_MODEL), lambda b, i, j: (0, 0)),
                  pl.BlockSpec((TN_PROJ, D_MODEL), lambda b, i, j: (j, 0)),
                  pl.BlockSpec((128, D_MODEL), lambda b, i, j: (0, 0)),
                  pl.BlockSpec((nkv, D_MODEL), lambda b, i, j: (0, 0))],
        out_specs=[pl.BlockSpec((1, TM_PROJ, TN_PROJ), lambda b, i, j: (b, i, j)),
                   pl.BlockSpec((1, TM_PROJ, 128), lambda b, i, j: (b, i, 0)),
                   pl.BlockSpec((1, TM_PROJ, nkv), lambda b, i, j: (b, i, 0))],
        out_shape=[jax.ShapeDtypeStruct((bsz, s, n), BF16),
                   jax.ShapeDtypeStruct((bsz, s, 128), F32),
                   jax.ShapeDtypeStruct((bsz, s, nkv), F32)],
        scratch_shapes=[pltpu.VMEM((TM_PROJ, D_MODEL), BF16)],
        compiler_params=_cp(("arbitrary", "arbitrary", "arbitrary")),
        name="proj",
    )(x, shift, scale, g_pre.reshape(1, D_MODEL), w, w_small, w_kvc)


def _ssm_kernel(xbc_ref, z_ref, small_ref, shift_ref, spread_ref, convw_ref, convb_ref, dtb_ref,
                alog_ref, dskip_ref, gn_ref, o_ref, tail_scr, state_scr, y_scr):
    cl = SSM_CHUNK

    @pl.when(pl.program_id(1) == 0)
    def _():
        tail_scr[...] = jnp.zeros(tail_scr.shape, BF16)
        state_scr[...] = jnp.zeros(state_scr.shape, F32)

    cur = xbc_ref[0]
    xpad = jnp.concatenate([tail_scr[...], cur], axis=0)
    delayed = jnp.dot(shift_ref[...], xpad, preferred_element_type=F32)
    tail_scr[...] = cur[cl - CONV_TAIL:cl, :]
    acc = convb_ref[...] + convw_ref[SSM_CONV - 1:SSM_CONV, :] * cur.astype(F32)
    for k in range(SSM_CONV - 1):
        acc = acc + convw_ref[k:k + 1, :] * delayed[k * cl:(k + 1) * cl]
    u = _silu(acc)
    xs = u[:, :D_INNER]
    bm = u[:, D_INNER:D_INNER + SSM_GROUPS * SSM_STATE]
    cm = u[:, D_INNER + SSM_GROUPS * SSM_STATE:]

    pre = small_ref[0] + dtb_ref[...]
    dt = jnp.maximum(pre, 0.0) + jnp.log1p(jnp.exp(-jnp.abs(pre)))
    a = -jnp.exp(alog_ref[...])
    adt = dt * a
    row = lax.broadcasted_iota(jnp.int32, (cl, cl), 0)
    col = lax.broadcasted_iota(jnp.int32, (cl, cl), 1)
    causal = row >= col
    tri = causal.astype(F32)
    a_cs = jnp.dot(tri, adt, preferred_element_type=F32, precision=HIGHEST)
    a_cs_t = a_cs.T
    dt_t = dt.T
    a_last = a_cs[cl - 1:cl, :]
    ea = jnp.exp(a_cs)
    dsc = jnp.exp(a_last - a_cs) * dt

    def spread(v):
        hi = v.astype(BF16)
        lo = (v - hi.astype(F32)).astype(BF16)
        return jnp.dot(jnp.concatenate([hi, lo], axis=1), spread_ref[...], preferred_element_type=F32)

    ea_x = spread(ea)
    dsc_x = spread(dsc)
    cdec_x = ea_x[cl - 1:cl, :]
    xsd = xs * dsc_x
    lane = lax.broadcasted_iota(jnp.int32, (cl, 2 * SSM_HEAD_DIM), 1)
    first_half = lane < SSM_HEAD_DIM

    hg = SSM_HEADS // SSM_GROUPS
    gw = hg * SSM_HEAD_DIM
    for g in range(SSM_GROUPS):
        bg = bm[:, g * SSM_STATE:(g + 1) * SSM_STATE].astype(BF16)
        cg = cm[:, g * SSM_STATE:(g + 1) * SSM_STATE].astype(BF16)
        gc = slice(g * gw, (g + 1) * gw)
        cb = lax.dot_general(cg, bg, NT, preferred_element_type=F32)
        st = state_scr[g]
        y_scr[:, gc] = jnp.dot(cg, st.astype(BF16), preferred_element_type=F32) * ea_x[:, gc]
        new = lax.dot_general(bg, xsd[:, gc].astype(BF16), (((0,), (0,)), ((), ())),
                              preferred_element_type=F32)
        state_scr[g] = st * cdec_x[:, gc] + new
        for pp in range(hg // 2):
            pair = g * (hg // 2) + pp
            pc = slice(pair * 128, (pair + 1) * 128)
            gm = []
            for e in range(2):
                h = 2 * pair + e
                seg = a_cs[:, h:h + 1] - a_cs_t[h:h + 1, :]
                lmat = jnp.exp(jnp.where(causal, seg, NEG))
                gm.append(cb * lmat * dt_t[h:h + 1, :])
            lhs = jnp.concatenate(gm, axis=0).astype(BF16)
            yy = jnp.dot(lhs, xs[:, pc].astype(BF16), preferred_element_type=F32)
            y_scr[:, pc] = y_scr[:, pc] + jnp.where(first_half, yy[:cl], yy[cl:])

    y = y_scr[...] + xs * dskip_ref[...]
    y = y * _silu(z_ref[0].astype(F32))
    gsz = D_INNER // SSM_GROUPS
    for g in range(SSM_GROUPS):
        yg = y[:, g * gsz:(g + 1) * gsz]
        yn = yg * lax.rsqrt(jnp.mean(yg * yg, axis=-1, keepdims=True) + EPS)
        o_ref[0, :, g * gsz:(g + 1) * gsz] = (yn * gn_ref[:, g * gsz:(g + 1) * gsz]).astype(o_ref.dtype)


def _conv_shift_matrix():
    cl = SSM_CHUNK
    m = np.zeros(((SSM_CONV - 1) * cl, CONV_TAIL + cl), np.float32)
    t = np.arange(cl)
    for k in range(SSM_CONV - 1):
        m[k * cl + t, CONV_TAIL + t - (SSM_CONV - 1 - k)] = 1.0
    return jnp.asarray(m, dtype=BF16)


def _head_spread_matrix():
    m = np.zeros((2, 128, D_INNER), np.float32)
    for h in range(SSM_HEADS):
        m[:, h, h * SSM_HEAD_DIM:(h + 1) * SSM_HEAD_DIM] = 1.0
    return jnp.asarray(m.reshape(256, D_INNER), dtype=BF16)


def _ssm(pb, small, conv_w, conv_b, dt_bias, a_log, d_skip, g_norm):
    bsz, s, _ = pb.shape
    cl = SSM_CHUNK
    pad = 128 - SSM_HEADS
    dtb = jnp.pad(dt_bias, (0, pad)).reshape(1, 128)
    alog = jnp.pad(a_log, (0, pad)).reshape(1, 128)
    dskip = jnp.repeat(d_skip, SSM_HEAD_DIM).reshape(1, D_INNER)
    const = lambda shape: pl.BlockSpec(shape, lambda b, c: (0, 0))
    return pl.pallas_call(
        _ssm_kernel,
        grid=(bsz, s // cl),
        in_specs=[pl.BlockSpec((1, cl, D_CONV), lambda b, c: (b, c, B_XBC // D_CONV)),
                  pl.BlockSpec((1, cl, D_INNER), lambda b, c: (b, c, B_ZSSM // D_INNER)),
                  pl.BlockSpec((1, cl, 128), lambda b, c: (b, c, 0)),
                  const(((SSM_CONV - 1) * cl, CONV_TAIL + cl)), const((256, D_INNER)),
                  const((SSM_CONV, D_CONV)), const((1, D_CONV)), const((1, 128)), const((1, 128)),
                  const((1, D_INNER)), const((1, D_INNER))],
        out_specs=pl.BlockSpec((1, cl, D_INNER), lambda b, c: (b, c, 0)),
        out_shape=jax.ShapeDtypeStruct((bsz, s, D_INNER), BF16),
        scratch_shapes=[pltpu.VMEM((CONV_TAIL, D_CONV), BF16),
                        pltpu.VMEM((SSM_GROUPS, SSM_STATE, D_INNER // SSM_GROUPS), F32),
                        pltpu.VMEM((cl, D_INNER), F32)],
        compiler_params=_cp(("arbitrary", "arbitrary")),
        name="ssm",
    )(pb, pb, small, _conv_shift_matrix(), _head_spread_matrix(), conv_w, conv_b.reshape(1, D_CONV),
      dtb, alog, dskip,
      g_norm.reshape(1, D_INNER))


def _cmp_kernel(r0_ref, r1_ref, r2_ref, r3_ref, pos_ref, w1_ref, w2_ref, o_ref):
    half = CMP_STRIDE * ATT_HEAD_DIM
    hd = ATT_HEAD_DIM
    nrow = r0_ref.shape[1] // CMP_STRIDE
    toks = [[r_ref[0, pl.ds(t, nrow, stride=CMP_STRIDE), :] for t in range(CMP_STRIDE)]
            for r_ref in (r0_ref, r1_ref, r2_ref, r3_ref)]
    for kv in range(2):
        w1 = w1_ref[kv]
        posb = jnp.broadcast_to(pos_ref[kv], (8, 2 * half)).astype(BF16)
        cpos = jnp.dot(posb, w1, preferred_element_type=F32)[0:1]
        for gg in range(ATT_KV_GROUPS):
            j = kv * ATT_KV_GROUPS + gg
            r = jnp.concatenate([toks[j // 2][t][:, (j % 2) * hd:(j % 2 + 1) * hd]
                                 for t in range(CMP_STRIDE)], axis=1).astype(BF16)
            first = jnp.dot(r, w1[:half], preferred_element_type=F32)
            second = jnp.dot(r, w1[half:], preferred_element_type=F32)
            hid = first + pltpu.roll(second, nrow - 1, 0) + cpos
            o_ref[0, j] = jnp.dot(_silu(hid).astype(BF16), w2_ref[kv], preferred_element_type=F32)


def _compress(kvc, pos, w1, w2):
    bsz, s, wid = kvc.shape
    nrow = s // CMP_STRIDE
    ng = 2 * ATT_KV_GROUPS
    return pl.pallas_call(
        _cmp_kernel,
        grid=(bsz,),
        in_specs=[pl.BlockSpec((1, s, 128), lambda b, c=c: (b, 0, c)) for c in range(wid // 128)]
                 + [pl.BlockSpec((2, 1, 2048), lambda b: (0, 0, 0)),
                  pl.BlockSpec((2, 2048, CMP_HIDDEN), lambda b: (0, 0, 0)),
                  pl.BlockSpec((2, CMP_HIDDEN, ATT_HEAD_DIM), lambda b: (0, 0, 0))],
        out_specs=pl.BlockSpec((1, ng, nrow, ATT_HEAD_DIM), lambda b: (b, 0, 0, 0)),
        out_shape=jax.ShapeDtypeStruct((bsz, ng, nrow, ATT_HEAD_DIM), F32),
        compiler_params=_cp(("arbitrary",)),
        name="cmp",
    )(kvc, kvc, kvc, kvc, pos, w1, w2)


def _bf16_round_np(x):
    u = np.asarray(x, np.float32).view(np.uint32)
    u = (u + (((u >> 16) & 1) + 0x7FFF)) & np.uint32(0xFFFF0000)
    return u.view(np.float32)


def _slope_table():
    slope = (2.0 ** (-8.0 * np.arange(1, ATT_HEADS + 1) / ATT_HEADS)).astype(np.float32)
    slope = (slope.astype(np.float64) * LOG2E).astype(np.float32)
    p0 = _bf16_round_np(slope)
    p1 = _bf16_round_np(slope - p0)
    p2 = _bf16_round_np(slope - p0 - p1)
    tab = np.zeros((ATT_HEADS, SLOPE_STRIDE), np.float32)
    tab[:, 0], tab[:, 1], tab[:, 2] = p0, p1, p2
    tab[:, 3:6] = tab[:, 0:3] * POS_BASE
    tab[:, 6] = slope
    return jnp.asarray(tab.reshape(-1))


def _key_aux_table(s):
    pos = np.arange(s)
    tab = np.zeros((s, 128), np.float32)
    tab[:, L_POS:L_POS + 3] = (pos % POS_BASE)[:, None]
    tab[:, L_TILE:L_TILE + 3] = (pos // POS_BASE)[:, None]
    tab[pos, L_MASK + pos // SLC_BLOCK] = 1.0
    return jnp.asarray(tab, dtype=BF16)


def _bias_table():
    r = np.arange(QH)[:, None]
    c = np.arange(KT)[None, :]
    tabs = []
    for jj in range(3):
        d = (2 - jj) * KT + r - c
        tabs.append(np.where((d >= 0) & (d < WINDOW), 0.0, NEG))
    tabs.append(np.where(r - c >= 0, 0.0, NEG))
    return jnp.asarray(np.stack(tabs), dtype=F32)


def _gate_expand_table():
    tab = np.zeros((ATT_KV_GROUPS, 2, 128, 3 * KV_WIDTH), np.float32)
    for g in range(ATT_KV_GROUPS):
        for h in range(ATT_HPG):
            for j in range(3):
                row = SMALL_GATE + g * 3 * ATT_HPG + 3 * h + j
                tab[g, :, row, j * KV_WIDTH + h * ATT_HEAD_DIM:j * KV_WIDTH + (h + 1) * ATT_HEAD_DIM] = 1.0
    return jnp.asarray(tab.reshape(ATT_KV_GROUPS, 256, 3 * KV_WIDTH), dtype=BF16)


def _cmp_to_slc_matrix_t(n_cmp_pad, n_cmp, n_slc):
    cs = np.arange(n_cmp_pad) * CMP_STRIDE
    ss = np.arange(n_slc) * SLC_BLOCK
    lo = np.maximum(cs[:, None], ss[None, :])
    hi = np.minimum(cs[:, None] + CMP_BLOCK, ss[None, :] + SLC_BLOCK)
    m = np.clip(hi - lo, 0, None) / CMP_BLOCK
    m[n_cmp:] = 0.0
    return jnp.asarray(m.T, dtype=F32)


def _cmpattn_kernel(slopes_ref, q_ref, ck_ref, cv_ref, mt_ref, ocmp_ref, qmask_ref,
                    khi_scr, klo_scr, vbd_scr):
    g = pl.program_id(1)
    i = pl.program_id(2)
    tq = q_ref.shape[1]
    t0 = i * tq
    npad = ck_ref.shape[2]
    n_slc = mt_ref.shape[0]

    @pl.when(i == 0)
    def _():
        kc = ck_ref[0, 0]
        vc = cv_ref[0, 0]
        k_hi = kc.astype(BF16).astype(F32)
        k_lo = kc - k_hi
        zero = jnp.zeros_like(kc)
        for h in range(ATT_HPG):
            def bd(a):
                return jnp.concatenate([a if hh == h else zero for hh in range(ATT_HPG)], axis=1).astype(BF16)
            khi_scr[h * npad:(h + 1) * npad, :] = bd(k_hi)
            klo_scr[h * npad:(h + 1) * npad, :] = bd(k_lo)
            vbd_scr[h * npad:(h + 1) * npad, :] = bd(vc)

    q = q_ref[0]
    s = (lax.dot_general(q, khi_scr[...], NT, preferred_element_type=F32)
         + lax.dot_general(q, klo_scr[...], NT, preferred_element_type=F32))
    jcol = lax.broadcasted_iota(jnp.int32, (1, npad), 1)
    cmp_end = jcol * CMP_STRIDE + (CMP_BLOCK - 1)
    trow = t0 + lax.broadcasted_iota(jnp.int32, (tq, npad), 0)
    valid = (cmp_end <= trow) & (jcol < npad - 1)
    rel = (cmp_end - t0).astype(F32)
    psum = jnp.zeros((tq, npad), F32)
    ps = []
    for h in range(ATT_HPG):
        slope = slopes_ref[(g * ATT_HPG + h) * SLOPE_STRIDE + 6]
        sh = jnp.where(valid, s[:, h * npad:(h + 1) * npad] + slope * rel, NEG)
        mx = jnp.max(sh, axis=-1, keepdims=True)
        p = jnp.where(valid, jnp.exp2(sh - mx), 0.0)
        l = jnp.sum(p, axis=-1, keepdims=True)
        p = p * jnp.where(l > 0.0, 1.0 / jnp.where(l > 0.0, l, 1.0), 0.0)
        ps.append(p)
        psum = psum + p
    ocmp_ref[0] = jnp.dot(jnp.concatenate(ps, axis=1).astype(BF16), vbd_scr[...],
                          preferred_element_type=F32)
    imp_t = lax.dot_general(mt_ref[...], psum, NT, preferred_element_type=F32,
                            precision=HIGHEST)

    nrb = n_slc // 8
    blk_t = (t0 + lax.broadcasted_iota(jnp.int32, (8, tq), 1)) >> SLC_SHIFT
    sub = lax.broadcasted_iota(jnp.int32, (8, tq), 0)
    score = []
    for rb in range(nrb):
        kk = sub + rb * 8
        imp = imp_t[rb * 8:(rb + 1) * 8]
        forced = (kk == 0) | (kk == blk_t) | (kk == blk_t - 1)
        score.append(jnp.where(forced, imp + FORCE_BONUS, jnp.where(kk <= blk_t, imp, -1.0)))
    rank = [jnp.zeros((8, tq), F32) for _ in range(nrb)]
    for j in range(n_slc):
        cj = jnp.broadcast_to(score[j // 8][j % 8:j % 8 + 1, :], (8, tq))
        for rb in range(nrb):
            ge = jnp.where(cj >= score[rb], 1.0, 0.0)
            gt = jnp.where(cj > score[rb], 1.0, 0.0)
            if rb * 8 > j:
                beats = ge
            elif rb * 8 + 7 <= j:
                beats = gt
            else:
                beats = jnp.where(sub > j - rb * 8, ge, gt)
            rank[rb] = rank[rb] + beats
    rows = [jnp.zeros((L_MASK, tq), F32)]
    rows += [jnp.where(rank[rb] < float(SLC_TOPK), 0.0, -BIG) for rb in range(nrb)]
    rows += [jnp.zeros((128 - L_MASK - n_slc, tq), F32)]
    qmask_ref[0, 0] = jnp.concatenate(rows, axis=0).T.astype(BF16)


def _cmpattn(pb, cmp_kv, s):
    bsz = pb.shape[0]
    g = ATT_KV_GROUPS
    nrow = cmp_kv.shape[2]
    n_cmp = (s - CMP_BLOCK) // CMP_STRIDE + 1
    n_slc = s // SLC_BLOCK
    mt = _cmp_to_slc_matrix_t(nrow, n_cmp, n_slc)
    return pl.pallas_call(
        _cmpattn_kernel,
        grid=(bsz, g, s // TQ_CMP),
        in_specs=[pl.BlockSpec(memory_space=pltpu.SMEM),
                  pl.BlockSpec((1, TQ_CMP, KV_WIDTH), lambda b, gg, i: (b, i, B_Q // KV_WIDTH + gg)),
                  pl.BlockSpec((1, 1, nrow, ATT_HEAD_DIM), lambda b, gg, i: (b, gg, 0, 0)),
                  pl.BlockSpec((1, 1, nrow, ATT_HEAD_DIM), lambda b, gg, i: (b, g + gg, 0, 0)),
                  pl.BlockSpec((n_slc, nrow), lambda b, gg, i: (0, 0))],
        out_specs=[pl.BlockSpec((1, TQ_CMP, KV_WIDTH), lambda b, gg, i: (b, i, gg)),
                   pl.BlockSpec((1, 1, TQ_CMP, 128), lambda b, gg, i: (b, gg, i, 0))],
        out_shape=[jax.ShapeDtypeStruct((bsz, s, ATT_WIDTH), F32),
                   jax.ShapeDtypeStruct((bsz, g, s, 128), BF16)],
        scratch_shapes=[pltpu.VMEM((ATT_HPG * nrow, KV_WIDTH), BF16),
                        pltpu.VMEM((ATT_HPG * nrow, KV_WIDTH), BF16),
                        pltpu.VMEM((ATT_HPG * nrow, KV_WIDTH), BF16)],
        compiler_params=_cp(("arbitrary", "arbitrary", "arbitrary")),
        name="cmpattn",
    )(_slope_table(), pb, cmp_kv, cmp_kv, mt)


def _flash_step(qa_ref, kv_ref, aux_ref, m_scr, acc_scr, rows, ksub, nsub, bias, first=False):
    r0, nr = rows
    nk = nsub * KT
    start = pl.multiple_of(ksub * KT, KT)
    kv = kv_ref[0, pl.ds(start, nk), :]
    left = jnp.where(lax.broadcasted_iota(jnp.int32, (nk, 128), 1) < ATT_HEAD_DIM, 1.0, 0.0).astype(BF16)
    k_aug = kv * left + aux_ref[pl.ds(start, nk), :]
    v_aug = kv * (1.0 - left) + left
    s = lax.dot_general(qa_ref[r0:r0 + nr, :], k_aug, NT, preferred_element_type=F32)
    if bias is not None:
        parts = []
        for t, b in enumerate(bias):
            st = s[:, t * KT:(t + 1) * KT]
            if b is not None:
                st = (st.reshape(nr // QH, QH, KT) + b[None]).reshape(nr, KT)
            parts.append(st)
        s = jnp.concatenate(parts, axis=1) if nsub > 1 else parts[0]
    m_cur = jnp.max(s, axis=-1, keepdims=True)
    if first:
        m_new = jnp.broadcast_to(m_cur, (nr, 128))
    else:
        m_prev = m_scr[r0:r0 + nr, :]
        m_new = jnp.maximum(m_prev, m_cur)
    p = jnp.exp2(s - jnp.concatenate([m_new] * (nk // 128), axis=1))
    pv_t = lax.dot_general(v_aug, p.astype(BF16), (((0,), (1,)), ((), ())),
                           preferred_element_type=F32)
    if first:
        acc_scr[:, r0:r0 + nr] = pv_t
    else:
        acc_scr[:, r0:r0 + nr] = jnp.exp2(m_prev - m_new).T * acc_scr[:, r0:r0 + nr] + pv_t
    m_scr[r0:r0 + nr, :] = m_new


def _flash_out(acc_scr):
    r = acc_scr.shape[1]
    den = acc_scr[0:8, :]
    num = acc_scr[ATT_HEAD_DIM:2 * ATT_HEAD_DIM, :]
    o_t = (num.reshape(ATT_HEAD_DIM // 8, 8, r) / den[None]).reshape(ATT_HEAD_DIM, r)
    halves = []
    for half in range(TQ // QH):
        cols = [o_t[:, (half * ATT_HPG + h) * QH:(half * ATT_HPG + h + 1) * QH] for h in range(ATT_HPG)]
        halves.append(jnp.concatenate(cols, axis=0).T)
    return jnp.concatenate(halves, axis=0)


def _attn_kernel(slopes_ref, q_ref, ksel_ref, kwin_ref, aux_ref, qmask_ref, ocmp_ref, small_ref,
                 zatt_ref, pick_ref, bias_ref, o_ref, qas_scr, qaw_scr, m_scr, acc_scr):
    g = pl.program_id(1)
    i = pl.program_id(2)

    qf = q_ref[0].astype(F32)
    qm = qmask_ref[0, 0].astype(F32)
    lane_row = lax.broadcasted_iota(jnp.int32, (1, 128), 1)
    left = lax.broadcasted_iota(jnp.int32, (TQ, 128), 1) < ATT_HEAD_DIM
    for h in range(ATT_HPG):
        slab = qf[:, 128 * (h // 2):128 * (h // 2) + 128]
        if h % 2:
            slab = pltpu.roll(slab, ATT_HEAD_DIM, 1)
        ext = jnp.zeros((1, 128), F32)
        for c in range(6):
            ext = jnp.where(lane_row == L_POS + c, slopes_ref[(g * ATT_HPG + h) * SLOPE_STRIDE + c], ext)
        base = jnp.where(left, slab, ext)
        for half in range(TQ // QH):
            rs = (half * ATT_HPG + h) * QH
            tok = slice(half * QH, (half + 1) * QH)
            qaw_scr[rs:rs + QH, :] = base[tok].astype(BF16)
            qas_scr[rs:rs + QH, :] = (base[tok] + qm[tok]).astype(BF16)

    half_rows = ATT_HPG * QH
    rows_a, rows_b, rows_all = (0, half_rows), (half_rows, half_rows), (0, 2 * half_rows)
    k0 = i * (TQ // KT)
    w0, w1, w2, causal = (bias_ref[n] for n in range(4))

    _flash_step(qas_scr, ksel_ref, aux_ref, m_scr, acc_scr, rows_a, k0, 1, [causal], first=True)
    _flash_step(qas_scr, ksel_ref, aux_ref, m_scr, acc_scr, rows_b, k0, 2, [None, causal], first=True)

    def sel_body(j, carry):
        _flash_step(qas_scr, ksel_ref, aux_ref, m_scr, acc_scr, rows_all, j * (TK // KT), TK // KT, None)
        return carry

    lax.fori_loop(0, i, sel_body, 0)
    o_slc = _flash_out(acc_scr)

    @pl.when(i == 0)
    def _():
        _flash_step(qaw_scr, kwin_ref, aux_ref, m_scr, acc_scr, rows_a, 0, 1, [w2], first=True)
        _flash_step(qaw_scr, kwin_ref, aux_ref, m_scr, acc_scr, rows_b, 0, 2, [w1, w2], first=True)

    @pl.when(i >= 1)
    def _():
        _flash_step(qaw_scr, kwin_ref, aux_ref, m_scr, acc_scr, rows_a, k0 - 2, 3, [w0, w1, w2], first=True)
        _flash_step(qaw_scr, kwin_ref, aux_ref, m_scr, acc_scr, rows_b, k0 - 1, 3, [w0, w1, w2], first=True)

    o_win = _flash_out(acc_scr)

    gate = _sigmoid(small_ref[0])
    g_hi = gate.astype(BF16)
    g_lo = (gate - g_hi.astype(F32)).astype(BF16)
    gts = jnp.dot(jnp.concatenate([g_hi, g_lo], axis=1), pick_ref[0],
                  preferred_element_type=F32)
    o = (gts[:, :KV_WIDTH] * ocmp_ref[0] + gts[:, KV_WIDTH:2 * KV_WIDTH] * o_slc
         + gts[:, 2 * KV_WIDTH:] * o_win)
    o_ref[0] = (o * _silu(zatt_ref[0].astype(F32))).astype(o_ref.dtype)


def _attn(pb, small, qmask, ocmp):
    bsz, s, _ = pb.shape
    g = ATT_KV_GROUPS
    r = ATT_HPG * TQ
    return pl.pallas_call(
        _attn_kernel,
        grid=(bsz, g, s // TQ),
        in_specs=[pl.BlockSpec(memory_space=pltpu.SMEM),
                  pl.BlockSpec((1, TQ, KV_WIDTH), lambda b, gg, i: (b, i, B_Q // KV_WIDTH + gg)),
                  pl.BlockSpec((1, s, 128), lambda b, gg, i: (b, 0, B_KSEL // 128 + gg)),
                  pl.BlockSpec((1, s, 128), lambda b, gg, i: (b, 0, B_KWIN // 128 + gg)),
                  pl.BlockSpec((s, 128), lambda b, gg, i: (0, 0)),
                  pl.BlockSpec((1, 1, TQ, 128), lambda b, gg, i: (b, gg, i, 0)),
                  pl.BlockSpec((1, TQ, KV_WIDTH), lambda b, gg, i: (b, i, gg)),
                  pl.BlockSpec((1, TQ, 128), lambda b, gg, i: (b, i, 0)),
                  pl.BlockSpec((1, TQ, KV_WIDTH), lambda b, gg, i: (b, i, B_ZATT // KV_WIDTH + gg)),
                  pl.BlockSpec((1, 256, 3 * KV_WIDTH), lambda b, gg, i: (gg, 0, 0)),
                  pl.BlockSpec((4, QH, KT), lambda b, gg, i: (0, 0, 0))],
        out_specs=pl.BlockSpec((1, TQ, KV_WIDTH), lambda b, gg, i: (b, i, gg)),
        out_shape=jax.ShapeDtypeStruct((bsz, s, ATT_WIDTH), BF16),
        scratch_shapes=[pltpu.VMEM((r, 128), BF16), pltpu.VMEM((r, 128), BF16),
                        pltpu.VMEM((r, 128), F32), pltpu.VMEM((128, r), F32)],
        compiler_params=_cp(("arbitrary", "arbitrary", "arbitrary")),
        name="attn",
    )(_slope_table(), pb, pb, pb, _key_aux_table(s), qmask, ocmp, small, pb, _gate_expand_table(),
      _bias_table())


def _merge_kernel(yssm_ref, onsa_ref, mg_ref, x_ref, gate_ref, gpost_ref, wssm_ref, wnsa_ref, wout_ref, o_ref):
    y_ssm = jnp.dot(yssm_ref[0], wssm_ref[...], preferred_element_type=F32)
    y_nsa = jnp.dot(onsa_ref[0], wnsa_ref[...], preferred_element_type=F32)
    mg = _sigmoid(mg_ref[0].astype(F32))
    merged = mg[:, :D_MODEL] * y_ssm + mg[:, D_MODEL:] * y_nsa
    out = jnp.dot(merged.astype(BF16), wout_ref[...], preferred_element_type=F32)
    yn = out * lax.rsqrt(jnp.mean(out * out, axis=-1, keepdims=True) + EPS) * gpost_ref[...]
    o_ref[0] = x_ref[0] + gate_ref[0] * yn


def _merge(yssm, onsa, pb, x, gate, g_post, w_ssm_out, w_nsa_out, w_out):
    bsz, s, _ = x.shape
    tm = 512
    const = lambda shape: pl.BlockSpec(shape, lambda b, i: (0, 0))
    return pl.pallas_call(
        _merge_kernel,
        grid=(bsz, s // tm),
        in_specs=[pl.BlockSpec((1, tm, D_INNER), lambda b, i: (b, i, 0)),
                  pl.BlockSpec((1, tm, ATT_WIDTH), lambda b, i: (b, i, 0)),
                  pl.BlockSpec((1, tm, 2 * D_MODEL), lambda b, i: (b, i, B_MERGE // (2 * D_MODEL))),
                  pl.BlockSpec((1, tm, D_MODEL), lambda b, i: (b, i, 0)),
                  pl.BlockSpec((1, 1, D_MODEL), lambda b, i: (b, 0, 0)),
                  const((1, D_MODEL)),
                  const((D_INNER, D_MODEL)), const((ATT_WIDTH, D_MODEL)), const((D_MODEL, D_MODEL))],
        out_specs=pl.BlockSpec((1, tm, D_MODEL), lambda b, i: (b, i, 0)),
        out_shape=jax.ShapeDtypeStruct((bsz, s, D_MODEL), F32),
        compiler_params=_cp(("arbitrary", "arbitrary")),
        name="merge",
    )(yssm, onsa, pb, x, gate, g_post.reshape(1, D_MODEL), w_ssm_out, w_nsa_out, w_out)


def _wprep_kernel(w_ref, wb_ref, ws_ref, wkv_ref):
    def put(ref, dst, off, n, scale=None):
        v = w_ref[off:off + n, :]
        ref[dst:dst + n, :] = (v if scale is None else v * scale).astype(BF16)

    put(wb_ref, B_XBC, IN_XBC, D_CONV)
    put(wb_ref, B_ZATT, IN_ZATT, ATT_WIDTH)
    put(wb_ref, B_MERGE, IN_MERGE, 2 * D_MODEL)
    put(wb_ref, B_ZSSM, IN_Z, D_INNER)
    put(wb_ref, B_Q, IN_Q, ATT_WIDTH, ATT_HEAD_DIM ** -0.5 * LOG2E)
    hd = ATT_HEAD_DIM
    for dst, k_off in ((B_KSEL, IN_KV + 2 * KV_WIDTH), (B_KWIN, IN_KV + 4 * KV_WIDTH)):
        for gg in range(ATT_KV_GROUPS):
            put(wb_ref, dst + 2 * gg * hd, k_off + gg * hd, hd)
            put(wb_ref, dst + (2 * gg + 1) * hd, k_off + KV_WIDTH + gg * hd, hd)
    put(wkv_ref, 0, IN_KV, 2 * KV_WIDTH)
    put(ws_ref, 0, IN_DT, SSM_HEADS)
    put(ws_ref, SMALL_GATE, IN_GATE, 3 * ATT_HEADS)
    used = SMALL_GATE + 3 * ATT_HEADS
    ws_ref[used:, :] = jnp.zeros((128 - used, ws_ref.shape[1]), BF16)


def _split_w_in(w_in):
    dm, n_in = w_in.shape
    tc = 128
    return pl.pallas_call(
        _wprep_kernel,
        grid=(dm // tc,),
        in_specs=[pl.BlockSpec((n_in, tc), lambda i: (0, i))],
        out_specs=[pl.BlockSpec((NB, tc), lambda i: (0, i)), pl.BlockSpec((128, tc), lambda i: (0, i)),
                   pl.BlockSpec((2 * KV_WIDTH, tc), lambda i: (0, i))],
        out_shape=[jax.ShapeDtypeStruct((NB, dm), BF16), jax.ShapeDtypeStruct((128, dm), BF16),
                   jax.ShapeDtypeStruct((2 * KV_WIDTH, dm), BF16)],
        compiler_params=_cp(("arbitrary",)),
        name="wprep",
    )(w_in.T)


def _layer(x, c, w_ada, b_ada, g_pre, g_post, w_in, conv_w, conv_b, dt_bias, a_log, d_skip,
           g_ssm_norm, w_ssm_out, cmp_pos_k, cmp_w1_k, cmp_w2_k, cmp_pos_v, cmp_w1_v, cmp_w2_v,
           w_nsa_out, w_out):
    bsz, s, dm = x.shape
    mod = _ada(c, w_ada, b_ada)
    shift = mod[:, None, :dm]
    scale = mod[:, None, dm:2 * dm]
    gate = mod[:, None, 2 * dm:]
    wb, w_small, w_kvc = _split_w_in(w_in)
    pb, small, kvc = _proj(x, shift, scale, g_pre, wb, w_small, w_kvc)

    yssm = _ssm(pb, small, conv_w, conv_b, dt_bias, a_log, d_skip, g_ssm_norm)

    pos = jnp.stack([cmp_pos_k.reshape(1, -1), cmp_pos_v.reshape(1, -1)])
    w1 = jnp.stack([cmp_w1_k, cmp_w1_v]).astype(BF16)
    w2 = jnp.stack([cmp_w2_k, cmp_w2_v]).astype(BF16)
    cmp_kv = _compress(kvc, pos, w1, w2)

    ocmp, qmask = _cmpattn(pb, cmp_kv, s)
    onsa = _attn(pb, small, qmask, ocmp)
    return _merge(yssm, onsa, pb, x, gate, g_post, w_ssm_out.astype(BF16), w_nsa_out.astype(BF16),
                  w_out.astype(BF16))


@jax.jit
def kernel(x, c, w_ada, b_ada, g_pre, g_post, w_in, conv_w, conv_b, dt_bias, a_log, d_skip, g_ssm_norm,
           w_ssm_out, cmp_pos_k, cmp_w1_k, cmp_w2_k, cmp_pos_v, cmp_w1_v, cmp_w2_v, w_nsa_out, w_out):
    for layer in range(w_in.shape[0]):
        x = _layer(x, c, w_ada[layer], b_ada[layer], g_pre[layer], g_post[layer], w_in[layer],
                   conv_w[layer], conv_b[layer], dt_bias[layer], a_log[layer], d_skip[layer],
                   g_ssm_norm[layer], w_ssm_out[layer], cmp_pos_k[layer], cmp_w1_k[layer],
                   cmp_w2_k[layer], cmp_pos_v[layer], cmp_w1_v[layer], cmp_w2_v[layer],
                   w_nsa_out[layer], w_out[layer])
    return x
```

```python
import numpy as np
import jax
import jax.numpy as jnp
from jax import lax
from jax.experimental import pallas as pl
from jax.experimental.pallas import tpu as pltpu

D_MODEL = 1024
D_INNER = 2048
SSM_HEAD_DIM = 64
SSM_HEADS = 32
SSM_GROUPS = 4
SSM_STATE = 128
SSM_CONV = 4
SSM_CHUNK = 128
D_CONV = D_INNER + 2 * SSM_GROUPS * SSM_STATE
CONV_TAIL = 16

ATT_HEADS = 16
ATT_HEAD_DIM = 64
ATT_KV_GROUPS = 4
ATT_HPG = 4
ATT_WIDTH = 1024
KV_WIDTH = 256
CMP_BLOCK = 32
CMP_STRIDE = 16
CMP_HIDDEN = 256
SLC_BLOCK = 64
SLC_SHIFT = 6
SLC_TOPK = 16
WINDOW = 512
FORCE_BONUS = 1000.0
EPS = 1e-6
NEG = -1e30
BIG = 2.0 ** 100

F32 = jnp.float32
BF16 = jnp.bfloat16
HIGHEST = lax.Precision.HIGHEST
NT = (((1,), (1,)), ((), ()))

_SIZES = (D_INNER, D_CONV, SSM_HEADS, ATT_WIDTH, 6 * KV_WIDTH, 3 * ATT_HEADS, ATT_WIDTH, 2 * D_MODEL)
_OFFS = tuple(int(v) for v in np.cumsum((0,) + _SIZES))
IN_Z, IN_XBC, IN_DT, IN_Q, IN_KV, IN_GATE, IN_ZATT, IN_MERGE = _OFFS[:8]

B_XBC, B_ZATT, B_MERGE, B_ZSSM, B_Q, B_KSEL, B_KWIN = 0, 3072, 4096, 6144, 8192, 9216, 9728
NB = 10240
SMALL_GATE = 32
LOG2E = 1.4426950408889634

TM_PROJ = 1024
TN_PROJ = 2560
TQ = 512
TK = 512
QH = 256
KT = 256
POS_BASE = 256
TQ_CMP = 2048
VMEM_LIMIT = 48 * 1024 * 1024

L_POS = 64
L_TILE = 67
L_MASK = 72
SLOPE_STRIDE = 8


def _cp(sem):
    return pltpu.CompilerParams(dimension_semantics=sem, vmem_limit_bytes=VMEM_LIMIT)


def _sigmoid(v):
    return 0.5 * jnp.tanh(0.5 * v) + 0.5


def _silu(v):
    return v * _sigmoid(v)


def _ada_kernel(c_ref, w_ref, b_ref, o_ref):
    o_ref[...] = jnp.dot(c_ref[...], w_ref[...], preferred_element_type=F32,
                         precision=HIGHEST) + b_ref[...]


def _ada(c, w_ada, b_ada):
    bsz = c.shape[0]
    return pl.pallas_call(
        _ada_kernel,
        grid=(3,),
        in_specs=[pl.BlockSpec((bsz, D_MODEL), lambda j: (0, 0)),
                  pl.BlockSpec((D_MODEL, D_MODEL), lambda j: (0, j)),
                  pl.BlockSpec((1, D_MODEL), lambda j: (0, j))],
        out_specs=pl.BlockSpec((bsz, D_MODEL), lambda j: (0, j)),
        out_shape=jax.ShapeDtypeStruct((bsz, 3 * D_MODEL), F32),
        compiler_params=_cp(("arbitrary",)),
        name="ada",
    )(c, w_ada, b_ada.reshape(1, 3 * D_MODEL))


def _proj_kernel(x_ref, shift_ref, scale_ref, g_ref, w_ref, ws_ref, wkv_ref, o_ref, osm_ref, okv_ref, h_scr):
    @pl.when(pl.program_id(2) == 0)
    def _():
        xf = x_ref[0]
        y = xf * lax.rsqrt(jnp.mean(xf * xf, axis=-1, keepdims=True) + EPS) * g_ref[...]
        h_scr[...] = (y * (1.0 + scale_ref[0]) + shift_ref[0]).astype(BF16)
        osm_ref[0] = lax.dot_general(h_scr[...], ws_ref[...], NT, preferred_element_type=F32)
        okv_ref[0] = lax.dot_general(h_scr[...], wkv_ref[...], NT,
                                     preferred_element_type=F32).astype(okv_ref.dtype)

    o_ref[0] = lax.dot_general(h_scr[...], w_ref[...], NT, preferred_element_type=F32).astype(o_ref.dtype)


def _proj(x, shift, scale, g_pre, w, w_small, w_kvc):
    bsz, s, _ = x.shape
    n = w.shape[0]
    nkv = w_kvc.shape[0]
    return pl.pallas_call(
        _proj_kernel,
        grid=(bsz, s // TM_PROJ, n // TN_PROJ),
        in_specs=[pl.BlockSpec((1, TM_PROJ, D_MODEL), lambda b, i, j: (b, i, 0)),
                  pl.BlockSpec((1, 1, D_MODEL), lambda b, i, j: (b, 0, 0)),
                  pl.BlockSpec((1, 1, D_MODEL), lambda b, i, j: (b, 0, 0)),
                  pl.BlockSpec((1, D_MODEL), lambda b, i, j: (0, 0)),
                  pl.BlockSpec((TN_PROJ, D_MODEL), lambda b, i, j: (j, 0)),
                  pl.BlockSpec((128, D_MODEL), lambda b, i, j: (0, 0)),
                  pl.BlockSpec((nkv, D_MODEL), lambda b, i, j: (0, 0))],
        out_specs=[pl.BlockSpec((1, TM_PROJ, TN_PROJ), lambda b, i, j: (b, i, j)),
                   pl.BlockSpec((1, TM_PROJ, 128), lambda b, i, j: (b, i, 0)),
                   pl.BlockSpec((1, TM_PROJ, nkv), lambda b, i, j: (b, i, 0))],
        out_shape=[jax.ShapeDtypeStruct((bsz, s, n), BF16),
                   jax.ShapeDtypeStruct((bsz, s, 128), F32),
                   jax.ShapeDtypeStruct((bsz, s, nkv), F32)],
        scratch_shapes=[pltpu.VMEM((TM_PROJ, D_MODEL), BF16)],
        compiler_params=_cp(("arbitrary", "arbitrary", "arbitrary")),
        name="proj",
    )(x, shift, scale, g_pre.reshape(1, D_MODEL), w, w_small, w_kvc)


def _ssm_kernel(xbc_ref, z_ref, small_ref, shift_ref, spread_ref, convw_ref, convb_ref, dtb_ref,
                alog_ref, dskip_ref, gn_ref, o_ref, tail_scr, state_scr, y_scr):
    cl = SSM_CHUNK

    @pl.when(pl.program_id(1) == 0)
    def _():
        tail_scr[...] = jnp.zeros(tail_scr.shape, BF16)
        state_scr[...] = jnp.zeros(state_scr.shape, F32)

    cur = xbc_ref[0]
    xpad = jnp.concatenate([tail_scr[...], cur], axis=0).astype(F32)
    taps = jnp.concatenate([(xpad * convw_ref[k:k + 1, :]).astype(BF16) for k in range(SSM_CONV - 1)],
                           axis=0)
    tail_scr[...] = cur[cl - CONV_TAIL:cl, :]
    acc = (jnp.dot(shift_ref[...], taps, preferred_element_type=F32)
           + convw_ref[SSM_CONV - 1:SSM_CONV, :] * xpad[CONV_TAIL:] + convb_ref[...])
    u = _silu(acc)
    xs = u[:, :D_INNER]
    bm = u[:, D_INNER:D_INNER + SSM_GROUPS * SSM_STATE]
    cm = u[:, D_INNER + SSM_GROUPS * SSM_STATE:]

    pre = small_ref[0] + dtb_ref[...]
    dt = jnp.maximum(pre, 0.0) + jnp.log1p(jnp.exp(-jnp.abs(pre)))
    a = -jnp.exp(alog_ref[...])
    adt = dt * a
    row = lax.broadcasted_iota(jnp.int32, (cl, cl), 0)
    col = lax.broadcasted_iota(jnp.int32, (cl, cl), 1)
    causal = row >= col
    tri = causal.astype(F32)
    a_cs = jnp.dot(tri, adt, preferred_element_type=F32, precision=HIGHEST)
    a_cs_t = a_cs.T
    dt_t = dt.T
    a_last = a_cs[cl - 1:cl, :]
    ea = jnp.exp(a_cs)
    dsc = jnp.exp(a_last - a_cs) * dt

    def spread(v):
        hi = v.astype(BF16)
        lo = (v - hi.astype(F32)).astype(BF16)
        return jnp.dot(jnp.concatenate([hi, lo], axis=1), spread_ref[...], preferred_element_type=F32)

    ea_x = spread(ea)
    dsc_x = spread(dsc)
    cdec_x = ea_x[cl - 1:cl, :]
    xsd = xs * dsc_x
    lane = lax.broadcasted_iota(jnp.int32, (cl, 2 * SSM_HEAD_DIM), 1)
    first_half = lane < SSM_HEAD_DIM

    hg = SSM_HEADS // SSM_GROUPS
    gw = hg * SSM_HEAD_DIM
    for g in range(SSM_GROUPS):
        bg = bm[:, g * SSM_STATE:(g + 1) * SSM_STATE].astype(BF16)
        cg = cm[:, g * SSM_STATE:(g + 1) * SSM_STATE].astype(BF16)
        gc = slice(g * gw, (g + 1) * gw)
        cb = lax.dot_general(cg, bg, NT, preferred_element_type=F32)
        st = state_scr[g]
        y_scr[:, gc] = jnp.dot(cg, st.astype(BF16), preferred_element_type=F32) * ea_x[:, gc]
        new = lax.dot_general(bg, xsd[:, gc].astype(BF16), (((0,), (0,)), ((), ())),
                              preferred_element_type=F32)
        state_scr[g] = st * cdec_x[:, gc] + new
        for pp in range(hg // 2):
            pair = g * (hg // 2) + pp
            pc = slice(pair * 128, (pair + 1) * 128)
            gm = []
            for e in range(2):
                h = 2 * pair + e
                seg = a_cs[:, h:h + 1] - a_cs_t[h:h + 1, :]
                lmat = jnp.exp(jnp.where(causal, seg, NEG))
                gm.append(cb * lmat * dt_t[h:h + 1, :])
            lhs = jnp.concatenate(gm, axis=0).astype(BF16)
            yy = jnp.dot(lhs, xs[:, pc].astype(BF16), preferred_element_type=F32)
            y_scr[:, pc] = y_scr[:, pc] + jnp.where(first_half, yy[:cl], yy[cl:])

    y = y_scr[...] + xs * dskip_ref[...]
    y = y * _silu(z_ref[0].astype(F32))
    gsz = D_INNER // SSM_GROUPS
    for g in range(SSM_GROUPS):
        yg = y[:, g * gsz:(g + 1) * gsz]
        yn = yg * lax.rsqrt(jnp.mean(yg * yg, axis=-1, keepdims=True) + EPS)
        o_ref[0, :, g * gsz:(g + 1) * gsz] = (yn * gn_ref[:, g * gsz:(g + 1) * gsz]).astype(o_ref.dtype)


def _conv_shift_matrix():
    cl = SSM_CHUNK
    ext = CONV_TAIL + cl
    m = np.zeros((cl, (SSM_CONV - 1) * ext), np.float32)
    t = np.arange(cl)
    for k in range(SSM_CONV - 1):
        m[t, k * ext + CONV_TAIL + t - (SSM_CONV - 1 - k)] = 1.0
    return jnp.asarray(m, dtype=BF16)


def _head_spread_matrix():
    m = np.zeros((2, 128, D_INNER), np.float32)
    for h in range(SSM_HEADS):
        m[:, h, h * SSM_HEAD_DIM:(h + 1) * SSM_HEAD_DIM] = 1.0
    return jnp.asarray(m.reshape(256, D_INNER), dtype=BF16)


def _ssm(pb, small, conv_w, conv_b, dt_bias, a_log, d_skip, g_norm):
    bsz, s, _ = pb.shape
    cl = SSM_CHUNK
    pad = 128 - SSM_HEADS
    dtb = jnp.pad(dt_bias, (0, pad)).reshape(1, 128)
    alog = jnp.pad(a_log, (0, pad)).reshape(1, 128)
    dskip = jnp.repeat(d_skip, SSM_HEAD_DIM).reshape(1, D_INNER)
    const = lambda shape: pl.BlockSpec(shape, lambda b, c: (0, 0))
    return pl.pallas_call(
        _ssm_kernel,
        grid=(bsz, s // cl),
        in_specs=[pl.BlockSpec((1, cl, D_CONV), lambda b, c: (b, c, B_XBC // D_CONV)),
                  pl.BlockSpec((1, cl, D_INNER), lambda b, c: (b, c, B_ZSSM // D_INNER)),
                  pl.BlockSpec((1, cl, 128), lambda b, c: (b, c, 0)),
                  const((cl, (SSM_CONV - 1) * (CONV_TAIL + cl))), const((256, D_INNER)),
                  const((SSM_CONV, D_CONV)), const((1, D_CONV)), const((1, 128)), const((1, 128)),
                  const((1, D_INNER)), const((1, D_INNER))],
        out_specs=pl.BlockSpec((1, cl, D_INNER), lambda b, c: (b, c, 0)),
        out_shape=jax.ShapeDtypeStruct((bsz, s, D_INNER), BF16),
        scratch_shapes=[pltpu.VMEM((CONV_TAIL, D_CONV), BF16),
                        pltpu.VMEM((SSM_GROUPS, SSM_STATE, D_INNER // SSM_GROUPS), F32),
                        pltpu.VMEM((cl, D_INNER), F32)],
        compiler_params=_cp(("arbitrary", "arbitrary")),
        name="ssm",
    )(pb, pb, small, _conv_shift_matrix(), _head_spread_matrix(), conv_w, conv_b.reshape(1, D_CONV),
      dtb, alog, dskip,
      g_norm.reshape(1, D_INNER))


def _cmp_kernel(r0_ref, r1_ref, r2_ref, r3_ref, pos_ref, w1_ref, w2_ref, o_ref):
    half = CMP_STRIDE * ATT_HEAD_DIM
    hd = ATT_HEAD_DIM
    nrow = r0_ref.shape[1] // CMP_STRIDE
    toks = [[r_ref[0, pl.ds(t, nrow, stride=CMP_STRIDE), :] for t in range(CMP_STRIDE)]
            for r_ref in (r0_ref, r1_ref, r2_ref, r3_ref)]
    for kv in range(2):
        w1 = w1_ref[kv]
        posb = jnp.broadcast_to(pos_ref[kv], (8, 2 * half)).astype(BF16)
        cpos = jnp.dot(posb, w1, preferred_element_type=F32)[0:1]
        for gg in range(ATT_KV_GROUPS):
            j = kv * ATT_KV_GROUPS + gg
            r = jnp.concatenate([toks[j // 2][t][:, (j % 2) * hd:(j % 2 + 1) * hd]
                                 for t in range(CMP_STRIDE)], axis=1).astype(BF16)
            first = jnp.dot(r, w1[:half], preferred_element_type=F32)
            second = jnp.dot(r, w1[half:], preferred_element_type=F32)
            hid = first + pltpu.roll(second, nrow - 1, 0) + cpos
            o_ref[0, j] = jnp.dot(_silu(hid).astype(BF16), w2_ref[kv], preferred_element_type=F32)


def _compress(kvc, pos, w1, w2):
    bsz, s, wid = kvc.shape
    nrow = s // CMP_STRIDE
    ng = 2 * ATT_KV_GROUPS
    return pl.pallas_call(
        _cmp_kernel,
        grid=(bsz,),
        in_specs=[pl.BlockSpec((1, s, 128), lambda b, c=c: (b, 0, c)) for c in range(wid // 128)]
                 + [pl.BlockSpec((2, 1, 2048), lambda b: (0, 0, 0)),
                  pl.BlockSpec((2, 2048, CMP_HIDDEN), lambda b: (0, 0, 0)),
                  pl.BlockSpec((2, CMP_HIDDEN, ATT_HEAD_DIM), lambda b: (0, 0, 0))],
        out_specs=pl.BlockSpec((1, ng, nrow, ATT_HEAD_DIM), lambda b: (b, 0, 0, 0)),
        out_shape=jax.ShapeDtypeStruct((bsz, ng, nrow, ATT_HEAD_DIM), F32),
        compiler_params=_cp(("arbitrary",)),
        name="cmp",
    )(kvc, kvc, kvc, kvc, pos, w1, w2)


def _bf16_round_np(x):
    u = np.asarray(x, np.float32).view(np.uint32)
    u = (u + (((u >> 16) & 1) + 0x7FFF)) & np.uint32(0xFFFF0000)
    return u.view(np.float32)


def _slope_table():
    slope = (2.0 ** (-8.0 * np.arange(1, ATT_HEADS + 1) / ATT_HEADS)).astype(np.float32)
    slope = (slope.astype(np.float64) * LOG2E).astype(np.float32)
    p0 = _bf16_round_np(slope)
    p1 = _bf16_round_np(slope - p0)
    p2 = _bf16_round_np(slope - p0 - p1)
    tab = np.zeros((ATT_HEADS, SLOPE_STRIDE), np.float32)
    tab[:, 0], tab[:, 1], tab[:, 2] = p0, p1, p2
    tab[:, 3:6] = tab[:, 0:3] * POS_BASE
    tab[:, 6] = slope
    return jnp.asarray(tab.reshape(-1))


def _key_aux_table(s):
    pos = np.arange(s)
    tab = np.zeros((s, 128), np.float32)
    tab[:, L_POS:L_POS + 3] = (pos % POS_BASE)[:, None]
    tab[:, L_TILE:L_TILE + 3] = (pos // POS_BASE)[:, None]
    tab[pos, L_MASK + pos // SLC_BLOCK] = 1.0
    return jnp.asarray(tab, dtype=BF16)


def _bias_table():
    r = np.arange(QH)[:, None]
    c = np.arange(KT)[None, :]
    tabs = []
    for jj in range(3):
        d = (2 - jj) * KT + r - c
        tabs.append(np.where((d >= 0) & (d < WINDOW), 0.0, NEG))
    tabs.append(np.where(r - c >= 0, 0.0, NEG))
    return jnp.asarray(np.stack(tabs), dtype=F32)


def _gate_expand_table():
    tab = np.zeros((ATT_KV_GROUPS, 2, 128, 3 * KV_WIDTH), np.float32)
    for g in range(ATT_KV_GROUPS):
        for h in range(ATT_HPG):
            for j in range(3):
                row = SMALL_GATE + g * 3 * ATT_HPG + 3 * h + j
                tab[g, :, row, j * KV_WIDTH + h * ATT_HEAD_DIM:j * KV_WIDTH + (h + 1) * ATT_HEAD_DIM] = 1.0
    return jnp.asarray(tab.reshape(ATT_KV_GROUPS, 256, 3 * KV_WIDTH), dtype=BF16)


def _cmp_to_slc_matrix_t(n_cmp_pad, n_cmp, n_slc):
    cs = np.arange(n_cmp_pad) * CMP_STRIDE
    ss = np.arange(n_slc) * SLC_BLOCK
    lo = np.maximum(cs[:, None], ss[None, :])
    hi = np.minimum(cs[:, None] + CMP_BLOCK, ss[None, :] + SLC_BLOCK)
    m = np.clip(hi - lo, 0, None) / CMP_BLOCK
    m[n_cmp:] = 0.0
    return jnp.asarray(m.T, dtype=F32)


def _cmpattn_kernel(slopes_ref, q_ref, ck_ref, cv_ref, mt_ref, ocmp_ref, qmask_ref,
                    khi_scr, klo_scr, vbd_scr):
    g = pl.program_id(1)
    i = pl.program_id(2)
    tq = q_ref.shape[1]
    t0 = i * tq
    npad = ck_ref.shape[2]
    n_slc = mt_ref.shape[0]

    @pl.when(i == 0)
    def _():
        kc = ck_ref[0, 0]
        vc = cv_ref[0, 0]
        k_hi = kc.astype(BF16).astype(F32)
        k_lo = kc - k_hi
        zero = jnp.zeros_like(kc)
        for h in range(ATT_HPG):
            def bd(a):
                return jnp.concatenate([a if hh == h else zero for hh in range(ATT_HPG)], axis=1).astype(BF16)
            khi_scr[h * npad:(h + 1) * npad, :] = bd(k_hi)
            klo_scr[h * npad:(h + 1) * npad, :] = bd(k_lo)
            vbd_scr[h * npad:(h + 1) * npad, :] = bd(vc)

    q = q_ref[0]
    s = (lax.dot_general(q, khi_scr[...], NT, preferred_element_type=F32)
         + lax.dot_general(q, klo_scr[...], NT, preferred_element_type=F32))
    jcol = lax.broadcasted_iota(jnp.int32, (1, npad), 1)
    cmp_end = jcol * CMP_STRIDE + (CMP_BLOCK - 1)
    trow = t0 + lax.broadcasted_iota(jnp.int32, (tq, npad), 0)
    valid = (cmp_end <= trow) & (jcol < npad - 1)
    rel = (cmp_end - t0).astype(F32)
    psum = jnp.zeros((tq, npad), F32)
    ps = []
    for h in range(ATT_HPG):
        slope = slopes_ref[(g * ATT_HPG + h) * SLOPE_STRIDE + 6]
        sh = jnp.where(valid, s[:, h * npad:(h + 1) * npad] + slope * rel, NEG)
        mx = jnp.max(sh, axis=-1, keepdims=True)
        p = jnp.where(valid, jnp.exp2(sh - mx), 0.0)
        l = jnp.sum(p, axis=-1, keepdims=True)
        p = p * jnp.where(l > 0.0, 1.0 / jnp.where(l > 0.0, l, 1.0), 0.0)
        ps.append(p)
        psum = psum + p
    ocmp_ref[0] = jnp.dot(jnp.concatenate(ps, axis=1).astype(BF16), vbd_scr[...],
                          preferred_element_type=F32)
    imp_t = lax.dot_general(mt_ref[...], psum, NT, preferred_element_type=F32,
                            precision=HIGHEST)

    nrb = n_slc // 8
    blk_t = (t0 + lax.broadcasted_iota(jnp.int32, (8, tq), 1)) >> SLC_SHIFT
    sub = lax.broadcasted_iota(jnp.int32, (8, tq), 0)
    score = []
    for rb in range(nrb):
        kk = sub + rb * 8
        imp = imp_t[rb * 8:(rb + 1) * 8]
        forced = (kk == 0) | (kk == blk_t) | (kk == blk_t - 1)
        score.append(jnp.where(forced, imp + FORCE_BONUS, jnp.where(kk <= blk_t, imp, -1.0)))
    rank = [jnp.zeros((8, tq), F32) for _ in range(nrb)]
    for j in range(n_slc):
        cj = jnp.broadcast_to(score[j // 8][j % 8:j % 8 + 1, :], (8, tq))
        for rb in range(nrb):
            ge = jnp.where(cj >= score[rb], 1.0, 0.0)
            gt = jnp.where(cj > score[rb], 1.0, 0.0)
            if rb * 8 > j:
                beats = ge
            elif rb * 8 + 7 <= j:
                beats = gt
            else:
                beats = jnp.where(sub > j - rb * 8, ge, gt)
            rank[rb] = rank[rb] + beats
    rows = [jnp.zeros((L_MASK, tq), F32)]
    rows += [jnp.where(rank[rb] < float(SLC_TOPK), 0.0, -BIG) for rb in range(nrb)]
    rows += [jnp.zeros((128 - L_MASK - n_slc, tq), F32)]
    qmask_ref[0, 0] = jnp.concatenate(rows, axis=0).T.astype(BF16)


def _cmpattn(pb, cmp_kv, s):
    bsz = pb.shape[0]
    g = ATT_KV_GROUPS
    nrow = cmp_kv.shape[2]
    n_cmp = (s - CMP_BLOCK) // CMP_STRIDE + 1
    n_slc = s // SLC_BLOCK
    mt = _cmp_to_slc_matrix_t(nrow, n_cmp, n_slc)
    return pl.pallas_call(
        _cmpattn_kernel,
        grid=(bsz, g, s // TQ_CMP),
        in_specs=[pl.BlockSpec(memory_space=pltpu.SMEM),
                  pl.BlockSpec((1, TQ_CMP, KV_WIDTH), lambda b, gg, i: (b, i, B_Q // KV_WIDTH + gg)),
                  pl.BlockSpec((1, 1, nrow, ATT_HEAD_DIM), lambda b, gg, i: (b, gg, 0, 0)),
                  pl.BlockSpec((1, 1, nrow, ATT_HEAD_DIM), lambda b, gg, i: (b, g + gg, 0, 0)),
                  pl.BlockSpec((n_slc, nrow), lambda b, gg, i: (0, 0))],
        out_specs=[pl.BlockSpec((1, TQ_CMP, KV_WIDTH), lambda b, gg, i: (b, i, gg)),
                   pl.BlockSpec((1, 1, TQ_CMP, 128), lambda b, gg, i: (b, gg, i, 0))],
        out_shape=[jax.ShapeDtypeStruct((bsz, s, ATT_WIDTH), F32),
                   jax.ShapeDtypeStruct((bsz, g, s, 128), BF16)],
        scratch_shapes=[pltpu.VMEM((ATT_HPG * nrow, KV_WIDTH), BF16),
                        pltpu.VMEM((ATT_HPG * nrow, KV_WIDTH), BF16),
                        pltpu.VMEM((ATT_HPG * nrow, KV_WIDTH), BF16)],
        compiler_params=_cp(("arbitrary", "arbitrary", "arbitrary")),
        name="cmpattn",
    )(_slope_table(), pb, cmp_kv, cmp_kv, mt)


def _flash_step(qa_ref, kv_ref, aux_ref, m_scr, acc_scr, rows, ksub, nsub, bias, first=False):
    r0, nr = rows
    nk = nsub * KT
    start = pl.multiple_of(ksub * KT, KT)
    kv = kv_ref[0, pl.ds(start, nk), :]
    left = jnp.where(lax.broadcasted_iota(jnp.int32, (nk, 128), 1) < ATT_HEAD_DIM, 1.0, 0.0).astype(BF16)
    k_aug = kv * left + aux_ref[pl.ds(start, nk), :]
    v_aug = kv * (1.0 - left) + left
    s = lax.dot_general(qa_ref[r0:r0 + nr, :], k_aug, NT, preferred_element_type=F32)
    if bias is not None:
        parts = []
        for t, b in enumerate(bias):
            st = s[:, t * KT:(t + 1) * KT]
            if b is not None:
                st = (st.reshape(nr // QH, QH, KT) + b[None]).reshape(nr, KT)
            parts.append(st)
        s = jnp.concatenate(parts, axis=1) if nsub > 1 else parts[0]
    m_cur = jnp.max(s, axis=-1, keepdims=True)
    if first:
        m_new = jnp.broadcast_to(m_cur, (nr, 128))
    else:
        m_prev = m_scr[r0:r0 + nr, :]
        m_new = jnp.maximum(m_prev, m_cur)
    p = jnp.exp2(s - jnp.concatenate([m_new] * (nk // 128), axis=1))
    pv_t = lax.dot_general(v_aug, p.astype(BF16), (((0,), (1,)), ((), ())),
                           preferred_element_type=F32)
    if first:
        acc_scr[:, r0:r0 + nr] = pv_t
    else:
        acc_scr[:, r0:r0 + nr] = jnp.exp2(m_prev - m_new).T * acc_scr[:, r0:r0 + nr] + pv_t
    m_scr[r0:r0 + nr, :] = m_new


def _flash_out(acc_scr):
    r = acc_scr.shape[1]
    den = acc_scr[0:8, :]
    num = acc_scr[ATT_HEAD_DIM:2 * ATT_HEAD_DIM, :]
    o_t = (num.reshape(ATT_HEAD_DIM // 8, 8, r) / den[None]).reshape(ATT_HEAD_DIM, r)
    halves = []
    for half in range(TQ // QH):
        cols = [o_t[:, (half * ATT_HPG + h) * QH:(half * ATT_HPG + h + 1) * QH] for h in range(ATT_HPG)]
        halves.append(jnp.concatenate(cols, axis=0).T)
    return jnp.concatenate(halves, axis=0)


def _attn_kernel(slopes_ref, q_ref, ksel_ref, kwin_ref, aux_ref, qmask_ref, ocmp_ref, small_ref,
                 zatt_ref, pick_ref, bias_ref, o_ref, qas_scr, qaw_scr, m_scr, acc_scr):
    g = pl.program_id(1)
    i = pl.program_id(2)

    qf = q_ref[0].astype(F32)
    qm = qmask_ref[0, 0].astype(F32)
    lane_row = lax.broadcasted_iota(jnp.int32, (1, 128), 1)
    left = lax.broadcasted_iota(jnp.int32, (TQ, 128), 1) < ATT_HEAD_DIM
    for h in range(ATT_HPG):
        slab = qf[:, 128 * (h // 2):128 * (h // 2) + 128]
        if h % 2:
            slab = pltpu.roll(slab, ATT_HEAD_DIM, 1)
        ext = jnp.zeros((1, 128), F32)
        for c in range(6):
            ext = jnp.where(lane_row == L_POS + c, slopes_ref[(g * ATT_HPG + h) * SLOPE_STRIDE + c], ext)
        base = jnp.where(left, slab, ext)
        for half in range(TQ // QH):
            rs = (half * ATT_HPG + h) * QH
            tok = slice(half * QH, (half + 1) * QH)
            qaw_scr[rs:rs + QH, :] = base[tok].astype(BF16)
            qas_scr[rs:rs + QH, :] = (base[tok] + qm[tok]).astype(BF16)

    half_rows = ATT_HPG * QH
    rows_a, rows_b, rows_all = (0, half_rows), (half_rows, half_rows), (0, 2 * half_rows)
    k0 = i * (TQ // KT)
    w0, w1, w2, causal = (bias_ref[n] for n in range(4))

    _flash_step(qas_scr, ksel_ref, aux_ref, m_scr, acc_scr, rows_a, k0, 1, [causal], first=True)
    _flash_step(qas_scr, ksel_ref, aux_ref, m_scr, acc_scr, rows_b, k0, 2, [None, causal], first=True)

    def sel_body(j, carry):
        _flash_step(qas_scr, ksel_ref, aux_ref, m_scr, acc_scr, rows_all, j * (TK // KT), TK // KT, None)
        return carry

    lax.fori_loop(0, i, sel_body, 0)
    o_slc = _flash_out(acc_scr)

    @pl.when(i == 0)
    def _():
        _flash_step(qaw_scr, kwin_ref, aux_ref, m_scr, acc_scr, rows_a, 0, 1, [w2], first=True)
        _flash_step(qaw_scr, kwin_ref, aux_ref, m_scr, acc_scr, rows_b, 0, 2, [w1, w2], first=True)

    @pl.when(i >= 1)
    def _():
        _flash_step(qaw_scr, kwin_ref, aux_ref, m_scr, acc_scr, rows_a, k0 - 2, 3, [w0, w1, w2], first=True)
        _flash_step(qaw_scr, kwin_ref, aux_ref, m_scr, acc_scr, rows_b, k0 - 1, 3, [w0, w1, w2], first=True)

    o_win = _flash_out(acc_scr)

    gate = _sigmoid(small_ref[0])
    g_hi = gate.astype(BF16)
    g_lo = (gate - g_hi.astype(F32)).astype(BF16)
    gts = jnp.dot(jnp.concatenate([g_hi, g_lo], axis=1), pick_ref[0],
                  preferred_element_type=F32)
    o = (gts[:, :KV_WIDTH] * ocmp_ref[0] + gts[:, KV_WIDTH:2 * KV_WIDTH] * o_slc
         + gts[:, 2 * KV_WIDTH:] * o_win)
    o_ref[0] = (o * _silu(zatt_ref[0].astype(F32))).astype(o_ref.dtype)


def _attn(pb, small, qmask, ocmp):
    bsz, s, _ = pb.shape
    g = ATT_KV_GROUPS
    r = ATT_HPG * TQ
    return pl.pallas_call(
        _attn_kernel,
        grid=(bsz, g, s // TQ),
        in_specs=[pl.BlockSpec(memory_space=pltpu.SMEM),
                  pl.BlockSpec((1, TQ, KV_WIDTH), lambda b, gg, i: (b, i, B_Q // KV_WIDTH + gg)),
                  pl.BlockSpec((1, s, 128), lambda b, gg, i: (b, 0, B_KSEL // 128 + gg)),
                  pl.BlockSpec((1, s, 128), lambda b, gg, i: (b, 0, B_KWIN // 128 + gg)),
                  pl.BlockSpec((s, 128), lambda b, gg, i: (0, 0)),
                  pl.BlockSpec((1, 1, TQ, 128), lambda b, gg, i: (b, gg, i, 0)),
                  pl.BlockSpec((1, TQ, KV_WIDTH), lambda b, gg, i: (b, i, gg)),
                  pl.BlockSpec((1, TQ, 128), lambda b, gg, i: (b, i, 0)),
                  pl.BlockSpec((1, TQ, KV_WIDTH), lambda b, gg, i: (b, i, B_ZATT // KV_WIDTH + gg)),
                  pl.BlockSpec((1, 256, 3 * KV_WIDTH), lambda b, gg, i: (gg, 0, 0)),
                  pl.BlockSpec((4, QH, KT), lambda b, gg, i: (0, 0, 0))],
        out_specs=pl.BlockSpec((1, TQ, KV_WIDTH), lambda b, gg, i: (b, i, gg)),
        out_shape=jax.ShapeDtypeStruct((bsz, s, ATT_WIDTH), BF16),
        scratch_shapes=[pltpu.VMEM((r, 128), BF16), pltpu.VMEM((r, 128), BF16),
                        pltpu.VMEM((r, 128), F32), pltpu.VMEM((128, r), F32)],
        compiler_params=_cp(("arbitrary", "arbitrary", "arbitrary")),
        name="attn",
    )(_slope_table(), pb, pb, pb, _key_aux_table(s), qmask, ocmp, small, pb, _gate_expand_table(),
      _bias_table())


def _merge_kernel(yssm_ref, onsa_ref, mg_ref, x_ref, gate_ref, gpost_ref, wssm_ref, wnsa_ref, wout_ref, o_ref):
    y_ssm = jnp.dot(yssm_ref[0], wssm_ref[...], preferred_element_type=F32)
    y_nsa = jnp.dot(onsa_ref[0], wnsa_ref[...], preferred_element_type=F32)
    mg = _sigmoid(mg_ref[0].astype(F32))
    merged = mg[:, :D_MODEL] * y_ssm + mg[:, D_MODEL:] * y_nsa
    out = jnp.dot(merged.astype(BF16), wout_ref[...], preferred_element_type=F32)
    yn = out * lax.rsqrt(jnp.mean(out * out, axis=-1, keepdims=True) + EPS) * gpost_ref[...]
    o_ref[0] = x_ref[0] + gate_ref[0] * yn


def _merge(yssm, onsa, pb, x, gate, g_post, w_ssm_out, w_nsa_out, w_out):
    bsz, s, _ = x.shape
    tm = 512
    const = lambda shape: pl.BlockSpec(shape, lambda b, i: (0, 0))
    return pl.pallas_call(
        _merge_kernel,
        grid=(bsz, s // tm),
        in_specs=[pl.BlockSpec((1, tm, D_INNER), lambda b, i: (b, i, 0)),
                  pl.BlockSpec((1, tm, ATT_WIDTH), lambda b, i: (b, i, 0)),
                  pl.BlockSpec((1, tm, 2 * D_MODEL), lambda b, i: (b, i, B_MERGE // (2 * D_MODEL))),
                  pl.BlockSpec((1, tm, D_MODEL), lambda b, i: (b, i, 0)),
                  pl.BlockSpec((1, 1, D_MODEL), lambda b, i: (b, 0, 0)),
                  const((1, D_MODEL)),
                  const((D_INNER, D_MODEL)), const((ATT_WIDTH, D_MODEL)), const((D_MODEL, D_MODEL))],
        out_specs=pl.BlockSpec((1, tm, D_MODEL), lambda b, i: (b, i, 0)),
        out_shape=jax.ShapeDtypeStruct((bsz, s, D_MODEL), F32),
        compiler_params=_cp(("arbitrary", "arbitrary")),
        name="merge",
    )(yssm, onsa, pb, x, gate, g_post.reshape(1, D_MODEL), w_ssm_out, w_nsa_out, w_out)


def _wprep_kernel(w_ref, wb_ref, ws_ref, wkv_ref):
    def put(ref, dst, off, n, scale=None):
        v = w_ref[off:off + n, :]
        ref[dst:dst + n, :] = (v if scale is None else v * scale).astype(BF16)

    put(wb_ref, B_XBC, IN_XBC, D_CONV)
    put(wb_ref, B_ZATT, IN_ZATT, ATT_WIDTH)
    put(wb_ref, B_MERGE, IN_MERGE, 2 * D_MODEL)
    put(wb_ref, B_ZSSM, IN_Z, D_INNER)
    put(wb_ref, B_Q, IN_Q, ATT_WIDTH, ATT_HEAD_DIM ** -0.5 * LOG2E)
    hd = ATT_HEAD_DIM
    for dst, k_off in ((B_KSEL, IN_KV + 2 * KV_WIDTH), (B_KWIN, IN_KV + 4 * KV_WIDTH)):
        for gg in range(ATT_KV_GROUPS):
            put(wb_ref, dst + 2 * gg * hd, k_off + gg * hd, hd)
            put(wb_ref, dst + (2 * gg + 1) * hd, k_off + KV_WIDTH + gg * hd, hd)
    put(wkv_ref, 0, IN_KV, 2 * KV_WIDTH)
    put(ws_ref, 0, IN_DT, SSM_HEADS)
    put(ws_ref, SMALL_GATE, IN_GATE, 3 * ATT_HEADS)
    used = SMALL_GATE + 3 * ATT_HEADS
    ws_ref[used:, :] = jnp.zeros((128 - used, ws_ref.shape[1]), BF16)


def _split_w_in(w_in):
    dm, n_in = w_in.shape
    tc = 128
    return pl.pallas_call(
        _wprep_kernel,
        grid=(dm // tc,),
        in_specs=[pl.BlockSpec((n_in, tc), lambda i: (0, i))],
        out_specs=[pl.BlockSpec((NB, tc), lambda i: (0, i)), pl.BlockSpec((128, tc), lambda i: (0, i)),
                   pl.BlockSpec((2 * KV_WIDTH, tc), lambda i: (0, i))],
        out_shape=[jax.ShapeDtypeStruct((NB, dm), BF16), jax.ShapeDtypeStruct((128, dm), BF16),
                   jax.ShapeDtypeStruct((2 * KV_WIDTH, dm), BF16)],
        compiler_params=_cp(("arbitrary",)),
        name="wprep",
    )(w_in.T)


def _layer(x, c, w_ada, b_ada, g_pre, g_post, w_in, conv_w, conv_b, dt_bias, a_log, d_skip,
           g_ssm_norm, w_ssm_out, cmp_pos_k, cmp_w1_k, cmp_w2_k, cmp_pos_v, cmp_w1_v, cmp_w2_v,
           w_nsa_out, w_out):
    bsz, s, dm = x.shape
    mod = _ada(c, w_ada, b_ada)
    shift = mod[:, None, :dm]
    scale = mod[:, None, dm:2 * dm]
    gate = mod[:, None, 2 * dm:]
    wb, w_small, w_kvc = _split_w_in(w_in)
    pb, small, kvc = _proj(x, shift, scale, g_pre, wb, w_small, w_kvc)

    yssm = _ssm(pb, small, conv_w, conv_b, dt_bias, a_log, d_skip, g_ssm_norm)

    pos = jnp.stack([cmp_pos_k.reshape(1, -1), cmp_pos_v.reshape(1, -1)])
    w1 = jnp.stack([cmp_w1_k, cmp_w1_v]).astype(BF16)
    w2 = jnp.stack([cmp_w2_k, cmp_w2_v]).astype(BF16)
    cmp_kv = _compress(kvc, pos, w1, w2)

    ocmp, qmask = _cmpattn(pb, cmp_kv, s)
    onsa = _attn(pb, small, qmask, ocmp)
    return _merge(yssm, onsa, pb, x, gate, g_post, w_ssm_out.astype(BF16), w_nsa_out.astype(BF16),
                  w_out.astype(BF16))


@jax.jit
def kernel(x, c, w_ada, b_ada, g_pre, g_post, w_in, conv_w, conv_b, dt_bias, a_log, d_skip, g_ssm_norm,
           w_ssm_out, cmp_pos_k, cmp_w1_k, cmp_w2_k, cmp_pos_v, cmp_w1_v, cmp_w2_v, w_nsa_out, w_out):
    for layer in range(w_in.shape[0]):
        x = _layer(x, c, w_ada[layer], b_ada[layer], g_pre[layer], g_post[layer], w_in[layer],
                   conv_w[layer], conv_b[layer], dt_bias[layer], a_log[layer], d_skip[layer],
                   g_ssm_norm[layer], w_ssm_out[layer], cmp_pos_k[layer], cmp_w1_k[layer],
                   cmp_w2_k[layer], cmp_pos_v[layer], cmp_w1_v[layer], cmp_w2_v[layer],
                   w_nsa_out[layer], w_out[layer])
    return x
```

```python
import numpy as np
import jax
import jax.numpy as jnp
from jax import lax
from jax.experimental import pallas as pl
from jax.experimental.pallas import tpu as pltpu

D_MODEL = 1024
D_INNER = 2048
SSM_HEAD_DIM = 64
SSM_HEADS = 32
SSM_GROUPS = 4
SSM_STATE = 128
SSM_CONV = 4
SSM_CHUNK = 128
D_CONV = D_INNER + 2 * SSM_GROUPS * SSM_STATE
CONV_TAIL = 16

ATT_HEADS = 16
ATT_HEAD_DIM = 64
ATT_KV_GROUPS = 4
ATT_HPG = 4
ATT_WIDTH = 1024
KV_WIDTH = 256
CMP_BLOCK = 32
CMP_STRIDE = 16
CMP_HIDDEN = 256
SLC_BLOCK = 64
SLC_SHIFT = 6
SLC_TOPK = 16
WINDOW = 512
FORCE_BONUS = 1000.0
EPS = 1e-6
NEG = -1e30
BIG = 2.0 ** 100

F32 = jnp.float32
BF16 = jnp.bfloat16
HIGHEST = lax.Precision.HIGHEST
NT = (((1,), (1,)), ((), ()))

_SIZES = (D_INNER, D_CONV, SSM_HEADS, ATT_WIDTH, 6 * KV_WIDTH, 3 * ATT_HEADS, ATT_WIDTH, 2 * D_MODEL)
_OFFS = tuple(int(v) for v in np.cumsum((0,) + _SIZES))
IN_Z, IN_XBC, IN_DT, IN_Q, IN_KV, IN_GATE, IN_ZATT, IN_MERGE = _OFFS[:8]

B_XBC, B_ZATT, B_MERGE, B_ZSSM, B_Q, B_KSEL, B_KWIN = 0, 3072, 4096, 6144, 8192, 9216, 9728
NB = 10240
SMALL_GATE = 32
LOG2E = 1.4426950408889634

TM_PROJ = 1024
TN_PROJ = 2560
TQ = 512
TK = 512
QH = 256
KT = 256
POS_BASE = 256
TQ_CMP = 2048
VMEM_LIMIT = 48 * 1024 * 1024

L_POS = 64
L_TILE = 67
L_MASK = 72
SLOPE_STRIDE = 8


def _cp(sem):
    return pltpu.CompilerParams(dimension_semantics=sem, vmem_limit_bytes=VMEM_LIMIT)


def _sigmoid(v):
    return 0.5 * jnp.tanh(0.5 * v) + 0.5


def _silu(v):
    return v * _sigmoid(v)


def _ada_kernel(c_ref, w_ref, b_ref, o_ref):
    o_ref[...] = jnp.dot(c_ref[...], w_ref[...], preferred_element_type=F32,
                         precision=HIGHEST) + b_ref[...]


def _ada(c, w_ada, b_ada):
    bsz = c.shape[0]
    return pl.pallas_call(
        _ada_kernel,
        grid=(3,),
        in_specs=[pl.BlockSpec((bsz, D_MODEL), lambda j: (0, 0)),
                  pl.BlockSpec((D_MODEL, D_MODEL), lambda j: (0, j)),
                  pl.BlockSpec((1, D_MODEL), lambda j: (0, j))],
        out_specs=pl.BlockSpec((bsz, D_MODEL), lambda j: (0, j)),
        out_shape=jax.ShapeDtypeStruct((bsz, 3 * D_MODEL), F32),
        compiler_params=_cp(("arbitrary",)),
        name="ada",
    )(c, w_ada, b_ada.reshape(1, 3 * D_MODEL))


def _proj_kernel(x_ref, shift_ref, scale_ref, g_ref, w_ref, ws_ref, wkv_ref, o_ref, osm_ref, okv_ref, h_scr):
    @pl.when(pl.program_id(2) == 0)
    def _():
        xf = x_ref[0]
        y = xf * lax.rsqrt(jnp.mean(xf * xf, axis=-1, keepdims=True) + EPS) * g_ref[...]
        h_scr[...] = (y * (1.0 + scale_ref[0]) + shift_ref[0]).astype(BF16)
        osm_ref[0] = lax.dot_general(h_scr[...], ws_ref[...], NT, preferred_element_type=F32)
        okv_ref[0] = lax.dot_general(h_scr[...], wkv_ref[...], NT,
                                     preferred_element_type=F32).astype(okv_ref.dtype)

    o_ref[0] = lax.dot_general(h_scr[...], w_ref[...], NT, preferred_element_type=F32).astype(o_ref.dtype)


def _proj(x, shift, scale, g_pre, w, w_small, w_kvc):
    bsz, s, _ = x.shape
    n = w.shape[0]
    nkv = w_kvc.shape[0]
    return pl.pallas_call(
        _proj_kernel,
        grid=(bsz, s // TM_PROJ, n // TN_PROJ),
        in_specs=[pl.BlockSpec((1, TM_PROJ, D_MODEL), lambda b, i, j: (b, i, 0)),
                  pl.BlockSpec((1, 1, D_MODEL), lambda b, i, j: (b, 0, 0)),
                  pl.BlockSpec((1, 1, D_MODEL), lambda b, i, j: (b, 0, 0)),
                  pl.BlockSpec((1, D_MODEL), lambda b, i, j: (0, 0)),
                  pl.BlockSpec((TN_PROJ, D_MODEL), lambda b, i, j: (j, 0)),
                  pl.BlockSpec((128, D_MODEL), lambda b, i, j: (0, 0)),
                  pl.BlockSpec((nkv, D_MODEL), lambda b, i, j: (0, 0))],
        out_specs=[pl.BlockSpec((1, TM_PROJ, TN_PROJ), lambda b, i, j: (b, i, j)),
                   pl.BlockSpec((1, TM_PROJ, 128), lambda b, i, j: (b, i, 0)),
                   pl.BlockSpec((1, TM_PROJ, nkv), lambda b, i, j: (b, i, 0))],
        out_shape=[jax.ShapeDtypeStruct((bsz, s, n), BF16),
                   jax.ShapeDtypeStruct((bsz, s, 128), F32),
                   jax.ShapeDtypeStruct((bsz, s, nkv), F32)],
        scratch_shapes=[pltpu.VMEM((TM_PROJ, D_MODEL), BF16)],
        compiler_params=_cp(("arbitrary", "arbitrary", "arbitrary")),
        name="proj",
    )(x, shift, scale, g_pre.reshape(1, D_MODEL), w, w_small, w_kvc)


def _ssm_kernel(xbc_ref, z_ref, small_ref, shift_ref, spread_ref, convw_ref, convb_ref, dtb_ref,
                alog_ref, dskip_ref, gn_ref, o_ref, tail_scr, state_scr, y_scr):
    cl = SSM_CHUNK

    @pl.when(pl.program_id(1) == 0)
    def _():
        tail_scr[...] = jnp.zeros(tail_scr.shape, BF16)
        state_scr[...] = jnp.zeros(state_scr.shape, F32)

    cur = xbc_ref[0]
    xpad = jnp.concatenate([tail_scr[...], cur], axis=0).astype(F32)
    taps = jnp.concatenate([(xpad * convw_ref[k:k + 1, :]).astype(BF16) for k in range(SSM_CONV - 1)],
                           axis=0)
    tail_scr[...] = cur[cl - CONV_TAIL:cl, :]
    acc = (jnp.dot(shift_ref[...], taps, preferred_element_type=F32)
           + convw_ref[SSM_CONV - 1:SSM_CONV, :] * xpad[CONV_TAIL:] + convb_ref[...])
    u = _silu(acc)
    xs = u[:, :D_INNER]
    bm = u[:, D_INNER:D_INNER + SSM_GROUPS * SSM_STATE]
    cm = u[:, D_INNER + SSM_GROUPS * SSM_STATE:]

    pre = small_ref[0] + dtb_ref[...]
    dt = jnp.maximum(pre, 0.0) + jnp.log1p(jnp.exp(-jnp.abs(pre)))
    a = -jnp.exp(alog_ref[...])
    adt = dt * a
    row = lax.broadcasted_iota(jnp.int32, (cl, cl), 0)
    col = lax.broadcasted_iota(jnp.int32, (cl, cl), 1)
    causal = row >= col
    tri = causal.astype(F32)
    a_cs = jnp.dot(tri, adt, preferred_element_type=F32, precision=HIGHEST)
    a_cs_t = a_cs.T
    dt_t = dt.T
    a_last = a_cs[cl - 1:cl, :]
    ea = jnp.exp(a_cs)
    dsc = jnp.exp(a_last - a_cs) * dt

    def spread(v):
        hi = v.astype(BF16)
        lo = (v - hi.astype(F32)).astype(BF16)
        return jnp.dot(jnp.concatenate([hi, lo], axis=1), spread_ref[...], preferred_element_type=F32)

    ea_x = spread(ea)
    dsc_x = spread(dsc)
    cdec_x = ea_x[cl - 1:cl, :]
    xsd = xs * dsc_x
    lane = lax.broadcasted_iota(jnp.int32, (cl, 2 * SSM_HEAD_DIM), 1)
    first_half = lane < SSM_HEAD_DIM

    hg = SSM_HEADS // SSM_GROUPS
    gw = hg * SSM_HEAD_DIM
    for g in range(SSM_GROUPS):
        bg = bm[:, g * SSM_STATE:(g + 1) * SSM_STATE].astype(BF16)
        cg = cm[:, g * SSM_STATE:(g + 1) * SSM_STATE].astype(BF16)
        gc = slice(g * gw, (g + 1) * gw)
        cb = lax.dot_general(cg, bg, NT, preferred_element_type=F32)
        st = state_scr[g]
        y_scr[:, gc] = jnp.dot(cg, st.astype(BF16), preferred_element_type=F32) * ea_x[:, gc]
        new = lax.dot_general(bg, xsd[:, gc].astype(BF16), (((0,), (0,)), ((), ())),
                              preferred_element_type=F32)
        state_scr[g] = st * cdec_x[:, gc] + new
        for pp in range(hg // 2):
            pair = g * (hg // 2) + pp
            pc = slice(pair * 128, (pair + 1) * 128)
            gm = []
            for e in range(2):
                h = 2 * pair + e
                seg = a_cs[:, h:h + 1] - a_cs_t[h:h + 1, :]
                lmat = jnp.exp(jnp.where(causal, seg, NEG))
                gm.append(cb * lmat * dt_t[h:h + 1, :])
            lhs = jnp.concatenate(gm, axis=0).astype(BF16)
            yy = jnp.dot(lhs, xs[:, pc].astype(BF16), preferred_element_type=F32)
            y_scr[:, pc] = y_scr[:, pc] + jnp.where(first_half, yy[:cl], yy[cl:])

    y = y_scr[...] + xs * dskip_ref[...]
    y = y * _silu(z_ref[0].astype(F32))
    gsz = D_INNER // SSM_GROUPS
    for g in range(SSM_GROUPS):
        yg = y[:, g * gsz:(g + 1) * gsz]
        yn = yg * lax.rsqrt(jnp.mean(yg * yg, axis=-1, keepdims=True) + EPS)
        o_ref[0, :, g * gsz:(g + 1) * gsz] = (yn * gn_ref[:, g * gsz:(g + 1) * gsz]).astype(o_ref.dtype)


def _conv_shift_matrix():
    cl = SSM_CHUNK
    ext = CONV_TAIL + cl
    m = np.zeros((cl, (SSM_CONV - 1) * ext), np.float32)
    t = np.arange(cl)
    for k in range(SSM_CONV - 1):
        m[t, k * ext + CONV_TAIL + t - (SSM_CONV - 1 - k)] = 1.0
    return jnp.asarray(m, dtype=BF16)


def _head_spread_matrix():
    m = np.zeros((2, 128, D_INNER), np.float32)
    for h in range(SSM_HEADS):
        m[:, h, h * SSM_HEAD_DIM:(h + 1) * SSM_HEAD_DIM] = 1.0
    return jnp.asarray(m.reshape(256, D_INNER), dtype=BF16)


def _ssm(pb, small, conv_w, conv_b, dt_bias, a_log, d_skip, g_norm):
    bsz, s, _ = pb.shape
    cl = SSM_CHUNK
    pad = 128 - SSM_HEADS
    dtb = jnp.pad(dt_bias, (0, pad)).reshape(1, 128)
    alog = jnp.pad(a_log, (0, pad)).reshape(1, 128)
    dskip = jnp.repeat(d_skip, SSM_HEAD_DIM).reshape(1, D_INNER)
    const = lambda shape: pl.BlockSpec(shape, lambda b, c: (0, 0))
    return pl.pallas_call(
        _ssm_kernel,
        grid=(bsz, s // cl),
        in_specs=[pl.BlockSpec((1, cl, D_CONV), lambda b, c: (b, c, B_XBC // D_CONV)),
                  pl.BlockSpec((1, cl, D_INNER), lambda b, c: (b, c, B_ZSSM // D_INNER)),
                  pl.BlockSpec((1, cl, 128), lambda b, c: (b, c, 0)),
                  const((cl, (SSM_CONV - 1) * (CONV_TAIL + cl))), const((256, D_INNER)),
                  const((SSM_CONV, D_CONV)), const((1, D_CONV)), const((1, 128)), const((1, 128)),
                  const((1, D_INNER)), const((1, D_INNER))],
        out_specs=pl.BlockSpec((1, cl, D_INNER), lambda b, c: (b, c, 0)),
        out_shape=jax.ShapeDtypeStruct((bsz, s, D_INNER), BF16),
        scratch_shapes=[pltpu.VMEM((CONV_TAIL, D_CONV), BF16),
                        pltpu.VMEM((SSM_GROUPS, SSM_STATE, D_INNER // SSM_GROUPS), F32),
                        pltpu.VMEM((cl, D_INNER), F32)],
        compiler_params=_cp(("arbitrary", "arbitrary")),
        name="ssm",
    )(pb, pb, small, _conv_shift_matrix(), _head_spread_matrix(), conv_w, conv_b.reshape(1, D_CONV),
      dtb, alog, dskip,
      g_norm.reshape(1, D_INNER))


def _cmp_kernel(r0_ref, r1_ref, r2_ref, r3_ref, pos_ref, w1_ref, w2_ref, o_ref):
    half = CMP_STRIDE * ATT_HEAD_DIM
    hd = ATT_HEAD_DIM
    nrow = r0_ref.shape[1] // CMP_STRIDE
    toks = [[r_ref[0, pl.ds(t, nrow, stride=CMP_STRIDE), :] for t in range(CMP_STRIDE)]
            for r_ref in (r0_ref, r1_ref, r2_ref, r3_ref)]
    for kv in range(2):
        w1 = w1_ref[kv]
        posb = jnp.broadcast_to(pos_ref[kv], (8, 2 * half)).astype(BF16)
        cpos = jnp.dot(posb, w1, preferred_element_type=F32)[0:1]
        for gg in range(ATT_KV_GROUPS):
            j = kv * ATT_KV_GROUPS + gg
            r = jnp.concatenate([toks[j // 2][t][:, (j % 2) * hd:(j % 2 + 1) * hd]
                                 for t in range(CMP_STRIDE)], axis=1).astype(BF16)
            first = jnp.dot(r, w1[:half], preferred_element_type=F32)
            second = jnp.dot(r, w1[half:], preferred_element_type=F32)
            hid = first + pltpu.roll(second, nrow - 1, 0) + cpos
            o_ref[0, j] = jnp.dot(_silu(hid).astype(BF16), w2_ref[kv], preferred_element_type=F32)


def _compress(kvc, pos, w1, w2):
    bsz, s, wid = kvc.shape
    nrow = s // CMP_STRIDE
    ng = 2 * ATT_KV_GROUPS
    return pl.pallas_call(
        _cmp_kernel,
        grid=(bsz,),
        in_specs=[pl.BlockSpec((1, s, 128), lambda b, c=c: (b, 0, c)) for c in range(wid // 128)]
                 + [pl.BlockSpec((2, 1, 2048), lambda b: (0, 0, 0)),
                  pl.BlockSpec((2, 2048, CMP_HIDDEN), lambda b: (0, 0, 0)),
                  pl.BlockSpec((2, CMP_HIDDEN, ATT_HEAD_DIM), lambda b: (0, 0, 0))],
        out_specs=pl.BlockSpec((1, ng, nrow, ATT_HEAD_DIM), lambda b: (b, 0, 0, 0)),
        out_shape=jax.ShapeDtypeStruct((bsz, ng, nrow, ATT_HEAD_DIM), F32),
        compiler_params=_cp(("arbitrary",)),
        name="cmp",
    )(kvc, kvc, kvc, kvc, pos, w1, w2)


def _bf16_round_np(x):
    u = np.asarray(x, np.float32).view(np.uint32)
    u = (u + (((u >> 16) & 1) + 0x7FFF)) & np.uint32(0xFFFF0000)
    return u.view(np.float32)


def _slope_table():
    slope = (2.0 ** (-8.0 * np.arange(1, ATT_HEADS + 1) / ATT_HEADS)).astype(np.float32)
    slope = (slope.astype(np.float64) * LOG2E).astype(np.float32)
    p0 = _bf16_round_np(slope)
    p1 = _bf16_round_np(slope - p0)
    p2 = _bf16_round_np(slope - p0 - p1)
    tab = np.zeros((ATT_HEADS, SLOPE_STRIDE), np.float32)
    tab[:, 0], tab[:, 1], tab[:, 2] = p0, p1, p2
    tab[:, 3:6] = tab[:, 0:3] * POS_BASE
    tab[:, 6] = slope
    return jnp.asarray(tab.reshape(-1))


def _key_aux_table(s):
    pos = np.arange(s)
    tab = np.zeros((s, 128), np.float32)
    tab[:, L_POS:L_POS + 3] = (pos % POS_BASE)[:, None]
    tab[:, L_TILE:L_TILE + 3] = (pos // POS_BASE)[:, None]
    tab[pos, L_MASK + pos // SLC_BLOCK] = 1.0
    return jnp.asarray(tab, dtype=BF16)


def _bias_table():
    r = np.arange(QH)[:, None]
    c = np.arange(KT)[None, :]
    tabs = []
    for jj in range(3):
        d = (2 - jj) * KT + r - c
        tabs.append(np.where((d >= 0) & (d < WINDOW), 0.0, NEG))
    tabs.append(np.where(r - c >= 0, 0.0, NEG))
    return jnp.asarray(np.stack(tabs), dtype=F32)


def _gate_expand_table():
    tab = np.zeros((ATT_KV_GROUPS, 2, 128, 3 * KV_WIDTH), np.float32)
    for g in range(ATT_KV_GROUPS):
        for h in range(ATT_HPG):
            for j in range(3):
                row = SMALL_GATE + g * 3 * ATT_HPG + 3 * h + j
                tab[g, :, row, j * KV_WIDTH + h * ATT_HEAD_DIM:j * KV_WIDTH + (h + 1) * ATT_HEAD_DIM] = 1.0
    return jnp.asarray(tab.reshape(ATT_KV_GROUPS, 256, 3 * KV_WIDTH), dtype=BF16)


def _cmp_to_slc_matrix_t(n_cmp_pad, n_cmp, n_slc):
    cs = np.arange(n_cmp_pad) * CMP_STRIDE
    ss = np.arange(n_slc) * SLC_BLOCK
    lo = np.maximum(cs[:, None], ss[None, :])
    hi = np.minimum(cs[:, None] + CMP_BLOCK, ss[None, :] + SLC_BLOCK)
    m = np.clip(hi - lo, 0, None) / CMP_BLOCK
    m[n_cmp:] = 0.0
    return jnp.asarray(m.T, dtype=F32)


def _cmpattn_kernel(slopes_ref, q_ref, ck_ref, cv_ref, mt_ref, ocmp_ref, qmask_ref,
                    khi_scr, klo_scr, vbd_scr):
    g = pl.program_id(1)
    i = pl.program_id(2)
    tq = q_ref.shape[1]
    t0 = i * tq
    npad = ck_ref.shape[2]
    n_slc = mt_ref.shape[0]

    @pl.when(i == 0)
    def _():
        kc = ck_ref[0, 0]
        vc = cv_ref[0, 0]
        k_hi = kc.astype(BF16).astype(F32)
        k_lo = kc - k_hi
        zero = jnp.zeros_like(kc)
        for h in range(ATT_HPG):
            def bd(a):
                return jnp.concatenate([a if hh == h else zero for hh in range(ATT_HPG)], axis=1).astype(BF16)
            khi_scr[h * npad:(h + 1) * npad, :] = bd(k_hi)
            klo_scr[h * npad:(h + 1) * npad, :] = bd(k_lo)
            vbd_scr[h * npad:(h + 1) * npad, :] = bd(vc)

    q = q_ref[0]
    s = (lax.dot_general(q, khi_scr[...], NT, preferred_element_type=F32)
         + lax.dot_general(q, klo_scr[...], NT, preferred_element_type=F32))
    jcol = lax.broadcasted_iota(jnp.int32, (1, npad), 1)
    cmp_end = jcol * CMP_STRIDE + (CMP_BLOCK - 1)
    trow = t0 + lax.broadcasted_iota(jnp.int32, (tq, npad), 0)
    valid = (cmp_end <= trow) & (jcol < npad - 1)
    rel = (cmp_end - t0).astype(F32)
    psum = jnp.zeros((tq, npad), F32)
    ps = []
    for h in range(ATT_HPG):
        slope = slopes_ref[(g * ATT_HPG + h) * SLOPE_STRIDE + 6]
        sh = jnp.where(valid, s[:, h * npad:(h + 1) * npad] + slope * rel, NEG)
        mx = jnp.max(sh, axis=-1, keepdims=True)
        p = jnp.where(valid, jnp.exp2(sh - mx), 0.0)
        l = jnp.sum(p, axis=-1, keepdims=True)
        p = p * jnp.where(l > 0.0, 1.0 / jnp.where(l > 0.0, l, 1.0), 0.0)
        ps.append(p)
        psum = psum + p
    ocmp_ref[0] = jnp.dot(jnp.concatenate(ps, axis=1).astype(BF16), vbd_scr[...],
                          preferred_element_type=F32)
    imp_t = lax.dot_general(mt_ref[...], psum, NT, preferred_element_type=F32,
                            precision=HIGHEST)

    nrb = n_slc // 8
    blk_t = (t0 + lax.broadcasted_iota(jnp.int32, (8, tq), 1)) >> SLC_SHIFT
    sub = lax.broadcasted_iota(jnp.int32, (8, tq), 0)
    score = []
    for rb in range(nrb):
        kk = sub + rb * 8
        imp = imp_t[rb * 8:(rb + 1) * 8]
        forced = (kk == 0) | (kk == blk_t) | (kk == blk_t - 1)
        score.append(jnp.where(forced, imp + FORCE_BONUS, jnp.where(kk <= blk_t, imp, -1.0)))
    rank = [jnp.zeros((8, tq), F32) for _ in range(nrb)]
    for j in range(n_slc):
        cj = jnp.broadcast_to(score[j // 8][j % 8:j % 8 + 1, :], (8, tq))
        for rb in range(nrb):
            ge = jnp.where(cj >= score[rb], 1.0, 0.0)
            gt = jnp.where(cj > score[rb], 1.0, 0.0)
            if rb * 8 > j:
                beats = ge
            elif rb * 8 + 7 <= j:
                beats = gt
            else:
                beats = jnp.where(sub > j - rb * 8, ge, gt)
            rank[rb] = rank[rb] + beats
    rows = [jnp.zeros((L_MASK, tq), F32)]
    rows += [jnp.where(rank[rb] < float(SLC_TOPK), 0.0, -BIG) for rb in range(nrb)]
    rows += [jnp.zeros((128 - L_MASK - n_slc, tq), F32)]
    qmask_ref[0, 0] = jnp.concatenate(rows, axis=0).T.astype(BF16)


def _cmpattn(pb, cmp_kv, s):
    bsz = pb.shape[0]
    g = ATT_KV_GROUPS
    nrow = cmp_kv.shape[2]
    n_cmp = (s - CMP_BLOCK) // CMP_STRIDE + 1
    n_slc = s // SLC_BLOCK
    mt = _cmp_to_slc_matrix_t(nrow, n_cmp, n_slc)
    return pl.pallas_call(
        _cmpattn_kernel,
        grid=(bsz, g, s // TQ_CMP),
        in_specs=[pl.BlockSpec(memory_space=pltpu.SMEM),
                  pl.BlockSpec((1, TQ_CMP, KV_WIDTH), lambda b, gg, i: (b, i, B_Q // KV_WIDTH + gg)),
                  pl.BlockSpec((1, 1, nrow, ATT_HEAD_DIM), lambda b, gg, i: (b, gg, 0, 0)),
                  pl.BlockSpec((1, 1, nrow, ATT_HEAD_DIM), lambda b, gg, i: (b, g + gg, 0, 0)),
                  pl.BlockSpec((n_slc, nrow), lambda b, gg, i: (0, 0))],
        out_specs=[pl.BlockSpec((1, TQ_CMP, KV_WIDTH), lambda b, gg, i: (b, i, gg)),
                   pl.BlockSpec((1, 1, TQ_CMP, 128), lambda b, gg, i: (b, gg, i, 0))],
        out_shape=[jax.ShapeDtypeStruct((bsz, s, ATT_WIDTH), F32),
                   jax.ShapeDtypeStruct((bsz, g, s, 128), BF16)],
        scratch_shapes=[pltpu.VMEM((ATT_HPG * nrow, KV_WIDTH), BF16),
                        pltpu.VMEM((ATT_HPG * nrow, KV_WIDTH), BF16),
                        pltpu.VMEM((ATT_HPG * nrow, KV_WIDTH), BF16)],
        compiler_params=_cp(("arbitrary", "arbitrary", "arbitrary")),
        name="cmpattn",
    )(_slope_table(), pb, cmp_kv, cmp_kv, mt)


def _flash_step(qa_ref, kv_ref, aux_ref, m_scr, acc_scr, rows, ksub, nsub, bias, first=False):
    r0, nr = rows
    nk = nsub * KT
    start = pl.multiple_of(ksub * KT, KT)
    kv = kv_ref[0, pl.ds(start, nk), :]
    left = jnp.where(lax.broadcasted_iota(jnp.int32, (nk, 128), 1) < ATT_HEAD_DIM, 1.0, 0.0).astype(BF16)
    k_aug = kv * left + aux_ref[pl.ds(start, nk), :]
    v_aug = kv * (1.0 - left) + left
    s = lax.dot_general(qa_ref[r0:r0 + nr, :], k_aug, NT, preferred_element_type=F32)
    if bias is not None:
        parts = []
        for t, b in enumerate(bias):
            st = s[:, t * KT:(t + 1) * KT]
            if b is not None:
                st = (st.reshape(nr // QH, QH, KT) + b[None]).reshape(nr, KT)
            parts.append(st)
        s = jnp.concatenate(parts, axis=1) if nsub > 1 else parts[0]
    m_cur = jnp.max(s, axis=-1, keepdims=True)
    if first:
        m_new = jnp.broadcast_to(m_cur, (nr, 128))
    else:
        m_prev = m_scr[r0:r0 + nr, :]
        m_new = jnp.maximum(m_prev, m_cur)
    p = jnp.exp2(s - jnp.concatenate([m_new] * (nk // 128), axis=1))
    pv_t = lax.dot_general(v_aug, p.astype(BF16), (((0,), (1,)), ((), ())),
                           preferred_element_type=F32)
    if first:
        acc_scr[:, r0:r0 + nr] = pv_t
    else:
        acc_scr[:, r0:r0 + nr] = jnp.exp2(m_prev - m_new).T * acc_scr[:, r0:r0 + nr] + pv_t
    m_scr[r0:r0 + nr, :] = m_new


def _flash_out(acc_scr):
    r = acc_scr.shape[1]
    den = acc_scr[0:8, :]
    num = acc_scr[ATT_HEAD_DIM:2 * ATT_HEAD_DIM, :]
    o_t = (num.reshape(ATT_HEAD_DIM // 8, 8, r) / den[None]).reshape(ATT_HEAD_DIM, r)
    halves = []
    for half in range(TQ // QH):
        cols = [o_t[:, (half * ATT_HPG + h) * QH:(half * ATT_HPG + h + 1) * QH] for h in range(ATT_HPG)]
        halves.append(jnp.concatenate(cols, axis=0).T)
    return jnp.concatenate(halves, axis=0)


def _attn_kernel(slopes_ref, q_ref, ksel_ref, kwin_ref, aux_ref, qmask_ref, ocmp_ref, small_ref,
                 zatt_ref, pick_ref, bias_ref, o_ref, qas_scr, qaw_scr, m_scr, acc_scr):
    g = pl.program_id(1)
    i = pl.program_id(2)

    qf = q_ref[0].astype(F32)
    qm = qmask_ref[0, 0].astype(F32)
    lane_row = lax.broadcasted_iota(jnp.int32, (1, 128), 1)
    left = lax.broadcasted_iota(jnp.int32, (TQ, 128), 1) < ATT_HEAD_DIM
    for h in range(ATT_HPG):
        slab = qf[:, 128 * (h // 2):128 * (h // 2) + 128]
        if h % 2:
            slab = pltpu.roll(slab, ATT_HEAD_DIM, 1)
        ext = jnp.zeros((1, 128), F32)
        for c in range(6):
            ext = jnp.where(lane_row == L_POS + c, slopes_ref[(g * ATT_HPG + h) * SLOPE_STRIDE + c], ext)
        base = jnp.where(left, slab, ext)
        for half in range(TQ // QH):
            rs = (half * ATT_HPG + h) * QH
            tok = slice(half * QH, (half + 1) * QH)
            qaw_scr[rs:rs + QH, :] = base[tok].astype(BF16)
            qas_scr[rs:rs + QH, :] = (base[tok] + qm[tok]).astype(BF16)

    half_rows = ATT_HPG * QH
    rows_a, rows_b, rows_all = (0, half_rows), (half_rows, half_rows), (0, 2 * half_rows)
    k0 = i * (TQ // KT)
    w0, w1, w2, causal = (bias_ref[n] for n in range(4))

    _flash_step(qas_scr, ksel_ref, aux_ref, m_scr, acc_scr, rows_a, k0, 1, [causal], first=True)
    _flash_step(qas_scr, ksel_ref, aux_ref, m_scr, acc_scr, rows_b, k0, 2, [None, causal], first=True)

    for n_prev in range(1, ksel_ref.shape[1] // TK):
        @pl.when(i == n_prev)
        def _():
            for j in range(n_prev):
                _flash_step(qas_scr, ksel_ref, aux_ref, m_scr, acc_scr, rows_all, j * (TK // KT),
                            TK // KT, None)

    o_slc = _flash_out(acc_scr)

    @pl.when(i == 0)
    def _():
        _flash_step(qaw_scr, kwin_ref, aux_ref, m_scr, acc_scr, rows_a, 0, 1, [w2], first=True)
        _flash_step(qaw_scr, kwin_ref, aux_ref, m_scr, acc_scr, rows_b, 0, 2, [w1, w2], first=True)

    @pl.when(i >= 1)
    def _():
        _flash_step(qaw_scr, kwin_ref, aux_ref, m_scr, acc_scr, rows_a, k0 - 2, 3, [w0, w1, w2], first=True)
        _flash_step(qaw_scr, kwin_ref, aux_ref, m_scr, acc_scr, rows_b, k0 - 1, 3, [w0, w1, w2], first=True)

    o_win = _flash_out(acc_scr)

    gate = _sigmoid(small_ref[0])
    g_hi = gate.astype(BF16)
    g_lo = (gate - g_hi.astype(F32)).astype(BF16)
    gts = jnp.dot(jnp.concatenate([g_hi, g_lo], axis=1), pick_ref[0],
                  preferred_element_type=F32)
    o = (gts[:, :KV_WIDTH] * ocmp_ref[0] + gts[:, KV_WIDTH:2 * KV_WIDTH] * o_slc
         + gts[:, 2 * KV_WIDTH:] * o_win)
    o_ref[0] = (o * _silu(zatt_ref[0].astype(F32))).astype(o_ref.dtype)


def _attn(pb, small, qmask, ocmp):
    bsz, s, _ = pb.shape
    g = ATT_KV_GROUPS
    r = ATT_HPG * TQ
    return pl.pallas_call(
        _attn_kernel,
        grid=(bsz, g, s // TQ),
        in_specs=[pl.BlockSpec(memory_space=pltpu.SMEM),
                  pl.BlockSpec((1, TQ, KV_WIDTH), lambda b, gg, i: (b, i, B_Q // KV_WIDTH + gg)),
                  pl.BlockSpec((1, s, 128), lambda b, gg, i: (b, 0, B_KSEL // 128 + gg)),
                  pl.BlockSpec((1, s, 128), lambda b, gg, i: (b, 0, B_KWIN // 128 + gg)),
                  pl.BlockSpec((s, 128), lambda b, gg, i: (0, 0)),
                  pl.BlockSpec((1, 1, TQ, 128), lambda b, gg, i: (b, gg, i, 0)),
                  pl.BlockSpec((1, TQ, KV_WIDTH), lambda b, gg, i: (b, i, gg)),
                  pl.BlockSpec((1, TQ, 128), lambda b, gg, i: (b, i, 0)),
                  pl.BlockSpec((1, TQ, KV_WIDTH), lambda b, gg, i: (b, i, B_ZATT // KV_WIDTH + gg)),
                  pl.BlockSpec((1, 256, 3 * KV_WIDTH), lambda b, gg, i: (gg, 0, 0)),
                  pl.BlockSpec((4, QH, KT), lambda b, gg, i: (0, 0, 0))],
        out_specs=pl.BlockSpec((1, TQ, KV_WIDTH), lambda b, gg, i: (b, i, gg)),
        out_shape=jax.ShapeDtypeStruct((bsz, s, ATT_WIDTH), BF16),
        scratch_shapes=[pltpu.VMEM((r, 128), BF16), pltpu.VMEM((r, 128), BF16),
                        pltpu.VMEM((r, 128), F32), pltpu.VMEM((128, r), F32)],
        compiler_params=_cp(("arbitrary", "arbitrary", "arbitrary")),
        name="attn",
    )(_slope_table(), pb, pb, pb, _key_aux_table(s), qmask, ocmp, small, pb, _gate_expand_table(),
      _bias_table())


def _merge_kernel(yssm_ref, onsa_ref, mg_ref, x_ref, gate_ref, gpost_ref, wssm_ref, wnsa_ref, wout_ref, o_ref):
    y_ssm = jnp.dot(yssm_ref[0], wssm_ref[...], preferred_element_type=F32)
    y_nsa = jnp.dot(onsa_ref[0], wnsa_ref[...], preferred_element_type=F32)
    mg = _sigmoid(mg_ref[0].astype(F32))
    merged = mg[:, :D_MODEL] * y_ssm + mg[:, D_MODEL:] * y_nsa
    out = jnp.dot(merged.astype(BF16), wout_ref[...], preferred_element_type=F32)
    yn = out * lax.rsqrt(jnp.mean(out * out, axis=-1, keepdims=True) + EPS) * gpost_ref[...]
    o_ref[0] = x_ref[0] + gate_ref[0] * yn


def _merge(yssm, onsa, pb, x, gate, g_post, w_ssm_out, w_nsa_out, w_out):
    bsz, s, _ = x.shape
    tm = 512
    const = lambda shape: pl.BlockSpec(shape, lambda b, i: (0, 0))
    return pl.pallas_call(
        _merge_kernel,
        grid=(bsz, s // tm),
        in_specs=[pl.BlockSpec((1, tm, D_INNER), lambda b, i: (b, i, 0)),
                  pl.BlockSpec((1, tm, ATT_WIDTH), lambda b, i: (b, i, 0)),
                  pl.BlockSpec((1, tm, 2 * D_MODEL), lambda b, i: (b, i, B_MERGE // (2 * D_MODEL))),
                  pl.BlockSpec((1, tm, D_MODEL), lambda b, i: (b, i, 0)),
                  pl.BlockSpec((1, 1, D_MODEL), lambda b, i: (b, 0, 0)),
                  const((1, D_MODEL)),
                  const((D_INNER, D_MODEL)), const((ATT_WIDTH, D_MODEL)), const((D_MODEL, D_MODEL))],
        out_specs=pl.BlockSpec((1, tm, D_MODEL), lambda b, i: (b, i, 0)),
        out_shape=jax.ShapeDtypeStruct((bsz, s, D_MODEL), F32),
        compiler_params=_cp(("arbitrary", "arbitrary")),
        name="merge",
    )(yssm, onsa, pb, x, gate, g_post.reshape(1, D_MODEL), w_ssm_out, w_nsa_out, w_out)


def _wprep_kernel(w_ref, wb_ref, ws_ref, wkv_ref):
    def put(ref, dst, off, n, scale=None):
        v = w_ref[off:off + n, :]
        ref[dst:dst + n, :] = (v if scale is None else v * scale).astype(BF16)

    put(wb_ref, B_XBC, IN_XBC, D_CONV)
    put(wb_ref, B_ZATT, IN_ZATT, ATT_WIDTH)
    put(wb_ref, B_MERGE, IN_MERGE, 2 * D_MODEL)
    put(wb_ref, B_ZSSM, IN_Z, D_INNER)
    put(wb_ref, B_Q, IN_Q, ATT_WIDTH, ATT_HEAD_DIM ** -0.5 * LOG2E)
    hd = ATT_HEAD_DIM
    for dst, k_off in ((B_KSEL, IN_KV + 2 * KV_WIDTH), (B_KWIN, IN_KV + 4 * KV_WIDTH)):
        for gg in range(ATT_KV_GROUPS):
            put(wb_ref, dst + 2 * gg * hd, k_off + gg * hd, hd)
            put(wb_ref, dst + (2 * gg + 1) * hd, k_off + KV_WIDTH + gg * hd, hd)
    put(wkv_ref, 0, IN_KV, 2 * KV_WIDTH)
    put(ws_ref, 0, IN_DT, SSM_HEADS)
    put(ws_ref, SMALL_GATE, IN_GATE, 3 * ATT_HEADS)
    used = SMALL_GATE + 3 * ATT_HEADS
    ws_ref[used:, :] = jnp.zeros((128 - used, ws_ref.shape[1]), BF16)


def _split_w_in(w_in):
    dm, n_in = w_in.shape
    tc = 128
    return pl.pallas_call(
        _wprep_kernel,
        grid=(dm // tc,),
        in_specs=[pl.BlockSpec((n_in, tc), lambda i: (0, i))],
        out_specs=[pl.BlockSpec((NB, tc), lambda i: (0, i)), pl.BlockSpec((128, tc), lambda i: (0, i)),
                   pl.BlockSpec((2 * KV_WIDTH, tc), lambda i: (0, i))],
        out_shape=[jax.ShapeDtypeStruct((NB, dm), BF16), jax.ShapeDtypeStruct((128, dm), BF16),
                   jax.ShapeDtypeStruct((2 * KV_WIDTH, dm), BF16)],
        compiler_params=_cp(("arbitrary",)),
        name="wprep",
    )(w_in.T)


def _layer(x, c, w_ada, b_ada, g_pre, g_post, w_in, conv_w, conv_b, dt_bias, a_log, d_skip,
           g_ssm_norm, w_ssm_out, cmp_pos_k, cmp_w1_k, cmp_w2_k, cmp_pos_v, cmp_w1_v, cmp_w2_v,
           w_nsa_out, w_out):
    bsz, s, dm = x.shape
    mod = _ada(c, w_ada, b_ada)
    shift = mod[:, None, :dm]
    scale = mod[:, None, dm:2 * dm]
    gate = mod[:, None, 2 * dm:]
    wb, w_small, w_kvc = _split_w_in(w_in)
    pb, small, kvc = _proj(x, shift, scale, g_pre, wb, w_small, w_kvc)

    yssm = _ssm(pb, small, conv_w, conv_b, dt_bias, a_log, d_skip, g_ssm_norm)

    pos = jnp.stack([cmp_pos_k.reshape(1, -1), cmp_pos_v.reshape(1, -1)])
    w1 = jnp.stack([cmp_w1_k, cmp_w1_v]).astype(BF16)
    w2 = jnp.stack([cmp_w2_k, cmp_w2_v]).astype(BF16)
    cmp_kv = _compress(kvc, pos, w1, w2)

    ocmp, qmask = _cmpattn(pb, cmp_kv, s)
    onsa = _attn(pb, small, qmask, ocmp)
    return _merge(yssm, onsa, pb, x, gate, g_post, w_ssm_out.astype(BF16), w_nsa_out.astype(BF16),
                  w_out.astype(BF16))


@jax.jit
def kernel(x, c, w_ada, b_ada, g_pre, g_post, w_in, conv_w, conv_b, dt_bias, a_log, d_skip, g_ssm_norm,
           w_ssm_out, cmp_pos_k, cmp_w1_k, cmp_w2_k, cmp_pos_v, cmp_w1_v, cmp_w2_v, w_nsa_out, w_out):
    for layer in range(w_in.shape[0]):
        x = _layer(x, c, w_ada[layer], b_ada[layer], g_pre[layer], g_post[layer], w_in[layer],
                   conv_w[layer], conv_b[layer], dt_bias[layer], a_log[layer], d_skip[layer],
                   g_ssm_norm[layer], w_ssm_out[layer], cmp_pos_k[layer], cmp_w1_k[layer],
                   cmp_w2_k[layer], cmp_pos_v[layer], cmp_w1_v[layer], cmp_w2_v[layer],
                   w_nsa_out[layer], w_out[layer])
    return x
```

```python
import numpy as np
import jax
import jax.numpy as jnp
from jax import lax
from jax.experimental import pallas as pl
from jax.experimental.pallas import tpu as pltpu

D_MODEL = 1024
D_INNER = 2048
SSM_HEAD_DIM = 64
SSM_HEADS = 32
SSM_GROUPS = 4
SSM_STATE = 128
SSM_CONV = 4
SSM_CHUNK = 128
D_CONV = D_INNER + 2 * SSM_GROUPS * SSM_STATE
SSM_STEP = 512
CONV_TAIL = 16

ATT_HEADS = 16
ATT_HEAD_DIM = 64
ATT_KV_GROUPS = 4
ATT_HPG = 4
ATT_WIDTH = 1024
KV_WIDTH = 256
CMP_BLOCK = 32
CMP_STRIDE = 16
CMP_HIDDEN = 256
SLC_BLOCK = 64
SLC_SHIFT = 6
SLC_TOPK = 16
WINDOW = 512
FORCE_BONUS = 1000.0
EPS = 1e-6
NEG = -1e30
BIG = 2.0 ** 100

F32 = jnp.float32
BF16 = jnp.bfloat16
HIGHEST = lax.Precision.HIGHEST
NT = (((1,), (1,)), ((), ()))

_SIZES = (D_INNER, D_CONV, SSM_HEADS, ATT_WIDTH, 6 * KV_WIDTH, 3 * ATT_HEADS, ATT_WIDTH, 2 * D_MODEL)
_OFFS = tuple(int(v) for v in np.cumsum((0,) + _SIZES))
IN_Z, IN_XBC, IN_DT, IN_Q, IN_KV, IN_GATE, IN_ZATT, IN_MERGE = _OFFS[:8]

B_XBC, B_ZATT, B_MERGE, B_ZSSM, B_Q, B_KSEL, B_KWIN = 0, 3072, 4096, 6144, 8192, 9216, 9728
NB = 10240
SMALL_GATE = 32
LOG2E = 1.4426950408889634

TM_PROJ = 1024
TN_PROJ = 2560
TQ = 512
TK = 512
QH = 256
KT = 256
POS_BASE = 256
TQ_CMP = 2048
VMEM_LIMIT = 48 * 1024 * 1024

L_POS = 64
L_TILE = 67
L_MASK = 72
SLOPE_STRIDE = 8


def _cp(sem):
    return pltpu.CompilerParams(dimension_semantics=sem, vmem_limit_bytes=VMEM_LIMIT)


def _sigmoid(v):
    return 0.5 * jnp.tanh(0.5 * v) + 0.5


def _silu(v):
    return v * _sigmoid(v)


def _ada_kernel(c_ref, w_ref, b_ref, o_ref):
    o_ref[...] = jnp.dot(c_ref[...], w_ref[...], preferred_element_type=F32,
                         precision=HIGHEST) + b_ref[...]


def _ada(c, w_ada, b_ada):
    bsz = c.shape[0]
    return pl.pallas_call(
        _ada_kernel,
        grid=(3,),
        in_specs=[pl.BlockSpec((bsz, D_MODEL), lambda j: (0, 0)),
                  pl.BlockSpec((D_MODEL, D_MODEL), lambda j: (0, j)),
                  pl.BlockSpec((1, D_MODEL), lambda j: (0, j))],
        out_specs=pl.BlockSpec((bsz, D_MODEL), lambda j: (0, j)),
        out_shape=jax.ShapeDtypeStruct((bsz, 3 * D_MODEL), F32),
        compiler_params=_cp(("arbitrary",)),
        name="ada",
    )(c, w_ada, b_ada.reshape(1, 3 * D_MODEL))


def _proj_kernel(x_ref, shift_ref, scale_ref, g_ref, w_ref, ws_ref, wkv_ref, o_ref, osm_ref, okv_ref, h_scr):
    @pl.when(pl.program_id(2) == 0)
    def _():
        xf = x_ref[0]
        y = xf * lax.rsqrt(jnp.mean(xf * xf, axis=-1, keepdims=True) + EPS) * g_ref[...]
        h_scr[...] = (y * (1.0 + scale_ref[0]) + shift_ref[0]).astype(BF16)
        osm_ref[0] = lax.dot_general(h_scr[...], ws_ref[...], NT, preferred_element_type=F32)
        okv_ref[0] = lax.dot_general(h_scr[...], wkv_ref[...], NT,
                                     preferred_element_type=F32).astype(okv_ref.dtype)

    o_ref[0] = lax.dot_general(h_scr[...], w_ref[...], NT, preferred_element_type=F32).astype(o_ref.dtype)


def _proj(x, shift, scale, g_pre, w, w_small, w_kvc):
    bsz, s, _ = x.shape
    n = w.shape[0]
    nkv = w_kvc.shape[0]
    return pl.pallas_call(
        _proj_kernel,
        grid=(bsz, s // TM_PROJ, n // TN_PROJ),
        in_specs=[pl.BlockSpec((1, TM_PROJ, D_MODEL), lambda b, i, j: (b, i, 0)),
                  pl.BlockSpec((1, 1, D_MODEL), lambda b, i, j: (b, 0, 0)),
                  pl.BlockSpec((1, 1, D_MODEL), lambda b, i, j: (b, 0, 0)),
                  pl.BlockSpec((1, D_MODEL), lambda b, i, j: (0, 0)),
                  pl.BlockSpec((TN_PROJ, D_MODEL), lambda b, i, j: (j, 0)),
                  pl.BlockSpec((128, D_MODEL), lambda b, i, j: (0, 0)),
                  pl.BlockSpec((nkv, D_MODEL), lambda b, i, j: (0, 0))],
        out_specs=[pl.BlockSpec((1, TM_PROJ, TN_PROJ), lambda b, i, j: (b, i, j)),
                   pl.BlockSpec((1, TM_PROJ, 128), lambda b, i, j: (b, i, 0)),
                   pl.BlockSpec((1, TM_PROJ, nkv), lambda b, i, j: (b, i, 0))],
        out_shape=[jax.ShapeDtypeStruct((bsz, s, n), BF16),
                   jax.ShapeDtypeStruct((bsz, s, 128), F32),
                   jax.ShapeDtypeStruct((bsz, s, nkv), F32)],
        scratch_shapes=[pltpu.VMEM((TM_PROJ, D_MODEL), BF16)],
        compiler_params=_cp(("arbitrary", "arbitrary", "arbitrary")),
        name="proj",
    )(x, shift, scale, g_pre.reshape(1, D_MODEL), w, w_small, w_kvc)


def _ssm_kernel(xbc_ref, z_ref, small_ref, shift_ref, spread_ref, convw_ref, convb_ref, dtb_ref,
                alog_ref, dskip_ref, gn_ref, o_ref, tail_scr, state_scr, y_scr):
    cl = SSM_CHUNK

    @pl.when(pl.program_id(1) == 0)
    def _():
        tail_scr[...] = jnp.zeros(tail_scr.shape, BF16)
        state_scr[...] = jnp.zeros(state_scr.shape, F32)

    for sc in range(xbc_ref.shape[1] // cl):
        rows = pl.ds(sc * cl, cl)
        _ssm_chunk(xbc_ref.at[0, rows], z_ref.at[0, rows], small_ref.at[0, rows], shift_ref, spread_ref,
                   convw_ref, convb_ref, dtb_ref, alog_ref, dskip_ref, gn_ref, o_ref.at[0, rows],
                   tail_scr, state_scr, y_scr.at[rows])


def _ssm_chunk(xbc_ref, z_ref, small_ref, shift_ref, spread_ref, convw_ref, convb_ref, dtb_ref,
               alog_ref, dskip_ref, gn_ref, o_ref, tail_scr, state_scr, y_scr):
    cl = SSM_CHUNK
    cur = xbc_ref[...]
    xpad = jnp.concatenate([tail_scr[...], cur], axis=0).astype(F32)
    taps = jnp.concatenate([(xpad * convw_ref[k:k + 1, :]).astype(BF16) for k in range(SSM_CONV - 1)],
                           axis=0)
    tail_scr[...] = cur[cl - CONV_TAIL:cl, :]
    acc = (jnp.dot(shift_ref[...], taps, preferred_element_type=F32)
           + convw_ref[SSM_CONV - 1:SSM_CONV, :] * xpad[CONV_TAIL:] + convb_ref[...])
    u = _silu(acc)
    xs = u[:, :D_INNER]
    bm = u[:, D_INNER:D_INNER + SSM_GROUPS * SSM_STATE]
    cm = u[:, D_INNER + SSM_GROUPS * SSM_STATE:]

    pre = small_ref[...] + dtb_ref[...]
    dt = jnp.maximum(pre, 0.0) + jnp.log1p(jnp.exp(-jnp.abs(pre)))
    a = -jnp.exp(alog_ref[...])
    adt = dt * a
    row = lax.broadcasted_iota(jnp.int32, (cl, cl), 0)
    col = lax.broadcasted_iota(jnp.int32, (cl, cl), 1)
    causal = row >= col
    tri = causal.astype(F32)
    a_cs = jnp.dot(tri, adt, preferred_element_type=F32, precision=HIGHEST)
    a_cs_t = a_cs.T
    dt_t = dt.T
    a_last = a_cs[cl - 1:cl, :]
    ea = jnp.exp(a_cs)
    dsc = jnp.exp(a_last - a_cs) * dt

    def spread(v):
        hi = v.astype(BF16)
        lo = (v - hi.astype(F32)).astype(BF16)
        return jnp.dot(jnp.concatenate([hi, lo], axis=1), spread_ref[...], preferred_element_type=F32)

    ea_x = spread(ea)
    dsc_x = spread(dsc)
    cdec_x = ea_x[cl - 1:cl, :]
    xsd = xs * dsc_x
    lane = lax.broadcasted_iota(jnp.int32, (cl, 2 * SSM_HEAD_DIM), 1)
    first_half = lane < SSM_HEAD_DIM

    hg = SSM_HEADS // SSM_GROUPS
    gw = hg * SSM_HEAD_DIM
    for g in range(SSM_GROUPS):
        bg = bm[:, g * SSM_STATE:(g + 1) * SSM_STATE].astype(BF16)
        cg = cm[:, g * SSM_STATE:(g + 1) * SSM_STATE].astype(BF16)
        gc = slice(g * gw, (g + 1) * gw)
        cb = lax.dot_general(cg, bg, NT, preferred_element_type=F32)
        st = state_scr[g]
        y_scr[:, gc] = jnp.dot(cg, st.astype(BF16), preferred_element_type=F32) * ea_x[:, gc]
        new = lax.dot_general(bg, xsd[:, gc].astype(BF16), (((0,), (0,)), ((), ())),
                              preferred_element_type=F32)
        state_scr[g] = st * cdec_x[:, gc] + new
        for pp in range(hg // 2):
            pair = g * (hg // 2) + pp
            pc = slice(pair * 128, (pair + 1) * 128)
            gm = []
            for e in range(2):
                h = 2 * pair + e
                seg = a_cs[:, h:h + 1] - a_cs_t[h:h + 1, :]
                lmat = jnp.exp(jnp.where(causal, seg, NEG))
                gm.append(cb * lmat * dt_t[h:h + 1, :])
            lhs = jnp.concatenate(gm, axis=0).astype(BF16)
            yy = jnp.dot(lhs, xs[:, pc].astype(BF16), preferred_element_type=F32)
            y_scr[:, pc] = y_scr[:, pc] + jnp.where(first_half, yy[:cl], yy[cl:])

    y = y_scr[...] + xs * dskip_ref[...]
    y = y * _silu(z_ref[...].astype(F32))
    gsz = D_INNER // SSM_GROUPS
    for g in range(SSM_GROUPS):
        yg = y[:, g * gsz:(g + 1) * gsz]
        yn = yg * lax.rsqrt(jnp.mean(yg * yg, axis=-1, keepdims=True) + EPS)
        o_ref[:, g * gsz:(g + 1) * gsz] = (yn * gn_ref[:, g * gsz:(g + 1) * gsz]).astype(o_ref.dtype)


def _conv_shift_matrix():
    cl = SSM_CHUNK
    ext = CONV_TAIL + cl
    m = np.zeros((cl, (SSM_CONV - 1) * ext), np.float32)
    t = np.arange(cl)
    for k in range(SSM_CONV - 1):
        m[t, k * ext + CONV_TAIL + t - (SSM_CONV - 1 - k)] = 1.0
    return jnp.asarray(m, dtype=BF16)


def _head_spread_matrix():
    m = np.zeros((2, 128, D_INNER), np.float32)
    for h in range(SSM_HEADS):
        m[:, h, h * SSM_HEAD_DIM:(h + 1) * SSM_HEAD_DIM] = 1.0
    return jnp.asarray(m.reshape(256, D_INNER), dtype=BF16)


def _ssm(pb, small, conv_w, conv_b, dt_bias, a_log, d_skip, g_norm):
    bsz, s, _ = pb.shape
    cl = SSM_CHUNK
    pad = 128 - SSM_HEADS
    dtb = jnp.pad(dt_bias, (0, pad)).reshape(1, 128)
    alog = jnp.pad(a_log, (0, pad)).reshape(1, 128)
    dskip = jnp.repeat(d_skip, SSM_HEAD_DIM).reshape(1, D_INNER)
    const = lambda shape: pl.BlockSpec(shape, lambda b, c: (0, 0))
    return pl.pallas_call(
        _ssm_kernel,
        grid=(bsz, s // SSM_STEP),
        in_specs=[pl.BlockSpec((1, SSM_STEP, D_CONV), lambda b, c: (b, c, B_XBC // D_CONV)),
                  pl.BlockSpec((1, SSM_STEP, D_INNER), lambda b, c: (b, c, B_ZSSM // D_INNER)),
                  pl.BlockSpec((1, SSM_STEP, 128), lambda b, c: (b, c, 0)),
                  const((cl, (SSM_CONV - 1) * (CONV_TAIL + cl))), const((256, D_INNER)),
                  const((SSM_CONV, D_CONV)), const((1, D_CONV)), const((1, 128)), const((1, 128)),
                  const((1, D_INNER)), const((1, D_INNER))],
        out_specs=pl.BlockSpec((1, SSM_STEP, D_INNER), lambda b, c: (b, c, 0)),
        out_shape=jax.ShapeDtypeStruct((bsz, s, D_INNER), BF16),
        scratch_shapes=[pltpu.VMEM((CONV_TAIL, D_CONV), BF16),
                        pltpu.VMEM((SSM_GROUPS, SSM_STATE, D_INNER // SSM_GROUPS), F32),
                        pltpu.VMEM((SSM_STEP, D_INNER), F32)],
        compiler_params=_cp(("arbitrary", "arbitrary")),
        name="ssm",
    )(pb, pb, small, _conv_shift_matrix(), _head_spread_matrix(), conv_w, conv_b.reshape(1, D_CONV),
      dtb, alog, dskip,
      g_norm.reshape(1, D_INNER))


def _cmp_kernel(r0_ref, r1_ref, r2_ref, r3_ref, pos_ref, w1_ref, w2_ref, o_ref):
    half = CMP_STRIDE * ATT_HEAD_DIM
    hd = ATT_HEAD_DIM
    nrow = r0_ref.shape[1] // CMP_STRIDE
    toks = [[r_ref[0, pl.ds(t, nrow, stride=CMP_STRIDE), :] for t in range(CMP_STRIDE)]
            for r_ref in (r0_ref, r1_ref, r2_ref, r3_ref)]
    for kv in range(2):
        w1 = w1_ref[kv]
        posb = jnp.broadcast_to(pos_ref[kv], (8, 2 * half)).astype(BF16)
        cpos = jnp.dot(posb, w1, preferred_element_type=F32)[0:1]
        for gg in range(ATT_KV_GROUPS):
            j = kv * ATT_KV_GROUPS + gg
            r = jnp.concatenate([toks[j // 2][t][:, (j % 2) * hd:(j % 2 + 1) * hd]
                                 for t in range(CMP_STRIDE)], axis=1).astype(BF16)
            first = jnp.dot(r, w1[:half], preferred_element_type=F32)
            second = jnp.dot(r, w1[half:], preferred_element_type=F32)
            hid = first + pltpu.roll(second, nrow - 1, 0) + cpos
            o_ref[0, j] = jnp.dot(_silu(hid).astype(BF16), w2_ref[kv], preferred_element_type=F32)


def _compress(kvc, pos, w1, w2):
    bsz, s, wid = kvc.shape
    nrow = s // CMP_STRIDE
    ng = 2 * ATT_KV_GROUPS
    return pl.pallas_call(
        _cmp_kernel,
        grid=(bsz,),
        in_specs=[pl.BlockSpec((1, s, 128), lambda b, c=c: (b, 0, c)) for c in range(wid // 128)]
                 + [pl.BlockSpec((2, 1, 2048), lambda b: (0, 0, 0)),
                  pl.BlockSpec((2, 2048, CMP_HIDDEN), lambda b: (0, 0, 0)),
                  pl.BlockSpec((2, CMP_HIDDEN, ATT_HEAD_DIM), lambda b: (0, 0, 0))],
        out_specs=pl.BlockSpec((1, ng, nrow, ATT_HEAD_DIM), lambda b: (b, 0, 0, 0)),
        out_shape=jax.ShapeDtypeStruct((bsz, ng, nrow, ATT_HEAD_DIM), F32),
        compiler_params=_cp(("arbitrary",)),
        name="cmp",
    )(kvc, kvc, kvc, kvc, pos, w1, w2)


def _bf16_round_np(x):
    u = np.asarray(x, np.float32).view(np.uint32)
    u = (u + (((u >> 16) & 1) + 0x7FFF)) & np.uint32(0xFFFF0000)
    return u.view(np.float32)


def _slope_table():
    slope = (2.0 ** (-8.0 * np.arange(1, ATT_HEADS + 1) / ATT_HEADS)).astype(np.float32)
    slope = (slope.astype(np.float64) * LOG2E).astype(np.float32)
    p0 = _bf16_round_np(slope)
    p1 = _bf16_round_np(slope - p0)
    p2 = _bf16_round_np(slope - p0 - p1)
    tab = np.zeros((ATT_HEADS, SLOPE_STRIDE), np.float32)
    tab[:, 0], tab[:, 1], tab[:, 2] = p0, p1, p2
    tab[:, 3:6] = tab[:, 0:3] * POS_BASE
    tab[:, 6] = slope
    return jnp.asarray(tab.reshape(-1))


def _key_aux_table(s):
    pos = np.arange(s)
    tab = np.zeros((s, 128), np.float32)
    tab[:, L_POS:L_POS + 3] = (pos % POS_BASE)[:, None]
    tab[:, L_TILE:L_TILE + 3] = (pos // POS_BASE)[:, None]
    tab[pos, L_MASK + pos // SLC_BLOCK] = 1.0
    return jnp.asarray(tab, dtype=BF16)


def _bias_table():
    r = np.arange(QH)[:, None]
    c = np.arange(KT)[None, :]
    tabs = []
    for jj in range(3):
        d = (2 - jj) * KT + r - c
        tabs.append(np.where((d >= 0) & (d < WINDOW), 0.0, NEG))
    tabs.append(np.where(r - c >= 0, 0.0, NEG))
    return jnp.asarray(np.stack(tabs), dtype=F32)


def _gate_expand_table():
    tab = np.zeros((ATT_KV_GROUPS, 2, 128, 3 * KV_WIDTH), np.float32)
    for g in range(ATT_KV_GROUPS):
        for h in range(ATT_HPG):
            for j in range(3):
                row = SMALL_GATE + g * 3 * ATT_HPG + 3 * h + j
                tab[g, :, row, j * KV_WIDTH + h * ATT_HEAD_DIM:j * KV_WIDTH + (h + 1) * ATT_HEAD_DIM] = 1.0
    return jnp.asarray(tab.reshape(ATT_KV_GROUPS, 256, 3 * KV_WIDTH), dtype=BF16)


def _cmp_to_slc_matrix_t(n_cmp_pad, n_cmp, n_slc):
    cs = np.arange(n_cmp_pad) * CMP_STRIDE
    ss = np.arange(n_slc) * SLC_BLOCK
    lo = np.maximum(cs[:, None], ss[None, :])
    hi = np.minimum(cs[:, None] + CMP_BLOCK, ss[None, :] + SLC_BLOCK)
    m = np.clip(hi - lo, 0, None) / CMP_BLOCK
    m[n_cmp:] = 0.0
    return jnp.asarray(m.T, dtype=F32)


def _cmpattn_kernel(slopes_ref, q_ref, ck_ref, cv_ref, mt_ref, ocmp_ref, qmask_ref,
                    khi_scr, klo_scr, vbd_scr):
    g = pl.program_id(1)
    i = pl.program_id(2)
    tq = q_ref.shape[1]
    t0 = i * tq
    npad = ck_ref.shape[2]
    n_slc = mt_ref.shape[0]

    @pl.when(i == 0)
    def _():
        kc = ck_ref[0, 0]
        vc = cv_ref[0, 0]
        k_hi = kc.astype(BF16).astype(F32)
        k_lo = kc - k_hi
        zero = jnp.zeros_like(kc)
        for h in range(ATT_HPG):
            def bd(a):
                return jnp.concatenate([a if hh == h else zero for hh in range(ATT_HPG)], axis=1).astype(BF16)
            khi_scr[h * npad:(h + 1) * npad, :] = bd(k_hi)
            klo_scr[h * npad:(h + 1) * npad, :] = bd(k_lo)
            vbd_scr[h * npad:(h + 1) * npad, :] = bd(vc)

    q = q_ref[0]
    s = (lax.dot_general(q, khi_scr[...], NT, preferred_element_type=F32)
         + lax.dot_general(q, klo_scr[...], NT, preferred_element_type=F32))
    jcol = lax.broadcasted_iota(jnp.int32, (1, npad), 1)
    cmp_end = jcol * CMP_STRIDE + (CMP_BLOCK - 1)
    trow = t0 + lax.broadcasted_iota(jnp.int32, (tq, npad), 0)
    valid = (cmp_end <= trow) & (jcol < npad - 1)
    rel = (cmp_end - t0).astype(F32)
    psum = jnp.zeros((tq, npad), F32)
    ps = []
    for h in range(ATT_HPG):
        slope = slopes_ref[(g * ATT_HPG + h) * SLOPE_STRIDE + 6]
        sh = jnp.where(valid, s[:, h * npad:(h + 1) * npad] + slope * rel, NEG)
        mx = jnp.max(sh, axis=-1, keepdims=True)
        p = jnp.where(valid, jnp.exp2(sh - mx), 0.0)
        l = jnp.sum(p, axis=-1, keepdims=True)
        p = p * jnp.where(l > 0.0, 1.0 / jnp.where(l > 0.0, l, 1.0), 0.0)
        ps.append(p)
        psum = psum + p
    ocmp_ref[0] = jnp.dot(jnp.concatenate(ps, axis=1).astype(BF16), vbd_scr[...],
                          preferred_element_type=F32)
    imp_t = lax.dot_general(mt_ref[...], psum, NT, preferred_element_type=F32,
                            precision=HIGHEST)

    nrb = n_slc // 8
    blk_t = (t0 + lax.broadcasted_iota(jnp.int32, (8, tq), 1)) >> SLC_SHIFT
    sub = lax.broadcasted_iota(jnp.int32, (8, tq), 0)
    score = []
    for rb in range(nrb):
        kk = sub + rb * 8
        imp = imp_t[rb * 8:(rb + 1) * 8]
        forced = (kk == 0) | (kk == blk_t) | (kk == blk_t - 1)
        score.append(jnp.where(forced, imp + FORCE_BONUS, jnp.where(kk <= blk_t, imp, -1.0)))
    rank = [jnp.zeros((8, tq), F32) for _ in range(nrb)]
    for j in range(n_slc):
        cj = jnp.broadcast_to(score[j // 8][j % 8:j % 8 + 1, :], (8, tq))
        for rb in range(nrb):
            ge = jnp.where(cj >= score[rb], 1.0, 0.0)
            gt = jnp.where(cj > score[rb], 1.0, 0.0)
            if rb * 8 > j:
                beats = ge
            elif rb * 8 + 7 <= j:
                beats = gt
            else:
                beats = jnp.where(sub > j - rb * 8, ge, gt)
            rank[rb] = rank[rb] + beats
    rows = [jnp.zeros((L_MASK, tq), F32)]
    rows += [jnp.where(rank[rb] < float(SLC_TOPK), 0.0, -BIG) for rb in range(nrb)]
    rows += [jnp.zeros((128 - L_MASK - n_slc, tq), F32)]
    qmask_ref[0, 0] = jnp.concatenate(rows, axis=0).T.astype(BF16)


def _cmpattn(pb, cmp_kv, s):
    bsz = pb.shape[0]
    g = ATT_KV_GROUPS
    nrow = cmp_kv.shape[2]
    n_cmp = (s - CMP_BLOCK) // CMP_STRIDE + 1
    n_slc = s // SLC_BLOCK
    mt = _cmp_to_slc_matrix_t(nrow, n_cmp, n_slc)
    return pl.pallas_call(
        _cmpattn_kernel,
        grid=(bsz, g, s // TQ_CMP),
        in_specs=[pl.BlockSpec(memory_space=pltpu.SMEM),
                  pl.BlockSpec((1, TQ_CMP, KV_WIDTH), lambda b, gg, i: (b, i, B_Q // KV_WIDTH + gg)),
                  pl.BlockSpec((1, 1, nrow, ATT_HEAD_DIM), lambda b, gg, i: (b, gg, 0, 0)),
                  pl.BlockSpec((1, 1, nrow, ATT_HEAD_DIM), lambda b, gg, i: (b, g + gg, 0, 0)),
                  pl.BlockSpec((n_slc, nrow), lambda b, gg, i: (0, 0))],
        out_specs=[pl.BlockSpec((1, TQ_CMP, KV_WIDTH), lambda b, gg, i: (b, i, gg)),
                   pl.BlockSpec((1, 1, TQ_CMP, 128), lambda b, gg, i: (b, gg, i, 0))],
        out_shape=[jax.ShapeDtypeStruct((bsz, s, ATT_WIDTH), F32),
                   jax.ShapeDtypeStruct((bsz, g, s, 128), BF16)],
        scratch_shapes=[pltpu.VMEM((ATT_HPG * nrow, KV_WIDTH), BF16),
                        pltpu.VMEM((ATT_HPG * nrow, KV_WIDTH), BF16),
                        pltpu.VMEM((ATT_HPG * nrow, KV_WIDTH), BF16)],
        compiler_params=_cp(("arbitrary", "arbitrary", "arbitrary")),
        name="cmpattn",
    )(_slope_table(), pb, cmp_kv, cmp_kv, mt)


def _flash_step(qa_ref, kv_ref, aux_ref, m_scr, acc_scr, rows, ksub, nsub, bias, first=False):
    r0, nr = rows
    nk = nsub * KT
    start = pl.multiple_of(ksub * KT, KT)
    kv = kv_ref[0, pl.ds(start, nk), :]
    left = jnp.where(lax.broadcasted_iota(jnp.int32, (nk, 128), 1) < ATT_HEAD_DIM, 1.0, 0.0).astype(BF16)
    k_aug = kv * left + aux_ref[pl.ds(start, nk), :]
    v_aug = kv * (1.0 - left) + left
    s = lax.dot_general(qa_ref[r0:r0 + nr, :], k_aug, NT, preferred_element_type=F32)
    if bias is not None:
        parts = []
        for t, b in enumerate(bias):
            st = s[:, t * KT:(t + 1) * KT]
            if b is not None:
                st = (st.reshape(nr // QH, QH, KT) + b[None]).reshape(nr, KT)
            parts.append(st)
        s = jnp.concatenate(parts, axis=1) if nsub > 1 else parts[0]
    m_cur = jnp.max(s, axis=-1, keepdims=True)
    if first:
        m_new = jnp.broadcast_to(m_cur, (nr, 128))
    else:
        m_prev = m_scr[r0:r0 + nr, :]
        m_new = jnp.maximum(m_prev, m_cur)
    p = jnp.exp2(s - jnp.concatenate([m_new] * (nk // 128), axis=1))
    pv_t = lax.dot_general(v_aug, p.astype(BF16), (((0,), (1,)), ((), ())),
                           preferred_element_type=F32)
    if first:
        acc_scr[:, r0:r0 + nr] = pv_t
    else:
        acc_scr[:, r0:r0 + nr] = jnp.exp2(m_prev - m_new).T * acc_scr[:, r0:r0 + nr] + pv_t
    m_scr[r0:r0 + nr, :] = m_new


def _flash_out(acc_scr):
    r = acc_scr.shape[1]
    den = acc_scr[0:8, :]
    num = acc_scr[ATT_HEAD_DIM:2 * ATT_HEAD_DIM, :]
    o_t = (num.reshape(ATT_HEAD_DIM // 8, 8, r) / den[None]).reshape(ATT_HEAD_DIM, r)
    halves = []
    for half in range(TQ // QH):
        cols = [o_t[:, (half * ATT_HPG + h) * QH:(half * ATT_HPG + h + 1) * QH] for h in range(ATT_HPG)]
        halves.append(jnp.concatenate(cols, axis=0).T)
    return jnp.concatenate(halves, axis=0)


def _attn_kernel(slopes_ref, q_ref, ksel_ref, kwin_ref, aux_ref, qmask_ref, ocmp_ref, small_ref,
                 zatt_ref, pick_ref, bias_ref, o_ref, qas_scr, qaw_scr, m_scr, acc_scr):
    g = pl.program_id(1)
    i = pl.program_id(2)

    qf = q_ref[0].astype(F32)
    qm = qmask_ref[0, 0].astype(F32)
    lane_row = lax.broadcasted_iota(jnp.int32, (1, 128), 1)
    left = lax.broadcasted_iota(jnp.int32, (TQ, 128), 1) < ATT_HEAD_DIM
    for h in range(ATT_HPG):
        slab = qf[:, 128 * (h // 2):128 * (h // 2) + 128]
        if h % 2:
            slab = pltpu.roll(slab, ATT_HEAD_DIM, 1)
        ext = jnp.zeros((1, 128), F32)
        for c in range(6):
            ext = jnp.where(lane_row == L_POS + c, slopes_ref[(g * ATT_HPG + h) * SLOPE_STRIDE + c], ext)
        base = jnp.where(left, slab, ext)
        for half in range(TQ // QH):
            rs = (half * ATT_HPG + h) * QH
            tok = slice(half * QH, (half + 1) * QH)
            qaw_scr[rs:rs + QH, :] = base[tok].astype(BF16)
            qas_scr[rs:rs + QH, :] = (base[tok] + qm[tok]).astype(BF16)

    half_rows = ATT_HPG * QH
    rows_a, rows_b, rows_all = (0, half_rows), (half_rows, half_rows), (0, 2 * half_rows)
    k0 = i * (TQ // KT)
    w0, w1, w2, causal = (bias_ref[n] for n in range(4))

    _flash_step(qas_scr, ksel_ref, aux_ref, m_scr, acc_scr, rows_a, k0, 1, [causal], first=True)
    _flash_step(qas_scr, ksel_ref, aux_ref, m_scr, acc_scr, rows_b, k0, 2, [None, causal], first=True)

    for n_prev in range(1, ksel_ref.shape[1] // TK):
        @pl.when(i == n_prev)
        def _():
            for j in range(n_prev):
                _flash_step(qas_scr, ksel_ref, aux_ref, m_scr, acc_scr, rows_all, j * (TK // KT),
                            TK // KT, None)

    o_slc = _flash_out(acc_scr)

    @pl.when(i == 0)
    def _():
        _flash_step(qaw_scr, kwin_ref, aux_ref, m_scr, acc_scr, rows_a, 0, 1, [w2], first=True)
        _flash_step(qaw_scr, kwin_ref, aux_ref, m_scr, acc_scr, rows_b, 0, 2, [w1, w2], first=True)

    @pl.when(i >= 1)
    def _():
        _flash_step(qaw_scr, kwin_ref, aux_ref, m_scr, acc_scr, rows_a, k0 - 2, 3, [w0, w1, w2], first=True)
        _flash_step(qaw_scr, kwin_ref, aux_ref, m_scr, acc_scr, rows_b, k0 - 1, 3, [w0, w1, w2], first=True)

    o_win = _flash_out(acc_scr)

    gate = _sigmoid(small_ref[0])
    g_hi = gate.astype(BF16)
    g_lo = (gate - g_hi.astype(F32)).astype(BF16)
    gts = jnp.dot(jnp.concatenate([g_hi, g_lo], axis=1), pick_ref[0],
                  preferred_element_type=F32)
    o = (gts[:, :KV_WIDTH] * ocmp_ref[0] + gts[:, KV_WIDTH:2 * KV_WIDTH] * o_slc
         + gts[:, 2 * KV_WIDTH:] * o_win)
    o_ref[0] = (o * _silu(zatt_ref[0].astype(F32))).astype(o_ref.dtype)


def _attn(pb, small, qmask, ocmp):
    bsz, s, _ = pb.shape
    g = ATT_KV_GROUPS
    r = ATT_HPG * TQ
    return pl.pallas_call(
        _attn_kernel,
        grid=(bsz, g, s // TQ),
        in_specs=[pl.BlockSpec(memory_space=pltpu.SMEM),
                  pl.BlockSpec((1, TQ, KV_WIDTH), lambda b, gg, i: (b, i, B_Q // KV_WIDTH + gg)),
                  pl.BlockSpec((1, s, 128), lambda b, gg, i: (b, 0, B_KSEL // 128 + gg)),
                  pl.BlockSpec((1, s, 128), lambda b, gg, i: (b, 0, B_KWIN // 128 + gg)),
                  pl.BlockSpec((s, 128), lambda b, gg, i: (0, 0)),
                  pl.BlockSpec((1, 1, TQ, 128), lambda b, gg, i: (b, gg, i, 0)),
                  pl.BlockSpec((1, TQ, KV_WIDTH), lambda b, gg, i: (b, i, gg)),
                  pl.BlockSpec((1, TQ, 128), lambda b, gg, i: (b, i, 0)),
                  pl.BlockSpec((1, TQ, KV_WIDTH), lambda b, gg, i: (b, i, B_ZATT // KV_WIDTH + gg)),
                  pl.BlockSpec((1, 256, 3 * KV_WIDTH), lambda b, gg, i: (gg, 0, 0)),
                  pl.BlockSpec((4, QH, KT), lambda b, gg, i: (0, 0, 0))],
        out_specs=pl.BlockSpec((1, TQ, KV_WIDTH), lambda b, gg, i: (b, i, gg)),
        out_shape=jax.ShapeDtypeStruct((bsz, s, ATT_WIDTH), BF16),
        scratch_shapes=[pltpu.VMEM((r, 128), BF16), pltpu.VMEM((r, 128), BF16),
                        pltpu.VMEM((r, 128), F32), pltpu.VMEM((128, r), F32)],
        compiler_params=_cp(("arbitrary", "arbitrary", "arbitrary")),
        name="attn",
    )(_slope_table(), pb, pb, pb, _key_aux_table(s), qmask, ocmp, small, pb, _gate_expand_table(),
      _bias_table())


def _merge_kernel(yssm_ref, onsa_ref, mg_ref, x_ref, gate_ref, gpost_ref, wssm_ref, wnsa_ref, wout_ref, o_ref):
    y_ssm = jnp.dot(yssm_ref[0], wssm_ref[...], preferred_element_type=F32)
    y_nsa = jnp.dot(onsa_ref[0], wnsa_ref[...], preferred_element_type=F32)
    mg = _sigmoid(mg_ref[0].astype(F32))
    merged = mg[:, :D_MODEL] * y_ssm + mg[:, D_MODEL:] * y_nsa
    out = jnp.dot(merged.astype(BF16), wout_ref[...], preferred_element_type=F32)
    yn = out * lax.rsqrt(jnp.mean(out * out, axis=-1, keepdims=True) + EPS) * gpost_ref[...]
    o_ref[0] = x_ref[0] + gate_ref[0] * yn


def _merge(yssm, onsa, pb, x, gate, g_post, w_ssm_out, w_nsa_out, w_out):
    bsz, s, _ = x.shape
    tm = 512
    const = lambda shape: pl.BlockSpec(shape, lambda b, i: (0, 0))
    return pl.pallas_call(
        _merge_kernel,
        grid=(bsz, s // tm),
        in_specs=[pl.BlockSpec((1, tm, D_INNER), lambda b, i: (b, i, 0)),
                  pl.BlockSpec((1, tm, ATT_WIDTH), lambda b, i: (b, i, 0)),
                  pl.BlockSpec((1, tm, 2 * D_MODEL), lambda b, i: (b, i, B_MERGE // (2 * D_MODEL))),
                  pl.BlockSpec((1, tm, D_MODEL), lambda b, i: (b, i, 0)),
                  pl.BlockSpec((1, 1, D_MODEL), lambda b, i: (b, 0, 0)),
                  const((1, D_MODEL)),
                  const((D_INNER, D_MODEL)), const((ATT_WIDTH, D_MODEL)), const((D_MODEL, D_MODEL))],
        out_specs=pl.BlockSpec((1, tm, D_MODEL), lambda b, i: (b, i, 0)),
        out_shape=jax.ShapeDtypeStruct((bsz, s, D_MODEL), F32),
        compiler_params=_cp(("arbitrary", "arbitrary")),
        name="merge",
    )(yssm, onsa, pb, x, gate, g_post.reshape(1, D_MODEL), w_ssm_out, w_nsa_out, w_out)


def _wprep_kernel(w_ref, wb_ref, ws_ref, wkv_ref):
    def put(ref, dst, off, n, scale=None):
        v = w_ref[off:off + n, :]
        ref[dst:dst + n, :] = (v if scale is None else v * scale).astype(BF16)

    put(wb_ref, B_XBC, IN_XBC, D_CONV)
    put(wb_ref, B_ZATT, IN_ZATT, ATT_WIDTH)
    put(wb_ref, B_MERGE, IN_MERGE, 2 * D_MODEL)
    put(wb_ref, B_ZSSM, IN_Z, D_INNER)
    put(wb_ref, B_Q, IN_Q, ATT_WIDTH, ATT_HEAD_DIM ** -0.5 * LOG2E)
    hd = ATT_HEAD_DIM
    for dst, k_off in ((B_KSEL, IN_KV + 2 * KV_WIDTH), (B_KWIN, IN_KV + 4 * KV_WIDTH)):
        for gg in range(ATT_KV_GROUPS):
            put(wb_ref, dst + 2 * gg * hd, k_off + gg * hd, hd)
            put(wb_ref, dst + (2 * gg + 1) * hd, k_off + KV_WIDTH + gg * hd, hd)
    put(wkv_ref, 0, IN_KV, 2 * KV_WIDTH)
    put(ws_ref, 0, IN_DT, SSM_HEADS)
    put(ws_ref, SMALL_GATE, IN_GATE, 3 * ATT_HEADS)
    used = SMALL_GATE + 3 * ATT_HEADS
    ws_ref[used:, :] = jnp.zeros((128 - used, ws_ref.shape[1]), BF16)


def _split_w_in(w_in):
    dm, n_in = w_in.shape
    tc = 128
    return pl.pallas_call(
        _wprep_kernel,
        grid=(dm // tc,),
        in_specs=[pl.BlockSpec((n_in, tc), lambda i: (0, i))],
        out_specs=[pl.BlockSpec((NB, tc), lambda i: (0, i)), pl.BlockSpec((128, tc), lambda i: (0, i)),
                   pl.BlockSpec((2 * KV_WIDTH, tc), lambda i: (0, i))],
        out_shape=[jax.ShapeDtypeStruct((NB, dm), BF16), jax.ShapeDtypeStruct((128, dm), BF16),
                   jax.ShapeDtypeStruct((2 * KV_WIDTH, dm), BF16)],
        compiler_params=_cp(("arbitrary",)),
        name="wprep",
    )(w_in.T)


def _layer(x, c, w_ada, b_ada, g_pre, g_post, w_in, conv_w, conv_b, dt_bias, a_log, d_skip,
           g_ssm_norm, w_ssm_out, cmp_pos_k, cmp_w1_k, cmp_w2_k, cmp_pos_v, cmp_w1_v, cmp_w2_v,
           w_nsa_out, w_out):
    bsz, s, dm = x.shape
    mod = _ada(c, w_ada, b_ada)
    shift = mod[:, None, :dm]
    scale = mod[:, None, dm:2 * dm]
    gate = mod[:, None, 2 * dm:]
    wb, w_small, w_kvc = _split_w_in(w_in)
    pb, small, kvc = _proj(x, shift, scale, g_pre, wb, w_small, w_kvc)

    yssm = _ssm(pb, small, conv_w, conv_b, dt_bias, a_log, d_skip, g_ssm_norm)

    pos = jnp.stack([cmp_pos_k.reshape(1, -1), cmp_pos_v.reshape(1, -1)])
    w1 = jnp.stack([cmp_w1_k, cmp_w1_v]).astype(BF16)
    w2 = jnp.stack([cmp_w2_k, cmp_w2_v]).astype(BF16)
    cmp_kv = _compress(kvc, pos, w1, w2)

    ocmp, qmask = _cmpattn(pb, cmp_kv, s)
    onsa = _attn(pb, small, qmask, ocmp)
    return _merge(yssm, onsa, pb, x, gate, g_post, w_ssm_out.astype(BF16), w_nsa_out.astype(BF16),
                  w_out.astype(BF16))


@jax.jit
def kernel(x, c, w_ada, b_ada, g_pre, g_post, w_in, conv_w, conv_b, dt_bias, a_log, d_skip, g_ssm_norm,
           w_ssm_out, cmp_pos_k, cmp_w1_k, cmp_w2_k, cmp_pos_v, cmp_w1_v, cmp_w2_v, w_nsa_out, w_out):
    for layer in range(w_in.shape[0]):
        x = _layer(x, c, w_ada[layer], b_ada[layer], g_pre[layer], g_post[layer], w_in[layer],
                   conv_w[layer], conv_b[layer], dt_bias[layer], a_log[layer], d_skip[layer],
                   g_ssm_norm[layer], w_ssm_out[layer], cmp_pos_k[layer], cmp_w1_k[layer],
                   cmp_w2_k[layer], cmp_pos_v[layer], cmp_w1_v[layer], cmp_w2_v[layer],
                   w_nsa_out[layer], w_out[layer])
    return x
```

```python
import numpy as np
import jax
import jax.numpy as jnp
from jax import lax
from jax.experimental import pallas as pl
from jax.experimental.pallas import tpu as pltpu

D_MODEL = 1024
D_INNER = 2048
SSM_HEAD_DIM = 64
SSM_HEADS = 32
SSM_GROUPS = 4
SSM_STATE = 128
SSM_CONV = 4
SSM_CHUNK = 128
D_CONV = D_INNER + 2 * SSM_GROUPS * SSM_STATE
SSM_STEP = 512
CONV_TAIL = 16

ATT_HEADS = 16
ATT_HEAD_DIM = 64
ATT_KV_GROUPS = 4
ATT_HPG = 4
ATT_WIDTH = 1024
KV_WIDTH = 256
CMP_BLOCK = 32
CMP_STRIDE = 16
CMP_HIDDEN = 256
SLC_BLOCK = 64
SLC_SHIFT = 6
SLC_TOPK = 16
WINDOW = 512
FORCE_BONUS = 1000.0
EPS = 1e-6
NEG = -1e30
BIG = 2.0 ** 100

F32 = jnp.float32
BF16 = jnp.bfloat16
HIGHEST = lax.Precision.HIGHEST
NT = (((1,), (1,)), ((), ()))

_SIZES = (D_INNER, D_CONV, SSM_HEADS, ATT_WIDTH, 6 * KV_WIDTH, 3 * ATT_HEADS, ATT_WIDTH, 2 * D_MODEL)
_OFFS = tuple(int(v) for v in np.cumsum((0,) + _SIZES))
IN_Z, IN_XBC, IN_DT, IN_Q, IN_KV, IN_GATE, IN_ZATT, IN_MERGE = _OFFS[:8]

B_XBC, B_ZATT, B_MERGE, B_ZSSM, B_Q, B_KSEL, B_KWIN = 0, 3072, 4096, 6144, 8192, 9216, 9728
NB = 10240
SMALL_GATE = 32
LOG2E = 1.4426950408889634

TM_PROJ = 1024
TN_PROJ = 2560
TQ = 512
TK = 512
QH = 256
KT = 256
POS_BASE = 256
TQ_CMP = 2048
VMEM_LIMIT = 48 * 1024 * 1024

L_POS = 64
L_TILE = 67
L_MASK = 72
SLOPE_STRIDE = 8


def _cp(sem):
    return pltpu.CompilerParams(dimension_semantics=sem, vmem_limit_bytes=VMEM_LIMIT)


def _sigmoid(v):
    return 0.5 * jnp.tanh(0.5 * v) + 0.5


def _silu(v):
    return v * _sigmoid(v)


def _ada_kernel(c_ref, w_ref, b_ref, o_ref):
    o_ref[...] = jnp.dot(c_ref[...], w_ref[...], preferred_element_type=F32,
                         precision=HIGHEST) + b_ref[...]


def _ada(c, w_ada, b_ada):
    bsz = c.shape[0]
    return pl.pallas_call(
        _ada_kernel,
        grid=(3,),
        in_specs=[pl.BlockSpec((bsz, D_MODEL), lambda j: (0, 0)),
                  pl.BlockSpec((D_MODEL, D_MODEL), lambda j: (0, j)),
                  pl.BlockSpec((1, D_MODEL), lambda j: (0, j))],
        out_specs=pl.BlockSpec((bsz, D_MODEL), lambda j: (0, j)),
        out_shape=jax.ShapeDtypeStruct((bsz, 3 * D_MODEL), F32),
        compiler_params=_cp(("arbitrary",)),
        name="ada",
    )(c, w_ada, b_ada.reshape(1, 3 * D_MODEL))


def _proj_kernel(x_ref, shift_ref, scale_ref, g_ref, w_ref, ws_ref, wkv_ref, o_ref, osm_ref, okv_ref, h_scr):
    @pl.when(pl.program_id(2) == 0)
    def _():
        xf = x_ref[0]
        y = xf * lax.rsqrt(jnp.mean(xf * xf, axis=-1, keepdims=True) + EPS) * g_ref[...]
        h_scr[...] = (y * (1.0 + scale_ref[0]) + shift_ref[0]).astype(BF16)
        osm_ref[0] = lax.dot_general(h_scr[...], ws_ref[...], NT, preferred_element_type=F32)
        okv_ref[0] = lax.dot_general(h_scr[...], wkv_ref[...], NT,
                                     preferred_element_type=F32).astype(okv_ref.dtype)

    o_ref[0] = lax.dot_general(h_scr[...], w_ref[...], NT, preferred_element_type=F32).astype(o_ref.dtype)


def _proj(x, shift, scale, g_pre, w, w_small, w_kvc):
    bsz, s, _ = x.shape
    n = w.shape[0]
    nkv = w_kvc.shape[0]
    return pl.pallas_call(
        _proj_kernel,
        grid=(bsz, s // TM_PROJ, n // TN_PROJ),
        in_specs=[pl.BlockSpec((1, TM_PROJ, D_MODEL), lambda b, i, j: (b, i, 0)),
                  pl.BlockSpec((1, 1, D_MODEL), lambda b, i, j: (b, 0, 0)),
                  pl.BlockSpec((1, 1, D_MODEL), lambda b, i, j: (b, 0, 0)),
                  pl.BlockSpec((1, D_MODEL), lambda b, i, j: (0, 0)),
                  pl.BlockSpec((TN_PROJ, D_MODEL), lambda b, i, j: (j, 0)),
                  pl.BlockSpec((128, D_MODEL), lambda b, i, j: (0, 0)),
                  pl.BlockSpec((nkv, D_MODEL), lambda b, i, j: (0, 0))],
        out_specs=[pl.BlockSpec((1, TM_PROJ, TN_PROJ), lambda b, i, j: (b, i, j)),
                   pl.BlockSpec((1, TM_PROJ, 128), lambda b, i, j: (b, i, 0)),
                   pl.BlockSpec((1, TM_PROJ, nkv), lambda b, i, j: (b, i, 0))],
        out_shape=[jax.ShapeDtypeStruct((bsz, s, n), BF16),
                   jax.ShapeDtypeStruct((bsz, s, 128), F32),
                   jax.ShapeDtypeStruct((bsz, s, nkv), F32)],
        scratch_shapes=[pltpu.VMEM((TM_PROJ, D_MODEL), BF16)],
        compiler_params=_cp(("arbitrary", "arbitrary", "arbitrary")),
        name="proj",
    )(x, shift, scale, g_pre.reshape(1, D_MODEL), w, w_small, w_kvc)


def _ssm_kernel(xbc_ref, z_ref, small_ref, shift_ref, spread_ref, convw_ref, convb_ref, dtb_ref,
                alog_ref, dskip_ref, gn_ref, o_ref, tail_scr, state_scr, y_scr):
    cl = SSM_CHUNK

    @pl.when(pl.program_id(1) == 0)
    def _():
        tail_scr[...] = jnp.zeros(tail_scr.shape, BF16)
        state_scr[...] = jnp.zeros(state_scr.shape, F32)

    for sc in range(xbc_ref.shape[1] // cl):
        rows = pl.ds(sc * cl, cl)
        _ssm_chunk(xbc_ref.at[0, rows], z_ref.at[0, rows], small_ref.at[0, rows], shift_ref, spread_ref,
                   convw_ref, convb_ref, dtb_ref, alog_ref, dskip_ref, gn_ref, o_ref.at[0, rows],
                   tail_scr, state_scr, y_scr.at[rows])


def _ssm_chunk(xbc_ref, z_ref, small_ref, shift_ref, spread_ref, convw_ref, convb_ref, dtb_ref,
               alog_ref, dskip_ref, gn_ref, o_ref, tail_scr, state_scr, y_scr):
    cl = SSM_CHUNK
    cur = xbc_ref[...]
    xpad = jnp.concatenate([tail_scr[...], cur], axis=0).astype(F32)
    taps = jnp.concatenate([(xpad * convw_ref[k:k + 1, :]).astype(BF16) for k in range(SSM_CONV - 1)],
                           axis=0)
    tail_scr[...] = cur[cl - CONV_TAIL:cl, :]
    acc = (jnp.dot(shift_ref[...], taps, preferred_element_type=F32)
           + convw_ref[SSM_CONV - 1:SSM_CONV, :] * xpad[CONV_TAIL:] + convb_ref[...])
    u = _silu(acc)
    xs = u[:, :D_INNER]
    bm = u[:, D_INNER:D_INNER + SSM_GROUPS * SSM_STATE]
    cm = u[:, D_INNER + SSM_GROUPS * SSM_STATE:]

    pre = small_ref[...] + dtb_ref[...]
    dt = jnp.maximum(pre, 0.0) + jnp.log1p(jnp.exp(-jnp.abs(pre)))
    a = -jnp.exp(alog_ref[...])
    adt = dt * a
    row = lax.broadcasted_iota(jnp.int32, (cl, cl), 0)
    col = lax.broadcasted_iota(jnp.int32, (cl, cl), 1)
    causal = row >= col
    tri = causal.astype(F32)
    a_cs = jnp.dot(tri, adt, preferred_element_type=F32, precision=HIGHEST)
    a_cs_t = a_cs.T
    dt_t = dt.T
    a_last = a_cs[cl - 1:cl, :]
    ea = jnp.exp(a_cs)
    dsc = jnp.exp(a_last - a_cs) * dt

    def spread(v):
        hi = v.astype(BF16)
        lo = (v - hi.astype(F32)).astype(BF16)
        return jnp.dot(jnp.concatenate([hi, lo], axis=1), spread_ref[...], preferred_element_type=F32)

    ea_x = spread(ea)
    dsc_x = spread(dsc)
    cdec_x = ea_x[cl - 1:cl, :]
    xsd = xs * dsc_x
    lane = lax.broadcasted_iota(jnp.int32, (cl, 2 * SSM_HEAD_DIM), 1)
    first_half = lane < SSM_HEAD_DIM

    hg = SSM_HEADS // SSM_GROUPS
    gw = hg * SSM_HEAD_DIM
    for g in range(SSM_GROUPS):
        bg = bm[:, g * SSM_STATE:(g + 1) * SSM_STATE].astype(BF16)
        cg = cm[:, g * SSM_STATE:(g + 1) * SSM_STATE].astype(BF16)
        gc = slice(g * gw, (g + 1) * gw)
        cb = lax.dot_general(cg, bg, NT, preferred_element_type=F32)
        st = state_scr[g]
        y_scr[:, gc] = jnp.dot(cg, st.astype(BF16), preferred_element_type=F32) * ea_x[:, gc]
        new = lax.dot_general(bg, xsd[:, gc].astype(BF16), (((0,), (0,)), ((), ())),
                              preferred_element_type=F32)
        state_scr[g] = st * cdec_x[:, gc] + new
        for pp in range(hg // 2):
            pair = g * (hg // 2) + pp
            pc = slice(pair * 128, (pair + 1) * 128)
            gm = []
            for e in range(2):
                h = 2 * pair + e
                seg = a_cs[:, h:h + 1] - a_cs_t[h:h + 1, :]
                lmat = jnp.exp(jnp.where(causal, seg, NEG))
                gm.append(cb * lmat * dt_t[h:h + 1, :])
            lhs = jnp.concatenate(gm, axis=0).astype(BF16)
            yy = jnp.dot(lhs, xs[:, pc].astype(BF16), preferred_element_type=F32)
            y_scr[:, pc] = y_scr[:, pc] + jnp.where(first_half, yy[:cl], yy[cl:])

    y = y_scr[...] + xs * dskip_ref[...]
    y = y * _silu(z_ref[...].astype(F32))
    gsz = D_INNER // SSM_GROUPS
    for g in range(SSM_GROUPS):
        yg = y[:, g * gsz:(g + 1) * gsz]
        yn = yg * lax.rsqrt(jnp.mean(yg * yg, axis=-1, keepdims=True) + EPS)
        o_ref[:, g * gsz:(g + 1) * gsz] = (yn * gn_ref[:, g * gsz:(g + 1) * gsz]).astype(o_ref.dtype)


def _conv_shift_matrix():
    cl = SSM_CHUNK
    ext = CONV_TAIL + cl
    m = np.zeros((cl, (SSM_CONV - 1) * ext), np.float32)
    t = np.arange(cl)
    for k in range(SSM_CONV - 1):
        m[t, k * ext + CONV_TAIL + t - (SSM_CONV - 1 - k)] = 1.0
    return jnp.asarray(m, dtype=BF16)


def _head_spread_matrix():
    m = np.zeros((2, 128, D_INNER), np.float32)
    for h in range(SSM_HEADS):
        m[:, h, h * SSM_HEAD_DIM:(h + 1) * SSM_HEAD_DIM] = 1.0
    return jnp.asarray(m.reshape(256, D_INNER), dtype=BF16)


def _ssm(pb, small, conv_w, conv_b, dt_bias, a_log, d_skip, g_norm):
    bsz, s, _ = pb.shape
    cl = SSM_CHUNK
    pad = 128 - SSM_HEADS
    dtb = jnp.pad(dt_bias, (0, pad)).reshape(1, 128)
    alog = jnp.pad(a_log, (0, pad)).reshape(1, 128)
    dskip = jnp.repeat(d_skip, SSM_HEAD_DIM).reshape(1, D_INNER)
    const = lambda shape: pl.BlockSpec(shape, lambda b, c: (0, 0))
    return pl.pallas_call(
        _ssm_kernel,
        grid=(bsz, s // SSM_STEP),
        in_specs=[pl.BlockSpec((1, SSM_STEP, D_CONV), lambda b, c: (b, c, B_XBC // D_CONV)),
                  pl.BlockSpec((1, SSM_STEP, D_INNER), lambda b, c: (b, c, B_ZSSM // D_INNER)),
                  pl.BlockSpec((1, SSM_STEP, 128), lambda b, c: (b, c, 0)),
                  const((cl, (SSM_CONV - 1) * (CONV_TAIL + cl))), const((256, D_INNER)),
                  const((SSM_CONV, D_CONV)), const((1, D_CONV)), const((1, 128)), const((1, 128)),
                  const((1, D_INNER)), const((1, D_INNER))],
        out_specs=pl.BlockSpec((1, SSM_STEP, D_INNER), lambda b, c: (b, c, 0)),
        out_shape=jax.ShapeDtypeStruct((bsz, s, D_INNER), BF16),
        scratch_shapes=[pltpu.VMEM((CONV_TAIL, D_CONV), BF16),
                        pltpu.VMEM((SSM_GROUPS, SSM_STATE, D_INNER // SSM_GROUPS), F32),
                        pltpu.VMEM((SSM_STEP, D_INNER), F32)],
        compiler_params=_cp(("arbitrary", "arbitrary")),
        name="ssm",
    )(pb, pb, small, _conv_shift_matrix(), _head_spread_matrix(), conv_w, conv_b.reshape(1, D_CONV),
      dtb, alog, dskip,
      g_norm.reshape(1, D_INNER))


def _cmp_kernel(r0_ref, r1_ref, r2_ref, r3_ref, pos_ref, w1_ref, w2_ref, o_ref):
    half = CMP_STRIDE * ATT_HEAD_DIM
    hd = ATT_HEAD_DIM
    nrow = r0_ref.shape[1] // CMP_STRIDE
    toks = [[r_ref[0, pl.ds(t, nrow, stride=CMP_STRIDE), :] for t in range(CMP_STRIDE)]
            for r_ref in (r0_ref, r1_ref, r2_ref, r3_ref)]
    for kv in range(2):
        w1 = w1_ref[kv]
        posb = jnp.broadcast_to(pos_ref[kv], (8, 2 * half)).astype(BF16)
        cpos = jnp.dot(posb, w1, preferred_element_type=F32)[0:1]
        for gg in range(ATT_KV_GROUPS):
            j = kv * ATT_KV_GROUPS + gg
            r = jnp.concatenate([toks[j // 2][t][:, (j % 2) * hd:(j % 2 + 1) * hd]
                                 for t in range(CMP_STRIDE)], axis=1).astype(BF16)
            first = jnp.dot(r, w1[:half], preferred_element_type=F32)
            second = jnp.dot(r, w1[half:], preferred_element_type=F32)
            hid = first + pltpu.roll(second, nrow - 1, 0) + cpos
            o_ref[0, j] = jnp.dot(_silu(hid).astype(BF16), w2_ref[kv], preferred_element_type=F32)


def _compress(kvc, pos, w1, w2):
    bsz, s, wid = kvc.shape
    nrow = s // CMP_STRIDE
    ng = 2 * ATT_KV_GROUPS
    return pl.pallas_call(
        _cmp_kernel,
        grid=(bsz,),
        in_specs=[pl.BlockSpec((1, s, 128), lambda b, c=c: (b, 0, c)) for c in range(wid // 128)]
                 + [pl.BlockSpec((2, 1, 2048), lambda b: (0, 0, 0)),
                  pl.BlockSpec((2, 2048, CMP_HIDDEN), lambda b: (0, 0, 0)),
                  pl.BlockSpec((2, CMP_HIDDEN, ATT_HEAD_DIM), lambda b: (0, 0, 0))],
        out_specs=pl.BlockSpec((1, ng, nrow, ATT_HEAD_DIM), lambda b: (b, 0, 0, 0)),
        out_shape=jax.ShapeDtypeStruct((bsz, ng, nrow, ATT_HEAD_DIM), F32),
        compiler_params=_cp(("arbitrary",)),
        name="cmp",
    )(kvc, kvc, kvc, kvc, pos, w1, w2)


def _bf16_round_np(x):
    u = np.asarray(x, np.float32).view(np.uint32)
    u = (u + (((u >> 16) & 1) + 0x7FFF)) & np.uint32(0xFFFF0000)
    return u.view(np.float32)


def _slope_table():
    slope = (2.0 ** (-8.0 * np.arange(1, ATT_HEADS + 1) / ATT_HEADS)).astype(np.float32)
    slope = (slope.astype(np.float64) * LOG2E).astype(np.float32)
    p0 = _bf16_round_np(slope)
    p1 = _bf16_round_np(slope - p0)
    p2 = _bf16_round_np(slope - p0 - p1)
    tab = np.zeros((ATT_HEADS, SLOPE_STRIDE), np.float32)
    tab[:, 0], tab[:, 1], tab[:, 2] = p0, p1, p2
    tab[:, 3:6] = tab[:, 0:3] * POS_BASE
    tab[:, 6] = slope
    return jnp.asarray(tab.reshape(-1))


def _key_aux_table(s):
    pos = np.arange(s)
    tab = np.zeros((s, 128), np.float32)
    tab[:, L_POS:L_POS + 3] = (pos % POS_BASE)[:, None]
    tab[:, L_TILE:L_TILE + 3] = (pos // POS_BASE)[:, None]
    tab[pos, L_MASK + pos // SLC_BLOCK] = 1.0
    return jnp.asarray(tab, dtype=BF16)


def _bias_table():
    assert WINDOW == 2 * KT and QH == KT
    r = np.arange(QH)[:, None]
    c = np.arange(KT)[None, :]
    oldest = np.where(2 * KT + r - c < WINDOW, 0.0, NEG)
    causal = np.where(r - c >= 0, 0.0, NEG)
    return jnp.asarray(np.stack([oldest, causal]), dtype=F32)


def _gate_expand_table():
    tab = np.zeros((ATT_KV_GROUPS, 2, 128, 3 * KV_WIDTH), np.float32)
    for g in range(ATT_KV_GROUPS):
        for h in range(ATT_HPG):
            for j in range(3):
                row = SMALL_GATE + g * 3 * ATT_HPG + 3 * h + j
                tab[g, :, row, j * KV_WIDTH + h * ATT_HEAD_DIM:j * KV_WIDTH + (h + 1) * ATT_HEAD_DIM] = 1.0
    return jnp.asarray(tab.reshape(ATT_KV_GROUPS, 256, 3 * KV_WIDTH), dtype=BF16)


def _cmp_to_slc_matrix_t(n_cmp_pad, n_cmp, n_slc):
    cs = np.arange(n_cmp_pad) * CMP_STRIDE
    ss = np.arange(n_slc) * SLC_BLOCK
    lo = np.maximum(cs[:, None], ss[None, :])
    hi = np.minimum(cs[:, None] + CMP_BLOCK, ss[None, :] + SLC_BLOCK)
    m = np.clip(hi - lo, 0, None) / CMP_BLOCK
    m[n_cmp:] = 0.0
    return jnp.asarray(m.T, dtype=F32)


def _cmpattn_kernel(slopes_ref, q_ref, ck_ref, cv_ref, mt_ref, ocmp_ref, qmask_ref,
                    khi_scr, klo_scr, vbd_scr):
    g = pl.program_id(1)
    i = pl.program_id(2)
    tq = q_ref.shape[1]
    t0 = i * tq
    npad = ck_ref.shape[2]
    n_slc = mt_ref.shape[0]

    @pl.when(i == 0)
    def _():
        kc = ck_ref[0, 0]
        vc = cv_ref[0, 0]
        k_hi = kc.astype(BF16).astype(F32)
        k_lo = kc - k_hi
        zero = jnp.zeros_like(kc)
        for h in range(ATT_HPG):
            def bd(a):
                return jnp.concatenate([a if hh == h else zero for hh in range(ATT_HPG)], axis=1).astype(BF16)
            khi_scr[h * npad:(h + 1) * npad, :] = bd(k_hi)
            klo_scr[h * npad:(h + 1) * npad, :] = bd(k_lo)
            vbd_scr[h * npad:(h + 1) * npad, :] = bd(vc)

    q = q_ref[0]
    s = (lax.dot_general(q, khi_scr[...], NT, preferred_element_type=F32)
         + lax.dot_general(q, klo_scr[...], NT, preferred_element_type=F32))
    jcol = lax.broadcasted_iota(jnp.int32, (1, npad), 1)
    cmp_end = jcol * CMP_STRIDE + (CMP_BLOCK - 1)
    trow = t0 + lax.broadcasted_iota(jnp.int32, (tq, npad), 0)
    valid = (cmp_end <= trow) & (jcol < npad - 1)
    rel = (cmp_end - t0).astype(F32)
    psum = jnp.zeros((tq, npad), F32)
    ps = []
    for h in range(ATT_HPG):
        slope = slopes_ref[(g * ATT_HPG + h) * SLOPE_STRIDE + 6]
        sh = jnp.where(valid, s[:, h * npad:(h + 1) * npad] + slope * rel, NEG)
        mx = jnp.max(sh, axis=-1, keepdims=True)
        p = jnp.where(valid, jnp.exp2(sh - mx), 0.0)
        l = jnp.sum(p, axis=-1, keepdims=True)
        p = p * jnp.where(l > 0.0, 1.0 / jnp.where(l > 0.0, l, 1.0), 0.0)
        ps.append(p)
        psum = psum + p
    ocmp_ref[0] = jnp.dot(jnp.concatenate(ps, axis=1).astype(BF16), vbd_scr[...],
                          preferred_element_type=F32)
    imp_t = lax.dot_general(mt_ref[...], psum, NT, preferred_element_type=F32,
                            precision=HIGHEST)

    nrb = n_slc // 8
    blk_t = (t0 + lax.broadcasted_iota(jnp.int32, (8, tq), 1)) >> SLC_SHIFT
    sub = lax.broadcasted_iota(jnp.int32, (8, tq), 0)
    score = []
    for rb in range(nrb):
        kk = sub + rb * 8
        imp = imp_t[rb * 8:(rb + 1) * 8]
        forced = (kk == 0) | (kk == blk_t) | (kk == blk_t - 1)
        score.append(jnp.where(forced, imp + FORCE_BONUS, jnp.where(kk <= blk_t, imp, -1.0)))
    rank = [jnp.zeros((8, tq), F32) for _ in range(nrb)]
    for j in range(n_slc):
        cj = jnp.broadcast_to(score[j // 8][j % 8:j % 8 + 1, :], (8, tq))
        for rb in range(nrb):
            ge = jnp.where(cj >= score[rb], 1.0, 0.0)
            gt = jnp.where(cj > score[rb], 1.0, 0.0)
            if rb * 8 > j:
                beats = ge
            elif rb * 8 + 7 <= j:
                beats = gt
            else:
                beats = jnp.where(sub > j - rb * 8, ge, gt)
            rank[rb] = rank[rb] + beats
    rows = [jnp.zeros((L_MASK, tq), F32)]
    rows += [jnp.where(rank[rb] < float(SLC_TOPK), 0.0, -BIG) for rb in range(nrb)]
    rows += [jnp.zeros((128 - L_MASK - n_slc, tq), F32)]
    qmask_ref[0, 0] = jnp.concatenate(rows, axis=0).T.astype(BF16)


def _cmpattn(pb, cmp_kv, s):
    bsz = pb.shape[0]
    g = ATT_KV_GROUPS
    nrow = cmp_kv.shape[2]
    n_cmp = (s - CMP_BLOCK) // CMP_STRIDE + 1
    n_slc = s // SLC_BLOCK
    mt = _cmp_to_slc_matrix_t(nrow, n_cmp, n_slc)
    return pl.pallas_call(
        _cmpattn_kernel,
        grid=(bsz, g, s // TQ_CMP),
        in_specs=[pl.BlockSpec(memory_space=pltpu.SMEM),
                  pl.BlockSpec((1, TQ_CMP, KV_WIDTH), lambda b, gg, i: (b, i, B_Q // KV_WIDTH + gg)),
                  pl.BlockSpec((1, 1, nrow, ATT_HEAD_DIM), lambda b, gg, i: (b, gg, 0, 0)),
                  pl.BlockSpec((1, 1, nrow, ATT_HEAD_DIM), lambda b, gg, i: (b, g + gg, 0, 0)),
                  pl.BlockSpec((n_slc, nrow), lambda b, gg, i: (0, 0))],
        out_specs=[pl.BlockSpec((1, TQ_CMP, KV_WIDTH), lambda b, gg, i: (b, i, gg)),
                   pl.BlockSpec((1, 1, TQ_CMP, 128), lambda b, gg, i: (b, gg, i, 0))],
        out_shape=[jax.ShapeDtypeStruct((bsz, s, ATT_WIDTH), F32),
                   jax.ShapeDtypeStruct((bsz, g, s, 128), BF16)],
        scratch_shapes=[pltpu.VMEM((ATT_HPG * nrow, KV_WIDTH), BF16),
                        pltpu.VMEM((ATT_HPG * nrow, KV_WIDTH), BF16),
                        pltpu.VMEM((ATT_HPG * nrow, KV_WIDTH), BF16)],
        compiler_params=_cp(("arbitrary", "arbitrary", "arbitrary")),
        name="cmpattn",
    )(_slope_table(), pb, cmp_kv, cmp_kv, mt)


def _flash_step(qa_ref, ka_ref, va_ref, m_scr, acc_scr, rows, ksub, nsub, bias, first=False):
    r0, nr = rows
    nk = nsub * KT
    start = pl.multiple_of(ksub * KT, KT)
    k_aug = ka_ref[pl.ds(start, nk), :]
    v_aug = va_ref[pl.ds(start, nk), :]
    s = lax.dot_general(qa_ref[r0:r0 + nr, :], k_aug, NT, preferred_element_type=F32)
    if bias is not None:
        parts = []
        for t, b in enumerate(bias):
            st = s[:, t * KT:(t + 1) * KT]
            if b is not None:
                st = (st.reshape(nr // QH, QH, KT) + b[None]).reshape(nr, KT)
            parts.append(st)
        s = jnp.concatenate(parts, axis=1) if nsub > 1 else parts[0]
    m_cur = jnp.max(s, axis=-1, keepdims=True)
    if first:
        m_new = jnp.broadcast_to(m_cur, (nr, 128))
    else:
        m_prev = m_scr[r0:r0 + nr, :]
        m_new = jnp.maximum(m_prev, m_cur)
    p = jnp.exp2(s - jnp.concatenate([m_new] * (nk // 128), axis=1))
    pv_t = lax.dot_general(v_aug, p.astype(BF16), (((0,), (1,)), ((), ())),
                           preferred_element_type=F32)
    if first:
        acc_scr[:, r0:r0 + nr] = pv_t
    else:
        acc_scr[:, r0:r0 + nr] = jnp.exp2(m_prev - m_new).T * acc_scr[:, r0:r0 + nr] + pv_t
    m_scr[r0:r0 + nr, :] = m_new


def _flash_out(acc_scr):
    r = acc_scr.shape[1]
    den = acc_scr[0:8, :]
    num = acc_scr[ATT_HEAD_DIM:2 * ATT_HEAD_DIM, :]
    o_t = (num.reshape(ATT_HEAD_DIM // 8, 8, r) / den[None]).reshape(ATT_HEAD_DIM, r)
    halves = []
    for half in range(TQ // QH):
        cols = [o_t[:, (half * ATT_HPG + h) * QH:(half * ATT_HPG + h + 1) * QH] for h in range(ATT_HPG)]
        halves.append(jnp.concatenate(cols, axis=0).T)
    return jnp.concatenate(halves, axis=0)


def _attn_kernel(slopes_ref, q_ref, ksel_ref, kwin_ref, aux_ref, qmask_ref, ocmp_ref, small_ref,
                 zatt_ref, pick_ref, bias_ref, o_ref, qas_scr, qaw_scr, m_scr, acc_scr,
                 kas_scr, vas_scr, kaw_scr, vaw_scr):
    g = pl.program_id(1)
    i = pl.program_id(2)

    @pl.when(i == 0)
    def _():
        s_len = ksel_ref.shape[1]
        left = jnp.where(lax.broadcasted_iota(jnp.int32, (s_len, 128), 1) < ATT_HEAD_DIM,
                         1.0, 0.0).astype(BF16)
        for kv_ref, ka_scr, va_scr in ((ksel_ref, kas_scr, vas_scr), (kwin_ref, kaw_scr, vaw_scr)):
            kv = kv_ref[0]
            ka_scr[...] = kv * left + aux_ref[...]
            va_scr[...] = kv * (1.0 - left) + left

    qf = q_ref[0].astype(F32)
    qm = qmask_ref[0, 0].astype(F32)
    lane_row = lax.broadcasted_iota(jnp.int32, (1, 128), 1)
    left = lax.broadcasted_iota(jnp.int32, (TQ, 128), 1) < ATT_HEAD_DIM
    for h in range(ATT_HPG):
        slab = qf[:, 128 * (h // 2):128 * (h // 2) + 128]
        if h % 2:
            slab = pltpu.roll(slab, ATT_HEAD_DIM, 1)
        ext = jnp.zeros((1, 128), F32)
        for c in range(6):
            ext = jnp.where(lane_row == L_POS + c, slopes_ref[(g * ATT_HPG + h) * SLOPE_STRIDE + c], ext)
        base = jnp.where(left, slab, ext)
        for half in range(TQ // QH):
            rs = (half * ATT_HPG + h) * QH
            tok = slice(half * QH, (half + 1) * QH)
            qaw_scr[rs:rs + QH, :] = base[tok].astype(BF16)
            qas_scr[rs:rs + QH, :] = (base[tok] + qm[tok]).astype(BF16)

    half_rows = ATT_HPG * QH
    rows_a, rows_b, rows_all = (0, half_rows), (half_rows, half_rows), (0, 2 * half_rows)
    k0 = i * (TQ // KT)
    oldest, causal = bias_ref[0], bias_ref[1]

    _flash_step(qas_scr, kas_scr, vas_scr, m_scr, acc_scr, rows_a, k0, 1, [causal], first=True)
    _flash_step(qas_scr, kas_scr, vas_scr, m_scr, acc_scr, rows_b, k0, 2, [None, causal], first=True)

    for n_prev in range(1, ksel_ref.shape[1] // TK):
        @pl.when(i == n_prev)
        def _():
            for j in range(n_prev):
                _flash_step(qas_scr, kas_scr, vas_scr, m_scr, acc_scr, rows_all, j * (TK // KT),
                            TK // KT, None)

    o_slc = _flash_out(acc_scr)

    @pl.when(i == 0)
    def _():
        _flash_step(qaw_scr, kaw_scr, vaw_scr, m_scr, acc_scr, rows_a, 0, 1, [causal], first=True)
        _flash_step(qaw_scr, kaw_scr, vaw_scr, m_scr, acc_scr, rows_b, 0, 2, [None, causal], first=True)

    @pl.when(i >= 1)
    def _():
        win = [oldest, None, causal]
        _flash_step(qaw_scr, kaw_scr, vaw_scr, m_scr, acc_scr, rows_a, k0 - 2, 3, win, first=True)
        _flash_step(qaw_scr, kaw_scr, vaw_scr, m_scr, acc_scr, rows_b, k0 - 1, 3, win, first=True)

    o_win = _flash_out(acc_scr)

    gate = _sigmoid(small_ref[0])
    g_hi = gate.astype(BF16)
    g_lo = (gate - g_hi.astype(F32)).astype(BF16)
    gts = jnp.dot(jnp.concatenate([g_hi, g_lo], axis=1), pick_ref[0],
                  preferred_element_type=F32)
    o = (gts[:, :KV_WIDTH] * ocmp_ref[0] + gts[:, KV_WIDTH:2 * KV_WIDTH] * o_slc
         + gts[:, 2 * KV_WIDTH:] * o_win)
    o_ref[0] = (o * _silu(zatt_ref[0].astype(F32))).astype(o_ref.dtype)


def _attn(pb, small, qmask, ocmp):
    bsz, s, _ = pb.shape
    g = ATT_KV_GROUPS
    r = ATT_HPG * TQ
    return pl.pallas_call(
        _attn_kernel,
        grid=(bsz, g, s // TQ),
        in_specs=[pl.BlockSpec(memory_space=pltpu.SMEM),
                  pl.BlockSpec((1, TQ, KV_WIDTH), lambda b, gg, i: (b, i, B_Q // KV_WIDTH + gg)),
                  pl.BlockSpec((1, s, 128), lambda b, gg, i: (b, 0, B_KSEL // 128 + gg)),
                  pl.BlockSpec((1, s, 128), lambda b, gg, i: (b, 0, B_KWIN // 128 + gg)),
                  pl.BlockSpec((s, 128), lambda b, gg, i: (0, 0)),
                  pl.BlockSpec((1, 1, TQ, 128), lambda b, gg, i: (b, gg, i, 0)),
                  pl.BlockSpec((1, TQ, KV_WIDTH), lambda b, gg, i: (b, i, gg)),
                  pl.BlockSpec((1, TQ, 128), lambda b, gg, i: (b, i, 0)),
                  pl.BlockSpec((1, TQ, KV_WIDTH), lambda b, gg, i: (b, i, B_ZATT // KV_WIDTH + gg)),
                  pl.BlockSpec((1, 256, 3 * KV_WIDTH), lambda b, gg, i: (gg, 0, 0)),
                  pl.BlockSpec((2, QH, KT), lambda b, gg, i: (0, 0, 0))],
        out_specs=pl.BlockSpec((1, TQ, KV_WIDTH), lambda b, gg, i: (b, i, gg)),
        out_shape=jax.ShapeDtypeStruct((bsz, s, ATT_WIDTH), BF16),
        scratch_shapes=[pltpu.VMEM((r, 128), BF16), pltpu.VMEM((r, 128), BF16),
                        pltpu.VMEM((r, 128), F32), pltpu.VMEM((128, r), F32)]
                       + [pltpu.VMEM((s, 128), BF16)] * 4,
        compiler_params=_cp(("arbitrary", "arbitrary", "arbitrary")),
        name="attn",
    )(_slope_table(), pb, pb, pb, _key_aux_table(s), qmask, ocmp, small, pb, _gate_expand_table(),
      _bias_table())


def _merge_kernel(yssm_ref, onsa_ref, mg_ref, x_ref, gate_ref, gpost_ref, wssm_ref, wnsa_ref, wout_ref, o_ref):
    y_ssm = jnp.dot(yssm_ref[0], wssm_ref[...], preferred_element_type=F32)
    y_nsa = jnp.dot(onsa_ref[0], wnsa_ref[...], preferred_element_type=F32)
    mg = _sigmoid(mg_ref[0].astype(F32))
    merged = mg[:, :D_MODEL] * y_ssm + mg[:, D_MODEL:] * y_nsa
    out = jnp.dot(merged.astype(BF16), wout_ref[...], preferred_element_type=F32)
    yn = out * lax.rsqrt(jnp.mean(out * out, axis=-1, keepdims=True) + EPS) * gpost_ref[...]
    o_ref[0] = x_ref[0] + gate_ref[0] * yn


def _merge(yssm, onsa, pb, x, gate, g_post, w_ssm_out, w_nsa_out, w_out):
    bsz, s, _ = x.shape
    tm = 512
    const = lambda shape: pl.BlockSpec(shape, lambda b, i: (0, 0))
    return pl.pallas_call(
        _merge_kernel,
        grid=(bsz, s // tm),
        in_specs=[pl.BlockSpec((1, tm, D_INNER), lambda b, i: (b, i, 0)),
                  pl.BlockSpec((1, tm, ATT_WIDTH), lambda b, i: (b, i, 0)),
                  pl.BlockSpec((1, tm, 2 * D_MODEL), lambda b, i: (b, i, B_MERGE // (2 * D_MODEL))),
                  pl.BlockSpec((1, tm, D_MODEL), lambda b, i: (b, i, 0)),
                  pl.BlockSpec((1, 1, D_MODEL), lambda b, i: (b, 0, 0)),
                  const((1, D_MODEL)),
                  const((D_INNER, D_MODEL)), const((ATT_WIDTH, D_MODEL)), const((D_MODEL, D_MODEL))],
        out_specs=pl.BlockSpec((1, tm, D_MODEL), lambda b, i: (b, i, 0)),
        out_shape=jax.ShapeDtypeStruct((bsz, s, D_MODEL), F32),
        compiler_params=_cp(("arbitrary", "arbitrary")),
        name="merge",
    )(yssm, onsa, pb, x, gate, g_post.reshape(1, D_MODEL), w_ssm_out, w_nsa_out, w_out)


def _wprep_kernel(w_ref, wb_ref, ws_ref, wkv_ref):
    def put(ref, dst, off, n, scale=None):
        v = w_ref[off:off + n, :]
        ref[dst:dst + n, :] = (v if scale is None else v * scale).astype(BF16)

    put(wb_ref, B_XBC, IN_XBC, D_CONV)
    put(wb_ref, B_ZATT, IN_ZATT, ATT_WIDTH)
    put(wb_ref, B_MERGE, IN_MERGE, 2 * D_MODEL)
    put(wb_ref, B_ZSSM, IN_Z, D_INNER)
    put(wb_ref, B_Q, IN_Q, ATT_WIDTH, ATT_HEAD_DIM ** -0.5 * LOG2E)
    hd = ATT_HEAD_DIM
    for dst, k_off in ((B_KSEL, IN_KV + 2 * KV_WIDTH), (B_KWIN, IN_KV + 4 * KV_WIDTH)):
        for gg in range(ATT_KV_GROUPS):
            put(wb_ref, dst + 2 * gg * hd, k_off + gg * hd, hd)
            put(wb_ref, dst + (2 * gg + 1) * hd, k_off + KV_WIDTH + gg * hd, hd)
    put(wkv_ref, 0, IN_KV, 2 * KV_WIDTH)
    put(ws_ref, 0, IN_DT, SSM_HEADS)
    put(ws_ref, SMALL_GATE, IN_GATE, 3 * ATT_HEADS)
    used = SMALL_GATE + 3 * ATT_HEADS
    ws_ref[used:, :] = jnp.zeros((128 - used, ws_ref.shape[1]), BF16)


def _split_w_in(w_in):
    dm, n_in = w_in.shape
    tc = 128
    return pl.pallas_call(
        _wprep_kernel,
        grid=(dm // tc,),
        in_specs=[pl.BlockSpec((n_in, tc), lambda i: (0, i))],
        out_specs=[pl.BlockSpec((NB, tc), lambda i: (0, i)), pl.BlockSpec((128, tc), lambda i: (0, i)),
                   pl.BlockSpec((2 * KV_WIDTH, tc), lambda i: (0, i))],
        out_shape=[jax.ShapeDtypeStruct((NB, dm), BF16), jax.ShapeDtypeStruct((128, dm), BF16),
                   jax.ShapeDtypeStruct((2 * KV_WIDTH, dm), BF16)],
        compiler_params=_cp(("arbitrary",)),
        name="wprep",
    )(w_in.T)


def _layer(x, c, w_ada, b_ada, g_pre, g_post, w_in, conv_w, conv_b, dt_bias, a_log, d_skip,
           g_ssm_norm, w_ssm_out, cmp_pos_k, cmp_w1_k, cmp_w2_k, cmp_pos_v, cmp_w1_v, cmp_w2_v,
           w_nsa_out, w_out):
    bsz, s, dm = x.shape
    mod = _ada(c, w_ada, b_ada)
    shift = mod[:, None, :dm]
    scale = mod[:, None, dm:2 * dm]
    gate = mod[:, None, 2 * dm:]
    wb, w_small, w_kvc = _split_w_in(w_in)
    pb, small, kvc = _proj(x, shift, scale, g_pre, wb, w_small, w_kvc)

    yssm = _ssm(pb, small, conv_w, conv_b, dt_bias, a_log, d_skip, g_ssm_norm)

    pos = jnp.stack([cmp_pos_k.reshape(1, -1), cmp_pos_v.reshape(1, -1)])
    w1 = jnp.stack([cmp_w1_k, cmp_w1_v]).astype(BF16)
    w2 = jnp.stack([cmp_w2_k, cmp_w2_v]).astype(BF16)
    cmp_kv = _compress(kvc, pos, w1, w2)

    ocmp, qmask = _cmpattn(pb, cmp_kv, s)
    onsa = _attn(pb, small, qmask, ocmp)
    return _merge(yssm, onsa, pb, x, gate, g_post, w_ssm_out.astype(BF16), w_nsa_out.astype(BF16),
                  w_out.astype(BF16))


@jax.jit
def kernel(x, c, w_ada, b_ada, g_pre, g_post, w_in, conv_w, conv_b, dt_bias, a_log, d_skip, g_ssm_norm,
           w_ssm_out, cmp_pos_k, cmp_w1_k, cmp_w2_k, cmp_pos_v, cmp_w1_v, cmp_w2_v, w_nsa_out, w_out):
    for layer in range(w_in.shape[0]):
        x = _layer(x, c, w_ada[layer], b_ada[layer], g_pre[layer], g_post[layer], w_in[layer],
                   conv_w[layer], conv_b[layer], dt_bias[layer], a_log[layer], d_skip[layer],
                   g_ssm_norm[layer], w_ssm_out[layer], cmp_pos_k[layer], cmp_w1_k[layer],
                   cmp_w2_k[layer], cmp_pos_v[layer], cmp_w1_v[layer], cmp_w2_v[layer],
                   w_nsa_out[layer], w_out[layer])
    return x
```

```python
import numpy as np
import jax
import jax.numpy as jnp
from jax import lax
from jax.experimental import pallas as pl
from jax.experimental.pallas import tpu as pltpu

D_MODEL = 1024
D_INNER = 2048
SSM_HEAD_DIM = 64
SSM_HEADS = 32
SSM_GROUPS = 4
SSM_STATE = 128
SSM_CONV = 4
SSM_CHUNK = 128
D_CONV = D_INNER + 2 * SSM_GROUPS * SSM_STATE
SSM_STEP = 512
CONV_TAIL = 16

ATT_HEADS = 16
ATT_HEAD_DIM = 64
ATT_KV_GROUPS = 4
ATT_HPG = 4
ATT_WIDTH = 1024
KV_WIDTH = 256
CMP_BLOCK = 32
CMP_STRIDE = 16
CMP_HIDDEN = 256
SLC_BLOCK = 64
SLC_SHIFT = 6
SLC_TOPK = 16
WINDOW = 512
FORCE_BONUS = 1000.0
EPS = 1e-6
NEG = -1e30
BIG = 2.0 ** 100

F32 = jnp.float32
BF16 = jnp.bfloat16
HIGHEST = lax.Precision.HIGHEST
NT = (((1,), (1,)), ((), ()))

_SIZES = (D_INNER, D_CONV, SSM_HEADS, ATT_WIDTH, 6 * KV_WIDTH, 3 * ATT_HEADS, ATT_WIDTH, 2 * D_MODEL)
_OFFS = tuple(int(v) for v in np.cumsum((0,) + _SIZES))
IN_Z, IN_XBC, IN_DT, IN_Q, IN_KV, IN_GATE, IN_ZATT, IN_MERGE = _OFFS[:8]

B_XBC, B_ZATT, B_MERGE, B_ZSSM, B_Q, B_KSEL, B_KWIN = 0, 3072, 4096, 6144, 8192, 9216, 9728
NB = 10240
SMALL_GATE = 32
LOG2E = 1.4426950408889634

TM_PROJ = 1024
TN_PROJ = 2560
TQ = 512
TK = 512
QH = 256
KT = 256
POS_BASE = 256
TQ_CMP = 2048
VMEM_LIMIT = 48 * 1024 * 1024

L_POS = 64
L_TILE = 67
L_MASK = 72
SLOPE_STRIDE = 8


def _cp(sem):
    return pltpu.CompilerParams(dimension_semantics=sem, vmem_limit_bytes=VMEM_LIMIT)


def _sigmoid(v):
    return 0.5 * jnp.tanh(0.5 * v) + 0.5


def _silu(v):
    return v * _sigmoid(v)


def _ada_kernel(c_ref, w_ref, b_ref, o_ref):
    o_ref[...] = jnp.dot(c_ref[...], w_ref[...], preferred_element_type=F32,
                         precision=HIGHEST) + b_ref[...]


def _ada(c, w_ada, b_ada):
    bsz = c.shape[0]
    return pl.pallas_call(
        _ada_kernel,
        grid=(3,),
        in_specs=[pl.BlockSpec((bsz, D_MODEL), lambda j: (0, 0)),
                  pl.BlockSpec((D_MODEL, D_MODEL), lambda j: (0, j)),
                  pl.BlockSpec((1, D_MODEL), lambda j: (0, j))],
        out_specs=pl.BlockSpec((bsz, D_MODEL), lambda j: (0, j)),
        out_shape=jax.ShapeDtypeStruct((bsz, 3 * D_MODEL), F32),
        compiler_params=_cp(("arbitrary",)),
        name="ada",
    )(c, w_ada, b_ada.reshape(1, 3 * D_MODEL))


def _proj_kernel(x_ref, shift_ref, scale_ref, g_ref, w_ref, ws_ref, wkv_ref, o_ref, osm_ref, okv_ref, h_scr):
    @pl.when(pl.program_id(2) == 0)
    def _():
        xf = x_ref[0]
        y = xf * lax.rsqrt(jnp.mean(xf * xf, axis=-1, keepdims=True) + EPS) * g_ref[...]
        h_scr[...] = (y * (1.0 + scale_ref[0]) + shift_ref[0]).astype(BF16)
        osm_ref[0] = lax.dot_general(h_scr[...], ws_ref[...], NT, preferred_element_type=F32)
        okv_ref[0] = lax.dot_general(h_scr[...], wkv_ref[...], NT,
                                     preferred_element_type=F32).astype(okv_ref.dtype)

    o_ref[0] = lax.dot_general(h_scr[...], w_ref[...], NT, preferred_element_type=F32).astype(o_ref.dtype)


def _proj(x, shift, scale, g_pre, w, w_small, w_kvc):
    bsz, s, _ = x.shape
    n = w.shape[0]
    nkv = w_kvc.shape[0]
    return pl.pallas_call(
        _proj_kernel,
        grid=(bsz, s // TM_PROJ, n // TN_PROJ),
        in_specs=[pl.BlockSpec((1, TM_PROJ, D_MODEL), lambda b, i, j: (b, i, 0)),
                  pl.BlockSpec((1, 1, D_MODEL), lambda b, i, j: (b, 0, 0)),
                  pl.BlockSpec((1, 1, D_MODEL), lambda b, i, j: (b, 0, 0)),
                  pl.BlockSpec((1, D_MODEL), lambda b, i, j: (0, 0)),
                  pl.BlockSpec((TN_PROJ, D_MODEL), lambda b, i, j: (j, 0)),
                  pl.BlockSpec((128, D_MODEL), lambda b, i, j: (0, 0)),
                  pl.BlockSpec((nkv, D_MODEL), lambda b, i, j: (0, 0))],
        out_specs=[pl.BlockSpec((1, TM_PROJ, TN_PROJ), lambda b, i, j: (b, i, j)),
                   pl.BlockSpec((1, TM_PROJ, 128), lambda b, i, j: (b, i, 0)),
                   pl.BlockSpec((1, TM_PROJ, nkv), lambda b, i, j: (b, i, 0))],
        out_shape=[jax.ShapeDtypeStruct((bsz, s, n), BF16),
                   jax.ShapeDtypeStruct((bsz, s, 128), F32),
                   jax.ShapeDtypeStruct((bsz, s, nkv), F32)],
        scratch_shapes=[pltpu.VMEM((TM_PROJ, D_MODEL), BF16)],
        compiler_params=_cp(("arbitrary", "arbitrary", "arbitrary")),
        name="proj",
    )(x, shift, scale, g_pre.reshape(1, D_MODEL), w, w_small, w_kvc)


def _ssm_kernel(xbc_ref, z_ref, small_ref, shift_ref, spread_ref, convw_ref, convb_ref, dtb_ref,
                alog_ref, dskip_ref, gn_ref, o_ref, tail_scr, state_scr, y_scr):
    cl = SSM_CHUNK

    @pl.when(pl.program_id(1) == 0)
    def _():
        tail_scr[...] = jnp.zeros(tail_scr.shape, BF16)
        state_scr[...] = jnp.zeros(state_scr.shape, F32)

    for sc in range(xbc_ref.shape[1] // cl):
        rows = pl.ds(sc * cl, cl)
        _ssm_chunk(xbc_ref.at[0, rows], z_ref.at[0, rows], small_ref.at[0, rows], shift_ref, spread_ref,
                   convw_ref, convb_ref, dtb_ref, alog_ref, dskip_ref, gn_ref, o_ref.at[0, rows],
                   tail_scr, state_scr, y_scr.at[rows])


def _ssm_chunk(xbc_ref, z_ref, small_ref, shift_ref, spread_ref, convw_ref, convb_ref, dtb_ref,
               alog_ref, dskip_ref, gn_ref, o_ref, tail_scr, state_scr, y_scr):
    cl = SSM_CHUNK
    cur = xbc_ref[...]
    xpad = jnp.concatenate([tail_scr[...], cur], axis=0).astype(F32)
    taps = jnp.concatenate([(xpad * convw_ref[k:k + 1, :]).astype(BF16) for k in range(SSM_CONV - 1)],
                           axis=0)
    tail_scr[...] = cur[cl - CONV_TAIL:cl, :]
    acc = (jnp.dot(shift_ref[...], taps, preferred_element_type=F32)
           + convw_ref[SSM_CONV - 1:SSM_CONV, :] * xpad[CONV_TAIL:] + convb_ref[...])
    u = _silu(acc)
    xs = u[:, :D_INNER]
    bm = u[:, D_INNER:D_INNER + SSM_GROUPS * SSM_STATE]
    cm = u[:, D_INNER + SSM_GROUPS * SSM_STATE:]

    pre = small_ref[...] + dtb_ref[...]
    dt = jnp.maximum(pre, 0.0) + jnp.log1p(jnp.exp(-jnp.abs(pre)))
    a = -jnp.exp(alog_ref[...])
    adt = dt * a
    row = lax.broadcasted_iota(jnp.int32, (cl, cl), 0)
    col = lax.broadcasted_iota(jnp.int32, (cl, cl), 1)
    causal = row >= col
    tri = causal.astype(F32)
    a_cs = jnp.dot(tri, adt, preferred_element_type=F32, precision=HIGHEST)
    a_cs_t = a_cs.T
    dt_t = dt.T
    a_last = a_cs[cl - 1:cl, :]
    ea = jnp.exp(a_cs)
    dsc = jnp.exp(a_last - a_cs) * dt

    def spread(v):
        hi = v.astype(BF16)
        lo = (v - hi.astype(F32)).astype(BF16)
        return jnp.dot(jnp.concatenate([hi, lo], axis=1), spread_ref[...], preferred_element_type=F32)

    ea_x = spread(ea)
    dsc_x = spread(dsc)
    cdec_x = ea_x[cl - 1:cl, :]
    xsd = xs * dsc_x
    lane = lax.broadcasted_iota(jnp.int32, (cl, 2 * SSM_HEAD_DIM), 1)
    first_half = lane < SSM_HEAD_DIM

    hg = SSM_HEADS // SSM_GROUPS
    gw = hg * SSM_HEAD_DIM
    for g in range(SSM_GROUPS):
        bg = bm[:, g * SSM_STATE:(g + 1) * SSM_STATE].astype(BF16)
        cg = cm[:, g * SSM_STATE:(g + 1) * SSM_STATE].astype(BF16)
        gc = slice(g * gw, (g + 1) * gw)
        cb = lax.dot_general(cg, bg, NT, preferred_element_type=F32)
        st = state_scr[g]
        y_scr[:, gc] = jnp.dot(cg, st.astype(BF16), preferred_element_type=F32) * ea_x[:, gc]
        new = lax.dot_general(bg, xsd[:, gc].astype(BF16), (((0,), (0,)), ((), ())),
                              preferred_element_type=F32)
        state_scr[g] = st * cdec_x[:, gc] + new
        for pp in range(hg // 2):
            pair = g * (hg // 2) + pp
            pc = slice(pair * 128, (pair + 1) * 128)
            gm = []
            for e in range(2):
                h = 2 * pair + e
                seg = a_cs[:, h:h + 1] - a_cs_t[h:h + 1, :]
                lmat = jnp.exp(jnp.where(causal, seg, NEG))
                gm.append(cb * lmat * dt_t[h:h + 1, :])
            lhs = jnp.concatenate(gm, axis=0).astype(BF16)
            yy = jnp.dot(lhs, xs[:, pc].astype(BF16), preferred_element_type=F32)
            y_scr[:, pc] = y_scr[:, pc] + jnp.where(first_half, yy[:cl], yy[cl:])

    y = y_scr[...] + xs * dskip_ref[...]
    y = y * _silu(z_ref[...].astype(F32))
    gsz = D_INNER // SSM_GROUPS
    for g in range(SSM_GROUPS):
        yg = y[:, g * gsz:(g + 1) * gsz]
        yn = yg * lax.rsqrt(jnp.mean(yg * yg, axis=-1, keepdims=True) + EPS)
        o_ref[:, g * gsz:(g + 1) * gsz] = (yn * gn_ref[:, g * gsz:(g + 1) * gsz]).astype(o_ref.dtype)


def _conv_shift_matrix():
    cl = SSM_CHUNK
    ext = CONV_TAIL + cl
    m = np.zeros((cl, (SSM_CONV - 1) * ext), np.float32)
    t = np.arange(cl)
    for k in range(SSM_CONV - 1):
        m[t, k * ext + CONV_TAIL + t - (SSM_CONV - 1 - k)] = 1.0
    return jnp.asarray(m, dtype=BF16)


def _head_spread_matrix():
    m = np.zeros((2, 128, D_INNER), np.float32)
    for h in range(SSM_HEADS):
        m[:, h, h * SSM_HEAD_DIM:(h + 1) * SSM_HEAD_DIM] = 1.0
    return jnp.asarray(m.reshape(256, D_INNER), dtype=BF16)


def _ssm(pb, small, conv_w, conv_b, dt_bias, a_log, d_skip, g_norm):
    bsz, s, _ = pb.shape
    cl = SSM_CHUNK
    pad = 128 - SSM_HEADS
    dtb = jnp.pad(dt_bias, (0, pad)).reshape(1, 128)
    alog = jnp.pad(a_log, (0, pad)).reshape(1, 128)
    dskip = jnp.repeat(d_skip, SSM_HEAD_DIM).reshape(1, D_INNER)
    const = lambda shape: pl.BlockSpec(shape, lambda b, c: (0, 0))
    return pl.pallas_call(
        _ssm_kernel,
        grid=(bsz, s // SSM_STEP),
        in_specs=[pl.BlockSpec((1, SSM_STEP, D_CONV), lambda b, c: (b, c, B_XBC // D_CONV)),
                  pl.BlockSpec((1, SSM_STEP, D_INNER), lambda b, c: (b, c, B_ZSSM // D_INNER)),
                  pl.BlockSpec((1, SSM_STEP, 128), lambda b, c: (b, c, 0)),
                  const((cl, (SSM_CONV - 1) * (CONV_TAIL + cl))), const((256, D_INNER)),
                  const((SSM_CONV, D_CONV)), const((1, D_CONV)), const((1, 128)), const((1, 128)),
                  const((1, D_INNER)), const((1, D_INNER))],
        out_specs=pl.BlockSpec((1, SSM_STEP, D_INNER), lambda b, c: (b, c, 0)),
        out_shape=jax.ShapeDtypeStruct((bsz, s, D_INNER), BF16),
        scratch_shapes=[pltpu.VMEM((CONV_TAIL, D_CONV), BF16),
                        pltpu.VMEM((SSM_GROUPS, SSM_STATE, D_INNER // SSM_GROUPS), F32),
                        pltpu.VMEM((SSM_STEP, D_INNER), F32)],
        compiler_params=_cp(("arbitrary", "arbitrary")),
        name="ssm",
    )(pb, pb, small, _conv_shift_matrix(), _head_spread_matrix(), conv_w, conv_b.reshape(1, D_CONV),
      dtb, alog, dskip,
      g_norm.reshape(1, D_INNER))


def _cmp_kernel(r0_ref, r1_ref, r2_ref, r3_ref, pos_ref, w1_ref, w2_ref, o_ref):
    half = CMP_STRIDE * ATT_HEAD_DIM
    hd = ATT_HEAD_DIM
    nrow = r0_ref.shape[1] // CMP_STRIDE
    toks = [[r_ref[0, pl.ds(t, nrow, stride=CMP_STRIDE), :] for t in range(CMP_STRIDE)]
            for r_ref in (r0_ref, r1_ref, r2_ref, r3_ref)]
    for kv in range(2):
        w1 = w1_ref[kv]
        posb = jnp.broadcast_to(pos_ref[kv], (8, 2 * half)).astype(BF16)
        cpos = jnp.dot(posb, w1, preferred_element_type=F32)[0:1]
        for gg in range(ATT_KV_GROUPS):
            j = kv * ATT_KV_GROUPS + gg
            r = jnp.concatenate([toks[j // 2][t][:, (j % 2) * hd:(j % 2 + 1) * hd]
                                 for t in range(CMP_STRIDE)], axis=1).astype(BF16)
            first = jnp.dot(r, w1[:half], preferred_element_type=F32)
            second = jnp.dot(r, w1[half:], preferred_element_type=F32)
            hid = first + pltpu.roll(second, nrow - 1, 0) + cpos
            o_ref[0, j] = jnp.dot(_silu(hid).astype(BF16), w2_ref[kv], preferred_element_type=F32)


def _compress(kvc, pos, w1, w2):
    bsz, s, wid = kvc.shape
    nrow = s // CMP_STRIDE
    ng = 2 * ATT_KV_GROUPS
    return pl.pallas_call(
        _cmp_kernel,
        grid=(bsz,),
        in_specs=[pl.BlockSpec((1, s, 128), lambda b, c=c: (b, 0, c)) for c in range(wid // 128)]
                 + [pl.BlockSpec((2, 1, 2048), lambda b: (0, 0, 0)),
                  pl.BlockSpec((2, 2048, CMP_HIDDEN), lambda b: (0, 0, 0)),
                  pl.BlockSpec((2, CMP_HIDDEN, ATT_HEAD_DIM), lambda b: (0, 0, 0))],
        out_specs=pl.BlockSpec((1, ng, nrow, ATT_HEAD_DIM), lambda b: (b, 0, 0, 0)),
        out_shape=jax.ShapeDtypeStruct((bsz, ng, nrow, ATT_HEAD_DIM), F32),
        compiler_params=_cp(("arbitrary",)),
        name="cmp",
    )(kvc, kvc, kvc, kvc, pos, w1, w2)


def _bf16_round_np(x):
    u = np.asarray(x, np.float32).view(np.uint32)
    u = (u + (((u >> 16) & 1) + 0x7FFF)) & np.uint32(0xFFFF0000)
    return u.view(np.float32)


def _slope_table():
    slope = (2.0 ** (-8.0 * np.arange(1, ATT_HEADS + 1) / ATT_HEADS)).astype(np.float32)
    slope = (slope.astype(np.float64) * LOG2E).astype(np.float32)
    p0 = _bf16_round_np(slope)
    p1 = _bf16_round_np(slope - p0)
    p2 = _bf16_round_np(slope - p0 - p1)
    tab = np.zeros((ATT_HEADS, SLOPE_STRIDE), np.float32)
    tab[:, 0], tab[:, 1], tab[:, 2] = p0, p1, p2
    tab[:, 3:6] = tab[:, 0:3] * POS_BASE
    tab[:, 6] = slope
    return jnp.asarray(tab.reshape(-1))


def _key_aux_table(s):
    pos = np.arange(s)
    tab = np.zeros((s, 128), np.float32)
    tab[:, L_POS:L_POS + 3] = (pos % POS_BASE)[:, None]
    tab[:, L_TILE:L_TILE + 3] = (pos // POS_BASE)[:, None]
    tab[pos, L_MASK + pos // SLC_BLOCK] = 1.0
    return jnp.asarray(tab, dtype=BF16)


def _bias_table():
    assert WINDOW == 2 * KT and QH == KT
    r = np.arange(QH)[:, None]
    c = np.arange(KT)[None, :]
    oldest = np.where(2 * KT + r - c < WINDOW, 0.0, NEG)
    causal = np.where(r - c >= 0, 0.0, NEG)
    return jnp.asarray(np.stack([oldest, causal]), dtype=F32)


def _gate_expand_table():
    tab = np.zeros((ATT_KV_GROUPS, 2, 128, 3 * KV_WIDTH), np.float32)
    for g in range(ATT_KV_GROUPS):
        for h in range(ATT_HPG):
            for j in range(3):
                row = SMALL_GATE + g * 3 * ATT_HPG + 3 * h + j
                tab[g, :, row, j * KV_WIDTH + h * ATT_HEAD_DIM:j * KV_WIDTH + (h + 1) * ATT_HEAD_DIM] = 1.0
    return jnp.asarray(tab.reshape(ATT_KV_GROUPS, 256, 3 * KV_WIDTH), dtype=BF16)


def _cmp_to_slc_matrix_t(n_cmp_pad, n_cmp, n_slc):
    cs = np.arange(n_cmp_pad) * CMP_STRIDE
    ss = np.arange(n_slc) * SLC_BLOCK
    lo = np.maximum(cs[:, None], ss[None, :])
    hi = np.minimum(cs[:, None] + CMP_BLOCK, ss[None, :] + SLC_BLOCK)
    m = np.clip(hi - lo, 0, None) / CMP_BLOCK
    m[n_cmp:] = 0.0
    return jnp.asarray(m.T, dtype=F32)


def _cmpattn_kernel(slopes_ref, q_ref, ck_ref, cv_ref, mt_ref, ocmp_ref, qmask_ref,
                    kcat_scr, vbd_scr):
    g = pl.program_id(1)
    i = pl.program_id(2)
    tq = q_ref.shape[1]
    t0 = i * tq
    npad = ck_ref.shape[2]
    n_slc = mt_ref.shape[0]

    @pl.when(i == 0)
    def _():
        kc = ck_ref[0, 0]
        vc = cv_ref[0, 0]
        k_hi = kc.astype(BF16).astype(F32)
        k_lo = kc - k_hi
        zero = jnp.zeros_like(kc)
        for h in range(ATT_HPG):
            def bd(a):
                return jnp.concatenate([a if hh == h else zero for hh in range(ATT_HPG)], axis=1).astype(BF16)
            kcat_scr[h * npad:(h + 1) * npad, :KV_WIDTH] = bd(k_hi)
            kcat_scr[h * npad:(h + 1) * npad, KV_WIDTH:] = bd(k_lo)
            vbd_scr[h * npad:(h + 1) * npad, :] = bd(vc)

    q = q_ref[0]
    s = lax.dot_general(jnp.concatenate([q, q], axis=1), kcat_scr[...], NT,
                        preferred_element_type=F32)
    jcol = lax.broadcasted_iota(jnp.int32, (1, npad), 1)
    cmp_end = jcol * CMP_STRIDE + (CMP_BLOCK - 1)
    trow = t0 + lax.broadcasted_iota(jnp.int32, (tq, npad), 0)
    valid = (cmp_end <= trow) & (jcol < npad - 1)
    any_valid = trow[:, 0:1] >= CMP_BLOCK - 1
    rel = (cmp_end - t0).astype(F32)
    psum = jnp.zeros((tq, npad), F32)
    ps = []
    for h in range(ATT_HPG):
        slope = slopes_ref[(g * ATT_HPG + h) * SLOPE_STRIDE + 6]
        sh = jnp.where(valid, s[:, h * npad:(h + 1) * npad] + slope * rel, NEG)
        mx = jnp.max(sh, axis=-1, keepdims=True)
        p = jnp.exp2(sh - mx)
        l = jnp.sum(p, axis=-1, keepdims=True)
        p = p * jnp.where(any_valid, 1.0 / l, 0.0)
        ps.append(p)
        psum = psum + p
    ocmp_ref[0] = jnp.dot(jnp.concatenate(ps, axis=1).astype(BF16), vbd_scr[...],
                          preferred_element_type=F32)
    imp_t = lax.dot_general(mt_ref[...], psum, NT, preferred_element_type=F32,
                            precision=HIGHEST)

    nrb = n_slc // 8
    blk_t = (t0 + lax.broadcasted_iota(jnp.int32, (8, tq), 1)) >> SLC_SHIFT
    sub = lax.broadcasted_iota(jnp.int32, (8, tq), 0)
    score = []
    for rb in range(nrb):
        kk = sub + rb * 8
        imp = imp_t[rb * 8:(rb + 1) * 8]
        forced = (kk == 0) | (kk == blk_t) | (kk == blk_t - 1)
        score.append(jnp.where(forced, imp + FORCE_BONUS, jnp.where(kk <= blk_t, imp, -1.0)))
    rank = [jnp.zeros((8, tq), F32) for _ in range(nrb)]
    for j in range(n_slc):
        cj = jnp.broadcast_to(score[j // 8][j % 8:j % 8 + 1, :], (8, tq))
        for rb in range(nrb):
            ge = jnp.where(cj >= score[rb], 1.0, 0.0)
            gt = jnp.where(cj > score[rb], 1.0, 0.0)
            if rb * 8 > j:
                beats = ge
            elif rb * 8 + 7 <= j:
                beats = gt
            else:
                beats = jnp.where(sub > j - rb * 8, ge, gt)
            rank[rb] = rank[rb] + beats
    rows = [jnp.zeros((L_MASK, tq), F32)]
    rows += [jnp.where(rank[rb] < float(SLC_TOPK), 0.0, -BIG) for rb in range(nrb)]
    rows += [jnp.zeros((128 - L_MASK - n_slc, tq), F32)]
    qmask_ref[0, 0] = jnp.concatenate(rows, axis=0).T.astype(BF16)


def _cmpattn(pb, cmp_kv, s):
    bsz = pb.shape[0]
    g = ATT_KV_GROUPS
    nrow = cmp_kv.shape[2]
    n_cmp = (s - CMP_BLOCK) // CMP_STRIDE + 1
    n_slc = s // SLC_BLOCK
    mt = _cmp_to_slc_matrix_t(nrow, n_cmp, n_slc)
    return pl.pallas_call(
        _cmpattn_kernel,
        grid=(bsz, g, s // TQ_CMP),
        in_specs=[pl.BlockSpec(memory_space=pltpu.SMEM),
                  pl.BlockSpec((1, TQ_CMP, KV_WIDTH), lambda b, gg, i: (b, i, B_Q // KV_WIDTH + gg)),
                  pl.BlockSpec((1, 1, nrow, ATT_HEAD_DIM), lambda b, gg, i: (b, gg, 0, 0)),
                  pl.BlockSpec((1, 1, nrow, ATT_HEAD_DIM), lambda b, gg, i: (b, g + gg, 0, 0)),
                  pl.BlockSpec((n_slc, nrow), lambda b, gg, i: (0, 0))],
        out_specs=[pl.BlockSpec((1, TQ_CMP, KV_WIDTH), lambda b, gg, i: (b, i, gg)),
                   pl.BlockSpec((1, 1, TQ_CMP, 128), lambda b, gg, i: (b, gg, i, 0))],
        out_shape=[jax.ShapeDtypeStruct((bsz, s, ATT_WIDTH), F32),
                   jax.ShapeDtypeStruct((bsz, g, s, 128), BF16)],
        scratch_shapes=[pltpu.VMEM((ATT_HPG * nrow, 2 * KV_WIDTH), BF16),
                        pltpu.VMEM((ATT_HPG * nrow, KV_WIDTH), BF16)],
        compiler_params=_cp(("arbitrary", "arbitrary", "arbitrary")),
        name="cmpattn",
    )(_slope_table(), pb, cmp_kv, cmp_kv, mt)


def _flash_step(qa_ref, ka_ref, va_ref, m_scr, acc_scr, rows, ksub, nsub, bias, first=False):
    r0, nr = rows
    nk = nsub * KT
    start = pl.multiple_of(ksub * KT, KT)
    k_aug = ka_ref[pl.ds(start, nk), :]
    v_aug = va_ref[pl.ds(start, nk), :]
    s = lax.dot_general(qa_ref[r0:r0 + nr, :], k_aug, NT, preferred_element_type=F32)
    if bias is not None:
        parts = []
        for t, b in enumerate(bias):
            st = s[:, t * KT:(t + 1) * KT]
            if b is not None:
                st = (st.reshape(nr // QH, QH, KT) + b[None]).reshape(nr, KT)
            parts.append(st)
        s = jnp.concatenate(parts, axis=1) if nsub > 1 else parts[0]
    m_cur = jnp.max(s, axis=-1, keepdims=True)
    if first:
        m_new = jnp.broadcast_to(m_cur, (nr, 128))
    else:
        m_prev = m_scr[r0:r0 + nr, :]
        m_new = jnp.maximum(m_prev, m_cur)
    p = jnp.exp2(s - jnp.concatenate([m_new] * (nk // 128), axis=1))
    pv_t = lax.dot_general(v_aug, p.astype(BF16), (((0,), (1,)), ((), ())),
                           preferred_element_type=F32)
    if first:
        acc_scr[:, r0:r0 + nr] = pv_t
    else:
        acc_scr[:, r0:r0 + nr] = jnp.exp2(m_prev - m_new).T * acc_scr[:, r0:r0 + nr] + pv_t
    m_scr[r0:r0 + nr, :] = m_new


def _flash_out(acc_scr):
    r = acc_scr.shape[1]
    den = acc_scr[0:8, :]
    num = acc_scr[ATT_HEAD_DIM:2 * ATT_HEAD_DIM, :]
    o_t = (num.reshape(ATT_HEAD_DIM // 8, 8, r) / den[None]).reshape(ATT_HEAD_DIM, r)
    halves = []
    for half in range(TQ // QH):
        cols = [o_t[:, (half * ATT_HPG + h) * QH:(half * ATT_HPG + h + 1) * QH] for h in range(ATT_HPG)]
        halves.append(jnp.concatenate(cols, axis=0).T)
    return jnp.concatenate(halves, axis=0)


def _attn_kernel(slopes_ref, q_ref, ksel_ref, kwin_ref, aux_ref, qmask_ref, ocmp_ref, small_ref,
                 zatt_ref, pick_ref, bias_ref, o_ref, qas_scr, qaw_scr, m_scr, acc_scr,
                 kas_scr, vas_scr, kaw_scr, vaw_scr):
    g = pl.program_id(1)
    i = pl.program_id(2)

    @pl.when(i == 0)
    def _():
        s_len = ksel_ref.shape[1]
        left = jnp.where(lax.broadcasted_iota(jnp.int32, (s_len, 128), 1) < ATT_HEAD_DIM,
                         1.0, 0.0).astype(BF16)
        for kv_ref, ka_scr, va_scr in ((ksel_ref, kas_scr, vas_scr), (kwin_ref, kaw_scr, vaw_scr)):
            kv = kv_ref[0]
            ka_scr[...] = kv * left + aux_ref[...]
            va_scr[...] = kv * (1.0 - left) + left

    qf = q_ref[0].astype(F32)
    qm = qmask_ref[0, 0].astype(F32)
    lane_row = lax.broadcasted_iota(jnp.int32, (1, 128), 1)
    left = lax.broadcasted_iota(jnp.int32, (TQ, 128), 1) < ATT_HEAD_DIM
    for h in range(ATT_HPG):
        slab = qf[:, 128 * (h // 2):128 * (h // 2) + 128]
        if h % 2:
            slab = pltpu.roll(slab, ATT_HEAD_DIM, 1)
        ext = jnp.zeros((1, 128), F32)
        for c in range(6):
            ext = jnp.where(lane_row == L_POS + c, slopes_ref[(g * ATT_HPG + h) * SLOPE_STRIDE + c], ext)
        base = jnp.where(left, slab, ext)
        for half in range(TQ // QH):
            rs = (half * ATT_HPG + h) * QH
            tok = slice(half * QH, (half + 1) * QH)
            qaw_scr[rs:rs + QH, :] = base[tok].astype(BF16)
            qas_scr[rs:rs + QH, :] = (base[tok] + qm[tok]).astype(BF16)

    half_rows = ATT_HPG * QH
    rows_a, rows_b, rows_all = (0, half_rows), (half_rows, half_rows), (0, 2 * half_rows)
    k0 = i * (TQ // KT)
    oldest, causal = bias_ref[0], bias_ref[1]

    _flash_step(qas_scr, kas_scr, vas_scr, m_scr, acc_scr, rows_a, k0, 1, [causal], first=True)
    _flash_step(qas_scr, kas_scr, vas_scr, m_scr, acc_scr, rows_b, k0, 2, [None, causal], first=True)

    for n_prev in range(1, ksel_ref.shape[1] // TK):
        @pl.when(i == n_prev)
        def _():
            for j in range(n_prev):
                _flash_step(qas_scr, kas_scr, vas_scr, m_scr, acc_scr, rows_all, j * (TK // KT),
                            TK // KT, None)

    o_slc = _flash_out(acc_scr)

    @pl.when(i == 0)
    def _():
        _flash_step(qaw_scr, kaw_scr, vaw_scr, m_scr, acc_scr, rows_a, 0, 1, [causal], first=True)
        _flash_step(qaw_scr, kaw_scr, vaw_scr, m_scr, acc_scr, rows_b, 0, 2, [None, causal], first=True)

    @pl.when(i >= 1)
    def _():
        win = [oldest, None, causal]
        _flash_step(qaw_scr, kaw_scr, vaw_scr, m_scr, acc_scr, rows_a, k0 - 2, 3, win, first=True)
        _flash_step(qaw_scr, kaw_scr, vaw_scr, m_scr, acc_scr, rows_b, k0 - 1, 3, win, first=True)

    o_win = _flash_out(acc_scr)

    gate = _sigmoid(small_ref[0])
    g_hi = gate.astype(BF16)
    g_lo = (gate - g_hi.astype(F32)).astype(BF16)
    gts = jnp.dot(jnp.concatenate([g_hi, g_lo], axis=1), pick_ref[0],
                  preferred_element_type=F32)
    o = (gts[:, :KV_WIDTH] * ocmp_ref[0] + gts[:, KV_WIDTH:2 * KV_WIDTH] * o_slc
         + gts[:, 2 * KV_WIDTH:] * o_win)
    o_ref[0] = (o * _silu(zatt_ref[0].astype(F32))).astype(o_ref.dtype)


def _attn(pb, small, qmask, ocmp):
    bsz, s, _ = pb.shape
    g = ATT_KV_GROUPS
    r = ATT_HPG * TQ
    return pl.pallas_call(
        _attn_kernel,
        grid=(bsz, g, s // TQ),
        in_specs=[pl.BlockSpec(memory_space=pltpu.SMEM),
                  pl.BlockSpec((1, TQ, KV_WIDTH), lambda b, gg, i: (b, i, B_Q // KV_WIDTH + gg)),
                  pl.BlockSpec((1, s, 128), lambda b, gg, i: (b, 0, B_KSEL // 128 + gg)),
                  pl.BlockSpec((1, s, 128), lambda b, gg, i: (b, 0, B_KWIN // 128 + gg)),
                  pl.BlockSpec((s, 128), lambda b, gg, i: (0, 0)),
                  pl.BlockSpec((1, 1, TQ, 128), lambda b, gg, i: (b, gg, i, 0)),
                  pl.BlockSpec((1, TQ, KV_WIDTH), lambda b, gg, i: (b, i, gg)),
                  pl.BlockSpec((1, TQ, 128), lambda b, gg, i: (b, i, 0)),
                  pl.BlockSpec((1, TQ, KV_WIDTH), lambda b, gg, i: (b, i, B_ZATT // KV_WIDTH + gg)),
                  pl.BlockSpec((1, 256, 3 * KV_WIDTH), lambda b, gg, i: (gg, 0, 0)),
                  pl.BlockSpec((2, QH, KT), lambda b, gg, i: (0, 0, 0))],
        out_specs=pl.BlockSpec((1, TQ, KV_WIDTH), lambda b, gg, i: (b, i, gg)),
        out_shape=jax.ShapeDtypeStruct((bsz, s, ATT_WIDTH), BF16),
        scratch_shapes=[pltpu.VMEM((r, 128), BF16), pltpu.VMEM((r, 128), BF16),
                        pltpu.VMEM((r, 128), F32), pltpu.VMEM((128, r), F32)]
                       + [pltpu.VMEM((s, 128), BF16)] * 4,
        compiler_params=_cp(("arbitrary", "arbitrary", "arbitrary")),
        name="attn",
    )(_slope_table(), pb, pb, pb, _key_aux_table(s), qmask, ocmp, small, pb, _gate_expand_table(),
      _bias_table())


def _merge_kernel(yssm_ref, onsa_ref, mg_ref, x_ref, gate_ref, gpost_ref, wssm_ref, wnsa_ref, wout_ref, o_ref):
    y_ssm = jnp.dot(yssm_ref[0], wssm_ref[...], preferred_element_type=F32)
    y_nsa = jnp.dot(onsa_ref[0], wnsa_ref[...], preferred_element_type=F32)
    mg = _sigmoid(mg_ref[0].astype(F32))
    merged = mg[:, :D_MODEL] * y_ssm + mg[:, D_MODEL:] * y_nsa
    out = jnp.dot(merged.astype(BF16), wout_ref[...], preferred_element_type=F32)
    yn = out * lax.rsqrt(jnp.mean(out * out, axis=-1, keepdims=True) + EPS) * gpost_ref[...]
    o_ref[0] = x_ref[0] + gate_ref[0] * yn


def _merge(yssm, onsa, pb, x, gate, g_post, w_ssm_out, w_nsa_out, w_out):
    bsz, s, _ = x.shape
    tm = 512
    const = lambda shape: pl.BlockSpec(shape, lambda b, i: (0, 0))
    return pl.pallas_call(
        _merge_kernel,
        grid=(bsz, s // tm),
        in_specs=[pl.BlockSpec((1, tm, D_INNER), lambda b, i: (b, i, 0)),
                  pl.BlockSpec((1, tm, ATT_WIDTH), lambda b, i: (b, i, 0)),
                  pl.BlockSpec((1, tm, 2 * D_MODEL), lambda b, i: (b, i, B_MERGE // (2 * D_MODEL))),
                  pl.BlockSpec((1, tm, D_MODEL), lambda b, i: (b, i, 0)),
                  pl.BlockSpec((1, 1, D_MODEL), lambda b, i: (b, 0, 0)),
                  const((1, D_MODEL)),
                  const((D_INNER, D_MODEL)), const((ATT_WIDTH, D_MODEL)), const((D_MODEL, D_MODEL))],
        out_specs=pl.BlockSpec((1, tm, D_MODEL), lambda b, i: (b, i, 0)),
        out_shape=jax.ShapeDtypeStruct((bsz, s, D_MODEL), F32),
        compiler_params=_cp(("arbitrary", "arbitrary")),
        name="merge",
    )(yssm, onsa, pb, x, gate, g_post.reshape(1, D_MODEL), w_ssm_out, w_nsa_out, w_out)


def _wprep_kernel(w_ref, wb_ref, ws_ref, wkv_ref):
    def put(ref, dst, off, n, scale=None):
        v = w_ref[off:off + n, :]
        ref[dst:dst + n, :] = (v if scale is None else v * scale).astype(BF16)

    put(wb_ref, B_XBC, IN_XBC, D_CONV)
    put(wb_ref, B_ZATT, IN_ZATT, ATT_WIDTH)
    put(wb_ref, B_MERGE, IN_MERGE, 2 * D_MODEL)
    put(wb_ref, B_ZSSM, IN_Z, D_INNER)
    put(wb_ref, B_Q, IN_Q, ATT_WIDTH, ATT_HEAD_DIM ** -0.5 * LOG2E)
    hd = ATT_HEAD_DIM
    for dst, k_off in ((B_KSEL, IN_KV + 2 * KV_WIDTH), (B_KWIN, IN_KV + 4 * KV_WIDTH)):
        for gg in range(ATT_KV_GROUPS):
            put(wb_ref, dst + 2 * gg * hd, k_off + gg * hd, hd)
            put(wb_ref, dst + (2 * gg + 1) * hd, k_off + KV_WIDTH + gg * hd, hd)
    put(wkv_ref, 0, IN_KV, 2 * KV_WIDTH)
    put(ws_ref, 0, IN_DT, SSM_HEADS)
    put(ws_ref, SMALL_GATE, IN_GATE, 3 * ATT_HEADS)
    used = SMALL_GATE + 3 * ATT_HEADS
    ws_ref[used:, :] = jnp.zeros((128 - used, ws_ref.shape[1]), BF16)


def _split_w_in(w_in):
    dm, n_in = w_in.shape
    tc = 128
    return pl.pallas_call(
        _wprep_kernel,
        grid=(dm // tc,),
        in_specs=[pl.BlockSpec((n_in, tc), lambda i: (0, i))],
        out_specs=[pl.BlockSpec((NB, tc), lambda i: (0, i)), pl.BlockSpec((128, tc), lambda i: (0, i)),
                   pl.BlockSpec((2 * KV_WIDTH, tc), lambda i: (0, i))],
        out_shape=[jax.ShapeDtypeStruct((NB, dm), BF16), jax.ShapeDtypeStruct((128, dm), BF16),
                   jax.ShapeDtypeStruct((2 * KV_WIDTH, dm), BF16)],
        compiler_params=_cp(("arbitrary",)),
        name="wprep",
    )(w_in.T)


def _layer(x, c, w_ada, b_ada, g_pre, g_post, w_in, conv_w, conv_b, dt_bias, a_log, d_skip,
           g_ssm_norm, w_ssm_out, cmp_pos_k, cmp_w1_k, cmp_w2_k, cmp_pos_v, cmp_w1_v, cmp_w2_v,
           w_nsa_out, w_out):
    bsz, s, dm = x.shape
    mod = _ada(c, w_ada, b_ada)
    shift = mod[:, None, :dm]
    scale = mod[:, None, dm:2 * dm]
    gate = mod[:, None, 2 * dm:]
    wb, w_small, w_kvc = _split_w_in(w_in)
    pb, small, kvc = _proj(x, shift, scale, g_pre, wb, w_small, w_kvc)

    yssm = _ssm(pb, small, conv_w, conv_b, dt_bias, a_log, d_skip, g_ssm_norm)

    pos = jnp.stack([cmp_pos_k.reshape(1, -1), cmp_pos_v.reshape(1, -1)])
    w1 = jnp.stack([cmp_w1_k, cmp_w1_v]).astype(BF16)
    w2 = jnp.stack([cmp_w2_k, cmp_w2_v]).astype(BF16)
    cmp_kv = _compress(kvc, pos, w1, w2)

    ocmp, qmask = _cmpattn(pb, cmp_kv, s)
    onsa = _attn(pb, small, qmask, ocmp)
    return _merge(yssm, onsa, pb, x, gate, g_post, w_ssm_out.astype(BF16), w_nsa_out.astype(BF16),
                  w_out.astype(BF16))


@jax.jit
def kernel(x, c, w_ada, b_ada, g_pre, g_post, w_in, conv_w, conv_b, dt_bias, a_log, d_skip, g_ssm_norm,
           w_ssm_out, cmp_pos_k, cmp_w1_k, cmp_w2_k, cmp_pos_v, cmp_w1_v, cmp_w2_v, w_nsa_out, w_out):
    for layer in range(w_in.shape[0]):
        x = _layer(x, c, w_ada[layer], b_ada[layer], g_pre[layer], g_post[layer], w_in[layer],
                   conv_w[layer], conv_b[layer], dt_bias[layer], a_log[layer], d_skip[layer],
                   g_ssm_norm[layer], w_ssm_out[layer], cmp_pos_k[layer], cmp_w1_k[layer],
                   cmp_w2_k[layer], cmp_pos_v[layer], cmp_w1_v[layer], cmp_w2_v[layer],
                   w_nsa_out[layer], w_out[layer])
    return x
```

```python
import numpy as np
import jax
import jax.numpy as jnp
from jax import lax
from jax.experimental import pallas as pl
from jax.experimental.pallas import tpu as pltpu

D_MODEL = 1024
D_INNER = 2048
SSM_HEAD_DIM = 64
SSM_HEADS = 32
SSM_GROUPS = 4
SSM_STATE = 128
SSM_CONV = 4
SSM_CHUNK = 128
D_CONV = D_INNER + 2 * SSM_GROUPS * SSM_STATE
SSM_STEP = 512
CONV_TAIL = 16

ATT_HEADS = 16
ATT_HEAD_DIM = 64
ATT_KV_GROUPS = 4
ATT_HPG = 4
ATT_WIDTH = 1024
KV_WIDTH = 256
CMP_BLOCK = 32
CMP_STRIDE = 16
CMP_HIDDEN = 256
SLC_BLOCK = 64
SLC_SHIFT = 6
SLC_TOPK = 16
WINDOW = 512
FORCE_BONUS = 1000.0
EPS = 1e-6
NEG = -1e30
BIG = 2.0 ** 100

F32 = jnp.float32
BF16 = jnp.bfloat16
HIGHEST = lax.Precision.HIGHEST
NT = (((1,), (1,)), ((), ()))

_SIZES = (D_INNER, D_CONV, SSM_HEADS, ATT_WIDTH, 6 * KV_WIDTH, 3 * ATT_HEADS, ATT_WIDTH, 2 * D_MODEL)
_OFFS = tuple(int(v) for v in np.cumsum((0,) + _SIZES))
IN_Z, IN_XBC, IN_DT, IN_Q, IN_KV, IN_GATE, IN_ZATT, IN_MERGE = _OFFS[:8]

B_XBC, B_ZATT, B_MERGE, B_ZSSM, B_Q, B_KSEL, B_KWIN = 0, 3072, 4096, 6144, 8192, 9216, 9728
NB = 10240
SMALL_GATE = 32
LOG2E = 1.4426950408889634

TM_PROJ = 1024
TN_PROJ = 2560
TQ = 512
TK = 512
QH = 256
KT = 256
POS_BASE = 256
TQ_CMP = 2048
VMEM_LIMIT = 48 * 1024 * 1024

L_POS = 64
L_TILE = 67
L_MASK = 72
SLOPE_STRIDE = 8


def _cp(sem):
    return pltpu.CompilerParams(dimension_semantics=sem, vmem_limit_bytes=VMEM_LIMIT)


def _sigmoid(v):
    return 0.5 * jnp.tanh(0.5 * v) + 0.5


def _silu(v):
    h = 0.5 * v
    return h * jnp.tanh(h) + h


def _ada_kernel(c_ref, w_ref, b_ref, o_ref):
    o_ref[...] = jnp.dot(c_ref[...], w_ref[...], preferred_element_type=F32,
                         precision=HIGHEST) + b_ref[...]


def _ada(c, w_ada, b_ada):
    bsz = c.shape[0]
    return pl.pallas_call(
        _ada_kernel,
        grid=(3,),
        in_specs=[pl.BlockSpec((bsz, D_MODEL), lambda j: (0, 0)),
                  pl.BlockSpec((D_MODEL, D_MODEL), lambda j: (0, j)),
                  pl.BlockSpec((1, D_MODEL), lambda j: (0, j))],
        out_specs=pl.BlockSpec((bsz, D_MODEL), lambda j: (0, j)),
        out_shape=jax.ShapeDtypeStruct((bsz, 3 * D_MODEL), F32),
        compiler_params=_cp(("arbitrary",)),
        name="ada",
    )(c, w_ada, b_ada.reshape(1, 3 * D_MODEL))


def _proj_kernel(x_ref, shift_ref, scale_ref, g_ref, w_ref, ws_ref, wkv_ref, o_ref, osm_ref, okv_ref, h_scr):
    @pl.when(pl.program_id(2) == 0)
    def _():
        xf = x_ref[0]
        y = xf * lax.rsqrt(jnp.mean(xf * xf, axis=-1, keepdims=True) + EPS) * g_ref[...]
        h_scr[...] = (y * (1.0 + scale_ref[0]) + shift_ref[0]).astype(BF16)
        osm_ref[0] = lax.dot_general(h_scr[...], ws_ref[...], NT, preferred_element_type=F32)
        okv_ref[0] = lax.dot_general(h_scr[...], wkv_ref[...], NT,
                                     preferred_element_type=F32).astype(okv_ref.dtype)

    o_ref[0] = lax.dot_general(h_scr[...], w_ref[...], NT, preferred_element_type=F32).astype(o_ref.dtype)


def _proj(x, shift, scale, g_pre, w, w_small, w_kvc):
    bsz, s, _ = x.shape
    n = w.shape[0]
    nkv = w_kvc.shape[0]
    return pl.pallas_call(
        _proj_kernel,
        grid=(bsz, s // TM_PROJ, n // TN_PROJ),
        in_specs=[pl.BlockSpec((1, TM_PROJ, D_MODEL), lambda b, i, j: (b, i, 0)),
                  pl.BlockSpec((1, 1, D_MODEL), lambda b, i, j: (b, 0, 0)),
                  pl.BlockSpec((1, 1, D_MODEL), lambda b, i, j: (b, 0, 0)),
                  pl.BlockSpec((1, D_MODEL), lambda b, i, j: (0, 0)),
                  pl.BlockSpec((TN_PROJ, D_MODEL), lambda b, i, j: (j, 0)),
                  pl.BlockSpec((128, D_MODEL), lambda b, i, j: (0, 0)),
                  pl.BlockSpec((nkv, D_MODEL), lambda b, i, j: (0, 0))],
        out_specs=[pl.BlockSpec((1, TM_PROJ, TN_PROJ), lambda b, i, j: (b, i, j)),
                   pl.BlockSpec((1, TM_PROJ, 128), lambda b, i, j: (b, i, 0)),
                   pl.BlockSpec((1, TM_PROJ, nkv), lambda b, i, j: (b, i, 0))],
        out_shape=[jax.ShapeDtypeStruct((bsz, s, n), BF16),
                   jax.ShapeDtypeStruct((bsz, s, 128), F32),
                   jax.ShapeDtypeStruct((bsz, s, nkv), F32)],
        scratch_shapes=[pltpu.VMEM((TM_PROJ, D_MODEL), BF16)],
        compiler_params=_cp(("arbitrary", "arbitrary", "arbitrary")),
        name="proj",
    )(x, shift, scale, g_pre.reshape(1, D_MODEL), w, w_small, w_kvc)


def _ssm_kernel(xbc_ref, z_ref, small_ref, shift_ref, spread_ref, convw_ref, convb_ref, dtb_ref,
                alog_ref, dskip_ref, gn_ref, o_ref, tail_scr, state_scr, y_scr):
    cl = SSM_CHUNK

    @pl.when(pl.program_id(1) == 0)
    def _():
        tail_scr[...] = jnp.zeros(tail_scr.shape, BF16)
        state_scr[...] = jnp.zeros(state_scr.shape, F32)

    for sc in range(xbc_ref.shape[1] // cl):
        rows = pl.ds(sc * cl, cl)
        _ssm_chunk(xbc_ref.at[0, rows], z_ref.at[0, rows], small_ref.at[0, rows], shift_ref, spread_ref,
                   convw_ref, convb_ref, dtb_ref, alog_ref, dskip_ref, gn_ref, o_ref.at[0, rows],
                   tail_scr, state_scr, y_scr.at[rows])


def _ssm_chunk(xbc_ref, z_ref, small_ref, shift_ref, spread_ref, convw_ref, convb_ref, dtb_ref,
               alog_ref, dskip_ref, gn_ref, o_ref, tail_scr, state_scr, y_scr):
    cl = SSM_CHUNK
    cur = xbc_ref[...]
    xpad = jnp.concatenate([tail_scr[...], cur], axis=0).astype(F32)
    taps = jnp.concatenate([(xpad * convw_ref[k:k + 1, :]).astype(BF16) for k in range(SSM_CONV - 1)],
                           axis=0)
    tail_scr[...] = cur[cl - CONV_TAIL:cl, :]
    acc = (jnp.dot(shift_ref[...], taps, preferred_element_type=F32)
           + convw_ref[SSM_CONV - 1:SSM_CONV, :] * xpad[CONV_TAIL:] + convb_ref[...])
    u = _silu(acc)
    xs = u[:, :D_INNER]
    bm = u[:, D_INNER:D_INNER + SSM_GROUPS * SSM_STATE]
    cm = u[:, D_INNER + SSM_GROUPS * SSM_STATE:]

    pre = small_ref[...] + dtb_ref[...]
    dt = jnp.maximum(pre, 0.0) + jnp.log1p(jnp.exp(-jnp.abs(pre)))
    a = -jnp.exp(alog_ref[...])
    adt = dt * a
    row = lax.broadcasted_iota(jnp.int32, (cl, cl), 0)
    col = lax.broadcasted_iota(jnp.int32, (cl, cl), 1)
    causal = row >= col
    tri = causal.astype(F32)
    a_cs = jnp.dot(tri, adt, preferred_element_type=F32, precision=HIGHEST) * LOG2E
    a_cs_t = a_cs.T
    dt_t = dt.T
    a_last = a_cs[cl - 1:cl, :]
    ea = jnp.exp2(a_cs)
    dsc = jnp.exp2(a_last - a_cs) * dt

    def spread(v):
        hi = v.astype(BF16)
        lo = (v - hi.astype(F32)).astype(BF16)
        return jnp.dot(jnp.concatenate([hi, lo], axis=1), spread_ref[...], preferred_element_type=F32)

    ea_x = spread(ea)
    dsc_x = spread(dsc)
    cdec_x = ea_x[cl - 1:cl, :]
    xsd = xs * dsc_x
    lane = lax.broadcasted_iota(jnp.int32, (cl, 2 * SSM_HEAD_DIM), 1)
    first_half = lane < SSM_HEAD_DIM

    hg = SSM_HEADS // SSM_GROUPS
    gw = hg * SSM_HEAD_DIM
    for g in range(SSM_GROUPS):
        bg = bm[:, g * SSM_STATE:(g + 1) * SSM_STATE].astype(BF16)
        cg = cm[:, g * SSM_STATE:(g + 1) * SSM_STATE].astype(BF16)
        gc = slice(g * gw, (g + 1) * gw)
        cb = lax.dot_general(cg, bg, NT, preferred_element_type=F32)
        st = state_scr[g]
        y_scr[:, gc] = jnp.dot(cg, st.astype(BF16), preferred_element_type=F32) * ea_x[:, gc]
        new = lax.dot_general(bg, xsd[:, gc].astype(BF16), (((0,), (0,)), ((), ())),
                              preferred_element_type=F32)
        state_scr[g] = st * cdec_x[:, gc] + new
        for pp in range(hg // 2):
            pair = g * (hg // 2) + pp
            pc = slice(pair * 128, (pair + 1) * 128)
            gm = []
            for e in range(2):
                h = 2 * pair + e
                seg = a_cs[:, h:h + 1] - a_cs_t[h:h + 1, :]
                lmat = jnp.exp2(jnp.where(causal, seg, NEG))
                gm.append(cb * lmat * dt_t[h:h + 1, :])
            lhs = jnp.concatenate(gm, axis=0).astype(BF16)
            yy = jnp.dot(lhs, xs[:, pc].astype(BF16), preferred_element_type=F32)
            y_scr[:, pc] = y_scr[:, pc] + jnp.where(first_half, yy[:cl], yy[cl:])

    y = y_scr[...] + xs * dskip_ref[...]
    y = y * _silu(z_ref[...].astype(F32))
    gsz = D_INNER // SSM_GROUPS
    for g in range(SSM_GROUPS):
        yg = y[:, g * gsz:(g + 1) * gsz]
        yn = yg * lax.rsqrt(jnp.mean(yg * yg, axis=-1, keepdims=True) + EPS)
        o_ref[:, g * gsz:(g + 1) * gsz] = (yn * gn_ref[:, g * gsz:(g + 1) * gsz]).astype(o_ref.dtype)


def _conv_shift_matrix():
    cl = SSM_CHUNK
    ext = CONV_TAIL + cl
    m = np.zeros((cl, (SSM_CONV - 1) * ext), np.float32)
    t = np.arange(cl)
    for k in range(SSM_CONV - 1):
        m[t, k * ext + CONV_TAIL + t - (SSM_CONV - 1 - k)] = 1.0
    return jnp.asarray(m, dtype=BF16)


def _head_spread_matrix():
    m = np.zeros((2, 128, D_INNER), np.float32)
    for h in range(SSM_HEADS):
        m[:, h, h * SSM_HEAD_DIM:(h + 1) * SSM_HEAD_DIM] = 1.0
    return jnp.asarray(m.reshape(256, D_INNER), dtype=BF16)


def _ssm(pb, small, conv_w, conv_b, dt_bias, a_log, d_skip, g_norm):
    bsz, s, _ = pb.shape
    cl = SSM_CHUNK
    pad = 128 - SSM_HEADS
    dtb = jnp.pad(dt_bias, (0, pad)).reshape(1, 128)
    alog = jnp.pad(a_log, (0, pad)).reshape(1, 128)
    dskip = jnp.repeat(d_skip, SSM_HEAD_DIM).reshape(1, D_INNER)
    const = lambda shape: pl.BlockSpec(shape, lambda b, c: (0, 0))
    return pl.pallas_call(
        _ssm_kernel,
        grid=(bsz, s // SSM_STEP),
        in_specs=[pl.BlockSpec((1, SSM_STEP, D_CONV), lambda b, c: (b, c, B_XBC // D_CONV)),
                  pl.BlockSpec((1, SSM_STEP, D_INNER), lambda b, c: (b, c, B_ZSSM // D_INNER)),
                  pl.BlockSpec((1, SSM_STEP, 128), lambda b, c: (b, c, 0)),
                  const((cl, (SSM_CONV - 1) * (CONV_TAIL + cl))), const((256, D_INNER)),
                  const((SSM_CONV, D_CONV)), const((1, D_CONV)), const((1, 128)), const((1, 128)),
                  const((1, D_INNER)), const((1, D_INNER))],
        out_specs=pl.BlockSpec((1, SSM_STEP, D_INNER), lambda b, c: (b, c, 0)),
        out_shape=jax.ShapeDtypeStruct((bsz, s, D_INNER), BF16),
        scratch_shapes=[pltpu.VMEM((CONV_TAIL, D_CONV), BF16),
                        pltpu.VMEM((SSM_GROUPS, SSM_STATE, D_INNER // SSM_GROUPS), F32),
                        pltpu.VMEM((SSM_STEP, D_INNER), F32)],
        compiler_params=_cp(("arbitrary", "arbitrary")),
        name="ssm",
    )(pb, pb, small, _conv_shift_matrix(), _head_spread_matrix(), conv_w, conv_b.reshape(1, D_CONV),
      dtb, alog, dskip,
      g_norm.reshape(1, D_INNER))


def _cmp_kernel(r0_ref, r1_ref, r2_ref, r3_ref, pos_ref, w1_ref, w2_ref, o_ref):
    half = CMP_STRIDE * ATT_HEAD_DIM
    hd = ATT_HEAD_DIM
    nrow = r0_ref.shape[1] // CMP_STRIDE
    toks = [[r_ref[0, pl.ds(t, nrow, stride=CMP_STRIDE), :] for t in range(CMP_STRIDE)]
            for r_ref in (r0_ref, r1_ref, r2_ref, r3_ref)]
    for kv in range(2):
        w1 = w1_ref[kv]
        posb = jnp.broadcast_to(pos_ref[kv], (8, 2 * half)).astype(BF16)
        cpos = jnp.dot(posb, w1, preferred_element_type=F32)[0:1]
        for gg in range(ATT_KV_GROUPS):
            j = kv * ATT_KV_GROUPS + gg
            r = jnp.concatenate([toks[j // 2][t][:, (j % 2) * hd:(j % 2 + 1) * hd]
                                 for t in range(CMP_STRIDE)], axis=1).astype(BF16)
            first = jnp.dot(r, w1[:half], preferred_element_type=F32)
            second = jnp.dot(r, w1[half:], preferred_element_type=F32)
            hid = first + pltpu.roll(second, nrow - 1, 0) + cpos
            o_ref[0, j] = jnp.dot(_silu(hid).astype(BF16), w2_ref[kv], preferred_element_type=F32)


def _compress(kvc, pos, w1, w2):
    bsz, s, wid = kvc.shape
    nrow = s // CMP_STRIDE
    ng = 2 * ATT_KV_GROUPS
    return pl.pallas_call(
        _cmp_kernel,
        grid=(bsz,),
        in_specs=[pl.BlockSpec((1, s, 128), lambda b, c=c: (b, 0, c)) for c in range(wid // 128)]
                 + [pl.BlockSpec((2, 1, 2048), lambda b: (0, 0, 0)),
                  pl.BlockSpec((2, 2048, CMP_HIDDEN), lambda b: (0, 0, 0)),
                  pl.BlockSpec((2, CMP_HIDDEN, ATT_HEAD_DIM), lambda b: (0, 0, 0))],
        out_specs=pl.BlockSpec((1, ng, nrow, ATT_HEAD_DIM), lambda b: (b, 0, 0, 0)),
        out_shape=jax.ShapeDtypeStruct((bsz, ng, nrow, ATT_HEAD_DIM), F32),
        compiler_params=_cp(("arbitrary",)),
        name="cmp",
    )(kvc, kvc, kvc, kvc, pos, w1, w2)


def _bf16_round_np(x):
    u = np.asarray(x, np.float32).view(np.uint32)
    u = (u + (((u >> 16) & 1) + 0x7FFF)) & np.uint32(0xFFFF0000)
    return u.view(np.float32)


def _slope_table():
    slope = (2.0 ** (-8.0 * np.arange(1, ATT_HEADS + 1) / ATT_HEADS)).astype(np.float32)
    slope = (slope.astype(np.float64) * LOG2E).astype(np.float32)
    p0 = _bf16_round_np(slope)
    p1 = _bf16_round_np(slope - p0)
    p2 = _bf16_round_np(slope - p0 - p1)
    tab = np.zeros((ATT_HEADS, SLOPE_STRIDE), np.float32)
    tab[:, 0], tab[:, 1], tab[:, 2] = p0, p1, p2
    tab[:, 3:6] = tab[:, 0:3] * POS_BASE
    tab[:, 6] = slope
    return jnp.asarray(tab.reshape(-1))


def _key_aux_table(s):
    pos = np.arange(s)
    tab = np.zeros((s, 128), np.float32)
    tab[:, L_POS:L_POS + 3] = (pos % POS_BASE)[:, None]
    tab[:, L_TILE:L_TILE + 3] = (pos // POS_BASE)[:, None]
    tab[pos, L_MASK + pos // SLC_BLOCK] = 1.0
    return jnp.asarray(tab, dtype=BF16)


def _bias_table():
    assert WINDOW == 2 * KT and QH == KT
    r = np.arange(QH)[:, None]
    c = np.arange(KT)[None, :]
    oldest = np.where(2 * KT + r - c < WINDOW, 0.0, NEG)
    causal = np.where(r - c >= 0, 0.0, NEG)
    return jnp.asarray(np.stack([oldest, causal]), dtype=F32)


def _gate_expand_table():
    tab = np.zeros((ATT_KV_GROUPS, 2, 128, 3 * KV_WIDTH), np.float32)
    for g in range(ATT_KV_GROUPS):
        for h in range(ATT_HPG):
            for j in range(3):
                row = SMALL_GATE + g * 3 * ATT_HPG + 3 * h + j
                tab[g, :, row, j * KV_WIDTH + h * ATT_HEAD_DIM:j * KV_WIDTH + (h + 1) * ATT_HEAD_DIM] = 1.0
    return jnp.asarray(tab.reshape(ATT_KV_GROUPS, 256, 3 * KV_WIDTH), dtype=BF16)


def _cmp_to_slc_matrix_t(n_cmp_pad, n_cmp, n_slc):
    cs = np.arange(n_cmp_pad) * CMP_STRIDE
    ss = np.arange(n_slc) * SLC_BLOCK
    lo = np.maximum(cs[:, None], ss[None, :])
    hi = np.minimum(cs[:, None] + CMP_BLOCK, ss[None, :] + SLC_BLOCK)
    m = np.clip(hi - lo, 0, None) / CMP_BLOCK
    m[n_cmp:] = 0.0
    return jnp.asarray(m.T, dtype=F32)


def _cmpattn_kernel(slopes_ref, q_ref, ck_ref, cv_ref, mt_ref, ocmp_ref, qmask_ref,
                    kcat_scr, vbd_scr):
    g = pl.program_id(1)
    i = pl.program_id(2)
    tq = q_ref.shape[1]
    t0 = i * tq
    npad = ck_ref.shape[2]
    n_slc = mt_ref.shape[0]

    @pl.when(i == 0)
    def _():
        kc = ck_ref[0, 0]
        vc = cv_ref[0, 0]
        k_hi = kc.astype(BF16).astype(F32)
        k_lo = kc - k_hi
        zero = jnp.zeros_like(kc)
        for h in range(ATT_HPG):
            def bd(a):
                return jnp.concatenate([a if hh == h else zero for hh in range(ATT_HPG)], axis=1).astype(BF16)
            kcat_scr[h * npad:(h + 1) * npad, :KV_WIDTH] = bd(k_hi)
            kcat_scr[h * npad:(h + 1) * npad, KV_WIDTH:] = bd(k_lo)
            vbd_scr[h * npad:(h + 1) * npad, :] = bd(vc)

    q = q_ref[0]
    s = lax.dot_general(jnp.concatenate([q, q], axis=1), kcat_scr[...], NT,
                        preferred_element_type=F32)
    jcol = lax.broadcasted_iota(jnp.int32, (1, npad), 1)
    cmp_end = jcol * CMP_STRIDE + (CMP_BLOCK - 1)
    trow = t0 + lax.broadcasted_iota(jnp.int32, (tq, npad), 0)
    valid = (cmp_end <= trow) & (jcol < npad - 1)
    any_valid = trow[:, 0:1] >= CMP_BLOCK - 1
    rel = (cmp_end - t0).astype(F32)
    psum = jnp.zeros((tq, npad), F32)
    ps = []
    for h in range(ATT_HPG):
        slope = slopes_ref[(g * ATT_HPG + h) * SLOPE_STRIDE + 6]
        sh = jnp.where(valid, s[:, h * npad:(h + 1) * npad] + slope * rel, NEG)
        mx = jnp.max(sh, axis=-1, keepdims=True)
        p = jnp.exp2(sh - mx)
        l = jnp.sum(p, axis=-1, keepdims=True)
        p = p * jnp.where(any_valid, 1.0 / l, 0.0)
        ps.append(p)
        psum = psum + p
    ocmp_ref[0] = jnp.dot(jnp.concatenate(ps, axis=1).astype(BF16), vbd_scr[...],
                          preferred_element_type=F32)
    imp_t = lax.dot_general(mt_ref[...], psum, NT, preferred_element_type=F32,
                            precision=HIGHEST)

    nrb = n_slc // 8
    blk_t = (t0 + lax.broadcasted_iota(jnp.int32, (8, tq), 1)) >> SLC_SHIFT
    sub = lax.broadcasted_iota(jnp.int32, (8, tq), 0)
    score = []
    for rb in range(nrb):
        kk = sub + rb * 8
        imp = imp_t[rb * 8:(rb + 1) * 8]
        forced = (kk == 0) | (kk == blk_t) | (kk == blk_t - 1)
        score.append(jnp.where(forced, imp + FORCE_BONUS, jnp.where(kk <= blk_t, imp, -1.0)))
    rank = [jnp.zeros((8, tq), F32) for _ in range(nrb)]
    for j in range(n_slc):
        cj = jnp.broadcast_to(score[j // 8][j % 8:j % 8 + 1, :], (8, tq))
        for rb in range(nrb):
            ge = jnp.where(cj >= score[rb], 1.0, 0.0)
            gt = jnp.where(cj > score[rb], 1.0, 0.0)
            if rb * 8 > j:
                beats = ge
            elif rb * 8 + 7 <= j:
                beats = gt
            else:
                beats = jnp.where(sub > j - rb * 8, ge, gt)
            rank[rb] = rank[rb] + beats
    rows = [jnp.zeros((L_MASK, tq), F32)]
    rows += [jnp.where(rank[rb] < float(SLC_TOPK), 0.0, -BIG) for rb in range(nrb)]
    rows += [jnp.zeros((128 - L_MASK - n_slc, tq), F32)]
    qmask_ref[0, 0] = jnp.concatenate(rows, axis=0).T.astype(BF16)


def _cmpattn(pb, cmp_kv, s):
    bsz = pb.shape[0]
    g = ATT_KV_GROUPS
    nrow = cmp_kv.shape[2]
    n_cmp = (s - CMP_BLOCK) // CMP_STRIDE + 1
    n_slc = s // SLC_BLOCK
    mt = _cmp_to_slc_matrix_t(nrow, n_cmp, n_slc)
    return pl.pallas_call(
        _cmpattn_kernel,
        grid=(bsz, g, s // TQ_CMP),
        in_specs=[pl.BlockSpec(memory_space=pltpu.SMEM),
                  pl.BlockSpec((1, TQ_CMP, KV_WIDTH), lambda b, gg, i: (b, i, B_Q // KV_WIDTH + gg)),
                  pl.BlockSpec((1, 1, nrow, ATT_HEAD_DIM), lambda b, gg, i: (b, gg, 0, 0)),
                  pl.BlockSpec((1, 1, nrow, ATT_HEAD_DIM), lambda b, gg, i: (b, g + gg, 0, 0)),
                  pl.BlockSpec((n_slc, nrow), lambda b, gg, i: (0, 0))],
        out_specs=[pl.BlockSpec((1, TQ_CMP, KV_WIDTH), lambda b, gg, i: (b, i, gg)),
                   pl.BlockSpec((1, 1, TQ_CMP, 128), lambda b, gg, i: (b, gg, i, 0))],
        out_shape=[jax.ShapeDtypeStruct((bsz, s, ATT_WIDTH), F32),
                   jax.ShapeDtypeStruct((bsz, g, s, 128), BF16)],
        scratch_shapes=[pltpu.VMEM((ATT_HPG * nrow, 2 * KV_WIDTH), BF16),
                        pltpu.VMEM((ATT_HPG * nrow, KV_WIDTH), BF16)],
        compiler_params=_cp(("arbitrary", "arbitrary", "arbitrary")),
        name="cmpattn",
    )(_slope_table(), pb, cmp_kv, cmp_kv, mt)


def _flash_step(qa_ref, ka_ref, va_ref, m_scr, acc_scr, rows, ksub, nsub, bias, first=False):
    r0, nr = rows
    nk = nsub * KT
    start = pl.multiple_of(ksub * KT, KT)
    k_aug = ka_ref[pl.ds(start, nk), :]
    v_aug = va_ref[pl.ds(start, nk), :]
    s = lax.dot_general(qa_ref[r0:r0 + nr, :], k_aug, NT, preferred_element_type=F32)
    if bias is not None:
        parts = []
        for t, b in enumerate(bias):
            st = s[:, t * KT:(t + 1) * KT]
            if b is not None:
                st = (st.reshape(nr // QH, QH, KT) + b[None]).reshape(nr, KT)
            parts.append(st)
        s = jnp.concatenate(parts, axis=1) if nsub > 1 else parts[0]
    m_cur = jnp.max(s, axis=-1, keepdims=True)
    if first:
        m_new = jnp.broadcast_to(m_cur, (nr, 128))
    else:
        m_prev = m_scr[r0:r0 + nr, :]
        m_new = jnp.maximum(m_prev, m_cur)
    p = jnp.exp2(s - jnp.concatenate([m_new] * (nk // 128), axis=1))
    pv_t = lax.dot_general(v_aug, p.astype(BF16), (((0,), (1,)), ((), ())),
                           preferred_element_type=F32)
    if first:
        acc_scr[:, r0:r0 + nr] = pv_t
    else:
        acc_scr[:, r0:r0 + nr] = jnp.exp2(m_prev - m_new).T * acc_scr[:, r0:r0 + nr] + pv_t
    m_scr[r0:r0 + nr, :] = m_new


def _flash_out(acc_scr):
    r = acc_scr.shape[1]
    den = acc_scr[0:8, :]
    num = acc_scr[ATT_HEAD_DIM:2 * ATT_HEAD_DIM, :]
    o_t = (num.reshape(ATT_HEAD_DIM // 8, 8, r) / den[None]).reshape(ATT_HEAD_DIM, r)
    halves = []
    for half in range(TQ // QH):
        cols = [o_t[:, (half * ATT_HPG + h) * QH:(half * ATT_HPG + h + 1) * QH] for h in range(ATT_HPG)]
        halves.append(jnp.concatenate(cols, axis=0).T)
    return jnp.concatenate(halves, axis=0)


def _attn_kernel(slopes_ref, q_ref, ksel_ref, kwin_ref, aux_ref, qmask_ref, ocmp_ref, small_ref,
                 zatt_ref, pick_ref, bias_ref, o_ref, qas_scr, qaw_scr, m_scr, acc_scr,
                 kas_scr, vas_scr, kaw_scr, vaw_scr):
    g = pl.program_id(1)
    i = pl.program_id(2)

    @pl.when(i == 0)
    def _():
        s_len = ksel_ref.shape[1]
        left = jnp.where(lax.broadcasted_iota(jnp.int32, (s_len, 128), 1) < ATT_HEAD_DIM,
                         1.0, 0.0).astype(BF16)
        for kv_ref, ka_scr, va_scr in ((ksel_ref, kas_scr, vas_scr), (kwin_ref, kaw_scr, vaw_scr)):
            kv = kv_ref[0]
            ka_scr[...] = kv * left + aux_ref[...]
            va_scr[...] = kv * (1.0 - left) + left

    qf = q_ref[0].astype(F32)
    qm = qmask_ref[0, 0].astype(F32)
    lane_row = lax.broadcasted_iota(jnp.int32, (1, 128), 1)
    left = lax.broadcasted_iota(jnp.int32, (TQ, 128), 1) < ATT_HEAD_DIM
    for h in range(ATT_HPG):
        slab = qf[:, 128 * (h // 2):128 * (h // 2) + 128]
        if h % 2:
            slab = pltpu.roll(slab, ATT_HEAD_DIM, 1)
        ext = jnp.zeros((1, 128), F32)
        for c in range(6):
            ext = jnp.where(lane_row == L_POS + c, slopes_ref[(g * ATT_HPG + h) * SLOPE_STRIDE + c], ext)
        base = jnp.where(left, slab, ext)
        for half in range(TQ // QH):
            rs = (half * ATT_HPG + h) * QH
            tok = slice(half * QH, (half + 1) * QH)
            qaw_scr[rs:rs + QH, :] = base[tok].astype(BF16)
            qas_scr[rs:rs + QH, :] = (base[tok] + qm[tok]).astype(BF16)

    half_rows = ATT_HPG * QH
    rows_a, rows_b, rows_all = (0, half_rows), (half_rows, half_rows), (0, 2 * half_rows)
    k0 = i * (TQ // KT)
    oldest, causal = bias_ref[0], bias_ref[1]

    _flash_step(qas_scr, kas_scr, vas_scr, m_scr, acc_scr, rows_a, k0, 1, [causal], first=True)
    _flash_step(qas_scr, kas_scr, vas_scr, m_scr, acc_scr, rows_b, k0, 2, [None, causal], first=True)

    for n_prev in range(1, ksel_ref.shape[1] // TK):
        @pl.when(i == n_prev)
        def _():
            for j in range(n_prev):
                _flash_step(qas_scr, kas_scr, vas_scr, m_scr, acc_scr, rows_all, j * (TK // KT),
                            TK // KT, None)

    o_slc = _flash_out(acc_scr)

    @pl.when(i == 0)
    def _():
        _flash_step(qaw_scr, kaw_scr, vaw_scr, m_scr, acc_scr, rows_a, 0, 1, [causal], first=True)
        _flash_step(qaw_scr, kaw_scr, vaw_scr, m_scr, acc_scr, rows_b, 0, 2, [None, causal], first=True)

    @pl.when(i >= 1)
    def _():
        win = [oldest, None, causal]
        _flash_step(qaw_scr, kaw_scr, vaw_scr, m_scr, acc_scr, rows_a, k0 - 2, 3, win, first=True)
        _flash_step(qaw_scr, kaw_scr, vaw_scr, m_scr, acc_scr, rows_b, k0 - 1, 3, win, first=True)

    o_win = _flash_out(acc_scr)

    gate = _sigmoid(small_ref[0])
    g_hi = gate.astype(BF16)
    g_lo = (gate - g_hi.astype(F32)).astype(BF16)
    gts = jnp.dot(jnp.concatenate([g_hi, g_lo], axis=1), pick_ref[0],
                  preferred_element_type=F32)
    o = (gts[:, :KV_WIDTH] * ocmp_ref[0] + gts[:, KV_WIDTH:2 * KV_WIDTH] * o_slc
         + gts[:, 2 * KV_WIDTH:] * o_win)
    o_ref[0] = (o * _silu(zatt_ref[0].astype(F32))).astype(o_ref.dtype)


def _attn(pb, small, qmask, ocmp):
    bsz, s, _ = pb.shape
    g = ATT_KV_GROUPS
    r = ATT_HPG * TQ
    return pl.pallas_call(
        _attn_kernel,
        grid=(bsz, g, s // TQ),
        in_specs=[pl.BlockSpec(memory_space=pltpu.SMEM),
                  pl.BlockSpec((1, TQ, KV_WIDTH), lambda b, gg, i: (b, i, B_Q // KV_WIDTH + gg)),
                  pl.BlockSpec((1, s, 128), lambda b, gg, i: (b, 0, B_KSEL // 128 + gg)),
                  pl.BlockSpec((1, s, 128), lambda b, gg, i: (b, 0, B_KWIN // 128 + gg)),
                  pl.BlockSpec((s, 128), lambda b, gg, i: (0, 0)),
                  pl.BlockSpec((1, 1, TQ, 128), lambda b, gg, i: (b, gg, i, 0)),
                  pl.BlockSpec((1, TQ, KV_WIDTH), lambda b, gg, i: (b, i, gg)),
                  pl.BlockSpec((1, TQ, 128), lambda b, gg, i: (b, i, 0)),
                  pl.BlockSpec((1, TQ, KV_WIDTH), lambda b, gg, i: (b, i, B_ZATT // KV_WIDTH + gg)),
                  pl.BlockSpec((1, 256, 3 * KV_WIDTH), lambda b, gg, i: (gg, 0, 0)),
                  pl.BlockSpec((2, QH, KT), lambda b, gg, i: (0, 0, 0))],
        out_specs=pl.BlockSpec((1, TQ, KV_WIDTH), lambda b, gg, i: (b, i, gg)),
        out_shape=jax.ShapeDtypeStruct((bsz, s, ATT_WIDTH), BF16),
        scratch_shapes=[pltpu.VMEM((r, 128), BF16), pltpu.VMEM((r, 128), BF16),
                        pltpu.VMEM((r, 128), F32), pltpu.VMEM((128, r), F32)]
                       + [pltpu.VMEM((s, 128), BF16)] * 4,
        compiler_params=_cp(("arbitrary", "arbitrary", "arbitrary")),
        name="attn",
    )(_slope_table(), pb, pb, pb, _key_aux_table(s), qmask, ocmp, small, pb, _gate_expand_table(),
      _bias_table())


def _merge_kernel(yssm_ref, onsa_ref, mg_ref, x_ref, gate_ref, gpost_ref, wssm_ref, wnsa_ref, wout_ref, o_ref):
    y_ssm = jnp.dot(yssm_ref[0], wssm_ref[...], preferred_element_type=F32)
    y_nsa = jnp.dot(onsa_ref[0], wnsa_ref[...], preferred_element_type=F32)
    mg = _sigmoid(mg_ref[0].astype(F32))
    merged = mg[:, :D_MODEL] * y_ssm + mg[:, D_MODEL:] * y_nsa
    out = jnp.dot(merged.astype(BF16), wout_ref[...], preferred_element_type=F32)
    yn = out * lax.rsqrt(jnp.mean(out * out, axis=-1, keepdims=True) + EPS) * gpost_ref[...]
    o_ref[0] = x_ref[0] + gate_ref[0] * yn


def _merge(yssm, onsa, pb, x, gate, g_post, w_ssm_out, w_nsa_out, w_out):
    bsz, s, _ = x.shape
    tm = 512
    const = lambda shape: pl.BlockSpec(shape, lambda b, i: (0, 0))
    return pl.pallas_call(
        _merge_kernel,
        grid=(bsz, s // tm),
        in_specs=[pl.BlockSpec((1, tm, D_INNER), lambda b, i: (b, i, 0)),
                  pl.BlockSpec((1, tm, ATT_WIDTH), lambda b, i: (b, i, 0)),
                  pl.BlockSpec((1, tm, 2 * D_MODEL), lambda b, i: (b, i, B_MERGE // (2 * D_MODEL))),
                  pl.BlockSpec((1, tm, D_MODEL), lambda b, i: (b, i, 0)),
                  pl.BlockSpec((1, 1, D_MODEL), lambda b, i: (b, 0, 0)),
                  const((1, D_MODEL)),
                  const((D_INNER, D_MODEL)), const((ATT_WIDTH, D_MODEL)), const((D_MODEL, D_MODEL))],
        out_specs=pl.BlockSpec((1, tm, D_MODEL), lambda b, i: (b, i, 0)),
        out_shape=jax.ShapeDtypeStruct((bsz, s, D_MODEL), F32),
        compiler_params=_cp(("arbitrary", "arbitrary")),
        name="merge",
    )(yssm, onsa, pb, x, gate, g_post.reshape(1, D_MODEL), w_ssm_out, w_nsa_out, w_out)


def _wprep_kernel(w_ref, wb_ref, ws_ref, wkv_ref):
    def put(ref, dst, off, n, scale=None):
        v = w_ref[off:off + n, :]
        ref[dst:dst + n, :] = (v if scale is None else v * scale).astype(BF16)

    put(wb_ref, B_XBC, IN_XBC, D_CONV)
    put(wb_ref, B_ZATT, IN_ZATT, ATT_WIDTH)
    put(wb_ref, B_MERGE, IN_MERGE, 2 * D_MODEL)
    put(wb_ref, B_ZSSM, IN_Z, D_INNER)
    put(wb_ref, B_Q, IN_Q, ATT_WIDTH, ATT_HEAD_DIM ** -0.5 * LOG2E)
    hd = ATT_HEAD_DIM
    for dst, k_off in ((B_KSEL, IN_KV + 2 * KV_WIDTH), (B_KWIN, IN_KV + 4 * KV_WIDTH)):
        for gg in range(ATT_KV_GROUPS):
            put(wb_ref, dst + 2 * gg * hd, k_off + gg * hd, hd)
            put(wb_ref, dst + (2 * gg + 1) * hd, k_off + KV_WIDTH + gg * hd, hd)
    put(wkv_ref, 0, IN_KV, 2 * KV_WIDTH)
    put(ws_ref, 0, IN_DT, SSM_HEADS)
    put(ws_ref, SMALL_GATE, IN_GATE, 3 * ATT_HEADS)
    used = SMALL_GATE + 3 * ATT_HEADS
    ws_ref[used:, :] = jnp.zeros((128 - used, ws_ref.shape[1]), BF16)


def _split_w_in(w_in):
    dm, n_in = w_in.shape
    tc = 128
    return pl.pallas_call(
        _wprep_kernel,
        grid=(dm // tc,),
        in_specs=[pl.BlockSpec((n_in, tc), lambda i: (0, i))],
        out_specs=[pl.BlockSpec((NB, tc), lambda i: (0, i)), pl.BlockSpec((128, tc), lambda i: (0, i)),
                   pl.BlockSpec((2 * KV_WIDTH, tc), lambda i: (0, i))],
        out_shape=[jax.ShapeDtypeStruct((NB, dm), BF16), jax.ShapeDtypeStruct((128, dm), BF16),
                   jax.ShapeDtypeStruct((2 * KV_WIDTH, dm), BF16)],
        compiler_params=_cp(("arbitrary",)),
        name="wprep",
    )(w_in.T)


def _layer(x, c, w_ada, b_ada, g_pre, g_post, w_in, conv_w, conv_b, dt_bias, a_log, d_skip,
           g_ssm_norm, w_ssm_out, cmp_pos_k, cmp_w1_k, cmp_w2_k, cmp_pos_v, cmp_w1_v, cmp_w2_v,
           w_nsa_out, w_out):
    bsz, s, dm = x.shape
    mod = _ada(c, w_ada, b_ada)
    shift = mod[:, None, :dm]
    scale = mod[:, None, dm:2 * dm]
    gate = mod[:, None, 2 * dm:]
    wb, w_small, w_kvc = _split_w_in(w_in)
    pb, small, kvc = _proj(x, shift, scale, g_pre, wb, w_small, w_kvc)

    yssm = _ssm(pb, small, conv_w, conv_b, dt_bias, a_log, d_skip, g_ssm_norm)

    pos = jnp.stack([cmp_pos_k.reshape(1, -1), cmp_pos_v.reshape(1, -1)])
    w1 = jnp.stack([cmp_w1_k, cmp_w1_v]).astype(BF16)
    w2 = jnp.stack([cmp_w2_k, cmp_w2_v]).astype(BF16)
    cmp_kv = _compress(kvc, pos, w1, w2)

    ocmp, qmask = _cmpattn(pb, cmp_kv, s)
    onsa = _attn(pb, small, qmask, ocmp)
    return _merge(yssm, onsa, pb, x, gate, g_post, w_ssm_out.astype(BF16), w_nsa_out.astype(BF16),
                  w_out.astype(BF16))


@jax.jit
def kernel(x, c, w_ada, b_ada, g_pre, g_post, w_in, conv_w, conv_b, dt_bias, a_log, d_skip, g_ssm_norm,
           w_ssm_out, cmp_pos_k, cmp_w1_k, cmp_w2_k, cmp_pos_v, cmp_w1_v, cmp_w2_v, w_nsa_out, w_out):
    for layer in range(w_in.shape[0]):
        x = _layer(x, c, w_ada[layer], b_ada[layer], g_pre[layer], g_post[layer], w_in[layer],
                   conv_w[layer], conv_b[layer], dt_bias[layer], a_log[layer], d_skip[layer],
                   g_ssm_norm[layer], w_ssm_out[layer], cmp_pos_k[layer], cmp_w1_k[layer],
                   cmp_w2_k[layer], cmp_pos_v[layer], cmp_w1_v[layer], cmp_w2_v[layer],
                   w_nsa_out[layer], w_out[layer])
    return x
```

```python
import numpy as np
import jax
import jax.numpy as jnp
from jax import lax
from jax.experimental import pallas as pl
from jax.experimental.pallas import tpu as pltpu

D_MODEL = 1024
D_INNER = 2048
SSM_HEAD_DIM = 64
SSM_HEADS = 32
SSM_GROUPS = 4
SSM_STATE = 128
SSM_CONV = 4
SSM_CHUNK = 128
D_CONV = D_INNER + 2 * SSM_GROUPS * SSM_STATE
SSM_STEP = 512
CONV_TAIL = 16

ATT_HEADS = 16
ATT_HEAD_DIM = 64
ATT_KV_GROUPS = 4
ATT_HPG = 4
ATT_WIDTH = 1024
KV_WIDTH = 256
CMP_BLOCK = 32
CMP_STRIDE = 16
CMP_HIDDEN = 256
SLC_BLOCK = 64
SLC_SHIFT = 6
SLC_TOPK = 16
WINDOW = 512
FORCE_BONUS = 1000.0
EPS = 1e-6
NEG = -1e30
BIG = 2.0 ** 100

F32 = jnp.float32
BF16 = jnp.bfloat16
HIGHEST = lax.Precision.HIGHEST
NT = (((1,), (1,)), ((), ()))

_SIZES = (D_INNER, D_CONV, SSM_HEADS, ATT_WIDTH, 6 * KV_WIDTH, 3 * ATT_HEADS, ATT_WIDTH, 2 * D_MODEL)
_OFFS = tuple(int(v) for v in np.cumsum((0,) + _SIZES))
IN_Z, IN_XBC, IN_DT, IN_Q, IN_KV, IN_GATE, IN_ZATT, IN_MERGE = _OFFS[:8]

B_XBC, B_ZATT, B_MERGE, B_ZSSM, B_Q, B_KSEL, B_KWIN = 0, 3072, 4096, 6144, 8192, 9216, 9728
NB = 10240
SMALL_GATE = 32
LOG2E = 1.4426950408889634

TM_PROJ = 1024
TN_PROJ = 2560
TQ = 512
TK = 512
QH = 256
KT = 256
POS_BASE = 256
TQ_CMP = 2048
VMEM_LIMIT = 48 * 1024 * 1024

L_POS = 64
L_TILE = 67
L_MASK = 72
SLOPE_STRIDE = 8


def _cp(sem):
    return pltpu.CompilerParams(dimension_semantics=sem, vmem_limit_bytes=VMEM_LIMIT)


def _sigmoid(v):
    return 0.5 * jnp.tanh(0.5 * v) + 0.5


def _silu(v):
    h = 0.5 * v
    return h * jnp.tanh(h) + h


def _ada_kernel(c_ref, w_ref, b_ref, o_ref):
    o_ref[...] = jnp.dot(c_ref[...], w_ref[...], preferred_element_type=F32,
                         precision=HIGHEST) + b_ref[...]


def _ada(c, w_ada, b_ada):
    bsz = c.shape[0]
    return pl.pallas_call(
        _ada_kernel,
        grid=(3,),
        in_specs=[pl.BlockSpec((bsz, D_MODEL), lambda j: (0, 0)),
                  pl.BlockSpec((D_MODEL, D_MODEL), lambda j: (0, j)),
                  pl.BlockSpec((1, D_MODEL), lambda j: (0, j))],
        out_specs=pl.BlockSpec((bsz, D_MODEL), lambda j: (0, j)),
        out_shape=jax.ShapeDtypeStruct((bsz, 3 * D_MODEL), F32),
        compiler_params=_cp(("arbitrary",)),
        name="ada",
    )(c, w_ada, b_ada.reshape(1, 3 * D_MODEL))


def _proj_kernel(x_ref, shift_ref, scale_ref, g_ref, w_ref, ws_ref, wkv_ref, o_ref, osm_ref, okv_ref, h_scr):
    @pl.when(pl.program_id(2) == 0)
    def _():
        xf = x_ref[0]
        y = xf * lax.rsqrt(jnp.mean(xf * xf, axis=-1, keepdims=True) + EPS) * g_ref[...]
        h_scr[...] = (y * (1.0 + scale_ref[0]) + shift_ref[0]).astype(BF16)
        osm_ref[0] = lax.dot_general(h_scr[...], ws_ref[...], NT, preferred_element_type=F32)
        okv_ref[0] = lax.dot_general(h_scr[...], wkv_ref[...], NT,
                                     preferred_element_type=F32).astype(okv_ref.dtype)

    o_ref[0] = lax.dot_general(h_scr[...], w_ref[...], NT, preferred_element_type=F32).astype(o_ref.dtype)


def _proj(x, shift, scale, g_pre, w, w_small, w_kvc):
    bsz, s, _ = x.shape
    n = w.shape[0]
    nkv = w_kvc.shape[0]
    return pl.pallas_call(
        _proj_kernel,
        grid=(bsz, s // TM_PROJ, n // TN_PROJ),
        in_specs=[pl.BlockSpec((1, TM_PROJ, D_MODEL), lambda b, i, j: (b, i, 0)),
                  pl.BlockSpec((1, 1, D_MODEL), lambda b, i, j: (b, 0, 0)),
                  pl.BlockSpec((1, 1, D_MODEL), lambda b, i, j: (b, 0, 0)),
                  pl.BlockSpec((1, D_MODEL), lambda b, i, j: (0, 0)),
                  pl.BlockSpec((TN_PROJ, D_MODEL), lambda b, i, j: (j, 0)),
                  pl.BlockSpec((128, D_MODEL), lambda b, i, j: (0, 0)),
                  pl.BlockSpec((nkv, D_MODEL), lambda b, i, j: (0, 0))],
        out_specs=[pl.BlockSpec((1, TM_PROJ, TN_PROJ), lambda b, i, j: (b, i, j)),
                   pl.BlockSpec((1, TM_PROJ, 128), lambda b, i, j: (b, i, 0)),
                   pl.BlockSpec((1, TM_PROJ, nkv), lambda b, i, j: (b, i, 0))],
        out_shape=[jax.ShapeDtypeStruct((bsz, s, n), BF16),
                   jax.ShapeDtypeStruct((bsz, s, 128), F32),
                   jax.ShapeDtypeStruct((bsz, s, nkv), F32)],
        scratch_shapes=[pltpu.VMEM((TM_PROJ, D_MODEL), BF16)],
        compiler_params=_cp(("arbitrary", "arbitrary", "arbitrary")),
        name="proj",
    )(x, shift, scale, g_pre.reshape(1, D_MODEL), w, w_small, w_kvc)


def _ssm_kernel(xbc_ref, z_ref, small_ref, shift_ref, spread_ref, convw_ref, convb_ref, dtb_ref,
                alog_ref, dskip_ref, gn_ref, o_ref, tail_scr, state_scr, y_scr):
    cl = SSM_CHUNK

    @pl.when(pl.program_id(1) == 0)
    def _():
        tail_scr[...] = jnp.zeros(tail_scr.shape, BF16)
        state_scr[...] = jnp.zeros(state_scr.shape, F32)

    for sc in range(xbc_ref.shape[1] // cl):
        rows = pl.ds(sc * cl, cl)
        _ssm_chunk(xbc_ref.at[0, rows], z_ref.at[0, rows], small_ref.at[0, rows], shift_ref, spread_ref,
                   convw_ref, convb_ref, dtb_ref, alog_ref, dskip_ref, gn_ref, o_ref.at[0, rows],
                   tail_scr, state_scr, y_scr.at[rows])


def _ssm_chunk(xbc_ref, z_ref, small_ref, shift_ref, spread_ref, convw_ref, convb_ref, dtb_ref,
               alog_ref, dskip_ref, gn_ref, o_ref, tail_scr, state_scr, y_scr):
    cl = SSM_CHUNK
    cur = xbc_ref[...]
    xpad = jnp.concatenate([tail_scr[...], cur], axis=0).astype(F32)
    taps = jnp.concatenate([(xpad * convw_ref[k:k + 1, :]).astype(BF16) for k in range(SSM_CONV - 1)],
                           axis=0)
    tail_scr[...] = cur[cl - CONV_TAIL:cl, :]
    acc = (jnp.dot(shift_ref[...], taps, preferred_element_type=F32)
           + convw_ref[SSM_CONV - 1:SSM_CONV, :] * xpad[CONV_TAIL:] + convb_ref[...])
    u = _silu(acc)
    xs = u[:, :D_INNER]
    bm = u[:, D_INNER:D_INNER + SSM_GROUPS * SSM_STATE]
    cm = u[:, D_INNER + SSM_GROUPS * SSM_STATE:]

    pre = small_ref[...] + dtb_ref[...]
    dt = jnp.maximum(pre, 0.0) + jnp.log1p(jnp.exp(-jnp.abs(pre)))
    a = -jnp.exp(alog_ref[...])
    adt = dt * a
    row = lax.broadcasted_iota(jnp.int32, (cl, cl), 0)
    col = lax.broadcasted_iota(jnp.int32, (cl, cl), 1)
    causal = row >= col
    tri = causal.astype(F32)
    a_cs = jnp.dot(tri, adt, preferred_element_type=F32, precision=HIGHEST) * LOG2E
    a_cs_t = a_cs.T
    dt_t = dt.T
    a_last = a_cs[cl - 1:cl, :]
    ea = jnp.exp2(a_cs)
    dsc = jnp.exp2(a_last - a_cs) * dt

    both = jnp.concatenate([ea, dsc], axis=0)
    hi = both.astype(BF16)
    lo = (both - hi.astype(F32)).astype(BF16)
    both_x = jnp.dot(jnp.concatenate([hi, lo], axis=1), spread_ref[...], preferred_element_type=F32)
    ea_x = both_x[:cl]
    dsc_x = both_x[cl:]
    cdec_x = ea_x[cl - 1:cl, :]
    xsd = xs * dsc_x
    lane = lax.broadcasted_iota(jnp.int32, (cl, 2 * SSM_HEAD_DIM), 1)
    first_half = lane < SSM_HEAD_DIM

    hg = SSM_HEADS // SSM_GROUPS
    gw = hg * SSM_HEAD_DIM
    for g in range(SSM_GROUPS):
        bg = bm[:, g * SSM_STATE:(g + 1) * SSM_STATE].astype(BF16)
        cg = cm[:, g * SSM_STATE:(g + 1) * SSM_STATE].astype(BF16)
        gc = slice(g * gw, (g + 1) * gw)
        cb = lax.dot_general(cg, bg, NT, preferred_element_type=F32)
        st = state_scr[g]
        y_scr[:, gc] = jnp.dot(cg, st.astype(BF16), preferred_element_type=F32) * ea_x[:, gc]
        new = lax.dot_general(bg, xsd[:, gc].astype(BF16), (((0,), (0,)), ((), ())),
                              preferred_element_type=F32)
        state_scr[g] = st * cdec_x[:, gc] + new
        for pp in range(hg // 2):
            pair = g * (hg // 2) + pp
            pc = slice(pair * 128, (pair + 1) * 128)
            gm = []
            for e in range(2):
                h = 2 * pair + e
                seg = a_cs[:, h:h + 1] - a_cs_t[h:h + 1, :]
                lmat = jnp.exp2(jnp.where(causal, seg, NEG))
                gm.append(cb * lmat * dt_t[h:h + 1, :])
            lhs = jnp.concatenate(gm, axis=0).astype(BF16)
            yy = jnp.dot(lhs, xs[:, pc].astype(BF16), preferred_element_type=F32)
            y_scr[:, pc] = y_scr[:, pc] + jnp.where(first_half, yy[:cl], yy[cl:])

    y = y_scr[...] + xs * dskip_ref[...]
    y = y * _silu(z_ref[...].astype(F32))
    gsz = D_INNER // SSM_GROUPS
    for g in range(SSM_GROUPS):
        yg = y[:, g * gsz:(g + 1) * gsz]
        yn = yg * lax.rsqrt(jnp.mean(yg * yg, axis=-1, keepdims=True) + EPS)
        o_ref[:, g * gsz:(g + 1) * gsz] = (yn * gn_ref[:, g * gsz:(g + 1) * gsz]).astype(o_ref.dtype)


def _conv_shift_matrix():
    cl = SSM_CHUNK
    ext = CONV_TAIL + cl
    m = np.zeros((cl, (SSM_CONV - 1) * ext), np.float32)
    t = np.arange(cl)
    for k in range(SSM_CONV - 1):
        m[t, k * ext + CONV_TAIL + t - (SSM_CONV - 1 - k)] = 1.0
    return jnp.asarray(m, dtype=BF16)


def _head_spread_matrix():
    m = np.zeros((2, 128, D_INNER), np.float32)
    for h in range(SSM_HEADS):
        m[:, h, h * SSM_HEAD_DIM:(h + 1) * SSM_HEAD_DIM] = 1.0
    return jnp.asarray(m.reshape(256, D_INNER), dtype=BF16)


def _ssm(pb, small, conv_w, conv_b, dt_bias, a_log, d_skip, g_norm):
    bsz, s, _ = pb.shape
    cl = SSM_CHUNK
    pad = 128 - SSM_HEADS
    dtb = jnp.pad(dt_bias, (0, pad)).reshape(1, 128)
    alog = jnp.pad(a_log, (0, pad)).reshape(1, 128)
    dskip = jnp.repeat(d_skip, SSM_HEAD_DIM).reshape(1, D_INNER)
    const = lambda shape: pl.BlockSpec(shape, lambda b, c: (0, 0))
    return pl.pallas_call(
        _ssm_kernel,
        grid=(bsz, s // SSM_STEP),
        in_specs=[pl.BlockSpec((1, SSM_STEP, D_CONV), lambda b, c: (b, c, B_XBC // D_CONV)),
                  pl.BlockSpec((1, SSM_STEP, D_INNER), lambda b, c: (b, c, B_ZSSM // D_INNER)),
                  pl.BlockSpec((1, SSM_STEP, 128), lambda b, c: (b, c, 0)),
                  const((cl, (SSM_CONV - 1) * (CONV_TAIL + cl))), const((256, D_INNER)),
                  const((SSM_CONV, D_CONV)), const((1, D_CONV)), const((1, 128)), const((1, 128)),
                  const((1, D_INNER)), const((1, D_INNER))],
        out_specs=pl.BlockSpec((1, SSM_STEP, D_INNER), lambda b, c: (b, c, 0)),
        out_shape=jax.ShapeDtypeStruct((bsz, s, D_INNER), BF16),
        scratch_shapes=[pltpu.VMEM((CONV_TAIL, D_CONV), BF16),
                        pltpu.VMEM((SSM_GROUPS, SSM_STATE, D_INNER // SSM_GROUPS), F32),
                        pltpu.VMEM((SSM_STEP, D_INNER), F32)],
        compiler_params=_cp(("arbitrary", "arbitrary")),
        name="ssm",
    )(pb, pb, small, _conv_shift_matrix(), _head_spread_matrix(), conv_w, conv_b.reshape(1, D_CONV),
      dtb, alog, dskip,
      g_norm.reshape(1, D_INNER))


def _cmp_kernel(r0_ref, r1_ref, r2_ref, r3_ref, pos_ref, w1_ref, w2_ref, o_ref):
    half = CMP_STRIDE * ATT_HEAD_DIM
    hd = ATT_HEAD_DIM
    nrow = r0_ref.shape[1] // CMP_STRIDE
    toks = [[r_ref[0, pl.ds(t, nrow, stride=CMP_STRIDE), :] for t in range(CMP_STRIDE)]
            for r_ref in (r0_ref, r1_ref, r2_ref, r3_ref)]
    for kv in range(2):
        w1 = w1_ref[kv]
        posb = jnp.broadcast_to(pos_ref[kv], (8, 2 * half)).astype(BF16)
        cpos = jnp.dot(posb, w1, preferred_element_type=F32)[0:1]
        for gg in range(ATT_KV_GROUPS):
            j = kv * ATT_KV_GROUPS + gg
            r = jnp.concatenate([toks[j // 2][t][:, (j % 2) * hd:(j % 2 + 1) * hd]
                                 for t in range(CMP_STRIDE)], axis=1).astype(BF16)
            first = jnp.dot(r, w1[:half], preferred_element_type=F32)
            second = jnp.dot(r, w1[half:], preferred_element_type=F32)
            hid = first + pltpu.roll(second, nrow - 1, 0) + cpos
            o_ref[0, j] = jnp.dot(_silu(hid).astype(BF16), w2_ref[kv], preferred_element_type=F32)


def _compress(kvc, pos, w1, w2):
    bsz, s, wid = kvc.shape
    nrow = s // CMP_STRIDE
    ng = 2 * ATT_KV_GROUPS
    return pl.pallas_call(
        _cmp_kernel,
        grid=(bsz,),
        in_specs=[pl.BlockSpec((1, s, 128), lambda b, c=c: (b, 0, c)) for c in range(wid // 128)]
                 + [pl.BlockSpec((2, 1, 2048), lambda b: (0, 0, 0)),
                  pl.BlockSpec((2, 2048, CMP_HIDDEN), lambda b: (0, 0, 0)),
                  pl.BlockSpec((2, CMP_HIDDEN, ATT_HEAD_DIM), lambda b: (0, 0, 0))],
        out_specs=pl.BlockSpec((1, ng, nrow, ATT_HEAD_DIM), lambda b: (b, 0, 0, 0)),
        out_shape=jax.ShapeDtypeStruct((bsz, ng, nrow, ATT_HEAD_DIM), F32),
        compiler_params=_cp(("arbitrary",)),
        name="cmp",
    )(kvc, kvc, kvc, kvc, pos, w1, w2)


def _bf16_round_np(x):
    u = np.asarray(x, np.float32).view(np.uint32)
    u = (u + (((u >> 16) & 1) + 0x7FFF)) & np.uint32(0xFFFF0000)
    return u.view(np.float32)


def _slope_table():
    slope = (2.0 ** (-8.0 * np.arange(1, ATT_HEADS + 1) / ATT_HEADS)).astype(np.float32)
    slope = (slope.astype(np.float64) * LOG2E).astype(np.float32)
    p0 = _bf16_round_np(slope)
    p1 = _bf16_round_np(slope - p0)
    p2 = _bf16_round_np(slope - p0 - p1)
    tab = np.zeros((ATT_HEADS, SLOPE_STRIDE), np.float32)
    tab[:, 0], tab[:, 1], tab[:, 2] = p0, p1, p2
    tab[:, 3:6] = tab[:, 0:3] * POS_BASE
    tab[:, 6] = slope
    return jnp.asarray(tab.reshape(-1))


def _key_aux_table(s):
    pos = np.arange(s)
    tab = np.zeros((s, 128), np.float32)
    tab[:, L_POS:L_POS + 3] = (pos % POS_BASE)[:, None]
    tab[:, L_TILE:L_TILE + 3] = (pos // POS_BASE)[:, None]
    tab[pos, L_MASK + pos // SLC_BLOCK] = 1.0
    return jnp.asarray(tab, dtype=BF16)


def _bias_table():
    assert WINDOW == 2 * KT and QH == KT
    r = np.arange(QH)[:, None]
    c = np.arange(KT)[None, :]
    oldest = np.where(2 * KT + r - c < WINDOW, 0.0, NEG)
    causal = np.where(r - c >= 0, 0.0, NEG)
    return jnp.asarray(np.stack([oldest, causal]), dtype=F32)


def _gate_expand_table():
    tab = np.zeros((ATT_KV_GROUPS, 2, 128, 3 * KV_WIDTH), np.float32)
    for g in range(ATT_KV_GROUPS):
        for h in range(ATT_HPG):
            for j in range(3):
                row = SMALL_GATE + g * 3 * ATT_HPG + 3 * h + j
                tab[g, :, row, j * KV_WIDTH + h * ATT_HEAD_DIM:j * KV_WIDTH + (h + 1) * ATT_HEAD_DIM] = 1.0
    return jnp.asarray(tab.reshape(ATT_KV_GROUPS, 256, 3 * KV_WIDTH), dtype=BF16)


def _cmp_to_slc_matrix_t(n_cmp_pad, n_cmp, n_slc):
    cs = np.arange(n_cmp_pad) * CMP_STRIDE
    ss = np.arange(n_slc) * SLC_BLOCK
    lo = np.maximum(cs[:, None], ss[None, :])
    hi = np.minimum(cs[:, None] + CMP_BLOCK, ss[None, :] + SLC_BLOCK)
    m = np.clip(hi - lo, 0, None) / CMP_BLOCK
    m[n_cmp:] = 0.0
    return jnp.asarray(m.T, dtype=F32)


def _cmpattn_kernel(slopes_ref, q_ref, ck_ref, cv_ref, mt_ref, ocmp_ref, qmask_ref,
                    kcat_scr, vbd_scr):
    g = pl.program_id(1)
    i = pl.program_id(2)
    tq = q_ref.shape[1]
    t0 = i * tq
    npad = ck_ref.shape[2]
    n_slc = mt_ref.shape[0]

    @pl.when(i == 0)
    def _():
        kc = ck_ref[0, 0]
        vc = cv_ref[0, 0]
        k_hi = kc.astype(BF16).astype(F32)
        k_lo = kc - k_hi
        zero = jnp.zeros_like(kc)
        for h in range(ATT_HPG):
            def bd(a):
                return jnp.concatenate([a if hh == h else zero for hh in range(ATT_HPG)], axis=1).astype(BF16)
            kcat_scr[h * npad:(h + 1) * npad, :KV_WIDTH] = bd(k_hi)
            kcat_scr[h * npad:(h + 1) * npad, KV_WIDTH:] = bd(k_lo)
            vbd_scr[h * npad:(h + 1) * npad, :] = bd(vc)

    q = q_ref[0]
    s = lax.dot_general(jnp.concatenate([q, q], axis=1), kcat_scr[...], NT,
                        preferred_element_type=F32)
    jcol = lax.broadcasted_iota(jnp.int32, (1, npad), 1)
    cmp_end = jcol * CMP_STRIDE + (CMP_BLOCK - 1)
    trow = t0 + lax.broadcasted_iota(jnp.int32, (tq, npad), 0)
    valid = (cmp_end <= trow) & (jcol < npad - 1)
    any_valid = trow[:, 0:1] >= CMP_BLOCK - 1
    rel = (cmp_end - t0).astype(F32)
    psum = jnp.zeros((tq, npad), F32)
    ps = []
    for h in range(ATT_HPG):
        slope = slopes_ref[(g * ATT_HPG + h) * SLOPE_STRIDE + 6]
        sh = jnp.where(valid, s[:, h * npad:(h + 1) * npad] + slope * rel, NEG)
        mx = jnp.max(sh, axis=-1, keepdims=True)
        p = jnp.exp2(sh - mx)
        l = jnp.sum(p, axis=-1, keepdims=True)
        p = p * jnp.where(any_valid, 1.0 / l, 0.0)
        ps.append(p)
        psum = psum + p
    ocmp_ref[0] = jnp.dot(jnp.concatenate(ps, axis=1).astype(BF16), vbd_scr[...],
                          preferred_element_type=F32)
    imp_t = lax.dot_general(mt_ref[...], psum, NT, preferred_element_type=F32,
                            precision=HIGHEST)

    nrb = n_slc // 8
    blk_t = (t0 + lax.broadcasted_iota(jnp.int32, (8, tq), 1)) >> SLC_SHIFT
    sub = lax.broadcasted_iota(jnp.int32, (8, tq), 0)
    score = []
    for rb in range(nrb):
        kk = sub + rb * 8
        imp = imp_t[rb * 8:(rb + 1) * 8]
        forced = (kk == 0) | (kk == blk_t) | (kk == blk_t - 1)
        score.append(jnp.where(forced, imp + FORCE_BONUS, jnp.where(kk <= blk_t, imp, -1.0)))
    rank = [jnp.zeros((8, tq), F32) for _ in range(nrb)]
    for j in range(n_slc):
        cj = jnp.broadcast_to(score[j // 8][j % 8:j % 8 + 1, :], (8, tq))
        for rb in range(nrb):
            ge = jnp.where(cj >= score[rb], 1.0, 0.0)
            gt = jnp.where(cj > score[rb], 1.0, 0.0)
            if rb * 8 > j:
                beats = ge
            elif rb * 8 + 7 <= j:
                beats = gt
            else:
                beats = jnp.where(sub > j - rb * 8, ge, gt)
            rank[rb] = rank[rb] + beats
    rows = [jnp.zeros((L_MASK, tq), F32)]
    rows += [jnp.where(rank[rb] < float(SLC_TOPK), 0.0, -BIG) for rb in range(nrb)]
    rows += [jnp.zeros((128 - L_MASK - n_slc, tq), F32)]
    qmask_ref[0, 0] = jnp.concatenate(rows, axis=0).T.astype(BF16)


def _cmpattn(pb, cmp_kv, s):
    bsz = pb.shape[0]
    g = ATT_KV_GROUPS
    nrow = cmp_kv.shape[2]
    n_cmp = (s - CMP_BLOCK) // CMP_STRIDE + 1
    n_slc = s // SLC_BLOCK
    mt = _cmp_to_slc_matrix_t(nrow, n_cmp, n_slc)
    return pl.pallas_call(
        _cmpattn_kernel,
        grid=(bsz, g, s // TQ_CMP),
        in_specs=[pl.BlockSpec(memory_space=pltpu.SMEM),
                  pl.BlockSpec((1, TQ_CMP, KV_WIDTH), lambda b, gg, i: (b, i, B_Q // KV_WIDTH + gg)),
                  pl.BlockSpec((1, 1, nrow, ATT_HEAD_DIM), lambda b, gg, i: (b, gg, 0, 0)),
                  pl.BlockSpec((1, 1, nrow, ATT_HEAD_DIM), lambda b, gg, i: (b, g + gg, 0, 0)),
                  pl.BlockSpec((n_slc, nrow), lambda b, gg, i: (0, 0))],
        out_specs=[pl.BlockSpec((1, TQ_CMP, KV_WIDTH), lambda b, gg, i: (b, i, gg)),
                   pl.BlockSpec((1, 1, TQ_CMP, 128), lambda b, gg, i: (b, gg, i, 0))],
        out_shape=[jax.ShapeDtypeStruct((bsz, s, ATT_WIDTH), F32),
                   jax.ShapeDtypeStruct((bsz, g, s, 128), BF16)],
        scratch_shapes=[pltpu.VMEM((ATT_HPG * nrow, 2 * KV_WIDTH), BF16),
                        pltpu.VMEM((ATT_HPG * nrow, KV_WIDTH), BF16)],
        compiler_params=_cp(("arbitrary", "arbitrary", "arbitrary")),
        name="cmpattn",
    )(_slope_table(), pb, cmp_kv, cmp_kv, mt)


def _flash_step(qa_ref, ka_ref, va_ref, m_scr, acc_scr, rows, ksub, nsub, bias, first=False):
    r0, nr = rows
    nk = nsub * KT
    start = pl.multiple_of(ksub * KT, KT)
    k_aug = ka_ref[pl.ds(start, nk), :]
    v_aug = va_ref[pl.ds(start, nk), :]
    s = lax.dot_general(qa_ref[r0:r0 + nr, :], k_aug, NT, preferred_element_type=F32)
    if bias is not None:
        parts = []
        for t, b in enumerate(bias):
            st = s[:, t * KT:(t + 1) * KT]
            if b is not None:
                st = (st.reshape(nr // QH, QH, KT) + b[None]).reshape(nr, KT)
            parts.append(st)
        s = jnp.concatenate(parts, axis=1) if nsub > 1 else parts[0]
    m_cur = jnp.max(s, axis=-1, keepdims=True)
    if first:
        m_new = jnp.broadcast_to(m_cur, (nr, 128))
    else:
        m_prev = m_scr[r0:r0 + nr, :]
        m_new = jnp.maximum(m_prev, m_cur)
    p = jnp.exp2(s - jnp.concatenate([m_new] * (nk // 128), axis=1))
    pv_t = lax.dot_general(v_aug, p.astype(BF16), (((0,), (1,)), ((), ())),
                           preferred_element_type=F32)
    if first:
        acc_scr[:, r0:r0 + nr] = pv_t
    else:
        acc_scr[:, r0:r0 + nr] = jnp.exp2(m_prev - m_new).T * acc_scr[:, r0:r0 + nr] + pv_t
    m_scr[r0:r0 + nr, :] = m_new


def _flash_out(acc_scr):
    r = acc_scr.shape[1]
    den = acc_scr[0:8, :]
    num = acc_scr[ATT_HEAD_DIM:2 * ATT_HEAD_DIM, :]
    o_t = (num.reshape(ATT_HEAD_DIM // 8, 8, r) / den[None]).reshape(ATT_HEAD_DIM, r)
    halves = []
    for half in range(TQ // QH):
        cols = [o_t[:, (half * ATT_HPG + h) * QH:(half * ATT_HPG + h + 1) * QH] for h in range(ATT_HPG)]
        halves.append(jnp.concatenate(cols, axis=0).T)
    return jnp.concatenate(halves, axis=0)


def _attn_kernel(slopes_ref, q_ref, ksel_ref, kwin_ref, aux_ref, qmask_ref, ocmp_ref, small_ref,
                 zatt_ref, pick_ref, bias_ref, o_ref, qas_scr, qaw_scr, m_scr, acc_scr,
                 kas_scr, vas_scr, kaw_scr, vaw_scr):
    g = pl.program_id(1)
    i = pl.program_id(2)

    @pl.when(i == 0)
    def _():
        s_len = ksel_ref.shape[1]
        left = jnp.where(lax.broadcasted_iota(jnp.int32, (s_len, 128), 1) < ATT_HEAD_DIM,
                         1.0, 0.0).astype(BF16)
        for kv_ref, ka_scr, va_scr in ((ksel_ref, kas_scr, vas_scr), (kwin_ref, kaw_scr, vaw_scr)):
            kv = kv_ref[0]
            ka_scr[...] = kv * left + aux_ref[...]
            va_scr[...] = kv * (1.0 - left) + left

    qf = q_ref[0].astype(F32)
    qm = qmask_ref[0, 0].astype(F32)
    lane_row = lax.broadcasted_iota(jnp.int32, (1, 128), 1)
    left = lax.broadcasted_iota(jnp.int32, (TQ, 128), 1) < ATT_HEAD_DIM
    for h in range(ATT_HPG):
        slab = qf[:, 128 * (h // 2):128 * (h // 2) + 128]
        if h % 2:
            slab = pltpu.roll(slab, ATT_HEAD_DIM, 1)
        ext = jnp.zeros((1, 128), F32)
        for c in range(6):
            ext = jnp.where(lane_row == L_POS + c, slopes_ref[(g * ATT_HPG + h) * SLOPE_STRIDE + c], ext)
        base = jnp.where(left, slab, ext)
        for half in range(TQ // QH):
            rs = (half * ATT_HPG + h) * QH
            tok = slice(half * QH, (half + 1) * QH)
            qaw_scr[rs:rs + QH, :] = base[tok].astype(BF16)
            qas_scr[rs:rs + QH, :] = (base[tok] + qm[tok]).astype(BF16)

    half_rows = ATT_HPG * QH
    rows_a, rows_b, rows_all = (0, half_rows), (half_rows, half_rows), (0, 2 * half_rows)
    k0 = i * (TQ // KT)
    oldest, causal = bias_ref[0], bias_ref[1]

    _flash_step(qas_scr, kas_scr, vas_scr, m_scr, acc_scr, rows_a, k0, 1, [causal], first=True)
    _flash_step(qas_scr, kas_scr, vas_scr, m_scr, acc_scr, rows_b, k0, 2, [None, causal], first=True)

    for n_prev in range(1, ksel_ref.shape[1] // TK):
        @pl.when(i == n_prev)
        def _():
            for j in range(n_prev):
                _flash_step(qas_scr, kas_scr, vas_scr, m_scr, acc_scr, rows_all, j * (TK // KT),
                            TK // KT, None)

    o_slc = _flash_out(acc_scr)

    @pl.when(i == 0)
    def _():
        _flash_step(qaw_scr, kaw_scr, vaw_scr, m_scr, acc_scr, rows_a, 0, 1, [causal], first=True)
        _flash_step(qaw_scr, kaw_scr, vaw_scr, m_scr, acc_scr, rows_b, 0, 2, [None, causal], first=True)

    @pl.when(i >= 1)
    def _():
        win = [oldest, None, causal]
        _flash_step(qaw_scr, kaw_scr, vaw_scr, m_scr, acc_scr, rows_a, k0 - 2, 3, win, first=True)
        _flash_step(qaw_scr, kaw_scr, vaw_scr, m_scr, acc_scr, rows_b, k0 - 1, 3, win, first=True)

    o_win = _flash_out(acc_scr)

    gate = _sigmoid(small_ref[0])
    g_hi = gate.astype(BF16)
    g_lo = (gate - g_hi.astype(F32)).astype(BF16)
    gts = jnp.dot(jnp.concatenate([g_hi, g_lo], axis=1), pick_ref[0],
                  preferred_element_type=F32)
    o = (gts[:, :KV_WIDTH] * ocmp_ref[0] + gts[:, KV_WIDTH:2 * KV_WIDTH] * o_slc
         + gts[:, 2 * KV_WIDTH:] * o_win)
    o_ref[0] = (o * _silu(zatt_ref[0].astype(F32))).astype(o_ref.dtype)


def _attn(pb, small, qmask, ocmp):
    bsz, s, _ = pb.shape
    g = ATT_KV_GROUPS
    r = ATT_HPG * TQ
    return pl.pallas_call(
        _attn_kernel,
        grid=(bsz, g, s // TQ),
        in_specs=[pl.BlockSpec(memory_space=pltpu.SMEM),
                  pl.BlockSpec((1, TQ, KV_WIDTH), lambda b, gg, i: (b, i, B_Q // KV_WIDTH + gg)),
                  pl.BlockSpec((1, s, 128), lambda b, gg, i: (b, 0, B_KSEL // 128 + gg)),
                  pl.BlockSpec((1, s, 128), lambda b, gg, i: (b, 0, B_KWIN // 128 + gg)),
                  pl.BlockSpec((s, 128), lambda b, gg, i: (0, 0)),
                  pl.BlockSpec((1, 1, TQ, 128), lambda b, gg, i: (b, gg, i, 0)),
                  pl.BlockSpec((1, TQ, KV_WIDTH), lambda b, gg, i: (b, i, gg)),
                  pl.BlockSpec((1, TQ, 128), lambda b, gg, i: (b, i, 0)),
                  pl.BlockSpec((1, TQ, KV_WIDTH), lambda b, gg, i: (b, i, B_ZATT // KV_WIDTH + gg)),
                  pl.BlockSpec((1, 256, 3 * KV_WIDTH), lambda b, gg, i: (gg, 0, 0)),
                  pl.BlockSpec((2, QH, KT), lambda b, gg, i: (0, 0, 0))],
        out_specs=pl.BlockSpec((1, TQ, KV_WIDTH), lambda b, gg, i: (b, i, gg)),
        out_shape=jax.ShapeDtypeStruct((bsz, s, ATT_WIDTH), BF16),
        scratch_shapes=[pltpu.VMEM((r, 128), BF16), pltpu.VMEM((r, 128), BF16),
                        pltpu.VMEM((r, 128), F32), pltpu.VMEM((128, r), F32)]
                       + [pltpu.VMEM((s, 128), BF16)] * 4,
        compiler_params=_cp(("arbitrary", "arbitrary", "arbitrary")),
        name="attn",
    )(_slope_table(), pb, pb, pb, _key_aux_table(s), qmask, ocmp, small, pb, _gate_expand_table(),
      _bias_table())


def _merge_kernel(yssm_ref, onsa_ref, mg_ref, x_ref, gate_ref, gpost_ref, wssm_ref, wnsa_ref, wout_ref, o_ref):
    y_ssm = jnp.dot(yssm_ref[0], wssm_ref[...], preferred_element_type=F32)
    y_nsa = jnp.dot(onsa_ref[0], wnsa_ref[...], preferred_element_type=F32)
    mg = _sigmoid(mg_ref[0].astype(F32))
    merged = mg[:, :D_MODEL] * y_ssm + mg[:, D_MODEL:] * y_nsa
    out = jnp.dot(merged.astype(BF16), wout_ref[...], preferred_element_type=F32)
    yn = out * lax.rsqrt(jnp.mean(out * out, axis=-1, keepdims=True) + EPS) * gpost_ref[...]
    o_ref[0] = x_ref[0] + gate_ref[0] * yn


def _merge(yssm, onsa, pb, x, gate, g_post, w_ssm_out, w_nsa_out, w_out):
    bsz, s, _ = x.shape
    tm = 512
    const = lambda shape: pl.BlockSpec(shape, lambda b, i: (0, 0))
    return pl.pallas_call(
        _merge_kernel,
        grid=(bsz, s // tm),
        in_specs=[pl.BlockSpec((1, tm, D_INNER), lambda b, i: (b, i, 0)),
                  pl.BlockSpec((1, tm, ATT_WIDTH), lambda b, i: (b, i, 0)),
                  pl.BlockSpec((1, tm, 2 * D_MODEL), lambda b, i: (b, i, B_MERGE // (2 * D_MODEL))),
                  pl.BlockSpec((1, tm, D_MODEL), lambda b, i: (b, i, 0)),
                  pl.BlockSpec((1, 1, D_MODEL), lambda b, i: (b, 0, 0)),
                  const((1, D_MODEL)),
                  const((D_INNER, D_MODEL)), const((ATT_WIDTH, D_MODEL)), const((D_MODEL, D_MODEL))],
        out_specs=pl.BlockSpec((1, tm, D_MODEL), lambda b, i: (b, i, 0)),
        out_shape=jax.ShapeDtypeStruct((bsz, s, D_MODEL), F32),
        compiler_params=_cp(("arbitrary", "arbitrary")),
        name="merge",
    )(yssm, onsa, pb, x, gate, g_post.reshape(1, D_MODEL), w_ssm_out, w_nsa_out, w_out)


def _wprep_kernel(w_ref, wb_ref, ws_ref, wkv_ref):
    def put(ref, dst, off, n, scale=None):
        v = w_ref[off:off + n, :]
        ref[dst:dst + n, :] = (v if scale is None else v * scale).astype(BF16)

    put(wb_ref, B_XBC, IN_XBC, D_CONV)
    put(wb_ref, B_ZATT, IN_ZATT, ATT_WIDTH)
    put(wb_ref, B_MERGE, IN_MERGE, 2 * D_MODEL)
    put(wb_ref, B_ZSSM, IN_Z, D_INNER)
    put(wb_ref, B_Q, IN_Q, ATT_WIDTH, ATT_HEAD_DIM ** -0.5 * LOG2E)
    hd = ATT_HEAD_DIM
    for dst, k_off in ((B_KSEL, IN_KV + 2 * KV_WIDTH), (B_KWIN, IN_KV + 4 * KV_WIDTH)):
        for gg in range(ATT_KV_GROUPS):
            put(wb_ref, dst + 2 * gg * hd, k_off + gg * hd, hd)
            put(wb_ref, dst + (2 * gg + 1) * hd, k_off + KV_WIDTH + gg * hd, hd)
    put(wkv_ref, 0, IN_KV, 2 * KV_WIDTH)
    put(ws_ref, 0, IN_DT, SSM_HEADS)
    put(ws_ref, SMALL_GATE, IN_GATE, 3 * ATT_HEADS)
    used = SMALL_GATE + 3 * ATT_HEADS
    ws_ref[used:, :] = jnp.zeros((128 - used, ws_ref.shape[1]), BF16)


def _split_w_in(w_in):
    dm, n_in = w_in.shape
    tc = 128
    return pl.pallas_call(
        _wprep_kernel,
        grid=(dm // tc,),
        in_specs=[pl.BlockSpec((n_in, tc), lambda i: (0, i))],
        out_specs=[pl.BlockSpec((NB, tc), lambda i: (0, i)), pl.BlockSpec((128, tc), lambda i: (0, i)),
                   pl.BlockSpec((2 * KV_WIDTH, tc), lambda i: (0, i))],
        out_shape=[jax.ShapeDtypeStruct((NB, dm), BF16), jax.ShapeDtypeStruct((128, dm), BF16),
                   jax.ShapeDtypeStruct((2 * KV_WIDTH, dm), BF16)],
        compiler_params=_cp(("arbitrary",)),
        name="wprep",
    )(w_in.T)


def _layer(x, c, w_ada, b_ada, g_pre, g_post, w_in, conv_w, conv_b, dt_bias, a_log, d_skip,
           g_ssm_norm, w_ssm_out, cmp_pos_k, cmp_w1_k, cmp_w2_k, cmp_pos_v, cmp_w1_v, cmp_w2_v,
           w_nsa_out, w_out):
    bsz, s, dm = x.shape
    mod = _ada(c, w_ada, b_ada)
    shift = mod[:, None, :dm]
    scale = mod[:, None, dm:2 * dm]
    gate = mod[:, None, 2 * dm:]
    wb, w_small, w_kvc = _split_w_in(w_in)
    pb, small, kvc = _proj(x, shift, scale, g_pre, wb, w_small, w_kvc)

    yssm = _ssm(pb, small, conv_w, conv_b, dt_bias, a_log, d_skip, g_ssm_norm)

    pos = jnp.stack([cmp_pos_k.reshape(1, -1), cmp_pos_v.reshape(1, -1)])
    w1 = jnp.stack([cmp_w1_k, cmp_w1_v]).astype(BF16)
    w2 = jnp.stack([cmp_w2_k, cmp_w2_v]).astype(BF16)
    cmp_kv = _compress(kvc, pos, w1, w2)

    ocmp, qmask = _cmpattn(pb, cmp_kv, s)
    onsa = _attn(pb, small, qmask, ocmp)
    return _merge(yssm, onsa, pb, x, gate, g_post, w_ssm_out.astype(BF16), w_nsa_out.astype(BF16),
                  w_out.astype(BF16))


@jax.jit
def kernel(x, c, w_ada, b_ada, g_pre, g_post, w_in, conv_w, conv_b, dt_bias, a_log, d_skip, g_ssm_norm,
           w_ssm_out, cmp_pos_k, cmp_w1_k, cmp_w2_k, cmp_pos_v, cmp_w1_v, cmp_w2_v, w_nsa_out, w_out):
    for layer in range(w_in.shape[0]):
        x = _layer(x, c, w_ada[layer], b_ada[layer], g_pre[layer], g_post[layer], w_in[layer],
                   conv_w[layer], conv_b[layer], dt_bias[layer], a_log[layer], d_skip[layer],
                   g_ssm_norm[layer], w_ssm_out[layer], cmp_pos_k[layer], cmp_w1_k[layer],
                   cmp_w2_k[layer], cmp_pos_v[layer], cmp_w1_v[layer], cmp_w2_v[layer],
                   w_nsa_out[layer], w_out[layer])
    return x
```

```python
import numpy as np
import jax
import jax.numpy as jnp
from jax import lax
from jax.experimental import pallas as pl
from jax.experimental.pallas import tpu as pltpu

D_MODEL = 1024
D_INNER = 2048
SSM_HEAD_DIM = 64
SSM_HEADS = 32
SSM_GROUPS = 4
SSM_STATE = 128
SSM_CONV = 4
SSM_CHUNK = 128
D_CONV = D_INNER + 2 * SSM_GROUPS * SSM_STATE
SSM_STEP = 512
CONV_TAIL = 16

ATT_HEADS = 16
ATT_HEAD_DIM = 64
ATT_KV_GROUPS = 4
ATT_HPG = 4
ATT_WIDTH = 1024
KV_WIDTH = 256
CMP_BLOCK = 32
CMP_STRIDE = 16
CMP_HIDDEN = 256
SLC_BLOCK = 64
SLC_SHIFT = 6
SLC_TOPK = 16
WINDOW = 512
FORCE_BONUS = 1000.0
EPS = 1e-6
NEG = -1e30
BIG = 2.0 ** 100

F32 = jnp.float32
BF16 = jnp.bfloat16
HIGHEST = lax.Precision.HIGHEST
NT = (((1,), (1,)), ((), ()))

_SIZES = (D_INNER, D_CONV, SSM_HEADS, ATT_WIDTH, 6 * KV_WIDTH, 3 * ATT_HEADS, ATT_WIDTH, 2 * D_MODEL)
_OFFS = tuple(int(v) for v in np.cumsum((0,) + _SIZES))
IN_Z, IN_XBC, IN_DT, IN_Q, IN_KV, IN_GATE, IN_ZATT, IN_MERGE = _OFFS[:8]

B_XBC, B_ZATT, B_MERGE, B_ZSSM, B_Q, B_KSEL, B_KWIN = 0, 3072, 4096, 6144, 8192, 9216, 9728
NB = 10240
SMALL_GATE = 32
LOG2E = 1.4426950408889634

TM_PROJ = 1024
TN_PROJ = 2560
TQ = 512
TK = 512
QH = 256
KT = 256
POS_BASE = 256
TQ_CMP = 2048
VMEM_LIMIT = 48 * 1024 * 1024

L_POS = 64
L_TILE = 67
L_MASK = 72
SLOPE_STRIDE = 8


def _cp(sem):
    return pltpu.CompilerParams(dimension_semantics=sem, vmem_limit_bytes=VMEM_LIMIT)


def _sigmoid(v):
    return 0.5 * jnp.tanh(0.5 * v) + 0.5


def _silu(v):
    h = 0.5 * v
    return h * jnp.tanh(h) + h


def _ada_kernel(c_ref, w_ref, b_ref, o_ref):
    o_ref[...] = jnp.dot(c_ref[...], w_ref[...], preferred_element_type=F32,
                         precision=HIGHEST) + b_ref[...]


def _ada(c, w_ada, b_ada):
    bsz = c.shape[0]
    return pl.pallas_call(
        _ada_kernel,
        grid=(3,),
        in_specs=[pl.BlockSpec((bsz, D_MODEL), lambda j: (0, 0)),
                  pl.BlockSpec((D_MODEL, D_MODEL), lambda j: (0, j)),
                  pl.BlockSpec((1, D_MODEL), lambda j: (0, j))],
        out_specs=pl.BlockSpec((bsz, D_MODEL), lambda j: (0, j)),
        out_shape=jax.ShapeDtypeStruct((bsz, 3 * D_MODEL), F32),
        compiler_params=_cp(("arbitrary",)),
        name="ada",
    )(c, w_ada, b_ada.reshape(1, 3 * D_MODEL))


def _proj_kernel(x_ref, shift_ref, scale_ref, g_ref, w_ref, ws_ref, wkv_ref, o_ref, osm_ref, okv_ref, h_scr):
    @pl.when(pl.program_id(2) == 0)
    def _():
        xf = x_ref[0]
        y = xf * lax.rsqrt(jnp.mean(xf * xf, axis=-1, keepdims=True) + EPS) * g_ref[...]
        h_scr[...] = (y * (1.0 + scale_ref[0]) + shift_ref[0]).astype(BF16)
        osm_ref[0] = lax.dot_general(h_scr[...], ws_ref[...], NT, preferred_element_type=F32)
        okv_ref[0] = lax.dot_general(h_scr[...], wkv_ref[...], NT,
                                     preferred_element_type=F32).astype(okv_ref.dtype)

    o_ref[0] = lax.dot_general(h_scr[...], w_ref[...], NT, preferred_element_type=F32).astype(o_ref.dtype)


def _proj(x, shift, scale, g_pre, w, w_small, w_kvc):
    bsz, s, _ = x.shape
    n = w.shape[0]
    nkv = w_kvc.shape[0]
    return pl.pallas_call(
        _proj_kernel,
        grid=(bsz, s // TM_PROJ, n // TN_PROJ),
        in_specs=[pl.BlockSpec((1, TM_PROJ, D_MODEL), lambda b, i, j: (b, i, 0)),
                  pl.BlockSpec((1, 1, D_MODEL), lambda b, i, j: (b, 0, 0)),
                  pl.BlockSpec((1, 1, D_MODEL), lambda b, i, j: (b, 0, 0)),
                  pl.BlockSpec((1, D_MODEL), lambda b, i, j: (0, 0)),
                  pl.BlockSpec((TN_PROJ, D_MODEL), lambda b, i, j: (j, 0)),
                  pl.BlockSpec((128, D_MODEL), lambda b, i, j: (0, 0)),
                  pl.BlockSpec((nkv, D_MODEL), lambda b, i, j: (0, 0))],
        out_specs=[pl.BlockSpec((1, TM_PROJ, TN_PROJ), lambda b, i, j: (b, i, j)),
                   pl.BlockSpec((1, TM_PROJ, 128), lambda b, i, j: (b, i, 0)),
                   pl.BlockSpec((1, TM_PROJ, nkv), lambda b, i, j: (b, i, 0))],
        out_shape=[jax.ShapeDtypeStruct((bsz, s, n), BF16),
                   jax.ShapeDtypeStruct((bsz, s, 128), F32),
                   jax.ShapeDtypeStruct((bsz, s, nkv), F32)],
        scratch_shapes=[pltpu.VMEM((TM_PROJ, D_MODEL), BF16)],
        compiler_params=_cp(("arbitrary", "arbitrary", "arbitrary")),
        name="proj",
    )(x, shift, scale, g_pre.reshape(1, D_MODEL), w, w_small, w_kvc)


def _ssm_kernel(xbc_ref, z_ref, small_ref, shift_ref, spread_ref, convw_ref, convb_ref, dtb_ref,
                alog_ref, dskip_ref, gn_ref, o_ref, tail_scr, state_scr, y_scr):
    cl = SSM_CHUNK

    @pl.when(pl.program_id(1) == 0)
    def _():
        tail_scr[...] = jnp.zeros(tail_scr.shape, BF16)
        state_scr[...] = jnp.zeros(state_scr.shape, F32)

    for sc in range(xbc_ref.shape[1] // cl):
        rows = pl.ds(sc * cl, cl)
        _ssm_chunk(xbc_ref.at[0, rows], z_ref.at[0, rows], small_ref.at[0, rows], shift_ref, spread_ref,
                   convw_ref, convb_ref, dtb_ref, alog_ref, dskip_ref, gn_ref, o_ref.at[0, rows],
                   tail_scr, state_scr, y_scr.at[rows])


def _ssm_chunk(xbc_ref, z_ref, small_ref, shift_ref, spread_ref, convw_ref, convb_ref, dtb_ref,
               alog_ref, dskip_ref, gn_ref, o_ref, tail_scr, state_scr, y_scr):
    cl = SSM_CHUNK
    cur = xbc_ref[...]
    xpad = jnp.concatenate([tail_scr[...], cur], axis=0).astype(F32)
    taps = jnp.concatenate([(xpad * convw_ref[k:k + 1, :]).astype(BF16) for k in range(SSM_CONV - 1)],
                           axis=0)
    tail_scr[...] = cur[cl - CONV_TAIL:cl, :]
    acc = (jnp.dot(shift_ref[...], taps, preferred_element_type=F32)
           + convw_ref[SSM_CONV - 1:SSM_CONV, :] * xpad[CONV_TAIL:] + convb_ref[...])
    u = _silu(acc)
    xs = u[:, :D_INNER]
    bm = u[:, D_INNER:D_INNER + SSM_GROUPS * SSM_STATE]
    cm = u[:, D_INNER + SSM_GROUPS * SSM_STATE:]

    pre = small_ref[...] + dtb_ref[...]
    dt = jnp.maximum(pre, 0.0) + jnp.log1p(jnp.exp(-jnp.abs(pre)))
    a = -jnp.exp(alog_ref[...])
    adt = dt * a
    row = lax.broadcasted_iota(jnp.int32, (cl, cl), 0)
    col = lax.broadcasted_iota(jnp.int32, (cl, cl), 1)
    causal = row >= col
    tri = causal.astype(F32)
    a_cs = jnp.dot(tri, adt, preferred_element_type=F32, precision=HIGHEST) * LOG2E
    a_cs_t = a_cs.T
    dt_t = dt.T
    a_last = a_cs[cl - 1:cl, :]
    ea = jnp.exp2(a_cs)
    dsc = jnp.exp2(a_last - a_cs) * dt

    both = jnp.concatenate([ea, dsc], axis=0)
    hi = both.astype(BF16)
    lo = (both - hi.astype(F32)).astype(BF16)
    both_x = jnp.dot(jnp.concatenate([hi, lo], axis=1), spread_ref[...], preferred_element_type=F32)
    ea_x = both_x[:cl]
    dsc_x = both_x[cl:]
    cdec_x = ea_x[cl - 1:cl, :]
    xsd = xs * dsc_x
    lane = lax.broadcasted_iota(jnp.int32, (cl, 2 * SSM_HEAD_DIM), 1)
    first_half = lane < SSM_HEAD_DIM

    hg = SSM_HEADS // SSM_GROUPS
    gw = hg * SSM_HEAD_DIM
    for g in range(SSM_GROUPS):
        bg = bm[:, g * SSM_STATE:(g + 1) * SSM_STATE].astype(BF16)
        cg = cm[:, g * SSM_STATE:(g + 1) * SSM_STATE].astype(BF16)
        gc = slice(g * gw, (g + 1) * gw)
        cb = lax.dot_general(cg, bg, NT, preferred_element_type=F32)
        st = state_scr[g]
        y_scr[:, gc] = jnp.dot(cg, st.astype(BF16), preferred_element_type=F32) * ea_x[:, gc]
        new = lax.dot_general(bg, xsd[:, gc].astype(BF16), (((0,), (0,)), ((), ())),
                              preferred_element_type=F32)
        state_scr[g] = st * cdec_x[:, gc] + new
        for pp in range(hg // 2):
            pair = g * (hg // 2) + pp
            pc = slice(pair * 128, (pair + 1) * 128)
            gm = []
            for e in range(2):
                h = 2 * pair + e
                seg = a_cs[:, h:h + 1] - a_cs_t[h:h + 1, :]
                lmat = jnp.exp2(jnp.where(causal, seg, NEG))
                gm.append(cb * lmat * dt_t[h:h + 1, :])
            lhs = jnp.concatenate(gm, axis=0).astype(BF16)
            yy = jnp.dot(lhs, xs[:, pc].astype(BF16), preferred_element_type=F32)
            y_scr[:, pc] = y_scr[:, pc] + jnp.where(first_half, yy[:cl], yy[cl:])

    y = y_scr[...] + xs * dskip_ref[...]
    y = y * _silu(z_ref[...].astype(F32))
    gsz = D_INNER // SSM_GROUPS
    for g in range(SSM_GROUPS):
        yg = y[:, g * gsz:(g + 1) * gsz]
        yn = yg * lax.rsqrt(jnp.mean(yg * yg, axis=-1, keepdims=True) + EPS)
        o_ref[:, g * gsz:(g + 1) * gsz] = (yn * gn_ref[:, g * gsz:(g + 1) * gsz]).astype(o_ref.dtype)


def _conv_shift_matrix():
    cl = SSM_CHUNK
    ext = CONV_TAIL + cl
    m = np.zeros((cl, (SSM_CONV - 1) * ext), np.float32)
    t = np.arange(cl)
    for k in range(SSM_CONV - 1):
        m[t, k * ext + CONV_TAIL + t - (SSM_CONV - 1 - k)] = 1.0
    return jnp.asarray(m, dtype=BF16)


def _head_spread_matrix():
    m = np.zeros((2, 128, D_INNER), np.float32)
    for h in range(SSM_HEADS):
        m[:, h, h * SSM_HEAD_DIM:(h + 1) * SSM_HEAD_DIM] = 1.0
    return jnp.asarray(m.reshape(256, D_INNER), dtype=BF16)


def _ssm(pb, small, conv_w, conv_b, dt_bias, a_log, d_skip, g_norm):
    bsz, s, _ = pb.shape
    cl = SSM_CHUNK
    pad = 128 - SSM_HEADS
    dtb = jnp.pad(dt_bias, (0, pad)).reshape(1, 128)
    alog = jnp.pad(a_log, (0, pad)).reshape(1, 128)
    dskip = jnp.repeat(d_skip, SSM_HEAD_DIM).reshape(1, D_INNER)
    const = lambda shape: pl.BlockSpec(shape, lambda b, c: (0, 0))
    return pl.pallas_call(
        _ssm_kernel,
        grid=(bsz, s // SSM_STEP),
        in_specs=[pl.BlockSpec((1, SSM_STEP, D_CONV), lambda b, c: (b, c, B_XBC // D_CONV)),
                  pl.BlockSpec((1, SSM_STEP, D_INNER), lambda b, c: (b, c, B_ZSSM // D_INNER)),
                  pl.BlockSpec((1, SSM_STEP, 128), lambda b, c: (b, c, 0)),
                  const((cl, (SSM_CONV - 1) * (CONV_TAIL + cl))), const((256, D_INNER)),
                  const((SSM_CONV, D_CONV)), const((1, D_CONV)), const((1, 128)), const((1, 128)),
                  const((1, D_INNER)), const((1, D_INNER))],
        out_specs=pl.BlockSpec((1, SSM_STEP, D_INNER), lambda b, c: (b, c, 0)),
        out_shape=jax.ShapeDtypeStruct((bsz, s, D_INNER), BF16),
        scratch_shapes=[pltpu.VMEM((CONV_TAIL, D_CONV), BF16),
                        pltpu.VMEM((SSM_GROUPS, SSM_STATE, D_INNER // SSM_GROUPS), F32),
                        pltpu.VMEM((SSM_STEP, D_INNER), F32)],
        compiler_params=_cp(("arbitrary", "arbitrary")),
        name="ssm",
    )(pb, pb, small, _conv_shift_matrix(), _head_spread_matrix(), conv_w, conv_b.reshape(1, D_CONV),
      dtb, alog, dskip,
      g_norm.reshape(1, D_INNER))


def _cmp_kernel(r0_ref, r1_ref, r2_ref, r3_ref, pos_ref, w1_ref, w2_ref, o_ref):
    half = CMP_STRIDE * ATT_HEAD_DIM
    hd = ATT_HEAD_DIM
    nrow = r0_ref.shape[1] // CMP_STRIDE
    toks = [[r_ref[0, pl.ds(t, nrow, stride=CMP_STRIDE), :] for t in range(CMP_STRIDE)]
            for r_ref in (r0_ref, r1_ref, r2_ref, r3_ref)]
    for kv in range(2):
        w1 = w1_ref[kv]
        posb = jnp.broadcast_to(pos_ref[kv], (8, 2 * half)).astype(BF16)
        cpos = jnp.dot(posb, w1, preferred_element_type=F32)[0:1]
        for gg in range(ATT_KV_GROUPS):
            j = kv * ATT_KV_GROUPS + gg
            r = jnp.concatenate([toks[j // 2][t][:, (j % 2) * hd:(j % 2 + 1) * hd]
                                 for t in range(CMP_STRIDE)], axis=1).astype(BF16)
            first = jnp.dot(r, w1[:half], preferred_element_type=F32)
            second = jnp.dot(r, w1[half:], preferred_element_type=F32)
            hid = first + pltpu.roll(second, nrow - 1, 0) + cpos
            o_ref[0, j] = jnp.dot(_silu(hid).astype(BF16), w2_ref[kv], preferred_element_type=F32)


def _compress(kvc, pos, w1, w2):
    bsz, s, wid = kvc.shape
    nrow = s // CMP_STRIDE
    ng = 2 * ATT_KV_GROUPS
    return pl.pallas_call(
        _cmp_kernel,
        grid=(bsz,),
        in_specs=[pl.BlockSpec((1, s, 128), lambda b, c=c: (b, 0, c)) for c in range(wid // 128)]
                 + [pl.BlockSpec((2, 1, 2048), lambda b: (0, 0, 0)),
                  pl.BlockSpec((2, 2048, CMP_HIDDEN), lambda b: (0, 0, 0)),
                  pl.BlockSpec((2, CMP_HIDDEN, ATT_HEAD_DIM), lambda b: (0, 0, 0))],
        out_specs=pl.BlockSpec((1, ng, nrow, ATT_HEAD_DIM), lambda b: (b, 0, 0, 0)),
        out_shape=jax.ShapeDtypeStruct((bsz, ng, nrow, ATT_HEAD_DIM), F32),
        compiler_params=_cp(("arbitrary",)),
        name="cmp",
    )(kvc, kvc, kvc, kvc, pos, w1, w2)


def _bf16_round_np(x):
    u = np.asarray(x, np.float32).view(np.uint32)
    u = (u + (((u >> 16) & 1) + 0x7FFF)) & np.uint32(0xFFFF0000)
    return u.view(np.float32)


def _slope_table():
    slope = (2.0 ** (-8.0 * np.arange(1, ATT_HEADS + 1) / ATT_HEADS)).astype(np.float32)
    slope = (slope.astype(np.float64) * LOG2E).astype(np.float32)
    p0 = _bf16_round_np(slope)
    p1 = _bf16_round_np(slope - p0)
    p2 = _bf16_round_np(slope - p0 - p1)
    tab = np.zeros((ATT_HEADS, SLOPE_STRIDE), np.float32)
    tab[:, 0], tab[:, 1], tab[:, 2] = p0, p1, p2
    tab[:, 3:6] = tab[:, 0:3] * POS_BASE
    tab[:, 6] = slope
    return jnp.asarray(tab.reshape(-1))


def _key_aux_table(s):
    pos = np.arange(s)
    tab = np.zeros((s, 128), np.float32)
    tab[:, L_POS:L_POS + 3] = (pos % POS_BASE)[:, None]
    tab[:, L_TILE:L_TILE + 3] = (pos // POS_BASE)[:, None]
    tab[pos, L_MASK + pos // SLC_BLOCK] = 1.0
    return jnp.asarray(tab, dtype=BF16)


def _bias_table():
    assert WINDOW == 2 * KT and QH == KT
    r = np.arange(QH)[:, None]
    c = np.arange(KT)[None, :]
    oldest = np.where(2 * KT + r - c < WINDOW, 0.0, NEG)
    causal = np.where(r - c >= 0, 0.0, NEG)
    return jnp.asarray(np.stack([oldest, causal]), dtype=F32)


def _gate_expand_table():
    tab = np.zeros((ATT_KV_GROUPS, 2, 128, 3 * KV_WIDTH), np.float32)
    for g in range(ATT_KV_GROUPS):
        for h in range(ATT_HPG):
            for j in range(3):
                row = SMALL_GATE + g * 3 * ATT_HPG + 3 * h + j
                tab[g, :, row, j * KV_WIDTH + h * ATT_HEAD_DIM:j * KV_WIDTH + (h + 1) * ATT_HEAD_DIM] = 1.0
    return jnp.asarray(tab.reshape(ATT_KV_GROUPS, 256, 3 * KV_WIDTH), dtype=BF16)


def _cmp_to_slc_matrix_t(n_cmp_pad, n_cmp, n_slc):
    cs = np.arange(n_cmp_pad) * CMP_STRIDE
    ss = np.arange(n_slc) * SLC_BLOCK
    lo = np.maximum(cs[:, None], ss[None, :])
    hi = np.minimum(cs[:, None] + CMP_BLOCK, ss[None, :] + SLC_BLOCK)
    m = np.clip(hi - lo, 0, None) / CMP_BLOCK
    m[n_cmp:] = 0.0
    return jnp.asarray(m.T, dtype=F32)


def _cmpattn_kernel(slopes_ref, q_ref, ck_ref, cv_ref, mt_ref, ocmp_ref, qmask_ref,
                    kcat_scr, vbd_scr):
    g = pl.program_id(1)
    i = pl.program_id(2)
    tq = q_ref.shape[1]
    t0 = i * tq
    npad = ck_ref.shape[2]
    n_slc = mt_ref.shape[0]

    @pl.when(i == 0)
    def _():
        kc = ck_ref[0, 0]
        vc = cv_ref[0, 0]
        k_hi = kc.astype(BF16).astype(F32)
        k_lo = kc - k_hi
        zero = jnp.zeros_like(kc)
        for h in range(ATT_HPG):
            def bd(a):
                return jnp.concatenate([a if hh == h else zero for hh in range(ATT_HPG)], axis=1).astype(BF16)
            kcat_scr[h * npad:(h + 1) * npad, :KV_WIDTH] = bd(k_hi)
            kcat_scr[h * npad:(h + 1) * npad, KV_WIDTH:] = bd(k_lo)
            vbd_scr[h * npad:(h + 1) * npad, :] = bd(vc)

    q = q_ref[0]
    s = lax.dot_general(jnp.concatenate([q, q], axis=1), kcat_scr[...], NT,
                        preferred_element_type=F32)
    jcol = lax.broadcasted_iota(jnp.int32, (1, npad), 1)
    cmp_end = jcol * CMP_STRIDE + (CMP_BLOCK - 1)
    trow = t0 + lax.broadcasted_iota(jnp.int32, (tq, npad), 0)
    valid = (cmp_end <= trow) & (jcol < npad - 1)
    any_valid = trow[:, 0:1] >= CMP_BLOCK - 1
    rel = (cmp_end - t0).astype(F32)
    psum = jnp.zeros((tq, npad), F32)
    ps = []
    for h in range(ATT_HPG):
        slope = slopes_ref[(g * ATT_HPG + h) * SLOPE_STRIDE + 6]
        sh = jnp.where(valid, s[:, h * npad:(h + 1) * npad] + slope * rel, NEG)
        mx = jnp.max(sh, axis=-1, keepdims=True)
        p = jnp.exp2(sh - mx)
        l = jnp.sum(p, axis=-1, keepdims=True)
        p = p * jnp.where(any_valid, 1.0 / l, 0.0)
        ps.append(p)
        psum = psum + p
    ocmp_ref[0] = jnp.dot(jnp.concatenate(ps, axis=1).astype(BF16), vbd_scr[...],
                          preferred_element_type=F32)
    ps_hi = psum.astype(BF16)
    ps_lo = (psum - ps_hi.astype(F32)).astype(BF16)
    mt = mt_ref[...].astype(BF16)
    imp_t = lax.dot_general(jnp.concatenate([mt, mt], axis=1), jnp.concatenate([ps_hi, ps_lo], axis=1),
                            NT, preferred_element_type=F32)

    nrb = n_slc // 8
    blk_t = (t0 + lax.broadcasted_iota(jnp.int32, (8, tq), 1)) >> SLC_SHIFT
    sub = lax.broadcasted_iota(jnp.int32, (8, tq), 0)
    score = []
    for rb in range(nrb):
        kk = sub + rb * 8
        imp = imp_t[rb * 8:(rb + 1) * 8]
        forced = (kk == 0) | (kk == blk_t) | (kk == blk_t - 1)
        score.append(jnp.where(forced, imp + FORCE_BONUS, jnp.where(kk <= blk_t, imp, -1.0)))
    rank = [jnp.zeros((8, tq), F32) for _ in range(nrb)]
    for j in range(n_slc):
        cj = jnp.broadcast_to(score[j // 8][j % 8:j % 8 + 1, :], (8, tq))
        for rb in range(nrb):
            ge = jnp.where(cj >= score[rb], 1.0, 0.0)
            gt = jnp.where(cj > score[rb], 1.0, 0.0)
            if rb * 8 > j:
                beats = ge
            elif rb * 8 + 7 <= j:
                beats = gt
            else:
                beats = jnp.where(sub > j - rb * 8, ge, gt)
            rank[rb] = rank[rb] + beats
    rows = [jnp.zeros((L_MASK, tq), F32)]
    rows += [jnp.where(rank[rb] < float(SLC_TOPK), 0.0, -BIG) for rb in range(nrb)]
    rows += [jnp.zeros((128 - L_MASK - n_slc, tq), F32)]
    qmask_ref[0, 0] = jnp.concatenate(rows, axis=0).T.astype(BF16)


def _cmpattn(pb, cmp_kv, s):
    bsz = pb.shape[0]
    g = ATT_KV_GROUPS
    nrow = cmp_kv.shape[2]
    n_cmp = (s - CMP_BLOCK) // CMP_STRIDE + 1
    n_slc = s // SLC_BLOCK
    mt = _cmp_to_slc_matrix_t(nrow, n_cmp, n_slc)
    return pl.pallas_call(
        _cmpattn_kernel,
        grid=(bsz, g, s // TQ_CMP),
        in_specs=[pl.BlockSpec(memory_space=pltpu.SMEM),
                  pl.BlockSpec((1, TQ_CMP, KV_WIDTH), lambda b, gg, i: (b, i, B_Q // KV_WIDTH + gg)),
                  pl.BlockSpec((1, 1, nrow, ATT_HEAD_DIM), lambda b, gg, i: (b, gg, 0, 0)),
                  pl.BlockSpec((1, 1, nrow, ATT_HEAD_DIM), lambda b, gg, i: (b, g + gg, 0, 0)),
                  pl.BlockSpec((n_slc, nrow), lambda b, gg, i: (0, 0))],
        out_specs=[pl.BlockSpec((1, TQ_CMP, KV_WIDTH), lambda b, gg, i: (b, i, gg)),
                   pl.BlockSpec((1, 1, TQ_CMP, 128), lambda b, gg, i: (b, gg, i, 0))],
        out_shape=[jax.ShapeDtypeStruct((bsz, s, ATT_WIDTH), F32),
                   jax.ShapeDtypeStruct((bsz, g, s, 128), BF16)],
        scratch_shapes=[pltpu.VMEM((ATT_HPG * nrow, 2 * KV_WIDTH), BF16),
                        pltpu.VMEM((ATT_HPG * nrow, KV_WIDTH), BF16)],
        compiler_params=_cp(("arbitrary", "arbitrary", "arbitrary")),
        name="cmpattn",
    )(_slope_table(), pb, cmp_kv, cmp_kv, mt)


def _flash_step(qa_ref, ka_ref, va_ref, m_scr, acc_scr, rows, ksub, nsub, bias, first=False):
    r0, nr = rows
    nk = nsub * KT
    start = pl.multiple_of(ksub * KT, KT)
    k_aug = ka_ref[pl.ds(start, nk), :]
    v_aug = va_ref[pl.ds(start, nk), :]
    s = lax.dot_general(qa_ref[r0:r0 + nr, :], k_aug, NT, preferred_element_type=F32)
    if bias is not None:
        parts = []
        for t, b in enumerate(bias):
            st = s[:, t * KT:(t + 1) * KT]
            if b is not None:
                st = (st.reshape(nr // QH, QH, KT) + b[None]).reshape(nr, KT)
            parts.append(st)
        s = jnp.concatenate(parts, axis=1) if nsub > 1 else parts[0]
    m_cur = jnp.max(s, axis=-1, keepdims=True)
    if first:
        m_new = jnp.broadcast_to(m_cur, (nr, 128))
    else:
        m_prev = m_scr[r0:r0 + nr, :]
        m_new = jnp.maximum(m_prev, m_cur)
    p = jnp.exp2(s - jnp.concatenate([m_new] * (nk // 128), axis=1))
    pv_t = lax.dot_general(v_aug, p.astype(BF16), (((0,), (1,)), ((), ())),
                           preferred_element_type=F32)
    if first:
        acc_scr[:, r0:r0 + nr] = pv_t
    else:
        acc_scr[:, r0:r0 + nr] = jnp.exp2(m_prev - m_new).T * acc_scr[:, r0:r0 + nr] + pv_t
    m_scr[r0:r0 + nr, :] = m_new


def _flash_out(acc_scr):
    r = acc_scr.shape[1]
    den = acc_scr[0:8, :]
    num = acc_scr[ATT_HEAD_DIM:2 * ATT_HEAD_DIM, :]
    o_t = (num.reshape(ATT_HEAD_DIM // 8, 8, r) / den[None]).reshape(ATT_HEAD_DIM, r)
    halves = []
    for half in range(TQ // QH):
        cols = [o_t[:, (half * ATT_HPG + h) * QH:(half * ATT_HPG + h + 1) * QH] for h in range(ATT_HPG)]
        halves.append(jnp.concatenate(cols, axis=0).T)
    return jnp.concatenate(halves, axis=0)


def _attn_kernel(slopes_ref, q_ref, ksel_ref, kwin_ref, aux_ref, qmask_ref, ocmp_ref, small_ref,
                 zatt_ref, pick_ref, bias_ref, o_ref, qas_scr, qaw_scr, m_scr, acc_scr,
                 kas_scr, vas_scr, kaw_scr, vaw_scr):
    g = pl.program_id(1)
    i = pl.program_id(2)

    @pl.when(i == 0)
    def _():
        s_len = ksel_ref.shape[1]
        left = jnp.where(lax.broadcasted_iota(jnp.int32, (s_len, 128), 1) < ATT_HEAD_DIM,
                         1.0, 0.0).astype(BF16)
        for kv_ref, ka_scr, va_scr in ((ksel_ref, kas_scr, vas_scr), (kwin_ref, kaw_scr, vaw_scr)):
            kv = kv_ref[0]
            ka_scr[...] = kv * left + aux_ref[...]
            va_scr[...] = kv * (1.0 - left) + left

    qf = q_ref[0].astype(F32)
    qm = qmask_ref[0, 0].astype(F32)
    lane_row = lax.broadcasted_iota(jnp.int32, (1, 128), 1)
    left = lax.broadcasted_iota(jnp.int32, (TQ, 128), 1) < ATT_HEAD_DIM
    for h in range(ATT_HPG):
        slab = qf[:, 128 * (h // 2):128 * (h // 2) + 128]
        if h % 2:
            slab = pltpu.roll(slab, ATT_HEAD_DIM, 1)
        ext = jnp.zeros((1, 128), F32)
        for c in range(6):
            ext = jnp.where(lane_row == L_POS + c, slopes_ref[(g * ATT_HPG + h) * SLOPE_STRIDE + c], ext)
        base = jnp.where(left, slab, ext)
        for half in range(TQ // QH):
            rs = (half * ATT_HPG + h) * QH
            tok = slice(half * QH, (half + 1) * QH)
            qaw_scr[rs:rs + QH, :] = base[tok].astype(BF16)
            qas_scr[rs:rs + QH, :] = (base[tok] + qm[tok]).astype(BF16)

    half_rows = ATT_HPG * QH
    rows_a, rows_b, rows_all = (0, half_rows), (half_rows, half_rows), (0, 2 * half_rows)
    k0 = i * (TQ // KT)
    oldest, causal = bias_ref[0], bias_ref[1]

    _flash_step(qas_scr, kas_scr, vas_scr, m_scr, acc_scr, rows_a, k0, 1, [causal], first=True)
    _flash_step(qas_scr, kas_scr, vas_scr, m_scr, acc_scr, rows_b, k0, 2, [None, causal], first=True)

    for n_prev in range(1, ksel_ref.shape[1] // TK):
        @pl.when(i == n_prev)
        def _():
            for j in range(n_prev):
                _flash_step(qas_scr, kas_scr, vas_scr, m_scr, acc_scr, rows_all, j * (TK // KT),
                            TK // KT, None)

    o_slc = _flash_out(acc_scr)

    @pl.when(i == 0)
    def _():
        _flash_step(qaw_scr, kaw_scr, vaw_scr, m_scr, acc_scr, rows_a, 0, 1, [causal], first=True)
        _flash_step(qaw_scr, kaw_scr, vaw_scr, m_scr, acc_scr, rows_b, 0, 2, [None, causal], first=True)

    @pl.when(i >= 1)
    def _():
        win = [oldest, None, causal]
        _flash_step(qaw_scr, kaw_scr, vaw_scr, m_scr, acc_scr, rows_a, k0 - 2, 3, win, first=True)
        _flash_step(qaw_scr, kaw_scr, vaw_scr, m_scr, acc_scr, rows_b, k0 - 1, 3, win, first=True)

    o_win = _flash_out(acc_scr)

    gate = _sigmoid(small_ref[0])
    g_hi = gate.astype(BF16)
    g_lo = (gate - g_hi.astype(F32)).astype(BF16)
    gts = jnp.dot(jnp.concatenate([g_hi, g_lo], axis=1), pick_ref[0],
                  preferred_element_type=F32)
    o = (gts[:, :KV_WIDTH] * ocmp_ref[0] + gts[:, KV_WIDTH:2 * KV_WIDTH] * o_slc
         + gts[:, 2 * KV_WIDTH:] * o_win)
    o_ref[0] = (o * _silu(zatt_ref[0].astype(F32))).astype(o_ref.dtype)


def _attn(pb, small, qmask, ocmp):
    bsz, s, _ = pb.shape
    g = ATT_KV_GROUPS
    r = ATT_HPG * TQ
    return pl.pallas_call(
        _attn_kernel,
        grid=(bsz, g, s // TQ),
        in_specs=[pl.BlockSpec(memory_space=pltpu.SMEM),
                  pl.BlockSpec((1, TQ, KV_WIDTH), lambda b, gg, i: (b, i, B_Q // KV_WIDTH + gg)),
                  pl.BlockSpec((1, s, 128), lambda b, gg, i: (b, 0, B_KSEL // 128 + gg)),
                  pl.BlockSpec((1, s, 128), lambda b, gg, i: (b, 0, B_KWIN // 128 + gg)),
                  pl.BlockSpec((s, 128), lambda b, gg, i: (0, 0)),
                  pl.BlockSpec((1, 1, TQ, 128), lambda b, gg, i: (b, gg, i, 0)),
                  pl.BlockSpec((1, TQ, KV_WIDTH), lambda b, gg, i: (b, i, gg)),
                  pl.BlockSpec((1, TQ, 128), lambda b, gg, i: (b, i, 0)),
                  pl.BlockSpec((1, TQ, KV_WIDTH), lambda b, gg, i: (b, i, B_ZATT // KV_WIDTH + gg)),
                  pl.BlockSpec((1, 256, 3 * KV_WIDTH), lambda b, gg, i: (gg, 0, 0)),
                  pl.BlockSpec((2, QH, KT), lambda b, gg, i: (0, 0, 0))],
        out_specs=pl.BlockSpec((1, TQ, KV_WIDTH), lambda b, gg, i: (b, i, gg)),
        out_shape=jax.ShapeDtypeStruct((bsz, s, ATT_WIDTH), BF16),
        scratch_shapes=[pltpu.VMEM((r, 128), BF16), pltpu.VMEM((r, 128), BF16),
                        pltpu.VMEM((r, 128), F32), pltpu.VMEM((128, r), F32)]
                       + [pltpu.VMEM((s, 128), BF16)] * 4,
        compiler_params=_cp(("arbitrary", "arbitrary", "arbitrary")),
        name="attn",
    )(_slope_table(), pb, pb, pb, _key_aux_table(s), qmask, ocmp, small, pb, _gate_expand_table(),
      _bias_table())


def _merge_kernel(yssm_ref, onsa_ref, mg_ref, x_ref, gate_ref, gpost_ref, wssm_ref, wnsa_ref, wout_ref, o_ref):
    y_ssm = jnp.dot(yssm_ref[0], wssm_ref[...], preferred_element_type=F32)
    y_nsa = jnp.dot(onsa_ref[0], wnsa_ref[...], preferred_element_type=F32)
    mg = _sigmoid(mg_ref[0].astype(F32))
    merged = mg[:, :D_MODEL] * y_ssm + mg[:, D_MODEL:] * y_nsa
    out = jnp.dot(merged.astype(BF16), wout_ref[...], preferred_element_type=F32)
    yn = out * lax.rsqrt(jnp.mean(out * out, axis=-1, keepdims=True) + EPS) * gpost_ref[...]
    o_ref[0] = x_ref[0] + gate_ref[0] * yn


def _merge(yssm, onsa, pb, x, gate, g_post, w_ssm_out, w_nsa_out, w_out):
    bsz, s, _ = x.shape
    tm = 512
    const = lambda shape: pl.BlockSpec(shape, lambda b, i: (0, 0))
    return pl.pallas_call(
        _merge_kernel,
        grid=(bsz, s // tm),
        in_specs=[pl.BlockSpec((1, tm, D_INNER), lambda b, i: (b, i, 0)),
                  pl.BlockSpec((1, tm, ATT_WIDTH), lambda b, i: (b, i, 0)),
                  pl.BlockSpec((1, tm, 2 * D_MODEL), lambda b, i: (b, i, B_MERGE // (2 * D_MODEL))),
                  pl.BlockSpec((1, tm, D_MODEL), lambda b, i: (b, i, 0)),
                  pl.BlockSpec((1, 1, D_MODEL), lambda b, i: (b, 0, 0)),
                  const((1, D_MODEL)),
                  const((D_INNER, D_MODEL)), const((ATT_WIDTH, D_MODEL)), const((D_MODEL, D_MODEL))],
        out_specs=pl.BlockSpec((1, tm, D_MODEL), lambda b, i: (b, i, 0)),
        out_shape=jax.ShapeDtypeStruct((bsz, s, D_MODEL), F32),
        compiler_params=_cp(("arbitrary", "arbitrary")),
        name="merge",
    )(yssm, onsa, pb, x, gate, g_post.reshape(1, D_MODEL), w_ssm_out, w_nsa_out, w_out)


def _wprep_kernel(w_ref, wb_ref, ws_ref, wkv_ref):
    def put(ref, dst, off, n, scale=None):
        v = w_ref[off:off + n, :]
        ref[dst:dst + n, :] = (v if scale is None else v * scale).astype(BF16)

    put(wb_ref, B_XBC, IN_XBC, D_CONV)
    put(wb_ref, B_ZATT, IN_ZATT, ATT_WIDTH)
    put(wb_ref, B_MERGE, IN_MERGE, 2 * D_MODEL)
    put(wb_ref, B_ZSSM, IN_Z, D_INNER)
    put(wb_ref, B_Q, IN_Q, ATT_WIDTH, ATT_HEAD_DIM ** -0.5 * LOG2E)
    hd = ATT_HEAD_DIM
    for dst, k_off in ((B_KSEL, IN_KV + 2 * KV_WIDTH), (B_KWIN, IN_KV + 4 * KV_WIDTH)):
        for gg in range(ATT_KV_GROUPS):
            put(wb_ref, dst + 2 * gg * hd, k_off + gg * hd, hd)
            put(wb_ref, dst + (2 * gg + 1) * hd, k_off + KV_WIDTH + gg * hd, hd)
    put(wkv_ref, 0, IN_KV, 2 * KV_WIDTH)
    put(ws_ref, 0, IN_DT, SSM_HEADS)
    put(ws_ref, SMALL_GATE, IN_GATE, 3 * ATT_HEADS)
    used = SMALL_GATE + 3 * ATT_HEADS
    ws_ref[used:, :] = jnp.zeros((128 - used, ws_ref.shape[1]), BF16)


def _split_w_in(w_in):
    dm, n_in = w_in.shape
    tc = 128
    return pl.pallas_call(
        _wprep_kernel,
        grid=(dm // tc,),
        in_specs=[pl.BlockSpec((n_in, tc), lambda i: (0, i))],
        out_specs=[pl.BlockSpec((NB, tc), lambda i: (0, i)), pl.BlockSpec((128, tc), lambda i: (0, i)),
                   pl.BlockSpec((2 * KV_WIDTH, tc), lambda i: (0, i))],
        out_shape=[jax.ShapeDtypeStruct((NB, dm), BF16), jax.ShapeDtypeStruct((128, dm), BF16),
                   jax.ShapeDtypeStruct((2 * KV_WIDTH, dm), BF16)],
        compiler_params=_cp(("arbitrary",)),
        name="wprep",
    )(w_in.T)


def _layer(x, c, w_ada, b_ada, g_pre, g_post, w_in, conv_w, conv_b, dt_bias, a_log, d_skip,
           g_ssm_norm, w_ssm_out, cmp_pos_k, cmp_w1_k, cmp_w2_k, cmp_pos_v, cmp_w1_v, cmp_w2_v,
           w_nsa_out, w_out):
    bsz, s, dm = x.shape
    mod = _ada(c, w_ada, b_ada)
    shift = mod[:, None, :dm]
    scale = mod[:, None, dm:2 * dm]
    gate = mod[:, None, 2 * dm:]
    wb, w_small, w_kvc = _split_w_in(w_in)
    pb, small, kvc = _proj(x, shift, scale, g_pre, wb, w_small, w_kvc)

    yssm = _ssm(pb, small, conv_w, conv_b, dt_bias, a_log, d_skip, g_ssm_norm)

    pos = jnp.stack([cmp_pos_k.reshape(1, -1), cmp_pos_v.reshape(1, -1)])
    w1 = jnp.stack([cmp_w1_k, cmp_w1_v]).astype(BF16)
    w2 = jnp.stack([cmp_w2_k, cmp_w2_v]).astype(BF16)
    cmp_kv = _compress(kvc, pos, w1, w2)

    ocmp, qmask = _cmpattn(pb, cmp_kv, s)
    onsa = _attn(pb, small, qmask, ocmp)
    return _merge(yssm, onsa, pb, x, gate, g_post, w_ssm_out.astype(BF16), w_nsa_out.astype(BF16),
                  w_out.astype(BF16))


@jax.jit
def kernel(x, c, w_ada, b_ada, g_pre, g_post, w_in, conv_w, conv_b, dt_bias, a_log, d_skip, g_ssm_norm,
           w_ssm_out, cmp_pos_k, cmp_w1_k, cmp_w2_k, cmp_pos_v, cmp_w1_v, cmp_w2_v, w_nsa_out, w_out):
    for layer in range(w_in.shape[0]):
        x = _layer(x, c, w_ada[layer], b_ada[layer], g_pre[layer], g_post[layer], w_in[layer],
                   conv_w[layer], conv_b[layer], dt_bias[layer], a_log[layer], d_skip[layer],
                   g_ssm_norm[layer], w_ssm_out[layer], cmp_pos_k[layer], cmp_w1_k[layer],
                   cmp_w2_k[layer], cmp_pos_v[layer], cmp_w1_v[layer], cmp_w2_v[layer],
                   w_nsa_out[layer], w_out[layer])
    return x
```
